```python
import math
import jax, jax.numpy as jnp
from jax import lax
import numpy as np

D_MODEL = 1024
BATCH = 16
SEQ = 2048
DEPTH = 1
DEC_BATCH = 32
DEC_SEQ = 1
PAST_LEN = 16384
PAGE_SIZE = 128

ATT_GROUPS = ((128, 1), (512, 4), (2048, 16))
HEADS_PER_GROUP = 4
HEAD_DIM = 64
N_ATT_GROUPS = len(ATT_GROUPS)
N_ATT_HEADS = N_ATT_GROUPS * HEADS_PER_GROUP
ATT_OUT = HEADS_PER_GROUP * HEAD_DIM
QKV_GROUP = 3 * ATT_OUT
Q_BLOCK = 128

SSM_WIDTH = D_MODEL // 2
SSM_GROUP_CH = 16
SSM_GROUPS = SSM_WIDTH // SSM_GROUP_CH
SSM_STATE = 64
DT_MIN = 1e-3
DT_MAX = 1e-1

MEM_TOKENS = 256
MEM_HEADS = 4
MEM_HEAD_DIM = D_MODEL // 8
MEM_WIDTH = MEM_HEADS * MEM_HEAD_DIM

N_BRANCHES = 3
SSM_COL = N_ATT_GROUPS * QKV_GROUP
MEMQ_COL = SSM_COL + SSM_WIDTH
GATE_COL = MEMQ_COL + MEM_WIDTH
IN_COLS = GATE_COL + N_BRANCHES * D_MODEL

MOE_GROUPS = 4
EXPERTS_PER_GROUP = 8
N_EXPERTS = MOE_GROUPS * EXPERTS_PER_GROUP
TOP_K = 2
D_EXPERT = D_MODEL // 2
MOE_BLOCK = 128

RMS_EPS = 1e-6

kernel_name = 'hybrid_dilated_s5_hmoe_decode_step'


def rmsnorm(x, g):
    xf = x.astype(jnp.float32)
    xf = xf * lax.rsqrt(jnp.mean(xf * xf, axis=-1, keepdims=True) + RMS_EPS)
    return xf.astype(x.dtype) * g


def alibi_slopes():
    h = jnp.arange(1, N_ATT_HEADS + 1, dtype=jnp.float32)
    return jnp.exp2(-8.0 * h / N_ATT_HEADS)


def dilated_attention(q, k, v, q_start, dilation, n_keys, slopes):
    bsz, lq, nh, dh = q.shape
    qb = lq if lq <= Q_BLOCK else math.gcd(lq, Q_BLOCK)
    n_blocks = lq // qb
    steps = jnp.arange(n_keys)
    bias = -slopes[:, None] * (steps * dilation).astype(jnp.float32)[None, :]
    scale = dh ** -0.5

    def one_block(bi):
        q_blk = lax.dynamic_slice_in_dim(q, bi * qb, qb, axis=1)
        rows = q_start + bi * qb + jnp.arange(qb)[:, None] - dilation * steps[None, :]
        valid = rows >= 0
        rows = jnp.maximum(rows, 0)
        k_g = k[:, rows]
        v_g = v[:, rows]
        s = jnp.einsum('bqhd,bqkhd->bqhk', q_blk, k_g).astype(jnp.float32) * scale + bias
        s = jnp.where(valid[None, :, None, :], s, -jnp.inf)
        m = jnp.max(s, axis=-1, keepdims=True)
        p = jnp.exp(s - m)
        den = jnp.sum(p, axis=-1, keepdims=True)
        o = jnp.einsum('bqhk,bqkhd->bqhd', (p / den).astype(v.dtype), v_g)
        return o, (m + jnp.log(den))[..., 0]

    o, lse = lax.map(one_block, jnp.arange(n_blocks))
    o = jnp.moveaxis(o, 0, 1).reshape(bsz, lq, nh, dh)
    lse = jnp.moveaxis(lse, 0, 1).reshape(bsz, lq, nh)
    return o, lse


def s5_branch(u, h0_re, h0_im, lam_re, lam_im, log_dt, b_re, b_im, c_re, c_im, d_skip, w_glu, b_glu):
    bsz, seq, _ = u.shape
    f32 = jnp.float32
    uf = u.astype(f32)
    ug = uf.reshape(bsz, seq, SSM_GROUPS, SSM_GROUP_CH)
    lr = lam_re.astype(f32)
    li = lam_im.astype(f32)
    dt = jnp.exp(log_dt.astype(f32))[:, None]
    mag = jnp.exp(lr * dt)
    ab_re = mag * jnp.cos(li * dt)
    ab_im = mag * jnp.sin(li * dt)
    den = lr * lr + li * li
    nr = ab_re - 1.0
    ni = ab_im
    z_re = ((nr * lr + ni * li) / den)[..., None]
    z_im = ((ni * lr - nr * li) / den)[..., None]
    br = b_re.astype(f32)
    bi = b_im.astype(f32)
    bb_re = z_re * br - z_im * bi
    bb_im = z_re * bi + z_im * br
    bu_re = jnp.einsum('blgc,gpc->blgp', ug, bb_re)
    bu_im = jnp.einsum('blgc,gpc->blgp', ug, bb_im)
    a_re = jnp.broadcast_to(ab_re, (1, seq, SSM_GROUPS, SSM_STATE))
    a_im = jnp.broadcast_to(ab_im, (1, seq, SSM_GROUPS, SSM_STATE))

    def combine(e1, e2):
        a1r, a1i, b1r, b1i = e1
        a2r, a2i, b2r, b2i = e2
        return (a2r * a1r - a2i * a1i, a2r * a1i + a2i * a1r,
                a2r * b1r - a2i * b1i + b2r, a2r * b1i + a2i * b1r + b2i)

    ac_re, ac_im, s_re, s_im = lax.associative_scan(combine, (a_re, a_im, bu_re, bu_im), axis=1)
    h0r = h0_re.astype(f32)[:, None]
    h0i = h0_im.astype(f32)[:, None]
    s_re = ac_re * h0r - ac_im * h0i + s_re
    s_im = ac_re * h0i + ac_im * h0r + s_im
    y = (jnp.einsum('gcp,blgp->blgc', c_re.astype(f32), s_re)
         - jnp.einsum('gcp,blgp->blgc', c_im.astype(f32), s_im))
    y = y.reshape(bsz, seq, SSM_WIDTH) + d_skip.astype(f32) * uf
    z = jax.nn.gelu(y)
    out = z * jax.nn.sigmoid(z @ w_glu.astype(f32) + b_glu.astype(f32))
    return out.astype(u.dtype), s_re[:, -1].astype(h0_re.dtype), s_im[:, -1].astype(h0_im.dtype)


def memory_kv(mem, g, w_mk, w_mv):
    bsz, n_mem, _ = mem.shape
    mn = rmsnorm(mem, g)
    k = (mn @ w_mk).reshape(bsz, n_mem, MEM_HEADS, MEM_HEAD_DIM)
    v = (mn @ w_mv).reshape(bsz, n_mem, MEM_HEADS, MEM_HEAD_DIM)
    return k, v


def memory_attention(q, mk, mv):
    s = jnp.einsum('blhd,bmhd->bhlm', q, mk).astype(jnp.float32) * (MEM_HEAD_DIM ** -0.5)
    p = jax.nn.softmax(s, axis=-1).astype(mv.dtype)
    return jnp.einsum('bhlm,bmhd->blhd', p, mv)


def hier_moe(h, w_grp, b_grp, w_exp, b_exp, w1, w3, w2):
    n_tok, d = h.shape
    grp_logits = (h @ w_grp).astype(jnp.float32) + b_grp.astype(jnp.float32)
    grp_prob = jax.nn.softmax(grp_logits, axis=-1)
    grp = jnp.argmax(grp_logits, axis=-1)
    p_grp = jnp.take_along_axis(grp_prob, grp[:, None], axis=-1)
    exp_logits = ((h @ w_exp).astype(jnp.float32) + b_exp.astype(jnp.float32)).reshape(
        n_tok, MOE_GROUPS, EXPERTS_PER_GROUP)
    in_grp = jnp.take_along_axis(exp_logits, grp[:, None, None], axis=1)[:, 0]
    top_val, top_idx = lax.top_k(in_grp, TOP_K)
    gate = jax.nn.softmax(top_val, axis=-1) * p_grp
    expert = grp[:, None] * EXPERTS_PER_GROUP + top_idx
    n_pairs = n_tok * TOP_K
    e_flat = expert.reshape(n_pairs)
    tok = jnp.repeat(jnp.arange(n_tok), TOP_K)
    order = jnp.argsort(e_flat)
    e_s = e_flat[order]
    tok_s = tok[order]
    w_s = gate.reshape(n_pairs)[order]
    counts = jnp.bincount(e_flat, length=N_EXPERTS)
    starts = jnp.cumsum(counts) - counts
    padded = (counts + MOE_BLOCK - 1) // MOE_BLOCK * MOE_BLOCK
    pad_ends = jnp.cumsum(padded)
    pad_starts = pad_ends - padded
    dest = pad_starts[e_s] + jnp.arange(n_pairs) - starts[e_s]
    n_blocks = -(-n_pairs // MOE_BLOCK) + N_EXPERTS
    xp = jnp.zeros((n_blocks * MOE_BLOCK, d), h.dtype).at[dest].set(h[tok_s])
    block_expert = jnp.minimum(
        jnp.searchsorted(pad_ends, jnp.arange(n_blocks) * MOE_BLOCK, side='right'), N_EXPERTS - 1)

    def expert_block(args):
        xb, e = args
        return (jax.nn.silu(xb @ w1[e]) * (xb @ w3[e])) @ w2[e]

    yp = lax.map(expert_block, (xp.reshape(n_blocks, MOE_BLOCK, d), block_expert))
    ys = yp.reshape(n_blocks * MOE_BLOCK, d)[dest] * w_s[:, None].astype(h.dtype)
    return jax.ops.segment_sum(ys, tok_s, num_segments=n_tok)


def decoder_layer(x, mem_k, mem_v, k_bufs, v_bufs, h0_re, h0_im, lw):
    bsz, seq, _ = x.shape
    xn = rmsnorm(x, lw['norm1'])
    proj = xn @ lw['w_in']
    slopes = alibi_slopes()
    outs, lses, new_k, new_v = [], [], [], []
    for gi, (window, dilation) in enumerate(ATT_GROUPS):
        c0 = gi * QKV_GROUP
        q = proj[..., c0:c0 + ATT_OUT].reshape(bsz, seq, HEADS_PER_GROUP, HEAD_DIM)
        k = proj[..., c0 + ATT_OUT:c0 + 2 * ATT_OUT].reshape(bsz, seq, HEADS_PER_GROUP, HEAD_DIM)
        v = proj[..., c0 + 2 * ATT_OUT:c0 + QKV_GROUP].reshape(bsz, seq, HEADS_PER_GROUP, HEAD_DIM)
        k_ext = jnp.concatenate([k_bufs[gi], k], axis=1)
        v_ext = jnp.concatenate([v_bufs[gi], v], axis=1)
        o, lse = dilated_attention(q, k_ext, v_ext, k_bufs[gi].shape[1], dilation, window // dilation + 1,
                                   slopes[gi * HEADS_PER_GROUP:(gi + 1) * HEADS_PER_GROUP])
        outs.append(o)
        lses.append(lse)
        n_ext = k_ext.shape[1]
        keep = min(window, n_ext)
        new_k.append(k_ext[:, n_ext - keep:])
        new_v.append(v_ext[:, n_ext - keep:])
    grp_w = jax.nn.softmax(jnp.stack(lses, axis=0), axis=0).astype(x.dtype)
    att = jnp.sum(grp_w[..., None] * jnp.stack(outs, axis=0), axis=0).reshape(bsz, seq, ATT_OUT)
    ssm, h_re, h_im = s5_branch(proj[..., SSM_COL:MEMQ_COL], h0_re, h0_im, lw['lam_re'], lw['lam_im'],
                                lw['log_dt'], lw['ssm_b_re'], lw['ssm_b_im'], lw['ssm_c_re'], lw['ssm_c_im'],
                                lw['ssm_d'], lw['w_glu'], lw['b_glu'])
    mq = proj[..., MEMQ_COL:GATE_COL].reshape(bsz, seq, MEM_HEADS, MEM_HEAD_DIM)
    mem = memory_attention(mq, mem_k, mem_v).reshape(bsz, seq, MEM_WIDTH)
    gates = jax.nn.sigmoid(proj[..., GATE_COL:].astype(jnp.float32)).astype(x.dtype).reshape(
        bsz, seq, N_BRANCHES, D_MODEL)
    merged = (gates[:, :, 0] * (att @ lw['w_att_out'])
              + gates[:, :, 1] * (ssm @ lw['w_ssm_out'])
              + gates[:, :, 2] * (mem @ lw['w_mem_out']))
    h = x + merged @ lw['w_o']
    hn = rmsnorm(h, lw['norm2']).reshape(bsz * seq, D_MODEL)
    h = h + hier_moe(hn, lw['w_grp'], lw['b_grp'], lw['w_exp'], lw['b_exp'],
                     lw['w1'], lw['w3'], lw['w2']).reshape(bsz, seq, D_MODEL)
    return h, tuple(new_k), tuple(new_v), h_re, h_im


def setup_inputs(seed: int = 0) -> dict:
    key = jax.random.key(seed)
    ks = iter(jax.random.split(key, 64))
    f32 = jnp.float32

    def nrm(shape, scale):
        return jax.random.normal(next(ks), shape, f32) * scale

    def gain(shape):
        return 1.0 + nrm(shape, 0.02)

    inp = {}
    inp['x_prompt'] = nrm((BATCH, SEQ, D_MODEL), 1.0)
    inp['x_sample'] = nrm((DEC_BATCH, DEC_SEQ, D_MODEL), 1.0)
    for g, (window, _) in enumerate(ATT_GROUPS):
        buf = min(window, PAST_LEN)
        inp[f'cache_swa{g}_k'] = nrm((DEPTH, DEC_BATCH, buf, HEADS_PER_GROUP, HEAD_DIM), 1.0)
        inp[f'cache_swa{g}_v'] = nrm((DEPTH, DEC_BATCH, buf, HEADS_PER_GROUP, HEAD_DIM), 1.0)
    inp['cache_mem_k'] = nrm((DEPTH, DEC_BATCH, MEM_TOKENS, MEM_HEADS, MEM_HEAD_DIM), 1.0)
    inp['cache_mem_v'] = nrm((DEPTH, DEC_BATCH, MEM_TOKENS, MEM_HEADS, MEM_HEAD_DIM), 1.0)
    inp['state_ssm_re'] = nrm((DEPTH, DEC_BATCH, SSM_GROUPS, SSM_STATE), 1.0)
    inp['state_ssm_im'] = nrm((DEPTH, DEC_BATCH, SSM_GROUPS, SSM_STATE), 1.0)
    inp['mem_prompt'] = nrm((BATCH, MEM_TOKENS, D_MODEL), 1.0)
    inp['norm1'] = gain((DEPTH, D_MODEL))
    inp['w_in'] = nrm((DEPTH, D_MODEL, IN_COLS), D_MODEL ** -0.5)
    inp['lam_re'] = -0.5 + nrm((DEPTH, SSM_GROUPS, SSM_STATE), 0.01)
    inp['lam_im'] = jnp.pi * jnp.arange(SSM_STATE, dtype=f32) + nrm((DEPTH, SSM_GROUPS, SSM_STATE), 0.01)
    inp['log_dt'] = jax.random.uniform(next(ks), (DEPTH, SSM_GROUPS), f32, math.log(DT_MIN), math.log(DT_MAX))
    inp['ssm_b_re'] = nrm((DEPTH, SSM_GROUPS, SSM_STATE, SSM_GROUP_CH), (2 * SSM_GROUP_CH) ** -0.5)
    inp['ssm_b_im'] = nrm((DEPTH, SSM_GROUPS, SSM_STATE, SSM_GROUP_CH), (2 * SSM_GROUP_CH) ** -0.5)
    inp['ssm_c_re'] = nrm((DEPTH, SSM_GROUPS, SSM_GROUP_CH, SSM_STATE), SSM_STATE ** -0.5)
    inp['ssm_c_im'] = nrm((DEPTH, SSM_GROUPS, SSM_GROUP_CH, SSM_STATE), SSM_STATE ** -0.5)
    inp['ssm_d'] = nrm((DEPTH, SSM_WIDTH), 0.5)
    inp['w_glu'] = nrm((DEPTH, SSM_WIDTH, SSM_WIDTH), SSM_WIDTH ** -0.5)
    inp['b_glu'] = nrm((DEPTH, SSM_WIDTH), 0.01)
    inp['w_att_out'] = nrm((DEPTH, ATT_OUT, D_MODEL), ATT_OUT ** -0.5)
    inp['w_ssm_out'] = nrm((DEPTH, SSM_WIDTH, D_MODEL), SSM_WIDTH ** -0.5)
    inp['w_mem_out'] = nrm((DEPTH, MEM_WIDTH, D_MODEL), MEM_WIDTH ** -0.5)
    inp['w_o'] = nrm((DEPTH, D_MODEL, D_MODEL), D_MODEL ** -0.5)
    inp['norm_mem'] = gain((DEPTH, D_MODEL))
    inp['w_mk'] = nrm((DEPTH, D_MODEL, MEM_WIDTH), D_MODEL ** -0.5)
    inp['w_mv'] = nrm((DEPTH, D_MODEL, MEM_WIDTH), D_MODEL ** -0.5)
    inp['norm2'] = gain((DEPTH, D_MODEL))
    inp['w_grp'] = nrm((DEPTH, D_MODEL, MOE_GROUPS), D_MODEL ** -0.5)
    inp['b_grp'] = nrm((DEPTH, MOE_GROUPS), 0.01)
    inp['w_exp'] = nrm((DEPTH, D_MODEL, N_EXPERTS), D_MODEL ** -0.5)
    inp['b_exp'] = nrm((DEPTH, N_EXPERTS), 0.01)
    inp['w1'] = nrm((DEPTH, N_EXPERTS, D_MODEL, D_EXPERT), D_MODEL ** -0.5)
    inp['w3'] = nrm((DEPTH, N_EXPERTS, D_MODEL, D_EXPERT), D_MODEL ** -0.5)
    inp['w2'] = nrm((DEPTH, N_EXPERTS, D_EXPERT, D_MODEL), D_EXPERT ** -0.5)
    inp['norm_f'] = gain((D_MODEL,))
    return inp


def reference(x_prompt, x_sample, cache_swa0_k, cache_swa0_v, cache_swa1_k, cache_swa1_v,
              cache_swa2_k, cache_swa2_v, cache_mem_k, cache_mem_v, state_ssm_re, state_ssm_im,
              mem_prompt, norm1, w_in, lam_re, lam_im, log_dt, ssm_b_re, ssm_b_im, ssm_c_re, ssm_c_im,
              ssm_d, w_glu, b_glu, w_att_out, w_ssm_out, w_mem_out, w_o, norm_mem, w_mk, w_mv,
              norm2, w_grp, b_grp, w_exp, b_exp, w1, w3, w2, norm_f):
    cache_k = (cache_swa0_k, cache_swa1_k, cache_swa2_k)
    cache_v = (cache_swa0_v, cache_swa1_v, cache_swa2_v)
    hp = x_prompt
    hs = x_sample
    pk = [[] for _ in ATT_GROUPS]
    pv = [[] for _ in ATT_GROUPS]
    sk = [[] for _ in ATT_GROUPS]
    sv = [[] for _ in ATT_GROUPS]
    p_mk, p_mv, p_re, p_im, s_re, s_im = [], [], [], [], [], []
    for l in range(DEPTH):
        lw = dict(norm1=norm1[l], w_in=w_in[l], lam_re=lam_re[l], lam_im=lam_im[l], log_dt=log_dt[l],
                  ssm_b_re=ssm_b_re[l], ssm_b_im=ssm_b_im[l], ssm_c_re=ssm_c_re[l], ssm_c_im=ssm_c_im[l],
                  ssm_d=ssm_d[l], w_glu=w_glu[l], b_glu=b_glu[l], w_att_out=w_att_out[l],
                  w_ssm_out=w_ssm_out[l], w_mem_out=w_mem_out[l], w_o=w_o[l], norm2=norm2[l],
                  w_grp=w_grp[l], b_grp=b_grp[l], w_exp=w_exp[l], b_exp=b_exp[l],
                  w1=w1[l], w3=w3[l], w2=w2[l])
        mk, mv = memory_kv(mem_prompt, norm_mem[l], w_mk[l], w_mv[l])
        empty = jnp.zeros((hp.shape[0], 0, HEADS_PER_GROUP, HEAD_DIM), hp.dtype)
        h0 = jnp.zeros((hp.shape[0], SSM_GROUPS, SSM_STATE), hp.dtype)
        hp, kp, vp, hr, hi = decoder_layer(hp, mk, mv, (empty, empty, empty), (empty, empty, empty), h0, h0, lw)
        hs, ks_, vs_, sr, si = decoder_layer(hs, cache_mem_k[l], cache_mem_v[l],
                                             (cache_k[0][l], cache_k[1][l], cache_k[2][l]),
                                             (cache_v[0][l], cache_v[1][l], cache_v[2][l]),
                                             state_ssm_re[l], state_ssm_im[l], lw)
        for g in range(N_ATT_GROUPS):
            pk[g].append(kp[g])
            pv[g].append(vp[g])
            sk[g].append(ks_[g])
            sv[g].append(vs_[g])
        p_mk.append(mk)
        p_mv.append(mv)
        p_re.append(hr)
        p_im.append(hi)
        s_re.append(sr)
        s_im.append(si)
    y_prompt = rmsnorm(hp, norm_f)
    y_sample = rmsnorm(hs, norm_f)
    st = jnp.stack
    return (y_prompt, y_sample,
            st(pk[0]), st(pv[0]), st(pk[1]), st(pv[1]), st(pk[2]), st(pv[2]),
            st(p_mk), st(p_mv), st(p_re), st(p_im),
            st(sk[0]), st(sv[0]), st(sk[1]), st(sv[1]), st(sk[2]), st(sv[2]),
            st(s_re), st(s_im))
```

```python
import functools
import math

import jax
import jax.numpy as jnp
from jax import lax
from jax.experimental import pallas as pl
from jax.experimental.pallas import tpu as pltpu

F32 = jnp.float32
BF16 = jnp.bfloat16

D_MODEL = 1024
ATT_GROUPS = ((128, 1), (512, 4), (2048, 16))
HEADS_PER_GROUP = 4
HEAD_DIM = 64
N_ATT_HEADS = len(ATT_GROUPS) * HEADS_PER_GROUP
ATT_OUT = HEADS_PER_GROUP * HEAD_DIM
QKV_GROUP = 3 * ATT_OUT
QKV_COLS = len(ATT_GROUPS) * QKV_GROUP
SSM_WIDTH = D_MODEL // 2
SSM_GROUP_CH = 16
SSM_GROUPS = SSM_WIDTH // SSM_GROUP_CH
SSM_STATE = 64
SSM_NSTATE = SSM_GROUPS * SSM_STATE
MEM_TOKENS = 256
MEM_HEADS = 4
MEM_HEAD_DIM = D_MODEL // 8
MEM_WIDTH = MEM_HEADS * MEM_HEAD_DIM
N_BRANCHES = 3
MOE_GROUPS = 4
EXPERTS_PER_GROUP = 8
N_EXPERTS = MOE_GROUPS * EXPERTS_PER_GROUP
TOP_K = 2
D_EXPERT = D_MODEL // 2
RMS_EPS = 1e-6

LANES = 128
SUBLANES = 8
Q_TILE = 128
ROUTER_COLS = 128
NEG_BIG = -1e30
VMEM_LIMIT = 56 * 1024 * 1024


def _cparams(sem):
    return pltpu.CompilerParams(dimension_semantics=sem, vmem_limit_bytes=VMEM_LIMIT)


def _rms(x, g):
    ms = jnp.mean(x * x, axis=-1, keepdims=True)
    return (x * lax.rsqrt(ms + RMS_EPS)) * g


def _norm_matmul_kernel(x_ref, g_ref, w_ref, *out_refs, splits, sigmoid, chunk):
    xb = _rms(x_ref[...], g_ref[...]).astype(BF16)
    c0 = 0
    for o_ref, n, sg in zip(out_refs, splits, sigmoid):
        for j in range(0, n, chunk):
            w = min(chunk, n - j)
            y = jnp.dot(xb, w_ref[:, c0 + j:c0 + j + w], preferred_element_type=F32)
            if sg:
                y = jax.nn.sigmoid(y)
            o_ref[:, j:j + w] = y
        c0 += n


def norm_matmul(x, g, w_bf16, splits, sigmoid, tm):
    n, d = x.shape
    tm = min(tm, n)
    kern = functools.partial(_norm_matmul_kernel, splits=tuple(splits), sigmoid=tuple(sigmoid), chunk=512)
    return pl.pallas_call(
        kern,
        grid=(n // tm,),
        in_specs=[
            pl.BlockSpec((tm, d), lambda i: (i, 0)),
            pl.BlockSpec((1, d), lambda i: (0, 0)),
            pl.BlockSpec((d, w_bf16.shape[1]), lambda i: (0, 0), pipeline_mode=pl.Buffered(1)),
        ],
        out_specs=[pl.BlockSpec((tm, s), lambda i: (i, 0)) for s in splits],
        out_shape=[jax.ShapeDtypeStruct((n, s), F32) for s in splits],
        compiler_params=_cparams(("arbitrary",)),
        name="norm_matmul",
    )(x, g.reshape(1, d), w_bf16)


def _attn_prompt_kernel(slopes_ref, *refs, seq):
    qkv_refs = refs[:9]
    o_ref = refs[9]
    o_scr, l_scr = refs[10], refs[11]
    hp = pl.program_id(1)
    scale = HEAD_DIM ** -0.5
    row = lax.broadcasted_iota(jnp.int32, (Q_TILE, 2 * Q_TILE), 0)
    col = lax.broadcasted_iota(jnp.int32, (Q_TILE, 2 * Q_TILE), 1)
    dist = (row + Q_TILE - col)
    in_window = jnp.logical_and(dist >= 0, dist <= Q_TILE)
    distf = dist.astype(F32)
    lane = lax.broadcasted_iota(jnp.int32, (Q_TILE, LANES), 1)
    head_a = lane < HEAD_DIM

    for g, (_, dil) in enumerate(ATT_GROUPS):
        q_ref, k_ref, v_ref = qkv_refs[3 * g:3 * g + 3]
        sub_len = seq // dil
        n_blk = sub_len // Q_TILE
        slope_a = slopes_ref[g * HEADS_PER_GROUP + 2 * hp] * float(dil)
        slope_b = slopes_ref[g * HEADS_PER_GROUP + 2 * hp + 1] * float(dil)

        def rows(start):
            if dil == 1:
                return pl.ds(start, Q_TILE)
            return pl.ds(start, Q_TILE, stride=dil)

        def body(it, carry, q_ref=q_ref, k_ref=k_ref, v_ref=v_ref, n_blk=n_blk, dil=dil,
                 slope_a=slope_a, slope_b=slope_b, rows=rows, g=g):
            r = it // n_blk
            blk = it % n_blk
            prev = jnp.maximum(blk - 1, 0)
            q_start = r + dil * Q_TILE * blk
            p_start = r + dil * Q_TILE * prev
            q = q_ref[rows(q_start), :]
            k2 = jnp.concatenate([k_ref[rows(p_start), :], k_ref[rows(q_start), :]], axis=0).astype(BF16)
            v2 = jnp.concatenate([v_ref[rows(p_start), :], v_ref[rows(q_start), :]], axis=0).astype(BF16)
            valid = jnp.logical_and(in_window, jnp.logical_or(col >= Q_TILE, blk > 0))
            outs, lses = [], []
            for is_a, slope in ((True, slope_a), (False, slope_b)):
                hmask = head_a if is_a else jnp.logical_not(head_a)
                qh = jnp.where(hmask, q, 0.0).astype(BF16)
                s = lax.dot_general(qh, k2, (((1,), (1,)), ((), ())), preferred_element_type=F32)
                s = s * scale - slope * distf
                s = jnp.where(valid, s, NEG_BIG)
                m = jnp.max(s, axis=-1, keepdims=True)
                p = jnp.exp(s - m)
                den = jnp.sum(p, axis=-1, keepdims=True)
                pv = jnp.dot(p.astype(BF16), v2, preferred_element_type=F32)
                outs.append(pv / den)
                lses.append(m + jnp.log(den))
            o_scr[g, rows(q_start), :] = jnp.where(head_a, outs[0], outs[1])
            l_scr[g, rows(q_start), :] = jnp.where(head_a, lses[0], lses[1])
            return carry

        lax.fori_loop(0, dil * n_blk, body, 0)

    def combine(c, carry):
        sl = pl.ds(pl.multiple_of(c * 256, 256), 256)
        l0, l1, l2 = l_scr[0, sl, :], l_scr[1, sl, :], l_scr[2, sl, :]
        m = jnp.maximum(jnp.maximum(l0, l1), l2)
        e0, e1, e2 = jnp.exp(l0 - m), jnp.exp(l1 - m), jnp.exp(l2 - m)
        tot = e0 + e1 + e2
        o_ref[sl, :] = (e0 * o_scr[0, sl, :] + e1 * o_scr[1, sl, :] + e2 * o_scr[2, sl, :]) / tot
        return carry

    lax.fori_loop(0, seq // 256, combine, 0)


def attn_prompt(qkv, slopes):
    bsz, seq, _ = qkv.shape
    in_specs = []
    for g in range(len(ATT_GROUPS)):
        for part in range(3):
            base = (g * QKV_GROUP + part * ATT_OUT) // LANES
            in_specs.append(pl.BlockSpec((None, seq, LANES),
                                         lambda b, h, sl, base=base: (b, 0, base + h)))
    grid_spec = pltpu.PrefetchScalarGridSpec(
        num_scalar_prefetch=1,
        grid=(bsz, ATT_OUT // LANES),
        in_specs=in_specs,
        out_specs=pl.BlockSpec((None, seq, LANES), lambda b, h, sl: (b, 0, h)),
        scratch_shapes=[pltpu.VMEM((3, seq, LANES), F32), pltpu.VMEM((3, seq, LANES), F32)],
    )
    return pl.pallas_call(
        functools.partial(_attn_prompt_kernel, seq=seq),
        grid_spec=grid_spec,
        out_shape=jax.ShapeDtypeStruct((bsz, seq, ATT_OUT), F32),
        compiler_params=_cparams(("arbitrary", "arbitrary")),
        name="attn_prompt",
    )(slopes, *([qkv] * 9))


def _attn_sample_kernel(qkv_ref, *refs):
    cache_refs = refs[:6]
    o_ref = refs[6]
    scale = HEAD_DIM ** -0.5
    hidx = lax.broadcasted_iota(jnp.int32, (1, HEADS_PER_GROUP, 1), 1).astype(F32)
    jidx = lax.broadcasted_iota(jnp.int32, (Q_TILE, 1, 1), 0).astype(F32)
    outs, lses = [], []
    for g, (_, dil) in enumerate(ATT_GROUPS):
        kc = cache_refs[2 * g][...]
        vc = cache_refs[2 * g + 1][...]
        q = qkv_ref[g, 0][None]
        kn = qkv_ref[g, 1][None]
        vn = qkv_ref[g, 2][None]
        slope = jnp.exp2(-8.0 * (hidx + float(g * HEADS_PER_GROUP + 1)) / N_ATT_HEADS)
        dist = float(dil) * (float(Q_TILE) - jidx)
        s = jnp.sum(kc * q, axis=-1, keepdims=True) * scale - slope * dist
        sn = jnp.sum(kn * q, axis=-1, keepdims=True) * scale
        m = jnp.maximum(jnp.max(s, axis=0, keepdims=True), sn)
        p = jnp.exp(s - m)
        pn = jnp.exp(sn - m)
        den = jnp.sum(p, axis=0, keepdims=True) + pn
        o = (jnp.sum(p * vc, axis=0, keepdims=True) + pn * vn) / den
        outs.append(o)
        lses.append(m + jnp.log(den))
    m = jnp.maximum(jnp.maximum(lses[0], lses[1]), lses[2])
    es = [jnp.exp(l - m) for l in lses]
    tot = es[0] + es[1] + es[2]
    o_ref[...] = ((es[0] * outs[0] + es[1] * outs[1] + es[2] * outs[2]) / tot)[0]


def attn_sample(qkv5, caches):
    bsz = qkv5.shape[0]
    in_specs = [pl.BlockSpec((None, 3, 3, HEADS_PER_GROUP, HEAD_DIM), lambda b: (b, 0, 0, 0, 0))]
    args = []
    for g, (window, dil) in enumerate(ATT_GROUPS):
        for part in range(2):
            c = caches[2 * g + part]
            args.append(c.reshape(bsz, window // dil, dil, HEADS_PER_GROUP, HEAD_DIM))
            in_specs.append(pl.BlockSpec((None, window // dil, None, HEADS_PER_GROUP, HEAD_DIM),
                                         lambda b: (b, 0, 0, 0, 0)))
    return pl.pallas_call(
        _attn_sample_kernel,
        grid=(bsz,),
        in_specs=in_specs,
        out_specs=pl.BlockSpec((None, HEADS_PER_GROUP, HEAD_DIM), lambda b: (b, 0, 0)),
        out_shape=jax.ShapeDtypeStruct((bsz, HEADS_PER_GROUP, HEAD_DIM), F32),
        compiler_params=_cparams(("arbitrary",)),
        name="attn_sample",
    )(qkv5, *args)


SSM_KBLK = SSM_WIDTH // LANES
SSM_KSTATE = SSM_NSTATE // SSM_KBLK


def _ssm_tail(y, u, dsk, wglu, bglu):
    y = y + dsk * u
    z = jax.nn.gelu(y)
    gl = jnp.dot(z.astype(BF16), wglu, preferred_element_type=F32) + bglu
    return z * jax.nn.sigmoid(gl)


def _ssm_prompt_kernel(u_ref, wb_ref, wc_ref, are_ref, aim_ref, dsk_ref, wglu_ref, bglu_ref,
                       o_ref, st_ref, uperm, bu, sbf, yperm, ynat, *, bsz, tt):
    rows = bsz * tt

    @pl.when(pl.program_id(0) == 0)
    def _():
        st_ref[...] = jnp.zeros_like(st_ref)

    for b in range(bsz):
        ub = u_ref[b]
        for k in range(SSM_KBLK):
            uperm[k, pl.ds(b, tt, stride=bsz), :] = ub[:, k * LANES:(k + 1) * LANES]

    for k in range(SSM_KBLK):
        bu[k] = jnp.dot(uperm[k].astype(BF16), wb_ref[k], preferred_element_type=F32)

    for k in range(SSM_KBLK):
        ar = jnp.broadcast_to(are_ref[k], (bsz, SSM_KSTATE))
        ai = jnp.broadcast_to(aim_ref[k], (bsz, SSM_KSTATE))
        st = st_ref[k]

        def step(t, carry, k=k, ar=ar, ai=ai):
            re, im = carry
            r0 = pl.multiple_of(t * bsz, bsz)
            b_t = bu[k, pl.ds(r0, bsz), :]
            nre = ar * re - ai * im + b_t[:, :SSM_KSTATE]
            nim = ar * im + ai * re + b_t[:, SSM_KSTATE:]
            sbf[k, pl.ds(r0, bsz), :] = jnp.concatenate([nre, nim], axis=1).astype(BF16)
            return nre, nim

        re, im = lax.fori_loop(0, tt, step, (st[:, :SSM_KSTATE], st[:, SSM_KSTATE:]), unroll=2)
        st_ref[k] = jnp.concatenate([re, im], axis=1)

    for k in range(SSM_KBLK):
        yperm[k] = jnp.dot(sbf[k], wc_ref[k], preferred_element_type=F32)

    for b in range(bsz):
        for k in range(SSM_KBLK):
            ynat[b * tt:(b + 1) * tt, k * LANES:(k + 1) * LANES] = yperm[k, pl.ds(b, tt, stride=bsz), :]

    u2 = u_ref[...].reshape(rows, SSM_WIDTH)
    out = _ssm_tail(ynat[...], u2, dsk_ref[...], wglu_ref[...], bglu_ref[...])
    o_ref[...] = out.reshape(bsz, tt, SSM_WIDTH)


def ssm_prompt(u, prm, tt=32):
    bsz, seq, _ = u.shape
    rows = bsz * tt
    const = lambda shape: pl.BlockSpec(shape, lambda i: (0,) * len(shape))
    return pl.pallas_call(
        functools.partial(_ssm_prompt_kernel, bsz=bsz, tt=tt),
        grid=(seq // tt,),
        in_specs=[
            pl.BlockSpec((bsz, tt, SSM_WIDTH), lambda i: (0, i, 0)),
            const((SSM_KBLK, LANES, 2 * SSM_KSTATE)),
            const((SSM_KBLK, 2 * SSM_KSTATE, LANES)),
            const((SSM_KBLK, 1, SSM_KSTATE)),
            const((SSM_KBLK, 1, SSM_KSTATE)),
            const((1, SSM_WIDTH)),
            const((SSM_WIDTH, SSM_WIDTH)),
            const((1, SSM_WIDTH)),
        ],
        out_specs=[
            pl.BlockSpec((bsz, tt, SSM_WIDTH), lambda i: (0, i, 0)),
            const((SSM_KBLK, bsz, 2 * SSM_KSTATE)),
        ],
        out_shape=[
            jax.ShapeDtypeStruct((bsz, seq, SSM_WIDTH), F32),
            jax.ShapeDtypeStruct((SSM_KBLK, bsz, 2 * SSM_KSTATE), F32),
        ],
        scratch_shapes=[
            pltpu.VMEM((SSM_KBLK, rows, LANES), F32),
            pltpu.VMEM((SSM_KBLK, rows, 2 * SSM_KSTATE), F32),
            pltpu.VMEM((SSM_KBLK, rows, 2 * SSM_KSTATE), BF16),
            pltpu.VMEM((SSM_KBLK, rows, LANES), F32),
            pltpu.VMEM((rows, SSM_WIDTH), F32),
        ],
        compiler_params=_cparams(("arbitrary",)),
        name="ssm_prompt",
    )(u, prm["wb"], prm["wc"], prm["a_re"], prm["a_im"], prm["dsk"], prm["wglu"], prm["bglu"])


def _ssm_sample_kernel(u_ref, h0_ref, wb_ref, wc_ref, are_ref, aim_ref, dsk_ref, wglu_ref, bglu_ref,
                       o_ref, st_ref):
    u = u_ref[...]
    ub = u.astype(BF16)
    ys = []
    for k in range(SSM_KBLK):
        b_t = jnp.dot(ub[:, k * LANES:(k + 1) * LANES], wb_ref[k], preferred_element_type=F32)
        h0 = h0_ref[k]
        re, im = h0[:, :SSM_KSTATE], h0[:, SSM_KSTATE:]
        ar, ai = are_ref[k], aim_ref[k]
        nre = ar * re - ai * im + b_t[:, :SSM_KSTATE]
        nim = ar * im + ai * re + b_t[:, SSM_KSTATE:]
        s = jnp.concatenate([nre, nim], axis=1)
        st_ref[k] = s
        ys.append(jnp.dot(s.astype(BF16), wc_ref[k], preferred_element_type=F32))
    y = jnp.concatenate(ys, axis=1)
    o_ref[...] = _ssm_tail(y, u, dsk_ref[...], wglu_ref[...], bglu_ref[...])


def ssm_sample(u, h0, prm):
    bsz = u.shape[0]
    return pl.pallas_call(
        _ssm_sample_kernel,
        out_shape=[jax.ShapeDtypeStruct((bsz, SSM_WIDTH), F32),
                   jax.ShapeDtypeStruct((SSM_KBLK, bsz, 2 * SSM_KSTATE), F32)],
        compiler_params=pltpu.CompilerParams(vmem_limit_bytes=VMEM_LIMIT),
        name="ssm_sample",
    )(u, h0, prm["wb"], prm["wc"], prm["a_re"], prm["a_im"], prm["dsk"], prm["wglu"], prm["bglu"])


def ssm_params(lam_re, lam_im, log_dt, b_re, b_im, c_re, c_im, d_skip, w_glu, b_glu):
    dt = jnp.exp(log_dt)[:, None]
    mag = jnp.exp(lam_re * dt)
    ab_re = mag * jnp.cos(lam_im * dt)
    ab_im = mag * jnp.sin(lam_im * dt)
    den = lam_re * lam_re + lam_im * lam_im
    nr = ab_re - 1.0
    ni = ab_im
    z_re = ((nr * lam_re + ni * lam_im) / den)[..., None]
    z_im = ((ni * lam_re - nr * lam_im) / den)[..., None]
    bb_re = z_re * b_re - z_im * b_im
    bb_im = z_re * b_im + z_im * b_re
    gpb = SSM_GROUPS // SSM_KBLK
    eye = jnp.eye(gpb, dtype=F32)

    def in_mat(bb):
        bbk = bb.reshape(SSM_KBLK, gpb, SSM_STATE, SSM_GROUP_CH)
        m = jnp.einsum("kgpc,gh->kgchp", bbk, eye)
        return m.reshape(SSM_KBLK, LANES, SSM_KSTATE)

    def out_mat(c):
        ck = c.reshape(SSM_KBLK, gpb, SSM_GROUP_CH, SSM_STATE)
        m = jnp.einsum("kgcp,gh->kgphc", ck, eye)
        return m.reshape(SSM_KBLK, SSM_KSTATE, LANES)

    wb = jnp.concatenate([in_mat(bb_re), in_mat(bb_im)], axis=2).astype(BF16)
    wc = jnp.concatenate([out_mat(c_re), -out_mat(c_im)], axis=1).astype(BF16)
    return dict(
        wb=wb, wc=wc,
        a_re=ab_re.reshape(SSM_KBLK, 1, SSM_KSTATE), a_im=ab_im.reshape(SSM_KBLK, 1, SSM_KSTATE),
        dsk=d_skip.reshape(1, SSM_WIDTH), wglu=w_glu.astype(BF16), bglu=b_glu.reshape(1, SSM_WIDTH))


def _state_to_blocks(h):
    bsz = h.shape[0]
    return h.reshape(bsz, SSM_KBLK, SSM_KSTATE).transpose(1, 0, 2)


def _blocks_to_state(s):
    bsz = s.shape[1]
    return s.transpose(1, 0, 2).reshape(bsz, SSM_GROUPS, SSM_STATE)


def _mem_attn_kernel(q_ref, k_ref, v_ref, o_ref):
    scale = MEM_HEAD_DIM ** -0.5
    for h in range(MEM_HEADS):
        sl = slice(h * MEM_HEAD_DIM, (h + 1) * MEM_HEAD_DIM)
        q = q_ref[:, sl].astype(BF16)
        k = k_ref[:, sl].astype(BF16)
        v = v_ref[:, sl].astype(BF16)
        s = lax.dot_general(q, k, (((1,), (1,)), ((), ())), preferred_element_type=F32) * scale
        m = jnp.max(s, axis=-1, keepdims=True)
        p = jnp.exp(s - m)
        den = jnp.sum(p, axis=-1, keepdims=True)
        o_ref[:, sl] = jnp.dot((p / den).astype(BF16), v, preferred_element_type=F32)


def mem_attention(q, mk, mv, tl):
    bsz, seq, _ = q.shape
    tl = min(tl, seq)
    return pl.pallas_call(
        _mem_attn_kernel,
        grid=(bsz, seq // tl),
        in_specs=[
            pl.BlockSpec((None, tl, MEM_WIDTH), lambda b, i: (b, i, 0)),
            pl.BlockSpec((None, MEM_TOKENS, MEM_WIDTH), lambda b, i: (b, 0, 0)),
            pl.BlockSpec((None, MEM_TOKENS, MEM_WIDTH), lambda b, i: (b, 0, 0)),
        ],
        out_specs=pl.BlockSpec((None, tl, MEM_WIDTH), lambda b, i: (b, i, 0)),
        out_shape=jax.ShapeDtypeStruct((bsz, seq, MEM_WIDTH), F32),
        compiler_params=_cparams(("arbitrary", "arbitrary")),
        name="mem_attention",
    )(q, mk, mv)


def _merge_kernel(att_ref, ssm_ref, mem_ref, gate_ref, x_ref, wa_ref, ws_ref, wm_ref, wo_ref,
                  n2_ref, wr_ref, br_ref, h_ref, hn_ref, lg_ref, *, tm):
    a = jnp.dot(att_ref[...].astype(BF16), wa_ref[...], preferred_element_type=F32)
    merged = gate_ref[:, 0:D_MODEL] * a
    s = jnp.dot(ssm_ref[...].astype(BF16), ws_ref[...], preferred_element_type=F32)
    merged = merged + gate_ref[:, D_MODEL:2 * D_MODEL] * s
    m = jnp.dot(mem_ref[...].astype(BF16), wm_ref[...], preferred_element_type=F32)
    merged = merged + gate_ref[:, 2 * D_MODEL:3 * D_MODEL] * m
    h = x_ref[...] + jnp.dot(merged.astype(BF16), wo_ref[...], preferred_element_type=F32)
    h_ref[...] = h
    hn = _rms(h, n2_ref[...])
    for j in range(D_MODEL // LANES):
        hn_ref[pl.ds(j, tm, stride=SUBLANES), :] = hn[:, j * LANES:(j + 1) * LANES]
    lg_ref[...] = jnp.dot(hn, wr_ref[...], preferred_element_type=F32,
                          precision=lax.Precision.HIGHEST) + br_ref[...]


def merge(att, ssm, mem, gates, x, wts, tm):
    n = x.shape[0]
    tm = min(tm, n)
    row = lambda w: pl.BlockSpec((tm, w), lambda i: (i, 0))
    const = lambda a: pl.BlockSpec(a.shape, lambda i: (0, 0))
    ws = [wts["wa"], wts["ws"], wts["wm"], wts["wo"], wts["n2"], wts["wr"], wts["br"]]
    return pl.pallas_call(
        functools.partial(_merge_kernel, tm=tm),
        grid=(n // tm,),
        in_specs=[row(ATT_OUT), row(SSM_WIDTH), row(MEM_WIDTH), row(N_BRANCHES * D_MODEL), row(D_MODEL)]
                 + [const(w) for w in ws],
        out_specs=[row(D_MODEL), pl.BlockSpec((tm * SUBLANES, LANES), lambda i: (i, 0)), row(ROUTER_COLS)],
        out_shape=[jax.ShapeDtypeStruct((n, D_MODEL), F32),
                   jax.ShapeDtypeStruct((n * SUBLANES, LANES), F32),
                   jax.ShapeDtypeStruct((n, ROUTER_COLS), F32)],
        compiler_params=_cparams(("arbitrary",)),
        name="merge",
    )(att, ssm, mem, gates, x, *ws)


def _moe_kernel(be_ref, idx_hbm, x_hbm, w1_ref, w3_ref, w2_ref, out_hbm,
                idx_smem, xbuf, xb, ybuf, w1b, w3b, w2b, gsem, ssem, isem, *, bm):
    blk = pl.program_id(0)
    nblk = pl.num_programs(0)
    slot = blk % 2
    tiles = D_MODEL // LANES

    def idx_copy(b, s):
        return pltpu.make_async_copy(idx_hbm.at[b], idx_smem.at[s], isem.at[s])

    def issue_gather(s):
        def body(i, c):
            tok = idx_smem[s, lax.shift_right_logical(i, 7), jnp.bitwise_and(i, LANES - 1)]
            r0 = pl.multiple_of(i * SUBLANES, SUBLANES)
            pltpu.make_async_copy(x_hbm.at[tok], xbuf.at[pl.ds(r0, SUBLANES)], gsem).start()
            return c
        lax.fori_loop(0, bm, body, 0, unroll=8)

    def issue_scatter(s):
        def body(i, c):
            j = i + bm
            pid = idx_smem[s, lax.shift_right_logical(j, 7), jnp.bitwise_and(j, LANES - 1)]
            r0 = pl.multiple_of(i * SUBLANES, SUBLANES)
            pltpu.make_async_copy(ybuf.at[pl.ds(r0, SUBLANES)], out_hbm.at[pid], ssem).start()
            return c
        lax.fori_loop(0, bm, body, 0, unroll=8)

    def wait_rows(sem, buf):
        pltpu.make_async_copy(buf, buf, sem).wait()

    @pl.when(blk == 0)
    def _():
        c = idx_copy(0, 0)
        c.start()
        c.wait()
        issue_gather(0)

    @pl.when(blk + 1 < nblk)
    def _():
        idx_copy(blk + 1, 1 - slot).start()

    prev = jnp.maximum(blk - 1, 0)
    changed = jnp.logical_or(blk == 0, be_ref[blk] != be_ref[prev])

    @pl.when(changed)
    def _():
        w1b[...] = w1_ref[...].astype(BF16)
        w3b[...] = w3_ref[...].astype(BF16)
        w2b[...] = w2_ref[...].astype(BF16)

    wait_rows(gsem, xbuf)
    for j in range(tiles):
        xb[:, j * LANES:(j + 1) * LANES] = xbuf[pl.ds(j, bm, stride=SUBLANES), :].astype(BF16)

    @pl.when(blk + 1 < nblk)
    def _():
        idx_copy(blk + 1, 1 - slot).wait()
        issue_gather(1 - slot)

    x = xb[...]
    h1 = jnp.dot(x, w1b[...], preferred_element_type=F32)
    h3 = jnp.dot(x, w3b[...], preferred_element_type=F32)
    act = (jax.nn.silu(h1) * h3).astype(BF16)
    y = jnp.dot(act, w2b[...], preferred_element_type=F32)

    @pl.when(blk > 0)
    def _():
        wait_rows(ssem, ybuf)

    for j in range(tiles):
        ybuf[pl.ds(j, bm, stride=SUBLANES), :] = y[:, j * LANES:(j + 1) * LANES]
    issue_scatter(slot)

    @pl.when(blk == nblk - 1)
    def _():
        wait_rows(ssem, ybuf)


def moe_experts(block_expert, idx, hn3, w1, w3, w2, n_out_rows, bm):
    n_blocks = idx.shape[0]
    grid_spec = pltpu.PrefetchScalarGridSpec(
        num_scalar_prefetch=1,
        grid=(n_blocks,),
        in_specs=[
            pl.BlockSpec(memory_space=pl.ANY),
            pl.BlockSpec(memory_space=pl.ANY),
            pl.BlockSpec((None, D_MODEL, D_EXPERT), lambda b, be: (be[b], 0, 0)),
            pl.BlockSpec((None, D_MODEL, D_EXPERT), lambda b, be: (be[b], 0, 0)),
            pl.BlockSpec((None, D_EXPERT, D_MODEL), lambda b, be: (be[b], 0, 0)),
        ],
        out_specs=pl.BlockSpec(memory_space=pl.ANY),
        scratch_shapes=[
            pltpu.SMEM((2, SUBLANES, LANES), jnp.int32),
            pltpu.VMEM((bm * SUBLANES, LANES), F32),
            pltpu.VMEM((bm, D_MODEL), BF16),
            pltpu.VMEM((bm * SUBLANES, LANES), F32),
            pltpu.VMEM((D_MODEL, D_EXPERT), BF16),
            pltpu.VMEM((D_MODEL, D_EXPERT), BF16),
            pltpu.VMEM((D_EXPERT, D_MODEL), BF16),
            pltpu.SemaphoreType.DMA,
            pltpu.SemaphoreType.DMA,
            pltpu.SemaphoreType.DMA((2,)),
        ],
    )
    return pl.pallas_call(
        functools.partial(_moe_kernel, bm=bm),
        grid_spec=grid_spec,
        out_shape=jax.ShapeDtypeStruct((n_out_rows, SUBLANES, LANES), F32),
        compiler_params=_cparams(("arbitrary",)),
        name="moe_experts",
    )(block_expert, idx, hn3, w1, w3, w2)


def route_tables(logits, bm):
    n_tok = logits.shape[0]
    grp_logits = logits[:, :MOE_GROUPS]
    grp_prob = jax.nn.softmax(grp_logits, axis=-1)
    grp = jnp.argmax(grp_logits, axis=-1)
    p_grp = jnp.take_along_axis(grp_prob, grp[:, None], axis=-1)
    exp_logits = logits[:, MOE_GROUPS:MOE_GROUPS + N_EXPERTS].reshape(n_tok, MOE_GROUPS, EXPERTS_PER_GROUP)
    in_grp = jnp.take_along_axis(exp_logits, grp[:, None, None], axis=1)[:, 0]
    top_val, top_idx = lax.top_k(in_grp, TOP_K)
    gate = jax.nn.softmax(top_val, axis=-1) * p_grp
    expert = (grp[:, None] * EXPERTS_PER_GROUP + top_idx).astype(jnp.int32)
    n_pairs = n_tok * TOP_K
    e_flat = expert.reshape(n_pairs)
    order = jnp.argsort(e_flat).astype(jnp.int32)
    e_s = e_flat[order]
    tok_s = order // TOP_K
    dst_s = (order % TOP_K) * n_tok + tok_s
    counts = jnp.bincount(e_flat, length=N_EXPERTS).astype(jnp.int32)
    starts = jnp.cumsum(counts) - counts
    padded = (counts + bm - 1) // bm * bm
    pad_ends = jnp.cumsum(padded)
    pad_starts = pad_ends - padded
    pos = pad_starts[e_s] + jnp.arange(n_pairs, dtype=jnp.int32) - starts[e_s]
    n_blocks = -(-n_pairs // bm) + N_EXPERTS
    rows = n_blocks * bm
    trash = n_pairs + (jnp.arange(rows, dtype=jnp.int32) % bm)
    src = jnp.zeros((rows,), jnp.int32).at[pos].set(tok_s)
    dst = trash.at[pos].set(dst_s)
    width = SUBLANES * LANES
    idx = jnp.zeros((n_blocks, width), jnp.int32)
    idx = idx.at[:, :bm].set(src.reshape(n_blocks, bm)).at[:, bm:2 * bm].set(dst.reshape(n_blocks, bm))
    block_expert = jnp.minimum(
        jnp.searchsorted(pad_ends, jnp.arange(n_blocks, dtype=jnp.int32) * bm, side="right"),
        N_EXPERTS - 1).astype(jnp.int32)
    gate_pad = jnp.zeros((n_tok, LANES), F32).at[:, :TOP_K].set(gate)
    return block_expert, idx.reshape(n_blocks, SUBLANES, LANES), gate_pad, n_pairs + bm


def _final_kernel(h_ref, p0_ref, p1_ref, gw_ref, nf_ref, y_ref, *, tm):
    tiles = D_MODEL // LANES
    p0 = jnp.concatenate([p0_ref[pl.ds(j, tm, stride=SUBLANES), :] for j in range(tiles)], axis=1)
    p1 = jnp.concatenate([p1_ref[pl.ds(j, tm, stride=SUBLANES), :] for j in range(tiles)], axis=1)
    gw = gw_ref[...]
    h = h_ref[...] + (gw[:, 0:1] * p0 + gw[:, 1:2] * p1)
    y_ref[...] = _rms(h, nf_ref[...])


def final(h, pairs2d, gate_pad, norm_f, tm):
    n = h.shape[0]
    tm = min(tm, n)
    nb = n // tm
    row = lambda w: pl.BlockSpec((tm, w), lambda i: (i, 0))
    return pl.pallas_call(
        functools.partial(_final_kernel, tm=tm),
        grid=(nb,),
        in_specs=[
            row(D_MODEL),
            pl.BlockSpec((tm * SUBLANES, LANES), lambda i: (i, 0)),
            pl.BlockSpec((tm * SUBLANES, LANES), lambda i: (nb + i, 0)),
            row(LANES),
            pl.BlockSpec((1, D_MODEL), lambda i: (0, 0)),
        ],
        out_specs=row(D_MODEL),
        out_shape=jax.ShapeDtypeStruct((n, D_MODEL), F32),
        compiler_params=_cparams(("arbitrary",)),
        name="final",
    )(h, pairs2d, pairs2d, gate_pad, norm_f.reshape(1, D_MODEL))


def _alibi_slopes():
    h = jnp.arange(1, N_ATT_HEADS + 1, dtype=F32)
    return jnp.exp2(-8.0 * h / N_ATT_HEADS)


def _token_stage(att, ssm, mem, gates, x2, wts, w1, w3, w2, norm_f, tm, bm):
    n = x2.shape[0]
    h, hn2d, logits = merge(att, ssm, mem, gates, x2, wts, tm)
    block_expert, idx, gate_pad, out_rows = route_tables(logits, bm)
    pairs = moe_experts(block_expert, idx, hn2d.reshape(n, SUBLANES, LANES), w1, w3, w2, out_rows, bm)
    return final(h, pairs.reshape(out_rows * SUBLANES, LANES), gate_pad, norm_f, tm)


def kernel(x_prompt, x_sample, cache_swa0_k, cache_swa0_v, cache_swa1_k, cache_swa1_v, cache_swa2_k, cache_swa2_v, cache_mem_k, cache_mem_v, state_ssm_re, state_ssm_im, mem_prompt, norm1, w_in, lam_re, lam_im, log_dt, ssm_b_re, ssm_b_im, ssm_c_re, ssm_c_im, ssm_d, w_glu, b_glu, w_att_out, w_ssm_out, w_mem_out, w_o, norm_mem, w_mk, w_mv, norm2, w_grp, b_grp, w_exp, b_exp, w1, w3, w2, norm_f):
    assert norm1.shape[0] == 1, "single-layer trunk"
    bsz, seq, _ = x_prompt.shape
    sbz = x_sample.shape[0]
    n_p = bsz * seq
    caches = (cache_swa0_k[0], cache_swa0_v[0], cache_swa1_k[0], cache_swa1_v[0], cache_swa2_k[0], cache_swa2_v[0])

    w_in_b = w_in[0].astype(BF16)
    w_mkv_b = jnp.concatenate([w_mk[0], w_mv[0]], axis=1).astype(BF16)
    wr = jnp.zeros((D_MODEL, ROUTER_COLS), F32)
    wr = wr.at[:, :MOE_GROUPS].set(w_grp[0]).at[:, MOE_GROUPS:MOE_GROUPS + N_EXPERTS].set(w_exp[0])
    br = jnp.zeros((1, ROUTER_COLS), F32)
    br = br.at[0, :MOE_GROUPS].set(b_grp[0]).at[0, MOE_GROUPS:MOE_GROUPS + N_EXPERTS].set(b_exp[0])
    wts = dict(wa=w_att_out[0].astype(BF16), ws=w_ssm_out[0].astype(BF16), wm=w_mem_out[0].astype(BF16),
               wo=w_o[0].astype(BF16), n2=norm2[0].reshape(1, D_MODEL), wr=wr, br=br)
    prm = ssm_params(lam_re[0], lam_im[0], log_dt[0], ssm_b_re[0], ssm_b_im[0], ssm_c_re[0], ssm_c_im[0],
                     ssm_d[0], w_glu[0], b_glu[0])
    slopes = _alibi_slopes()
    splits = (QKV_COLS, SSM_WIDTH, MEM_WIDTH, N_BRANCHES * D_MODEL)
    sig = (False, False, False, True)

    x2 = x_prompt.reshape(n_p, D_MODEL)
    qkv, u, mq, gates = norm_matmul(x2, norm1[0], w_in_b, splits, sig, tm=256)
    qkv3 = qkv.reshape(bsz, seq, QKV_COLS)
    att = attn_prompt(qkv3, slopes)
    ssm, st = ssm_prompt(u.reshape(bsz, seq, SSM_WIDTH), prm)
    mk, mv = norm_matmul(mem_prompt.reshape(bsz * MEM_TOKENS, D_MODEL), norm_mem[0], w_mkv_b,
                         (MEM_WIDTH, MEM_WIDTH), (False, False), tm=256)
    mk3 = mk.reshape(bsz, MEM_TOKENS, MEM_WIDTH)
    mv3 = mv.reshape(bsz, MEM_TOKENS, MEM_WIDTH)
    mem = mem_attention(mq.reshape(bsz, seq, MEM_WIDTH), mk3, mv3, tl=512)
    y_p = _token_stage(att.reshape(n_p, ATT_OUT), ssm.reshape(n_p, SSM_WIDTH), mem.reshape(n_p, MEM_WIDTH),
                       gates, x2, wts, w1[0], w3[0], w2[0], norm_f, tm=256, bm=256)

    xs2 = x_sample.reshape(sbz, D_MODEL)
    qkv_s, u_s, mq_s, gates_s = norm_matmul(xs2, norm1[0], w_in_b, splits, sig, tm=sbz)
    qkv5 = qkv_s.reshape(sbz, len(ATT_GROUPS), 3, HEADS_PER_GROUP, HEAD_DIM)
    att_s = attn_sample(qkv5, caches).reshape(sbz, ATT_OUT)
    h0 = jnp.concatenate([_state_to_blocks(state_ssm_re[0]), _state_to_blocks(state_ssm_im[0])], axis=2)
    ssm_s, st_s = ssm_sample(u_s, h0, prm)
    cmk = cache_mem_k[0].reshape(sbz, MEM_TOKENS, MEM_WIDTH)
    cmv = cache_mem_v[0].reshape(sbz, MEM_TOKENS, MEM_WIDTH)
    mem_s = mem_attention(mq_s.reshape(sbz, 1, MEM_WIDTH), cmk, cmv, tl=1).reshape(sbz, MEM_WIDTH)
    y_s = _token_stage(att_s, ssm_s, mem_s, gates_s, xs2, wts, w1[0], w3[0], w2[0], norm_f, tm=sbz, bm=16)

    outs = [y_p.reshape(bsz, seq, D_MODEL), y_s.reshape(sbz, 1, D_MODEL)]
    for g, (window, _) in enumerate(ATT_GROUPS):
        keep = min(window, seq)
        for part in (1, 2):
            c0 = g * QKV_GROUP + part * ATT_OUT
            outs.append(qkv3[:, seq - keep:, c0:c0 + ATT_OUT].reshape(1, bsz, keep, HEADS_PER_GROUP, HEAD_DIM))
    outs.append(mk3.reshape(1, bsz, MEM_TOKENS, MEM_HEADS, MEM_HEAD_DIM))
    outs.append(mv3.reshape(1, bsz, MEM_TOKENS, MEM_HEADS, MEM_HEAD_DIM))
    outs.append(_blocks_to_state(st[:, :, :SSM_KSTATE])[None])
    outs.append(_blocks_to_state(st[:, :, SSM_KSTATE:])[None])
    for g in range(len(ATT_GROUPS)):
        for part in (1, 2):
            new_row = qkv5[:, g, part][:, None]
            outs.append(jnp.concatenate([caches[2 * g + part - 1][:, 1:], new_row], axis=1)[None])
    outs.append(_blocks_to_state(st_s[:, :, :SSM_KSTATE])[None])
    outs.append(_blocks_to_state(st_s[:, :, SSM_KSTATE:])[None])
    return tuple(outs)
```

```python
import functools
import math

import jax
import jax.numpy as jnp
from jax import lax
from jax.experimental import pallas as pl
from jax.experimental.pallas import tpu as pltpu

F32 = jnp.float32
BF16 = jnp.bfloat16

D_MODEL = 1024
ATT_GROUPS = ((128, 1), (512, 4), (2048, 16))
HEADS_PER_GROUP = 4
HEAD_DIM = 64
N_ATT_HEADS = len(ATT_GROUPS) * HEADS_PER_GROUP
ATT_OUT = HEADS_PER_GROUP * HEAD_DIM
QKV_GROUP = 3 * ATT_OUT
QKV_COLS = len(ATT_GROUPS) * QKV_GROUP
SSM_WIDTH = D_MODEL // 2
SSM_GROUP_CH = 16
SSM_GROUPS = SSM_WIDTH // SSM_GROUP_CH
SSM_STATE = 64
SSM_NSTATE = SSM_GROUPS * SSM_STATE
MEM_TOKENS = 256
MEM_HEADS = 4
MEM_HEAD_DIM = D_MODEL // 8
MEM_WIDTH = MEM_HEADS * MEM_HEAD_DIM
N_BRANCHES = 3
MOE_GROUPS = 4
EXPERTS_PER_GROUP = 8
N_EXPERTS = MOE_GROUPS * EXPERTS_PER_GROUP
TOP_K = 2
D_EXPERT = D_MODEL // 2
RMS_EPS = 1e-6

LANES = 128
SUBLANES = 8
Q_TILE = 128
ROUTER_COLS = 128
NEG_BIG = -1e30
VMEM_LIMIT = 56 * 1024 * 1024


def _cparams(sem):
    return pltpu.CompilerParams(dimension_semantics=sem, vmem_limit_bytes=VMEM_LIMIT)


def _rms(x, g):
    ms = jnp.mean(x * x, axis=-1, keepdims=True)
    return (x * lax.rsqrt(ms + RMS_EPS)) * g


def _norm_matmul_kernel(x_ref, g_ref, w_ref, *out_refs, splits, sigmoid, chunk):
    xb = _rms(x_ref[...], g_ref[...]).astype(BF16)
    c0 = 0
    for o_ref, n, sg in zip(out_refs, splits, sigmoid):
        for j in range(0, n, chunk):
            w = min(chunk, n - j)
            y = jnp.dot(xb, w_ref[:, c0 + j:c0 + j + w], preferred_element_type=F32)
            if sg:
                y = jax.nn.sigmoid(y)
            o_ref[:, j:j + w] = y
        c0 += n


def norm_matmul(x, g, w_bf16, splits, sigmoid, tm):
    n, d = x.shape
    tm = min(tm, n)
    kern = functools.partial(_norm_matmul_kernel, splits=tuple(splits), sigmoid=tuple(sigmoid), chunk=512)
    return pl.pallas_call(
        kern,
        grid=(n // tm,),
        in_specs=[
            pl.BlockSpec((tm, d), lambda i: (i, 0)),
            pl.BlockSpec((1, d), lambda i: (0, 0)),
            pl.BlockSpec((d, w_bf16.shape[1]), lambda i: (0, 0), pipeline_mode=pl.Buffered(1)),
        ],
        out_specs=[pl.BlockSpec((tm, s), lambda i: (i, 0)) for s in splits],
        out_shape=[jax.ShapeDtypeStruct((n, s), F32) for s in splits],
        compiler_params=_cparams(("arbitrary",)),
        name="norm_matmul",
    )(x, g.reshape(1, d), w_bf16)


def _attn_prompt_kernel(slopes_ref, *refs, seq):
    qkv_refs = refs[:9]
    o_ref = refs[9]
    o_scr, l_scr = refs[10], refs[11]
    hp = pl.program_id(1)
    scale = HEAD_DIM ** -0.5
    row = lax.broadcasted_iota(jnp.int32, (Q_TILE, 2 * Q_TILE), 0)
    col = lax.broadcasted_iota(jnp.int32, (Q_TILE, 2 * Q_TILE), 1)
    dist = (row + Q_TILE - col)
    in_window = jnp.logical_and(dist >= 0, dist <= Q_TILE)
    distf = dist.astype(F32)
    lane = lax.broadcasted_iota(jnp.int32, (Q_TILE, LANES), 1)
    head_a = lane < HEAD_DIM

    for g, (_, dil) in enumerate(ATT_GROUPS):
        q_ref, k_ref, v_ref = qkv_refs[3 * g:3 * g + 3]
        sub_len = seq // dil
        n_blk = sub_len // Q_TILE
        slope_a = slopes_ref[g * HEADS_PER_GROUP + 2 * hp] * float(dil)
        slope_b = slopes_ref[g * HEADS_PER_GROUP + 2 * hp + 1] * float(dil)

        def rows(start):
            if dil == 1:
                return pl.ds(start, Q_TILE)
            return pl.ds(start, Q_TILE, stride=dil)

        def body(it, carry, q_ref=q_ref, k_ref=k_ref, v_ref=v_ref, n_blk=n_blk, dil=dil,
                 slope_a=slope_a, slope_b=slope_b, rows=rows, g=g):
            r = it // n_blk
            blk = it % n_blk
            prev = jnp.maximum(blk - 1, 0)
            q_start = r + dil * Q_TILE * blk
            p_start = r + dil * Q_TILE * prev
            q = q_ref[rows(q_start), :]
            k2 = jnp.concatenate([k_ref[rows(p_start), :], k_ref[rows(q_start), :]], axis=0).astype(BF16)
            v2 = jnp.concatenate([v_ref[rows(p_start), :], v_ref[rows(q_start), :]], axis=0).astype(BF16)
            valid = jnp.logical_and(in_window, jnp.logical_or(col >= Q_TILE, blk > 0))
            outs, lses = [], []
            for is_a, slope in ((True, slope_a), (False, slope_b)):
                hmask = head_a if is_a else jnp.logical_not(head_a)
                qh = jnp.where(hmask, q, 0.0).astype(BF16)
                s = lax.dot_general(qh, k2, (((1,), (1,)), ((), ())), preferred_element_type=F32)
                s = s * scale - slope * distf
                s = jnp.where(valid, s, NEG_BIG)
                m = jnp.max(s, axis=-1, keepdims=True)
                p = jnp.exp(s - m)
                den = jnp.sum(p, axis=-1, keepdims=True)
                pn = (p * (1.0 / den)).astype(BF16)
                outs.append(jnp.dot(pn, v2, preferred_element_type=F32))
                lses.append(m + jnp.log(den))
            o_scr[g, rows(q_start), :] = jnp.where(head_a, outs[0], outs[1])
            l_scr[g, rows(q_start), :] = jnp.where(head_a, lses[0], lses[1])
            return carry

        lax.fori_loop(0, dil * n_blk, body, 0)

    def combine(c, carry):
        sl = pl.ds(pl.multiple_of(c * 256, 256), 256)
        l0, l1, l2 = l_scr[0, sl, :], l_scr[1, sl, :], l_scr[2, sl, :]
        m = jnp.maximum(jnp.maximum(l0, l1), l2)
        e0, e1, e2 = jnp.exp(l0 - m), jnp.exp(l1 - m), jnp.exp(l2 - m)
        tot = e0 + e1 + e2
        o_ref[sl, :] = (e0 * o_scr[0, sl, :] + e1 * o_scr[1, sl, :] + e2 * o_scr[2, sl, :]) / tot
        return carry

    lax.fori_loop(0, seq // 256, combine, 0)


def attn_prompt(qkv, slopes):
    bsz, seq, _ = qkv.shape
    in_specs = []
    for g in range(len(ATT_GROUPS)):
        for part in range(3):
            base = (g * QKV_GROUP + part * ATT_OUT) // LANES
            in_specs.append(pl.BlockSpec((None, seq, LANES),
                                         lambda b, h, sl, base=base: (b, 0, base + h)))
    grid_spec = pltpu.PrefetchScalarGridSpec(
        num_scalar_prefetch=1,
        grid=(bsz, ATT_OUT // LANES),
        in_specs=in_specs,
        out_specs=pl.BlockSpec((None, seq, LANES), lambda b, h, sl: (b, 0, h)),
        scratch_shapes=[pltpu.VMEM((3, seq, LANES), F32), pltpu.VMEM((3, seq, LANES), F32)],
    )
    return pl.pallas_call(
        functools.partial(_attn_prompt_kernel, seq=seq),
        grid_spec=grid_spec,
        out_shape=jax.ShapeDtypeStruct((bsz, seq, ATT_OUT), F32),
        compiler_params=_cparams(("arbitrary", "arbitrary")),
        name="attn_prompt",
    )(slopes, *([qkv] * 9))


def _attn_sample_kernel(qkv_ref, col_ref, *refs):
    cache_refs = refs[:6]
    o_ref = refs[6]
    new_refs = refs[7:13]
    scale = HEAD_DIM ** -0.5
    n_grp = len(ATT_GROUPS)
    outs = [[None] * HEADS_PER_GROUP for _ in range(n_grp)]
    lses = [[None] * HEADS_PER_GROUP for _ in range(n_grp)]
    for g, (window, dil) in enumerate(ATT_GROUPS):
        lane = lax.broadcasted_iota(jnp.int32, (1, window), 1)
        on_grid = jnp.bitwise_and(lane, dil - 1) == 0
        dist = (window - lane).astype(F32)
        last = lax.broadcasted_iota(jnp.int32, (HEAD_DIM, window), 1) == window - 1
        for h in range(HEADS_PER_GROUP):
            slope = 2.0 ** (-8.0 * (g * HEADS_PER_GROUP + h + 1) / N_ATT_HEADS)
            q = qkv_ref[g, 0][h:h + 1, :].astype(BF16)
            kn = qkv_ref[g, 1][h:h + 1, :].astype(BF16).astype(F32)
            vn = qkv_ref[g, 2][h:h + 1, :].astype(BF16).astype(F32)
            kt = cache_refs[2 * g][h]
            vt = cache_refs[2 * g + 1][h]
            s = jnp.dot(q, kt.astype(BF16), preferred_element_type=F32) * scale - slope * dist
            s = jnp.where(on_grid, s, NEG_BIG)
            sn = jnp.sum(q.astype(F32) * kn, axis=-1, keepdims=True) * scale
            m = jnp.maximum(jnp.max(s, axis=-1, keepdims=True), sn)
            p = jnp.exp(s - m)
            pn = jnp.exp(sn - m)
            den = jnp.sum(p, axis=-1, keepdims=True) + pn
            pb = (p / den).astype(BF16)
            pnb = (pn / den).astype(BF16).astype(F32)
            pv = lax.dot_general(pb, vt.astype(BF16), (((1,), (1,)), ((), ())), preferred_element_type=F32)
            outs[g][h] = pv + pnb * vn
            lses[g][h] = m + jnp.log(den)
            new_refs[2 * g][h] = jnp.where(last, col_ref[g, 0, h], pltpu.roll(kt, window - 1, axis=1))
            new_refs[2 * g + 1][h] = jnp.where(last, col_ref[g, 1, h], pltpu.roll(vt, window - 1, axis=1))
    for h in range(HEADS_PER_GROUP):
        m = jnp.maximum(jnp.maximum(lses[0][h], lses[1][h]), lses[2][h])
        es = [jnp.exp(lses[g][h] - m) for g in range(n_grp)]
        num = es[0] * outs[0][h] + es[1] * outs[1][h] + es[2] * outs[2][h]
        o_ref[h:h + 1, :] = num / (es[0] + es[1] + es[2])


def attn_sample(qkv5, caches_t):
    bsz = qkv5.shape[0]
    cols = qkv5[:, :, 1:3].reshape(bsz, len(ATT_GROUPS), 2, HEADS_PER_GROUP, HEAD_DIM, 1)
    in_specs = [pl.BlockSpec((None, 3, 3, HEADS_PER_GROUP, HEAD_DIM), lambda b: (b, 0, 0, 0, 0)),
                pl.BlockSpec((None, 3, 2, HEADS_PER_GROUP, HEAD_DIM, 1), lambda b: (b, 0, 0, 0, 0, 0))]
    cache_specs = [pl.BlockSpec((None,) + c.shape[1:], lambda b: (b, 0, 0, 0)) for c in caches_t]
    return pl.pallas_call(
        _attn_sample_kernel,
        grid=(bsz,),
        in_specs=in_specs + cache_specs,
        out_specs=[pl.BlockSpec((None, HEADS_PER_GROUP, HEAD_DIM), lambda b: (b, 0, 0))] + cache_specs,
        out_shape=[jax.ShapeDtypeStruct((bsz, HEADS_PER_GROUP, HEAD_DIM), F32)]
                  + [jax.ShapeDtypeStruct(c.shape, F32) for c in caches_t],
        compiler_params=_cparams(("arbitrary",)),
        name="attn_sample",
    )(qkv5, cols, *caches_t)


SSM_KBLK = SSM_WIDTH // LANES
SSM_KSTATE = SSM_NSTATE // SSM_KBLK


def _ssm_tail(y, u, dsk, wglu, bglu):
    y = y + dsk * u
    z = jax.nn.gelu(y)
    gl = jnp.dot(z.astype(BF16), wglu, preferred_element_type=F32) + bglu
    return z * jax.nn.sigmoid(gl)


def _ssm_prompt_kernel(u_ref, wb_ref, wc_ref, are_ref, aim_ref, dsk_ref, wglu_ref, bglu_ref,
                       o_ref, st_ref, uperm, bu, sbf, yperm, ynat, *, bsz, tt):
    rows = bsz * tt

    @pl.when(pl.program_id(0) == 0)
    def _():
        st_ref[...] = jnp.zeros_like(st_ref)

    for b in range(bsz):
        ub = u_ref[b]
        for k in range(SSM_KBLK):
            uperm[k, pl.ds(b, tt, stride=bsz), :] = ub[:, k * LANES:(k + 1) * LANES]

    for k in range(SSM_KBLK):
        bu[k] = jnp.dot(uperm[k].astype(BF16), wb_ref[k], preferred_element_type=F32)

    for k in range(SSM_KBLK):
        ar = jnp.broadcast_to(are_ref[k], (bsz, SSM_KSTATE))
        ai = jnp.broadcast_to(aim_ref[k], (bsz, SSM_KSTATE))
        st = st_ref[k]

        def step(t, carry, k=k, ar=ar, ai=ai):
            re, im = carry
            r0 = pl.multiple_of(t * bsz, bsz)
            b_t = bu[k, pl.ds(r0, bsz), :]
            nre = ar * re - ai * im + b_t[:, :SSM_KSTATE]
            nim = ar * im + ai * re + b_t[:, SSM_KSTATE:]
            sbf[k, pl.ds(r0, bsz), :] = jnp.concatenate([nre, nim], axis=1).astype(BF16)
            return nre, nim

        re, im = lax.fori_loop(0, tt, step, (st[:, :SSM_KSTATE], st[:, SSM_KSTATE:]), unroll=2)
        st_ref[k] = jnp.concatenate([re, im], axis=1)

    for k in range(SSM_KBLK):
        yperm[k] = jnp.dot(sbf[k], wc_ref[k], preferred_element_type=F32)

    for b in range(bsz):
        for k in range(SSM_KBLK):
            ynat[b * tt:(b + 1) * tt, k * LANES:(k + 1) * LANES] = yperm[k, pl.ds(b, tt, stride=bsz), :]

    u2 = u_ref[...].reshape(rows, SSM_WIDTH)
    out = _ssm_tail(ynat[...], u2, dsk_ref[...], wglu_ref[...], bglu_ref[...])
    o_ref[...] = out.reshape(bsz, tt, SSM_WIDTH)


def ssm_prompt(u, prm, tt=32):
    bsz, seq, _ = u.shape
    rows = bsz * tt
    const = lambda shape: pl.BlockSpec(shape, lambda i: (0,) * len(shape))
    return pl.pallas_call(
        functools.partial(_ssm_prompt_kernel, bsz=bsz, tt=tt),
        grid=(seq // tt,),
        in_specs=[
            pl.BlockSpec((bsz, tt, SSM_WIDTH), lambda i: (0, i, 0)),
            const((SSM_KBLK, LANES, 2 * SSM_KSTATE)),
            const((SSM_KBLK, 2 * SSM_KSTATE, LANES)),
            const((SSM_KBLK, 1, SSM_KSTATE)),
            const((SSM_KBLK, 1, SSM_KSTATE)),
            const((1, SSM_WIDTH)),
            const((SSM_WIDTH, SSM_WIDTH)),
            const((1, SSM_WIDTH)),
        ],
        out_specs=[
            pl.BlockSpec((bsz, tt, SSM_WIDTH), lambda i: (0, i, 0)),
            const((SSM_KBLK, bsz, 2 * SSM_KSTATE)),
        ],
        out_shape=[
            jax.ShapeDtypeStruct((bsz, seq, SSM_WIDTH), F32),
            jax.ShapeDtypeStruct((SSM_KBLK, bsz, 2 * SSM_KSTATE), F32),
        ],
        scratch_shapes=[
            pltpu.VMEM((SSM_KBLK, rows, LANES), F32),
            pltpu.VMEM((SSM_KBLK, rows, 2 * SSM_KSTATE), F32),
            pltpu.VMEM((SSM_KBLK, rows, 2 * SSM_KSTATE), BF16),
            pltpu.VMEM((SSM_KBLK, rows, LANES), F32),
            pltpu.VMEM((rows, SSM_WIDTH), F32),
        ],
        compiler_params=_cparams(("arbitrary",)),
        name="ssm_prompt",
    )(u, prm["wb"], prm["wc"], prm["a_re"], prm["a_im"], prm["dsk"], prm["wglu"], prm["bglu"])


def _ssm_sample_kernel(u_ref, h0_ref, wb_ref, wc_ref, are_ref, aim_ref, dsk_ref, wglu_ref, bglu_ref,
                       o_ref, st_ref):
    u = u_ref[...]
    ub = u.astype(BF16)
    ys = []
    for k in range(SSM_KBLK):
        b_t = jnp.dot(ub[:, k * LANES:(k + 1) * LANES], wb_ref[k], preferred_element_type=F32)
        h0 = h0_ref[k]
        re, im = h0[:, :SSM_KSTATE], h0[:, SSM_KSTATE:]
        ar, ai = are_ref[k], aim_ref[k]
        nre = ar * re - ai * im + b_t[:, :SSM_KSTATE]
        nim = ar * im + ai * re + b_t[:, SSM_KSTATE:]
        s = jnp.concatenate([nre, nim], axis=1)
        st_ref[k] = s
        ys.append(jnp.dot(s.astype(BF16), wc_ref[k], preferred_element_type=F32))
    y = jnp.concatenate(ys, axis=1)
    o_ref[...] = _ssm_tail(y, u, dsk_ref[...], wglu_ref[...], bglu_ref[...])


def ssm_sample(u, h0, prm):
    bsz = u.shape[0]
    return pl.pallas_call(
        _ssm_sample_kernel,
        out_shape=[jax.ShapeDtypeStruct((bsz, SSM_WIDTH), F32),
                   jax.ShapeDtypeStruct((SSM_KBLK, bsz, 2 * SSM_KSTATE), F32)],
        compiler_params=pltpu.CompilerParams(vmem_limit_bytes=VMEM_LIMIT),
        name="ssm_sample",
    )(u, h0, prm["wb"], prm["wc"], prm["a_re"], prm["a_im"], prm["dsk"], prm["wglu"], prm["bglu"])


def ssm_params(lam_re, lam_im, log_dt, b_re, b_im, c_re, c_im, d_skip, w_glu, b_glu):
    dt = jnp.exp(log_dt)[:, None]
    mag = jnp.exp(lam_re * dt)
    ab_re = mag * jnp.cos(lam_im * dt)
    ab_im = mag * jnp.sin(lam_im * dt)
    den = lam_re * lam_re + lam_im * lam_im
    nr = ab_re - 1.0
    ni = ab_im
    z_re = ((nr * lam_re + ni * lam_im) / den)[..., None]
    z_im = ((ni * lam_re - nr * lam_im) / den)[..., None]
    bb_re = z_re * b_re - z_im * b_im
    bb_im = z_re * b_im + z_im * b_re
    gpb = SSM_GROUPS // SSM_KBLK
    eye = jnp.eye(gpb, dtype=F32)

    def in_mat(bb):
        bbk = bb.reshape(SSM_KBLK, gpb, SSM_STATE, SSM_GROUP_CH)
        m = jnp.einsum("kgpc,gh->kgchp", bbk, eye)
        return m.reshape(SSM_KBLK, LANES, SSM_KSTATE)

    def out_mat(c):
        ck = c.reshape(SSM_KBLK, gpb, SSM_GROUP_CH, SSM_STATE)
        m = jnp.einsum("kgcp,gh->kgphc", ck, eye)
        return m.reshape(SSM_KBLK, SSM_KSTATE, LANES)

    wb = jnp.concatenate([in_mat(bb_re), in_mat(bb_im)], axis=2).astype(BF16)
    wc = jnp.concatenate([out_mat(c_re), -out_mat(c_im)], axis=1).astype(BF16)
    return dict(
        wb=wb, wc=wc,
        a_re=ab_re.reshape(SSM_KBLK, 1, SSM_KSTATE), a_im=ab_im.reshape(SSM_KBLK, 1, SSM_KSTATE),
        dsk=d_skip.reshape(1, SSM_WIDTH), wglu=w_glu.astype(BF16), bglu=b_glu.reshape(1, SSM_WIDTH))


def _state_to_blocks(h):
    bsz = h.shape[0]
    return h.reshape(bsz, SSM_KBLK, SSM_KSTATE).transpose(1, 0, 2)


def _blocks_to_state(s):
    bsz = s.shape[1]
    return s.transpose(1, 0, 2).reshape(bsz, SSM_GROUPS, SSM_STATE)


def _mem_attn_kernel(q_ref, k_ref, v_ref, o_ref):
    scale = MEM_HEAD_DIM ** -0.5
    for h in range(MEM_HEADS):
        sl = slice(h * MEM_HEAD_DIM, (h + 1) * MEM_HEAD_DIM)
        q = q_ref[:, sl].astype(BF16)
        k = k_ref[:, sl].astype(BF16)
        v = v_ref[:, sl].astype(BF16)
        s = lax.dot_general(q, k, (((1,), (1,)), ((), ())), preferred_element_type=F32) * scale
        m = jnp.max(s, axis=-1, keepdims=True)
        p = jnp.exp(s - m)
        den = jnp.sum(p, axis=-1, keepdims=True)
        o_ref[:, sl] = jnp.dot((p / den).astype(BF16), v, preferred_element_type=F32)


def mem_attention(q, mk, mv, tl):
    bsz, seq, _ = q.shape
    tl = min(tl, seq)
    return pl.pallas_call(
        _mem_attn_kernel,
        grid=(bsz, seq // tl),
        in_specs=[
            pl.BlockSpec((None, tl, MEM_WIDTH), lambda b, i: (b, i, 0)),
            pl.BlockSpec((None, MEM_TOKENS, MEM_WIDTH), lambda b, i: (b, 0, 0)),
            pl.BlockSpec((None, MEM_TOKENS, MEM_WIDTH), lambda b, i: (b, 0, 0)),
        ],
        out_specs=pl.BlockSpec((None, tl, MEM_WIDTH), lambda b, i: (b, i, 0)),
        out_shape=jax.ShapeDtypeStruct((bsz, seq, MEM_WIDTH), F32),
        compiler_params=_cparams(("arbitrary", "arbitrary")),
        name="mem_attention",
    )(q, mk, mv)


def _route(logits):
    lane = lax.broadcasted_iota(jnp.int32, logits.shape, 1)
    lanef = lane.astype(F32)
    none = float(ROUTER_COLS)
    neg = -jnp.inf
    gl = jnp.where(lane < MOE_GROUPS, logits, neg)
    gmax = jnp.max(gl, axis=-1, keepdims=True)
    grp = jnp.min(jnp.where(gl == gmax, lanef, none), axis=-1, keepdims=True)
    p_grp = 1.0 / jnp.sum(jnp.exp(gl - gmax), axis=-1, keepdims=True)
    lo = MOE_GROUPS + grp * EXPERTS_PER_GROUP
    el = jnp.where(jnp.logical_and(lanef >= lo, lanef < lo + EXPERTS_PER_GROUP), logits, neg)
    v1 = jnp.max(el, axis=-1, keepdims=True)
    i1 = jnp.min(jnp.where(el == v1, lanef, none), axis=-1, keepdims=True)
    el2 = jnp.where(lanef == i1, neg, el)
    v2 = jnp.max(el2, axis=-1, keepdims=True)
    i2 = jnp.min(jnp.where(el2 == v2, lanef, none), axis=-1, keepdims=True)
    t = jnp.exp(v2 - v1)
    g1 = p_grp / (1.0 + t)
    g2 = g1 * t
    out = jnp.where(lane == 0, g1, 0.0)
    out = jnp.where(lane == 1, g2, out)
    out = jnp.where(lane == 2, i1 - MOE_GROUPS, out)
    return jnp.where(lane == 3, i2 - MOE_GROUPS, out)


def _merge_kernel(att_ref, ssm_ref, mem_ref, gate_ref, x_ref, wa_ref, ws_ref, wm_ref, wo_ref,
                  n2_ref, wr_ref, br_ref, h_ref, hn_ref, lg_ref, *, tm):
    a = jnp.dot(att_ref[...].astype(BF16), wa_ref[...], preferred_element_type=F32)
    merged = gate_ref[:, 0:D_MODEL] * a
    s = jnp.dot(ssm_ref[...].astype(BF16), ws_ref[...], preferred_element_type=F32)
    merged = merged + gate_ref[:, D_MODEL:2 * D_MODEL] * s
    m = jnp.dot(mem_ref[...].astype(BF16), wm_ref[...], preferred_element_type=F32)
    merged = merged + gate_ref[:, 2 * D_MODEL:3 * D_MODEL] * m
    h = x_ref[...] + jnp.dot(merged.astype(BF16), wo_ref[...], preferred_element_type=F32)
    h_ref[...] = h
    hn = _rms(h, n2_ref[...])
    for j in range(D_MODEL // LANES):
        hn_ref[pl.ds(j, tm, stride=SUBLANES), :] = hn[:, j * LANES:(j + 1) * LANES]
    logits = jnp.dot(hn.astype(BF16), wr_ref[...], preferred_element_type=F32) + br_ref[...]
    lg_ref[...] = _route(logits)


def merge(att, ssm, mem, gates, x, wts, tm):
    n = x.shape[0]
    tm = min(tm, n)
    row = lambda w: pl.BlockSpec((tm, w), lambda i: (i, 0))
    const = lambda a: pl.BlockSpec(a.shape, lambda i: (0, 0))
    ws = [wts["wa"], wts["ws"], wts["wm"], wts["wo"], wts["n2"], wts["wr"], wts["br"]]
    return pl.pallas_call(
        functools.partial(_merge_kernel, tm=tm),
        grid=(n // tm,),
        in_specs=[row(ATT_OUT), row(SSM_WIDTH), row(MEM_WIDTH), row(N_BRANCHES * D_MODEL), row(D_MODEL)]
                 + [const(w) for w in ws],
        out_specs=[row(D_MODEL), pl.BlockSpec((tm * SUBLANES, LANES), lambda i: (i, 0)), row(ROUTER_COLS)],
        out_shape=[jax.ShapeDtypeStruct((n, D_MODEL), F32),
                   jax.ShapeDtypeStruct((n * SUBLANES, LANES), F32),
                   jax.ShapeDtypeStruct((n, ROUTER_COLS), F32)],
        compiler_params=_cparams(("arbitrary",)),
        name="merge",
    )(att, ssm, mem, gates, x, *ws)


def _moe_kernel(be_ref, idx_hbm, x_hbm, w1_ref, w3_ref, w2_ref, out_hbm,
                idx_smem, xbuf, xb, ybuf, w1b, w3b, w2b, gsem, ssem, isem, *, bm):
    blk = pl.program_id(0)
    nblk = pl.num_programs(0)
    slot = blk % 2
    tiles = D_MODEL // LANES

    def idx_copy(b, s):
        return pltpu.make_async_copy(idx_hbm.at[b], idx_smem.at[s], isem.at[s])

    def issue_gather(s):
        def body(i, c):
            tok = idx_smem[s, lax.shift_right_logical(i, 7), jnp.bitwise_and(i, LANES - 1)]
            r0 = pl.multiple_of(i * SUBLANES, SUBLANES)
            pltpu.make_async_copy(x_hbm.at[tok], xbuf.at[pl.ds(r0, SUBLANES)], gsem).start()
            return c
        lax.fori_loop(0, bm, body, 0, unroll=8)

    def issue_scatter(s):
        def body(i, c):
            j = i + bm
            pid = idx_smem[s, lax.shift_right_logical(j, 7), jnp.bitwise_and(j, LANES - 1)]
            r0 = pl.multiple_of(i * SUBLANES, SUBLANES)
            pltpu.make_async_copy(ybuf.at[pl.ds(r0, SUBLANES)], out_hbm.at[pid], ssem).start()
            return c
        lax.fori_loop(0, bm, body, 0, unroll=8)

    def wait_rows(sem, buf):
        pltpu.make_async_copy(buf, buf, sem).wait()

    @pl.when(blk == 0)
    def _():
        c = idx_copy(0, 0)
        c.start()
        c.wait()
        issue_gather(0)

    @pl.when(blk + 1 < nblk)
    def _():
        idx_copy(blk + 1, 1 - slot).start()

    prev = jnp.maximum(blk - 1, 0)
    changed = jnp.logical_or(blk == 0, be_ref[blk] != be_ref[prev])

    @pl.when(changed)
    def _():
        w1b[...] = w1_ref[...].astype(BF16)
        w3b[...] = w3_ref[...].astype(BF16)
        w2b[...] = w2_ref[...].astype(BF16)

    wait_rows(gsem, xbuf)
    for j in range(tiles):
        xb[:, j * LANES:(j + 1) * LANES] = xbuf[pl.ds(j, bm, stride=SUBLANES), :].astype(BF16)

    @pl.when(blk + 1 < nblk)
    def _():
        idx_copy(blk + 1, 1 - slot).wait()
        issue_gather(1 - slot)

    x = xb[...]
    h1 = jnp.dot(x, w1b[...], preferred_element_type=F32)
    h3 = jnp.dot(x, w3b[...], preferred_element_type=F32)
    act = (jax.nn.silu(h1) * h3).astype(BF16)
    y = jnp.dot(act, w2b[...], preferred_element_type=F32)

    @pl.when(blk > 0)
    def _():
        wait_rows(ssem, ybuf)

    for j in range(tiles):
        ybuf[pl.ds(j, bm, stride=SUBLANES), :] = y[:, j * LANES:(j + 1) * LANES]
    issue_scatter(slot)

    @pl.when(blk == nblk - 1)
    def _():
        wait_rows(ssem, ybuf)


def moe_experts(block_expert, idx, hn3, w1, w3, w2, n_out_rows, bm):
    n_blocks = idx.shape[0]
    grid_spec = pltpu.PrefetchScalarGridSpec(
        num_scalar_prefetch=1,
        grid=(n_blocks,),
        in_specs=[
            pl.BlockSpec(memory_space=pl.ANY),
            pl.BlockSpec(memory_space=pl.ANY),
            pl.BlockSpec((None, D_MODEL, D_EXPERT), lambda b, be: (be[b], 0, 0)),
            pl.BlockSpec((None, D_MODEL, D_EXPERT), lambda b, be: (be[b], 0, 0)),
            pl.BlockSpec((None, D_EXPERT, D_MODEL), lambda b, be: (be[b], 0, 0)),
        ],
        out_specs=pl.BlockSpec(memory_space=pl.ANY),
        scratch_shapes=[
            pltpu.SMEM((2, SUBLANES, LANES), jnp.int32),
            pltpu.VMEM((bm * SUBLANES, LANES), F32),
            pltpu.VMEM((bm, D_MODEL), BF16),
            pltpu.VMEM((bm * SUBLANES, LANES), F32),
            pltpu.VMEM((D_MODEL, D_EXPERT), BF16),
            pltpu.VMEM((D_MODEL, D_EXPERT), BF16),
            pltpu.VMEM((D_EXPERT, D_MODEL), BF16),
            pltpu.SemaphoreType.DMA,
            pltpu.SemaphoreType.DMA,
            pltpu.SemaphoreType.DMA((2,)),
        ],
    )
    return pl.pallas_call(
        functools.partial(_moe_kernel, bm=bm),
        grid_spec=grid_spec,
        out_shape=jax.ShapeDtypeStruct((n_out_rows, SUBLANES, LANES), F32),
        compiler_params=_cparams(("arbitrary",)),
        name="moe_experts",
    )(block_expert, idx, hn3, w1, w3, w2)


def route_tables(route, bm):
    n_tok = route.shape[0]
    n_pairs = n_tok * TOP_K
    e_flat = route[:, 2:2 + TOP_K].astype(jnp.int32).reshape(n_pairs)
    order = jnp.argsort(e_flat).astype(jnp.int32)
    experts = jnp.arange(N_EXPERTS, dtype=jnp.int32)
    counts = jnp.sum(e_flat[:, None] == experts[None, :], axis=0, dtype=jnp.int32)
    starts = jnp.cumsum(counts) - counts
    padded = (counts + bm - 1) // bm * bm
    pad_ends = jnp.cumsum(padded)
    pad_starts = pad_ends - padded
    n_blocks = -(-n_pairs // bm) + N_EXPERTS
    row0 = jnp.arange(n_blocks, dtype=jnp.int32) * bm
    block_expert = jnp.minimum(jnp.sum(pad_ends[None, :] <= row0[:, None], axis=1, dtype=jnp.int32),
                               N_EXPERTS - 1)
    off = row0 - pad_starts[block_expert]
    n_valid = jnp.clip(counts[block_expert] - off, 0, bm)
    s_base = jnp.clip(starts[block_expert] + off, 0, n_pairs)
    order_pad = jnp.concatenate([order, jnp.zeros((bm,), jnp.int32)])
    win = jax.vmap(lambda s0: lax.dynamic_slice(order_pad, (s0,), (bm,)))(s_base)
    pos = jnp.arange(bm, dtype=jnp.int32)[None, :]
    valid = pos < n_valid[:, None]
    tok = win // TOP_K
    src = jnp.where(valid, tok, 0)
    dst = jnp.where(valid, (win % TOP_K) * n_tok + tok, n_pairs + pos)
    fill = jnp.zeros((n_blocks, SUBLANES * LANES - 2 * bm), jnp.int32)
    idx = jnp.concatenate([src, dst, fill], axis=1).reshape(n_blocks, SUBLANES, LANES)
    return block_expert, idx, n_pairs + bm


def _final_kernel(h_ref, p0_ref, p1_ref, gw_ref, nf_ref, y_ref, *, tm):
    tiles = D_MODEL // LANES
    p0 = jnp.concatenate([p0_ref[pl.ds(j, tm, stride=SUBLANES), :] for j in range(tiles)], axis=1)
    p1 = jnp.concatenate([p1_ref[pl.ds(j, tm, stride=SUBLANES), :] for j in range(tiles)], axis=1)
    gw = gw_ref[...]
    h = h_ref[...] + (gw[:, 0:1] * p0 + gw[:, 1:2] * p1)
    y_ref[...] = _rms(h, nf_ref[...])


def final(h, pairs2d, gate_pad, norm_f, tm):
    n = h.shape[0]
    tm = min(tm, n)
    nb = n // tm
    row = lambda w: pl.BlockSpec((tm, w), lambda i: (i, 0))
    return pl.pallas_call(
        functools.partial(_final_kernel, tm=tm),
        grid=(nb,),
        in_specs=[
            row(D_MODEL),
            pl.BlockSpec((tm * SUBLANES, LANES), lambda i: (i, 0)),
            pl.BlockSpec((tm * SUBLANES, LANES), lambda i: (nb + i, 0)),
            row(LANES),
            pl.BlockSpec((1, D_MODEL), lambda i: (0, 0)),
        ],
        out_specs=row(D_MODEL),
        out_shape=jax.ShapeDtypeStruct((n, D_MODEL), F32),
        compiler_params=_cparams(("arbitrary",)),
        name="final",
    )(h, pairs2d, pairs2d, gate_pad, norm_f.reshape(1, D_MODEL))


def _alibi_slopes():
    h = jnp.arange(1, N_ATT_HEADS + 1, dtype=F32)
    return jnp.exp2(-8.0 * h / N_ATT_HEADS)


def _token_stage(att, ssm, mem, gates, x2, wts, w1, w3, w2, norm_f, tm, bm):
    n = x2.shape[0]
    h, hn2d, route = merge(att, ssm, mem, gates, x2, wts, tm)
    block_expert, idx, out_rows = route_tables(route, bm)
    pairs = moe_experts(block_expert, idx, hn2d.reshape(n, SUBLANES, LANES), w1, w3, w2, out_rows, bm)
    return final(h, pairs.reshape(out_rows * SUBLANES, LANES), route, norm_f, tm)


def kernel(x_prompt, x_sample, cache_swa0_k, cache_swa0_v, cache_swa1_k, cache_swa1_v, cache_swa2_k, cache_swa2_v, cache_mem_k, cache_mem_v, state_ssm_re, state_ssm_im, mem_prompt, norm1, w_in, lam_re, lam_im, log_dt, ssm_b_re, ssm_b_im, ssm_c_re, ssm_c_im, ssm_d, w_glu, b_glu, w_att_out, w_ssm_out, w_mem_out, w_o, norm_mem, w_mk, w_mv, norm2, w_grp, b_grp, w_exp, b_exp, w1, w3, w2, norm_f):
    assert norm1.shape[0] == 1, "single-layer trunk"
    bsz, seq, _ = x_prompt.shape
    sbz = x_sample.shape[0]
    n_p = bsz * seq
    caches = (cache_swa0_k[0], cache_swa0_v[0], cache_swa1_k[0], cache_swa1_v[0], cache_swa2_k[0], cache_swa2_v[0])

    w_in_b = w_in[0].astype(BF16)
    w_mkv_b = jnp.concatenate([w_mk[0], w_mv[0]], axis=1).astype(BF16)
    wr = jnp.zeros((D_MODEL, ROUTER_COLS), F32)
    wr = wr.at[:, :MOE_GROUPS].set(w_grp[0]).at[:, MOE_GROUPS:MOE_GROUPS + N_EXPERTS].set(w_exp[0])
    br = jnp.zeros((1, ROUTER_COLS), F32)
    br = br.at[0, :MOE_GROUPS].set(b_grp[0]).at[0, MOE_GROUPS:MOE_GROUPS + N_EXPERTS].set(b_exp[0])
    wts = dict(wa=w_att_out[0].astype(BF16), ws=w_ssm_out[0].astype(BF16), wm=w_mem_out[0].astype(BF16),
               wo=w_o[0].astype(BF16), n2=norm2[0].reshape(1, D_MODEL), wr=wr.astype(BF16), br=br)
    prm = ssm_params(lam_re[0], lam_im[0], log_dt[0], ssm_b_re[0], ssm_b_im[0], ssm_c_re[0], ssm_c_im[0],
                     ssm_d[0], w_glu[0], b_glu[0])
    slopes = _alibi_slopes()
    splits = (QKV_COLS, SSM_WIDTH, MEM_WIDTH, N_BRANCHES * D_MODEL)
    sig = (False, False, False, True)

    x2 = x_prompt.reshape(n_p, D_MODEL)
    qkv, u, mq, gates = norm_matmul(x2, norm1[0], w_in_b, splits, sig, tm=256)
    qkv3 = qkv.reshape(bsz, seq, QKV_COLS)
    att = attn_prompt(qkv3, slopes)
    ssm, st = ssm_prompt(u.reshape(bsz, seq, SSM_WIDTH), prm)
    mk, mv = norm_matmul(mem_prompt.reshape(bsz * MEM_TOKENS, D_MODEL), norm_mem[0], w_mkv_b,
                         (MEM_WIDTH, MEM_WIDTH), (False, False), tm=256)
    mk3 = mk.reshape(bsz, MEM_TOKENS, MEM_WIDTH)
    mv3 = mv.reshape(bsz, MEM_TOKENS, MEM_WIDTH)
    mem = mem_attention(mq.reshape(bsz, seq, MEM_WIDTH), mk3, mv3, tl=512)
    y_p = _token_stage(att.reshape(n_p, ATT_OUT), ssm.reshape(n_p, SSM_WIDTH), mem.reshape(n_p, MEM_WIDTH),
                       gates, x2, wts, w1[0], w3[0], w2[0], norm_f, tm=256, bm=256)

    xs2 = x_sample.reshape(sbz, D_MODEL)
    qkv_s, u_s, mq_s, gates_s = norm_matmul(xs2, norm1[0], w_in_b, splits, sig, tm=sbz)
    qkv5 = qkv_s.reshape(sbz, len(ATT_GROUPS), 3, HEADS_PER_GROUP, HEAD_DIM)
    att_s, *new_caches = attn_sample(qkv5, [jnp.transpose(c, (0, 2, 3, 1)) for c in caches])
    att_s = att_s.reshape(sbz, ATT_OUT)
    h0 = jnp.concatenate([_state_to_blocks(state_ssm_re[0]), _state_to_blocks(state_ssm_im[0])], axis=2)
    ssm_s, st_s = ssm_sample(u_s, h0, prm)
    cmk = cache_mem_k[0].reshape(sbz, MEM_TOKENS, MEM_WIDTH)
    cmv = cache_mem_v[0].reshape(sbz, MEM_TOKENS, MEM_WIDTH)
    mem_s = mem_attention(mq_s.reshape(sbz, 1, MEM_WIDTH), cmk, cmv, tl=1).reshape(sbz, MEM_WIDTH)
    y_s = _token_stage(att_s, ssm_s, mem_s, gates_s, xs2, wts, w1[0], w3[0], w2[0], norm_f, tm=sbz, bm=16)

    outs = [y_p.reshape(bsz, seq, D_MODEL), y_s.reshape(sbz, 1, D_MODEL)]
    for g, (window, _) in enumerate(ATT_GROUPS):
        keep = min(window, seq)
        for part in (1, 2):
            c0 = g * QKV_GROUP + part * ATT_OUT
            outs.append(qkv3[:, seq - keep:, c0:c0 + ATT_OUT].reshape(1, bsz, keep, HEADS_PER_GROUP, HEAD_DIM))
    outs.append(mk3.reshape(1, bsz, MEM_TOKENS, MEM_HEADS, MEM_HEAD_DIM))
    outs.append(mv3.reshape(1, bsz, MEM_TOKENS, MEM_HEADS, MEM_HEAD_DIM))
    outs.append(_blocks_to_state(st[:, :, :SSM_KSTATE])[None])
    outs.append(_blocks_to_state(st[:, :, SSM_KSTATE:])[None])
    outs.extend(jnp.transpose(c, (0, 3, 1, 2))[None] for c in new_caches)
    outs.append(_blocks_to_state(st_s[:, :, :SSM_KSTATE])[None])
    outs.append(_blocks_to_state(st_s[:, :, SSM_KSTATE:])[None])
    return tuple(outs)
```

```python
import functools
import math

import jax
import jax.numpy as jnp
from jax import lax
from jax.experimental import pallas as pl
from jax.experimental.pallas import tpu as pltpu

F32 = jnp.float32
BF16 = jnp.bfloat16

D_MODEL = 1024
ATT_GROUPS = ((128, 1), (512, 4), (2048, 16))
HEADS_PER_GROUP = 4
HEAD_DIM = 64
N_ATT_HEADS = len(ATT_GROUPS) * HEADS_PER_GROUP
ATT_OUT = HEADS_PER_GROUP * HEAD_DIM
QKV_GROUP = 3 * ATT_OUT
QKV_COLS = len(ATT_GROUPS) * QKV_GROUP
SSM_WIDTH = D_MODEL // 2
SSM_GROUP_CH = 16
SSM_GROUPS = SSM_WIDTH // SSM_GROUP_CH
SSM_STATE = 64
SSM_NSTATE = SSM_GROUPS * SSM_STATE
MEM_TOKENS = 256
MEM_HEADS = 4
MEM_HEAD_DIM = D_MODEL // 8
MEM_WIDTH = MEM_HEADS * MEM_HEAD_DIM
N_BRANCHES = 3
MOE_GROUPS = 4
EXPERTS_PER_GROUP = 8
N_EXPERTS = MOE_GROUPS * EXPERTS_PER_GROUP
TOP_K = 2
D_EXPERT = D_MODEL // 2
RMS_EPS = 1e-6

LANES = 128
SUBLANES = 8
Q_TILE = 128
ROUTER_COLS = 128
NEG_BIG = -1e30
VMEM_LIMIT = 56 * 1024 * 1024


def _cparams(sem):
    return pltpu.CompilerParams(dimension_semantics=sem, vmem_limit_bytes=VMEM_LIMIT)


def _rms(x, g):
    ms = jnp.mean(x * x, axis=-1, keepdims=True)
    return (x * lax.rsqrt(ms + RMS_EPS)) * g


def _norm_matmul_kernel(x_ref, g_ref, w_ref, *out_refs, splits, sigmoid, chunk):
    xb = _rms(x_ref[...], g_ref[...]).astype(BF16)
    c0 = 0
    for o_ref, n, sg in zip(out_refs, splits, sigmoid):
        for j in range(0, n, chunk):
            w = min(chunk, n - j)
            y = jnp.dot(xb, w_ref[:, c0 + j:c0 + j + w], preferred_element_type=F32)
            if sg:
                y = jax.nn.sigmoid(y)
            o_ref[:, j:j + w] = y
        c0 += n


def norm_matmul(x, g, w_bf16, splits, sigmoid, tm):
    n, d = x.shape
    tm = min(tm, n)
    kern = functools.partial(_norm_matmul_kernel, splits=tuple(splits), sigmoid=tuple(sigmoid), chunk=512)
    return pl.pallas_call(
        kern,
        grid=(n // tm,),
        in_specs=[
            pl.BlockSpec((tm, d), lambda i: (i, 0)),
            pl.BlockSpec((1, d), lambda i: (0, 0)),
            pl.BlockSpec((d, w_bf16.shape[1]), lambda i: (0, 0), pipeline_mode=pl.Buffered(1)),
        ],
        out_specs=[pl.BlockSpec((tm, s), lambda i: (i, 0)) for s in splits],
        out_shape=[jax.ShapeDtypeStruct((n, s), F32) for s in splits],
        compiler_params=_cparams(("arbitrary",)),
        name="norm_matmul",
    )(x, g.reshape(1, d), w_bf16)


def _attn_prompt_kernel(slopes_ref, *refs, seq):
    qkv_refs = refs[:9]
    o_ref = refs[9]
    o_scr, l_scr = refs[10], refs[11]
    hp = pl.program_id(1)
    scale = HEAD_DIM ** -0.5
    row = lax.broadcasted_iota(jnp.int32, (Q_TILE, 2 * Q_TILE), 0)
    col = lax.broadcasted_iota(jnp.int32, (Q_TILE, 2 * Q_TILE), 1)
    dist = (row + Q_TILE - col)
    in_window = jnp.logical_and(dist >= 0, dist <= Q_TILE)
    distf = dist.astype(F32)
    lane = lax.broadcasted_iota(jnp.int32, (Q_TILE, LANES), 1)
    head_a = lane < HEAD_DIM

    for g, (_, dil) in enumerate(ATT_GROUPS):
        q_ref, k_ref, v_ref = qkv_refs[3 * g:3 * g + 3]
        sub_len = seq // dil
        n_blk = sub_len // Q_TILE
        slope_a = slopes_ref[g * HEADS_PER_GROUP + 2 * hp] * float(dil)
        slope_b = slopes_ref[g * HEADS_PER_GROUP + 2 * hp + 1] * float(dil)

        def rows(start):
            if dil == 1:
                return pl.ds(start, Q_TILE)
            return pl.ds(start, Q_TILE, stride=dil)

        def body(it, carry, q_ref=q_ref, k_ref=k_ref, v_ref=v_ref, n_blk=n_blk, dil=dil,
                 slope_a=slope_a, slope_b=slope_b, rows=rows, g=g):
            r = it // n_blk
            blk = it % n_blk
            prev = jnp.maximum(blk - 1, 0)
            q_start = r + dil * Q_TILE * blk
            p_start = r + dil * Q_TILE * prev
            q = q_ref[rows(q_start), :]
            k2 = jnp.concatenate([k_ref[rows(p_start), :], k_ref[rows(q_start), :]], axis=0).astype(BF16)
            v2 = jnp.concatenate([v_ref[rows(p_start), :], v_ref[rows(q_start), :]], axis=0).astype(BF16)
            valid = jnp.logical_and(in_window, jnp.logical_or(col >= Q_TILE, blk > 0))
            outs, lses = [], []
            for is_a, slope in ((True, slope_a), (False, slope_b)):
                hmask = head_a if is_a else jnp.logical_not(head_a)
                qh = jnp.where(hmask, q, 0.0).astype(BF16)
                s = lax.dot_general(qh, k2, (((1,), (1,)), ((), ())), preferred_element_type=F32)
                s = s * scale - slope * distf
                s = jnp.where(valid, s, NEG_BIG)
                m = jnp.max(s, axis=-1, keepdims=True)
                p = jnp.exp(s - m)
                den = jnp.sum(p, axis=-1, keepdims=True)
                pv = jnp.dot(p.astype(BF16), v2, preferred_element_type=F32)
                outs.append(pv / den)
                lses.append(m + jnp.log(den))
            o_scr[g, rows(q_start), :] = jnp.where(head_a, outs[0], outs[1])
            l_scr[g, rows(q_start), :] = jnp.where(head_a, lses[0], lses[1])
            return carry

        lax.fori_loop(0, dil * n_blk, body, 0, unroll=4)

    def combine(c, carry):
        sl = pl.ds(pl.multiple_of(c * 256, 256), 256)
        l0, l1, l2 = l_scr[0, sl, :], l_scr[1, sl, :], l_scr[2, sl, :]
        m = jnp.maximum(jnp.maximum(l0, l1), l2)
        e0, e1, e2 = jnp.exp(l0 - m), jnp.exp(l1 - m), jnp.exp(l2 - m)
        tot = e0 + e1 + e2
        o_ref[sl, :] = (e0 * o_scr[0, sl, :] + e1 * o_scr[1, sl, :] + e2 * o_scr[2, sl, :]) / tot
        return carry

    lax.fori_loop(0, seq // 256, combine, 0)


def attn_prompt(qkv, slopes):
    bsz, seq, _ = qkv.shape
    in_specs = []
    for g in range(len(ATT_GROUPS)):
        for part in range(3):
            base = (g * QKV_GROUP + part * ATT_OUT) // LANES
            in_specs.append(pl.BlockSpec((None, seq, LANES),
                                         lambda b, h, sl, base=base: (b, 0, base + h)))
    grid_spec = pltpu.PrefetchScalarGridSpec(
        num_scalar_prefetch=1,
        grid=(bsz, ATT_OUT // LANES),
        in_specs=in_specs,
        out_specs=pl.BlockSpec((None, seq, LANES), lambda b, h, sl: (b, 0, h)),
        scratch_shapes=[pltpu.VMEM((3, seq, LANES), F32), pltpu.VMEM((3, seq, LANES), F32)],
    )
    return pl.pallas_call(
        functools.partial(_attn_prompt_kernel, seq=seq),
        grid_spec=grid_spec,
        out_shape=jax.ShapeDtypeStruct((bsz, seq, ATT_OUT), F32),
        compiler_params=_cparams(("arbitrary", "arbitrary")),
        name="attn_prompt",
    )(slopes, *([qkv] * 9))


def _attn_sample_kernel(qkv_ref, col_ref, *refs):
    cache_refs = refs[:6]
    o_ref = refs[6]
    new_refs = refs[7:13]
    scale = HEAD_DIM ** -0.5
    n_grp = len(ATT_GROUPS)
    outs = [[None] * HEADS_PER_GROUP for _ in range(n_grp)]
    lses = [[None] * HEADS_PER_GROUP for _ in range(n_grp)]
    for g, (window, dil) in enumerate(ATT_GROUPS):
        lane = lax.broadcasted_iota(jnp.int32, (1, window), 1)
        on_grid = jnp.bitwise_and(lane, dil - 1) == 0
        dist = (window - lane).astype(F32)
        last = lax.broadcasted_iota(jnp.int32, (HEAD_DIM, window), 1) == window - 1
        for h in range(HEADS_PER_GROUP):
            slope = 2.0 ** (-8.0 * (g * HEADS_PER_GROUP + h + 1) / N_ATT_HEADS)
            q = qkv_ref[g, 0][h:h + 1, :].astype(BF16)
            kn = qkv_ref[g, 1][h:h + 1, :].astype(BF16).astype(F32)
            vn = qkv_ref[g, 2][h:h + 1, :].astype(BF16).astype(F32)
            kt = cache_refs[2 * g][h]
            vt = cache_refs[2 * g + 1][h]
            s = jnp.dot(q, kt.astype(BF16), preferred_element_type=F32) * scale - slope * dist
            s = jnp.where(on_grid, s, NEG_BIG)
            sn = jnp.sum(q.astype(F32) * kn, axis=-1, keepdims=True) * scale
            m = jnp.maximum(jnp.max(s, axis=-1, keepdims=True), sn)
            p = jnp.exp(s - m)
            pn = jnp.exp(sn - m)
            den = jnp.sum(p, axis=-1, keepdims=True) + pn
            pb = (p / den).astype(BF16)
            pnb = (pn / den).astype(BF16).astype(F32)
            pv = lax.dot_general(pb, vt.astype(BF16), (((1,), (1,)), ((), ())), preferred_element_type=F32)
            outs[g][h] = pv + pnb * vn
            lses[g][h] = m + jnp.log(den)
            new_refs[2 * g][h] = jnp.where(last, col_ref[g, 0, h], pltpu.roll(kt, window - 1, axis=1))
            new_refs[2 * g + 1][h] = jnp.where(last, col_ref[g, 1, h], pltpu.roll(vt, window - 1, axis=1))
    for h in range(HEADS_PER_GROUP):
        m = jnp.maximum(jnp.maximum(lses[0][h], lses[1][h]), lses[2][h])
        es = [jnp.exp(lses[g][h] - m) for g in range(n_grp)]
        num = es[0] * outs[0][h] + es[1] * outs[1][h] + es[2] * outs[2][h]
        o_ref[h:h + 1, :] = num / (es[0] + es[1] + es[2])


def attn_sample(qkv5, caches_t):
    bsz = qkv5.shape[0]
    cols = qkv5[:, :, 1:3].reshape(bsz, len(ATT_GROUPS), 2, HEADS_PER_GROUP, HEAD_DIM, 1)
    in_specs = [pl.BlockSpec((None, 3, 3, HEADS_PER_GROUP, HEAD_DIM), lambda b: (b, 0, 0, 0, 0)),
                pl.BlockSpec((None, 3, 2, HEADS_PER_GROUP, HEAD_DIM, 1), lambda b: (b, 0, 0, 0, 0, 0))]
    cache_specs = [pl.BlockSpec((None,) + c.shape[1:], lambda b: (b, 0, 0, 0)) for c in caches_t]
    return pl.pallas_call(
        _attn_sample_kernel,
        grid=(bsz,),
        in_specs=in_specs + cache_specs,
        out_specs=[pl.BlockSpec((None, HEADS_PER_GROUP, HEAD_DIM), lambda b: (b, 0, 0))] + cache_specs,
        out_shape=[jax.ShapeDtypeStruct((bsz, HEADS_PER_GROUP, HEAD_DIM), F32)]
                  + [jax.ShapeDtypeStruct(c.shape, F32) for c in caches_t],
        compiler_params=_cparams(("arbitrary",)),
        name="attn_sample",
    )(qkv5, cols, *caches_t)


SSM_KBLK = SSM_WIDTH // LANES
SSM_KSTATE = SSM_NSTATE // SSM_KBLK


def _ssm_tail(y, u, dsk, wglu, bglu):
    y = y + dsk * u
    z = jax.nn.gelu(y)
    gl = jnp.dot(z.astype(BF16), wglu, preferred_element_type=F32) + bglu
    return z * jax.nn.sigmoid(gl)


def _ssm_prompt_kernel(u_ref, wb_ref, wc_ref, are_ref, aim_ref, dsk_ref, wglu_ref, bglu_ref,
                       o_ref, st_ref, uperm, bu, sbf, yperm, ynat, *, bsz, tt):
    rows = bsz * tt

    @pl.when(pl.program_id(0) == 0)
    def _():
        st_ref[...] = jnp.zeros_like(st_ref)

    for b in range(bsz):
        ub = u_ref[b]
        for k in range(SSM_KBLK):
            uperm[k, pl.ds(b, tt, stride=bsz), :] = ub[:, k * LANES:(k + 1) * LANES]

    for k in range(SSM_KBLK):
        bu[k] = jnp.dot(uperm[k].astype(BF16), wb_ref[k], preferred_element_type=F32)

    for k in range(SSM_KBLK):
        ar = jnp.broadcast_to(are_ref[k], (bsz, SSM_KSTATE))
        ai = jnp.broadcast_to(aim_ref[k], (bsz, SSM_KSTATE))
        st = st_ref[k]

        def step(t, carry, k=k, ar=ar, ai=ai):
            re, im = carry
            r0 = pl.multiple_of(t * bsz, bsz)
            b_t = bu[k, pl.ds(r0, bsz), :]
            nre = ar * re - ai * im + b_t[:, :SSM_KSTATE]
            nim = ar * im + ai * re + b_t[:, SSM_KSTATE:]
            sbf[k, pl.ds(r0, bsz), :] = jnp.concatenate([nre, nim], axis=1).astype(BF16)
            return nre, nim

        re, im = lax.fori_loop(0, tt, step, (st[:, :SSM_KSTATE], st[:, SSM_KSTATE:]), unroll=2)
        st_ref[k] = jnp.concatenate([re, im], axis=1)

    for k in range(SSM_KBLK):
        yperm[k] = jnp.dot(sbf[k], wc_ref[k], preferred_element_type=F32)

    for b in range(bsz):
        for k in range(SSM_KBLK):
            ynat[b * tt:(b + 1) * tt, k * LANES:(k + 1) * LANES] = yperm[k, pl.ds(b, tt, stride=bsz), :]

    u2 = u_ref[...].reshape(rows, SSM_WIDTH)
    out = _ssm_tail(ynat[...], u2, dsk_ref[...], wglu_ref[...], bglu_ref[...])
    o_ref[...] = out.reshape(bsz, tt, SSM_WIDTH)


def ssm_prompt(u, prm, tt=32):
    bsz, seq, _ = u.shape
    rows = bsz * tt
    const = lambda shape: pl.BlockSpec(shape, lambda i: (0,) * len(shape))
    return pl.pallas_call(
        functools.partial(_ssm_prompt_kernel, bsz=bsz, tt=tt),
        grid=(seq // tt,),
        in_specs=[
            pl.BlockSpec((bsz, tt, SSM_WIDTH), lambda i: (0, i, 0)),
            const((SSM_KBLK, LANES, 2 * SSM_KSTATE)),
            const((SSM_KBLK, 2 * SSM_KSTATE, LANES)),
            const((SSM_KBLK, 1, SSM_KSTATE)),
            const((SSM_KBLK, 1, SSM_KSTATE)),
            const((1, SSM_WIDTH)),
            const((SSM_WIDTH, SSM_WIDTH)),
            const((1, SSM_WIDTH)),
        ],
        out_specs=[
            pl.BlockSpec((bsz, tt, SSM_WIDTH), lambda i: (0, i, 0)),
            const((SSM_KBLK, bsz, 2 * SSM_KSTATE)),
        ],
        out_shape=[
            jax.ShapeDtypeStruct((bsz, seq, SSM_WIDTH), F32),
            jax.ShapeDtypeStruct((SSM_KBLK, bsz, 2 * SSM_KSTATE), F32),
        ],
        scratch_shapes=[
            pltpu.VMEM((SSM_KBLK, rows, LANES), F32),
            pltpu.VMEM((SSM_KBLK, rows, 2 * SSM_KSTATE), F32),
            pltpu.VMEM((SSM_KBLK, rows, 2 * SSM_KSTATE), BF16),
            pltpu.VMEM((SSM_KBLK, rows, LANES), F32),
            pltpu.VMEM((rows, SSM_WIDTH), F32),
        ],
        compiler_params=_cparams(("arbitrary",)),
        name="ssm_prompt",
    )(u, prm["wb"], prm["wc"], prm["a_re"], prm["a_im"], prm["dsk"], prm["wglu"], prm["bglu"])


def _ssm_sample_kernel(u_ref, h0_ref, wb_ref, wc_ref, are_ref, aim_ref, dsk_ref, wglu_ref, bglu_ref,
                       o_ref, st_ref):
    u = u_ref[...]
    ub = u.astype(BF16)
    ys = []
    for k in range(SSM_KBLK):
        b_t = jnp.dot(ub[:, k * LANES:(k + 1) * LANES], wb_ref[k], preferred_element_type=F32)
        h0 = h0_ref[k]
        re, im = h0[:, :SSM_KSTATE], h0[:, SSM_KSTATE:]
        ar, ai = are_ref[k], aim_ref[k]
        nre = ar * re - ai * im + b_t[:, :SSM_KSTATE]
        nim = ar * im + ai * re + b_t[:, SSM_KSTATE:]
        s = jnp.concatenate([nre, nim], axis=1)
        st_ref[k] = s
        ys.append(jnp.dot(s.astype(BF16), wc_ref[k], preferred_element_type=F32))
    y = jnp.concatenate(ys, axis=1)
    o_ref[...] = _ssm_tail(y, u, dsk_ref[...], wglu_ref[...], bglu_ref[...])


def ssm_sample(u, h0, prm):
    bsz = u.shape[0]
    return pl.pallas_call(
        _ssm_sample_kernel,
        out_shape=[jax.ShapeDtypeStruct((bsz, SSM_WIDTH), F32),
                   jax.ShapeDtypeStruct((SSM_KBLK, bsz, 2 * SSM_KSTATE), F32)],
        compiler_params=pltpu.CompilerParams(vmem_limit_bytes=VMEM_LIMIT),
        name="ssm_sample",
    )(u, h0, prm["wb"], prm["wc"], prm["a_re"], prm["a_im"], prm["dsk"], prm["wglu"], prm["bglu"])


def ssm_params(lam_re, lam_im, log_dt, b_re, b_im, c_re, c_im, d_skip, w_glu, b_glu):
    dt = jnp.exp(log_dt)[:, None]
    mag = jnp.exp(lam_re * dt)
    ab_re = mag * jnp.cos(lam_im * dt)
    ab_im = mag * jnp.sin(lam_im * dt)
    den = lam_re * lam_re + lam_im * lam_im
    nr = ab_re - 1.0
    ni = ab_im
    z_re = ((nr * lam_re + ni * lam_im) / den)[..., None]
    z_im = ((ni * lam_re - nr * lam_im) / den)[..., None]
    bb_re = z_re * b_re - z_im * b_im
    bb_im = z_re * b_im + z_im * b_re
    gpb = SSM_GROUPS // SSM_KBLK
    eye = jnp.eye(gpb, dtype=F32)

    def in_mat(bb):
        bbk = bb.reshape(SSM_KBLK, gpb, SSM_STATE, SSM_GROUP_CH)
        m = jnp.einsum("kgpc,gh->kgchp", bbk, eye)
        return m.reshape(SSM_KBLK, LANES, SSM_KSTATE)

    def out_mat(c):
        ck = c.reshape(SSM_KBLK, gpb, SSM_GROUP_CH, SSM_STATE)
        m = jnp.einsum("kgcp,gh->kgphc", ck, eye)
        return m.reshape(SSM_KBLK, SSM_KSTATE, LANES)

    wb = jnp.concatenate([in_mat(bb_re), in_mat(bb_im)], axis=2).astype(BF16)
    wc = jnp.concatenate([out_mat(c_re), -out_mat(c_im)], axis=1).astype(BF16)
    return dict(
        wb=wb, wc=wc,
        a_re=ab_re.reshape(SSM_KBLK, 1, SSM_KSTATE), a_im=ab_im.reshape(SSM_KBLK, 1, SSM_KSTATE),
        dsk=d_skip.reshape(1, SSM_WIDTH), wglu=w_glu.astype(BF16), bglu=b_glu.reshape(1, SSM_WIDTH))


def _state_to_blocks(h):
    bsz = h.shape[0]
    return h.reshape(bsz, SSM_KBLK, SSM_KSTATE).transpose(1, 0, 2)


def _blocks_to_state(s):
    bsz = s.shape[1]
    return s.transpose(1, 0, 2).reshape(bsz, SSM_GROUPS, SSM_STATE)


def _mem_attn_kernel(q_ref, k_ref, v_ref, o_ref):
    scale = MEM_HEAD_DIM ** -0.5
    for h in range(MEM_HEADS):
        sl = slice(h * MEM_HEAD_DIM, (h + 1) * MEM_HEAD_DIM)
        q = q_ref[:, sl].astype(BF16)
        k = k_ref[:, sl].astype(BF16)
        v = v_ref[:, sl].astype(BF16)
        s = lax.dot_general(q, k, (((1,), (1,)), ((), ())), preferred_element_type=F32) * scale
        m = jnp.max(s, axis=-1, keepdims=True)
        p = jnp.exp(s - m)
        den = jnp.sum(p, axis=-1, keepdims=True)
        o_ref[:, sl] = jnp.dot((p / den).astype(BF16), v, preferred_element_type=F32)


def mem_attention(q, mk, mv, tl):
    bsz, seq, _ = q.shape
    tl = min(tl, seq)
    return pl.pallas_call(
        _mem_attn_kernel,
        grid=(bsz, seq // tl),
        in_specs=[
            pl.BlockSpec((None, tl, MEM_WIDTH), lambda b, i: (b, i, 0)),
            pl.BlockSpec((None, MEM_TOKENS, MEM_WIDTH), lambda b, i: (b, 0, 0)),
            pl.BlockSpec((None, MEM_TOKENS, MEM_WIDTH), lambda b, i: (b, 0, 0)),
        ],
        out_specs=pl.BlockSpec((None, tl, MEM_WIDTH), lambda b, i: (b, i, 0)),
        out_shape=jax.ShapeDtypeStruct((bsz, seq, MEM_WIDTH), F32),
        compiler_params=_cparams(("arbitrary", "arbitrary")),
        name="mem_attention",
    )(q, mk, mv)


def _route(logits):
    lane = lax.broadcasted_iota(jnp.int32, logits.shape, 1)
    lanef = lane.astype(F32)
    none = float(ROUTER_COLS)
    neg = -jnp.inf
    gl = jnp.where(lane < MOE_GROUPS, logits, neg)
    gmax = jnp.max(gl, axis=-1, keepdims=True)
    grp = jnp.min(jnp.where(gl == gmax, lanef, none), axis=-1, keepdims=True)
    p_grp = 1.0 / jnp.sum(jnp.exp(gl - gmax), axis=-1, keepdims=True)
    lo = MOE_GROUPS + grp * EXPERTS_PER_GROUP
    el = jnp.where(jnp.logical_and(lanef >= lo, lanef < lo + EXPERTS_PER_GROUP), logits, neg)
    v1 = jnp.max(el, axis=-1, keepdims=True)
    i1 = jnp.min(jnp.where(el == v1, lanef, none), axis=-1, keepdims=True)
    el2 = jnp.where(lanef == i1, neg, el)
    v2 = jnp.max(el2, axis=-1, keepdims=True)
    i2 = jnp.min(jnp.where(el2 == v2, lanef, none), axis=-1, keepdims=True)
    t = jnp.exp(v2 - v1)
    g1 = p_grp / (1.0 + t)
    g2 = g1 * t
    out = jnp.where(lane == 0, g1, 0.0)
    out = jnp.where(lane == 1, g2, out)
    out = jnp.where(lane == 2, i1 - MOE_GROUPS, out)
    return jnp.where(lane == 3, i2 - MOE_GROUPS, out)


def _merge_kernel(att_ref, ssm_ref, mem_ref, gate_ref, x_ref, wa_ref, ws_ref, wm_ref, wo_ref,
                  n2_ref, wr_ref, br_ref, h_ref, hn_ref, lg_ref, *, tm):
    a = jnp.dot(att_ref[...].astype(BF16), wa_ref[...], preferred_element_type=F32)
    merged = gate_ref[:, 0:D_MODEL] * a
    s = jnp.dot(ssm_ref[...].astype(BF16), ws_ref[...], preferred_element_type=F32)
    merged = merged + gate_ref[:, D_MODEL:2 * D_MODEL] * s
    m = jnp.dot(mem_ref[...].astype(BF16), wm_ref[...], preferred_element_type=F32)
    merged = merged + gate_ref[:, 2 * D_MODEL:3 * D_MODEL] * m
    h = x_ref[...] + jnp.dot(merged.astype(BF16), wo_ref[...], preferred_element_type=F32)
    h_ref[...] = h
    hn = _rms(h, n2_ref[...])
    for j in range(D_MODEL // LANES):
        hn_ref[pl.ds(j, tm, stride=SUBLANES), :] = hn[:, j * LANES:(j + 1) * LANES]
    logits = jnp.dot(hn.astype(BF16), wr_ref[...], preferred_element_type=F32) + br_ref[...]
    lg_ref[...] = _route(logits)


def merge(att, ssm, mem, gates, x, wts, tm):
    n = x.shape[0]
    tm = min(tm, n)
    row = lambda w: pl.BlockSpec((tm, w), lambda i: (i, 0))
    const = lambda a: pl.BlockSpec(a.shape, lambda i: (0, 0))
    ws = [wts["wa"], wts["ws"], wts["wm"], wts["wo"], wts["n2"], wts["wr"], wts["br"]]
    return pl.pallas_call(
        functools.partial(_merge_kernel, tm=tm),
        grid=(n // tm,),
        in_specs=[row(ATT_OUT), row(SSM_WIDTH), row(MEM_WIDTH), row(N_BRANCHES * D_MODEL), row(D_MODEL)]
                 + [const(w) for w in ws],
        out_specs=[row(D_MODEL), pl.BlockSpec((tm * SUBLANES, LANES), lambda i: (i, 0)), row(ROUTER_COLS)],
        out_shape=[jax.ShapeDtypeStruct((n, D_MODEL), F32),
                   jax.ShapeDtypeStruct((n * SUBLANES, LANES), F32),
                   jax.ShapeDtypeStruct((n, ROUTER_COLS), F32)],
        compiler_params=_cparams(("arbitrary",)),
        name="merge",
    )(att, ssm, mem, gates, x, *ws)


def _moe_kernel(be_ref, idx_hbm, x_hbm, w1_ref, w3_ref, w2_ref, out_hbm,
                idx_smem, xbuf, xb, ybuf, w1b, w3b, w2b, gsem, ssem, isem, *, bm):
    blk = pl.program_id(0)
    nblk = pl.num_programs(0)
    slot = blk % 2
    tiles = D_MODEL // LANES

    def idx_copy(b, s):
        return pltpu.make_async_copy(idx_hbm.at[b], idx_smem.at[s], isem.at[s])

    def issue_gather(s):
        def body(i, c):
            tok = idx_smem[s, lax.shift_right_logical(i, 7), jnp.bitwise_and(i, LANES - 1)]
            r0 = pl.multiple_of(i * SUBLANES, SUBLANES)
            pltpu.make_async_copy(x_hbm.at[tok], xbuf.at[pl.ds(r0, SUBLANES)], gsem).start()
            return c
        lax.fori_loop(0, bm, body, 0, unroll=8)

    def issue_scatter(s):
        def body(i, c):
            j = i + bm
            pid = idx_smem[s, lax.shift_right_logical(j, 7), jnp.bitwise_and(j, LANES - 1)]
            r0 = pl.multiple_of(i * SUBLANES, SUBLANES)
            pltpu.make_async_copy(ybuf.at[pl.ds(r0, SUBLANES)], out_hbm.at[pid], ssem).start()
            return c
        lax.fori_loop(0, bm, body, 0, unroll=8)

    def wait_rows(sem, buf):
        pltpu.make_async_copy(buf, buf, sem).wait()

    @pl.when(blk == 0)
    def _():
        c = idx_copy(0, 0)
        c.start()
        c.wait()
        issue_gather(0)

    @pl.when(blk + 1 < nblk)
    def _():
        idx_copy(blk + 1, 1 - slot).start()

    prev = jnp.maximum(blk - 1, 0)
    changed = jnp.logical_or(blk == 0, be_ref[blk] != be_ref[prev])

    @pl.when(changed)
    def _():
        w1b[...] = w1_ref[...].astype(BF16)
        w3b[...] = w3_ref[...].astype(BF16)
        w2b[...] = w2_ref[...].astype(BF16)

    wait_rows(gsem, xbuf)
    for j in range(tiles):
        xb[:, j * LANES:(j + 1) * LANES] = xbuf[pl.ds(j, bm, stride=SUBLANES), :].astype(BF16)

    @pl.when(blk + 1 < nblk)
    def _():
        idx_copy(blk + 1, 1 - slot).wait()
        issue_gather(1 - slot)

    x = xb[...]
    h1 = jnp.dot(x, w1b[...], preferred_element_type=F32)
    h3 = jnp.dot(x, w3b[...], preferred_element_type=F32)
    act = (jax.nn.silu(h1) * h3).astype(BF16)
    y = jnp.dot(act, w2b[...], preferred_element_type=F32)

    @pl.when(blk > 0)
    def _():
        wait_rows(ssem, ybuf)

    for j in range(tiles):
        ybuf[pl.ds(j, bm, stride=SUBLANES), :] = y[:, j * LANES:(j + 1) * LANES]
    issue_scatter(slot)

    @pl.when(blk == nblk - 1)
    def _():
        wait_rows(ssem, ybuf)


def moe_experts(block_expert, idx, hn3, w1, w3, w2, n_out_rows, bm):
    n_blocks = idx.shape[0]
    grid_spec = pltpu.PrefetchScalarGridSpec(
        num_scalar_prefetch=1,
        grid=(n_blocks,),
        in_specs=[
            pl.BlockSpec(memory_space=pl.ANY),
            pl.BlockSpec(memory_space=pl.ANY),
            pl.BlockSpec((None, D_MODEL, D_EXPERT), lambda b, be: (be[b], 0, 0)),
            pl.BlockSpec((None, D_MODEL, D_EXPERT), lambda b, be: (be[b], 0, 0)),
            pl.BlockSpec((None, D_EXPERT, D_MODEL), lambda b, be: (be[b], 0, 0)),
        ],
        out_specs=pl.BlockSpec(memory_space=pl.ANY),
        scratch_shapes=[
            pltpu.SMEM((2, SUBLANES, LANES), jnp.int32),
            pltpu.VMEM((bm * SUBLANES, LANES), F32),
            pltpu.VMEM((bm, D_MODEL), BF16),
            pltpu.VMEM((bm * SUBLANES, LANES), F32),
            pltpu.VMEM((D_MODEL, D_EXPERT), BF16),
            pltpu.VMEM((D_MODEL, D_EXPERT), BF16),
            pltpu.VMEM((D_EXPERT, D_MODEL), BF16),
            pltpu.SemaphoreType.DMA,
            pltpu.SemaphoreType.DMA,
            pltpu.SemaphoreType.DMA((2,)),
        ],
    )
    return pl.pallas_call(
        functools.partial(_moe_kernel, bm=bm),
        grid_spec=grid_spec,
        out_shape=jax.ShapeDtypeStruct((n_out_rows, SUBLANES, LANES), F32),
        compiler_params=_cparams(("arbitrary",)),
        name="moe_experts",
    )(block_expert, idx, hn3, w1, w3, w2)


def route_tables(route, bm):
    n_tok = route.shape[0]
    n_pairs = n_tok * TOP_K
    e_flat = route[:, 2:2 + TOP_K].astype(jnp.int32).reshape(n_pairs)
    experts = jnp.arange(N_EXPERTS, dtype=jnp.int32)
    counts = jnp.sum(e_flat[:, None] == experts[None, :], axis=0, dtype=jnp.int32)
    padded = (counts + bm - 1) // bm * bm
    pad_ends = jnp.cumsum(padded)
    n_blocks = -(-n_pairs // bm) + N_EXPERTS
    row0 = jnp.arange(n_blocks, dtype=jnp.int32) * bm
    block_expert = jnp.minimum(jnp.sum(pad_ends[None, :] <= row0[:, None], axis=1, dtype=jnp.int32),
                               N_EXPERTS - 1)
    fill_id = jnp.arange(n_blocks * bm - n_pairs, dtype=jnp.int32)
    fill_e = jnp.minimum(fill_id // bm, N_EXPERTS - 1)
    fill_used = jnp.logical_and(fill_id < N_EXPERTS * bm, fill_id % bm < (padded - counts)[fill_e])
    keys = jnp.concatenate([e_flat * 2, jnp.where(fill_used, fill_e * 2 + 1, 2 * N_EXPERTS)])
    vals = jnp.concatenate([jnp.arange(n_pairs, dtype=jnp.int32), jnp.full(fill_id.shape, -1, jnp.int32)])
    _, win = lax.sort((keys, vals), num_keys=1, is_stable=True)
    win = win.reshape(n_blocks, bm)
    pos = jnp.arange(bm, dtype=jnp.int32)[None, :]
    valid = win >= 0
    tok = win // TOP_K
    src = jnp.where(valid, tok, 0)
    dst = jnp.where(valid, (win % TOP_K) * n_tok + tok, n_pairs + pos)
    fill = jnp.zeros((n_blocks, SUBLANES * LANES - 2 * bm), jnp.int32)
    idx = jnp.concatenate([src, dst, fill], axis=1).reshape(n_blocks, SUBLANES, LANES)
    return block_expert, idx, n_pairs + bm


def _final_kernel(h_ref, p0_ref, p1_ref, gw_ref, nf_ref, y_ref, *, tm):
    tiles = D_MODEL // LANES
    p0 = jnp.concatenate([p0_ref[pl.ds(j, tm, stride=SUBLANES), :] for j in range(tiles)], axis=1)
    p1 = jnp.concatenate([p1_ref[pl.ds(j, tm, stride=SUBLANES), :] for j in range(tiles)], axis=1)
    gw = gw_ref[...]
    h = h_ref[...] + (gw[:, 0:1] * p0 + gw[:, 1:2] * p1)
    y_ref[...] = _rms(h, nf_ref[...])


def final(h, pairs2d, gate_pad, norm_f, tm):
    n = h.shape[0]
    tm = min(tm, n)
    nb = n // tm
    row = lambda w: pl.BlockSpec((tm, w), lambda i: (i, 0))
    return pl.pallas_call(
        functools.partial(_final_kernel, tm=tm),
        grid=(nb,),
        in_specs=[
            row(D_MODEL),
            pl.BlockSpec((tm * SUBLANES, LANES), lambda i: (i, 0)),
            pl.BlockSpec((tm * SUBLANES, LANES), lambda i: (nb + i, 0)),
            row(LANES),
            pl.BlockSpec((1, D_MODEL), lambda i: (0, 0)),
        ],
        out_specs=row(D_MODEL),
        out_shape=jax.ShapeDtypeStruct((n, D_MODEL), F32),
        compiler_params=_cparams(("arbitrary",)),
        name="final",
    )(h, pairs2d, pairs2d, gate_pad, norm_f.reshape(1, D_MODEL))


def _alibi_slopes():
    h = jnp.arange(1, N_ATT_HEADS + 1, dtype=F32)
    return jnp.exp2(-8.0 * h / N_ATT_HEADS)


def _token_stage(att, ssm, mem, gates, x2, wts, w1, w3, w2, norm_f, tm, bm):
    n = x2.shape[0]
    h, hn2d, route = merge(att, ssm, mem, gates, x2, wts, tm)
    block_expert, idx, out_rows = route_tables(route, bm)
    pairs = moe_experts(block_expert, idx, hn2d.reshape(n, SUBLANES, LANES), w1, w3, w2, out_rows, bm)
    return final(h, pairs.reshape(out_rows * SUBLANES, LANES), route, norm_f, tm)


def kernel(x_prompt, x_sample, cache_swa0_k, cache_swa0_v, cache_swa1_k, cache_swa1_v, cache_swa2_k, cache_swa2_v, cache_mem_k, cache_mem_v, state_ssm_re, state_ssm_im, mem_prompt, norm1, w_in, lam_re, lam_im, log_dt, ssm_b_re, ssm_b_im, ssm_c_re, ssm_c_im, ssm_d, w_glu, b_glu, w_att_out, w_ssm_out, w_mem_out, w_o, norm_mem, w_mk, w_mv, norm2, w_grp, b_grp, w_exp, b_exp, w1, w3, w2, norm_f):
    assert norm1.shape[0] == 1, "single-layer trunk"
    bsz, seq, _ = x_prompt.shape
    sbz = x_sample.shape[0]
    n_p = bsz * seq
    caches = (cache_swa0_k[0], cache_swa0_v[0], cache_swa1_k[0], cache_swa1_v[0], cache_swa2_k[0], cache_swa2_v[0])

    w_in_b = w_in[0].astype(BF16)
    w_mkv_b = jnp.concatenate([w_mk[0], w_mv[0]], axis=1).astype(BF16)
    wr = jnp.zeros((D_MODEL, ROUTER_COLS), F32)
    wr = wr.at[:, :MOE_GROUPS].set(w_grp[0]).at[:, MOE_GROUPS:MOE_GROUPS + N_EXPERTS].set(w_exp[0])
    br = jnp.zeros((1, ROUTER_COLS), F32)
    br = br.at[0, :MOE_GROUPS].set(b_grp[0]).at[0, MOE_GROUPS:MOE_GROUPS + N_EXPERTS].set(b_exp[0])
    wts = dict(wa=w_att_out[0].astype(BF16), ws=w_ssm_out[0].astype(BF16), wm=w_mem_out[0].astype(BF16),
               wo=w_o[0].astype(BF16), n2=norm2[0].reshape(1, D_MODEL), wr=wr.astype(BF16), br=br)
    prm = ssm_params(lam_re[0], lam_im[0], log_dt[0], ssm_b_re[0], ssm_b_im[0], ssm_c_re[0], ssm_c_im[0],
                     ssm_d[0], w_glu[0], b_glu[0])
    slopes = _alibi_slopes()
    splits = (QKV_COLS, SSM_WIDTH, MEM_WIDTH, N_BRANCHES * D_MODEL)
    sig = (False, False, False, True)

    x2 = x_prompt.reshape(n_p, D_MODEL)
    qkv, u, mq, gates = norm_matmul(x2, norm1[0], w_in_b, splits, sig, tm=256)
    qkv3 = qkv.reshape(bsz, seq, QKV_COLS)
    att = attn_prompt(qkv3, slopes)
    ssm, st = ssm_prompt(u.reshape(bsz, seq, SSM_WIDTH), prm)
    mk, mv = norm_matmul(mem_prompt.reshape(bsz * MEM_TOKENS, D_MODEL), norm_mem[0], w_mkv_b,
                         (MEM_WIDTH, MEM_WIDTH), (False, False), tm=256)
    mk3 = mk.reshape(bsz, MEM_TOKENS, MEM_WIDTH)
    mv3 = mv.reshape(bsz, MEM_TOKENS, MEM_WIDTH)
    mem = mem_attention(mq.reshape(bsz, seq, MEM_WIDTH), mk3, mv3, tl=512)
    y_p = _token_stage(att.reshape(n_p, ATT_OUT), ssm.reshape(n_p, SSM_WIDTH), mem.reshape(n_p, MEM_WIDTH),
                       gates, x2, wts, w1[0], w3[0], w2[0], norm_f, tm=256, bm=256)

    xs2 = x_sample.reshape(sbz, D_MODEL)
    qkv_s, u_s, mq_s, gates_s = norm_matmul(xs2, norm1[0], w_in_b, splits, sig, tm=sbz)
    qkv5 = qkv_s.reshape(sbz, len(ATT_GROUPS), 3, HEADS_PER_GROUP, HEAD_DIM)
    att_s, *new_caches = attn_sample(qkv5, [jnp.transpose(c, (0, 2, 3, 1)) for c in caches])
    att_s = att_s.reshape(sbz, ATT_OUT)
    h0 = jnp.concatenate([_state_to_blocks(state_ssm_re[0]), _state_to_blocks(state_ssm_im[0])], axis=2)
    ssm_s, st_s = ssm_sample(u_s, h0, prm)
    cmk = cache_mem_k[0].reshape(sbz, MEM_TOKENS, MEM_WIDTH)
    cmv = cache_mem_v[0].reshape(sbz, MEM_TOKENS, MEM_WIDTH)
    mem_s = mem_attention(mq_s.reshape(sbz, 1, MEM_WIDTH), cmk, cmv, tl=1).reshape(sbz, MEM_WIDTH)
    y_s = _token_stage(att_s, ssm_s, mem_s, gates_s, xs2, wts, w1[0], w3[0], w2[0], norm_f, tm=sbz, bm=16)

    outs = [y_p.reshape(bsz, seq, D_MODEL), y_s.reshape(sbz, 1, D_MODEL)]
    for g, (window, _) in enumerate(ATT_GROUPS):
        keep = min(window, seq)
        for part in (1, 2):
            c0 = g * QKV_GROUP + part * ATT_OUT
            outs.append(qkv3[:, seq - keep:, c0:c0 + ATT_OUT].reshape(1, bsz, keep, HEADS_PER_GROUP, HEAD_DIM))
    outs.append(mk3.reshape(1, bsz, MEM_TOKENS, MEM_HEADS, MEM_HEAD_DIM))
    outs.append(mv3.reshape(1, bsz, MEM_TOKENS, MEM_HEADS, MEM_HEAD_DIM))
    outs.append(_blocks_to_state(st[:, :, :SSM_KSTATE])[None])
    outs.append(_blocks_to_state(st[:, :, SSM_KSTATE:])[None])
    outs.extend(jnp.transpose(c, (0, 3, 1, 2))[None] for c in new_caches)
    outs.append(_blocks_to_state(st_s[:, :, :SSM_KSTATE])[None])
    outs.append(_blocks_to_state(st_s[:, :, SSM_KSTATE:])[None])
    return tuple(outs)
```

```python
import functools
import math

import jax
import jax.numpy as jnp
from jax import lax
from jax.experimental import pallas as pl
from jax.experimental.pallas import tpu as pltpu

F32 = jnp.float32
BF16 = jnp.bfloat16

D_MODEL = 1024
ATT_GROUPS = ((128, 1), (512, 4), (2048, 16))
HEADS_PER_GROUP = 4
HEAD_DIM = 64
N_ATT_HEADS = len(ATT_GROUPS) * HEADS_PER_GROUP
ATT_OUT = HEADS_PER_GROUP * HEAD_DIM
QKV_GROUP = 3 * ATT_OUT
QKV_COLS = len(ATT_GROUPS) * QKV_GROUP
SSM_WIDTH = D_MODEL // 2
SSM_GROUP_CH = 16
SSM_GROUPS = SSM_WIDTH // SSM_GROUP_CH
SSM_STATE = 64
SSM_NSTATE = SSM_GROUPS * SSM_STATE
MEM_TOKENS = 256
MEM_HEADS = 4
MEM_HEAD_DIM = D_MODEL // 8
MEM_WIDTH = MEM_HEADS * MEM_HEAD_DIM
N_BRANCHES = 3
MOE_GROUPS = 4
EXPERTS_PER_GROUP = 8
N_EXPERTS = MOE_GROUPS * EXPERTS_PER_GROUP
TOP_K = 2
D_EXPERT = D_MODEL // 2
RMS_EPS = 1e-6

LANES = 128
SUBLANES = 8
Q_TILE = 128
ROUTER_COLS = 128
NEG_BIG = -1e30
VMEM_LIMIT = 56 * 1024 * 1024


def _cparams(sem):
    return pltpu.CompilerParams(dimension_semantics=sem, vmem_limit_bytes=VMEM_LIMIT)


def _rms(x, g):
    ms = jnp.mean(x * x, axis=-1, keepdims=True)
    return (x * lax.rsqrt(ms + RMS_EPS)) * g


def _norm_matmul_kernel(x_ref, g_ref, w_ref, *out_refs, splits, sigmoid, chunk):
    xb = _rms(x_ref[...], g_ref[...]).astype(BF16)
    c0 = 0
    for o_ref, n, sg in zip(out_refs, splits, sigmoid):
        for j in range(0, n, chunk):
            w = min(chunk, n - j)
            y = jnp.dot(xb, w_ref[:, c0 + j:c0 + j + w], preferred_element_type=F32)
            if sg:
                y = jax.nn.sigmoid(y)
            o_ref[:, j:j + w] = y.astype(o_ref.dtype)
        c0 += n


def norm_matmul(x, g, w_bf16, splits, sigmoid, tm, dtypes=None):
    n, d = x.shape
    tm = min(tm, n)
    kern = functools.partial(_norm_matmul_kernel, splits=tuple(splits), sigmoid=tuple(sigmoid), chunk=512)
    return pl.pallas_call(
        kern,
        grid=(n // tm,),
        in_specs=[
            pl.BlockSpec((tm, d), lambda i: (i, 0)),
            pl.BlockSpec((1, d), lambda i: (0, 0)),
            pl.BlockSpec((d, w_bf16.shape[1]), lambda i: (0, 0), pipeline_mode=pl.Buffered(1)),
        ],
        out_specs=[pl.BlockSpec((tm, s), lambda i: (i, 0)) for s in splits],
        out_shape=[jax.ShapeDtypeStruct((n, s), dt) for s, dt in zip(splits, dtypes or (F32,) * len(splits))],
        compiler_params=_cparams(("arbitrary",)),
        name="norm_matmul",
    )(x, g.reshape(1, d), w_bf16)


def _attn_prompt_kernel(slopes_ref, *refs, seq):
    qkv_refs = refs[:9]
    o_ref = refs[9]
    o_scr, l_scr = refs[10], refs[11]
    hp = pl.program_id(1)
    scale = HEAD_DIM ** -0.5
    row = lax.broadcasted_iota(jnp.int32, (Q_TILE, 2 * Q_TILE), 0)
    col = lax.broadcasted_iota(jnp.int32, (Q_TILE, 2 * Q_TILE), 1)
    dist = (row + Q_TILE - col)
    in_window = jnp.logical_and(dist >= 0, dist <= Q_TILE)
    distf = dist.astype(F32)
    lane = lax.broadcasted_iota(jnp.int32, (Q_TILE, LANES), 1)
    head_a = lane < HEAD_DIM

    for g, (_, dil) in enumerate(ATT_GROUPS):
        q_ref, k_ref, v_ref = qkv_refs[3 * g:3 * g + 3]
        sub_len = seq // dil
        n_blk = sub_len // Q_TILE
        slope_a = slopes_ref[g * HEADS_PER_GROUP + 2 * hp] * float(dil)
        slope_b = slopes_ref[g * HEADS_PER_GROUP + 2 * hp + 1] * float(dil)

        def rows(start):
            if dil == 1:
                return pl.ds(start, Q_TILE)
            return pl.ds(start, Q_TILE, stride=dil)

        def body(it, carry, q_ref=q_ref, k_ref=k_ref, v_ref=v_ref, n_blk=n_blk, dil=dil,
                 slope_a=slope_a, slope_b=slope_b, rows=rows, g=g):
            r = it // n_blk
            blk = it % n_blk
            prev = jnp.maximum(blk - 1, 0)
            q_start = r + dil * Q_TILE * blk
            p_start = r + dil * Q_TILE * prev
            q = q_ref[rows(q_start), :]
            k2 = jnp.concatenate([k_ref[rows(p_start), :], k_ref[rows(q_start), :]], axis=0).astype(BF16)
            v2 = jnp.concatenate([v_ref[rows(p_start), :], v_ref[rows(q_start), :]], axis=0).astype(BF16)
            valid = jnp.logical_and(in_window, jnp.logical_or(col >= Q_TILE, blk > 0))
            outs, lses = [], []
            for is_a, slope in ((True, slope_a), (False, slope_b)):
                hmask = head_a if is_a else jnp.logical_not(head_a)
                qh = jnp.where(hmask, q, 0.0).astype(BF16)
                s = lax.dot_general(qh, k2, (((1,), (1,)), ((), ())), preferred_element_type=F32)
                s = s * scale - slope * distf
                s = jnp.where(valid, s, NEG_BIG)
                m = jnp.max(s, axis=-1, keepdims=True)
                p = jnp.exp(s - m)
                den = jnp.sum(p, axis=-1, keepdims=True)
                pv = jnp.dot(p.astype(BF16), v2, preferred_element_type=F32)
                outs.append(pv / den)
                lses.append(m + jnp.log(den))
            o_scr[g, rows(q_start), :] = jnp.where(head_a, outs[0], outs[1])
            l_scr[g, rows(q_start), :] = jnp.where(head_a, lses[0], lses[1])
            return carry

        lax.fori_loop(0, dil * n_blk, body, 0, unroll=4)

    def combine(c, carry):
        sl = pl.ds(pl.multiple_of(c * 256, 256), 256)
        l0, l1, l2 = l_scr[0, sl, :], l_scr[1, sl, :], l_scr[2, sl, :]
        m = jnp.maximum(jnp.maximum(l0, l1), l2)
        e0, e1, e2 = jnp.exp(l0 - m), jnp.exp(l1 - m), jnp.exp(l2 - m)
        tot = e0 + e1 + e2
        att = (e0 * o_scr[0, sl, :] + e1 * o_scr[1, sl, :] + e2 * o_scr[2, sl, :]) / tot
        o_ref[sl, :] = att.astype(o_ref.dtype)
        return carry

    lax.fori_loop(0, seq // 256, combine, 0)


def attn_prompt(qkv, slopes):
    bsz, seq, _ = qkv.shape
    in_specs = []
    for g in range(len(ATT_GROUPS)):
        for part in range(3):
            base = (g * QKV_GROUP + part * ATT_OUT) // LANES
            in_specs.append(pl.BlockSpec((None, seq, LANES),
                                         lambda b, h, sl, base=base: (b, 0, base + h)))
    grid_spec = pltpu.PrefetchScalarGridSpec(
        num_scalar_prefetch=1,
        grid=(bsz, ATT_OUT // LANES),
        in_specs=in_specs,
        out_specs=pl.BlockSpec((None, seq, LANES), lambda b, h, sl: (b, 0, h)),
        scratch_shapes=[pltpu.VMEM((3, seq, LANES), F32), pltpu.VMEM((3, seq, LANES), F32)],
    )
    return pl.pallas_call(
        functools.partial(_attn_prompt_kernel, seq=seq),
        grid_spec=grid_spec,
        out_shape=jax.ShapeDtypeStruct((bsz, seq, ATT_OUT), BF16),
        compiler_params=_cparams(("arbitrary", "arbitrary")),
        name="attn_prompt",
    )(slopes, *([qkv] * 9))


def _attn_sample_kernel(qkv_ref, col_ref, *refs):
    cache_refs = refs[:6]
    o_ref = refs[6]
    new_refs = refs[7:13]
    scale = HEAD_DIM ** -0.5
    n_grp = len(ATT_GROUPS)
    outs = [[None] * HEADS_PER_GROUP for _ in range(n_grp)]
    lses = [[None] * HEADS_PER_GROUP for _ in range(n_grp)]
    for g, (window, dil) in enumerate(ATT_GROUPS):
        lane = lax.broadcasted_iota(jnp.int32, (1, window), 1)
        on_grid = jnp.bitwise_and(lane, dil - 1) == 0
        dist = (window - lane).astype(F32)
        last = lax.broadcasted_iota(jnp.int32, (HEAD_DIM, window), 1) == window - 1
        for h in range(HEADS_PER_GROUP):
            slope = 2.0 ** (-8.0 * (g * HEADS_PER_GROUP + h + 1) / N_ATT_HEADS)
            q = qkv_ref[g, 0][h:h + 1, :].astype(BF16)
            kn = qkv_ref[g, 1][h:h + 1, :].astype(BF16).astype(F32)
            vn = qkv_ref[g, 2][h:h + 1, :].astype(BF16).astype(F32)
            kt = cache_refs[2 * g][h]
            vt = cache_refs[2 * g + 1][h]
            s = jnp.dot(q, kt.astype(BF16), preferred_element_type=F32) * scale - slope * dist
            s = jnp.where(on_grid, s, NEG_BIG)
            sn = jnp.sum(q.astype(F32) * kn, axis=-1, keepdims=True) * scale
            m = jnp.maximum(jnp.max(s, axis=-1, keepdims=True), sn)
            p = jnp.exp(s - m)
            pn = jnp.exp(sn - m)
            den = jnp.sum(p, axis=-1, keepdims=True) + pn
            pb = (p / den).astype(BF16)
            pnb = (pn / den).astype(BF16).astype(F32)
            pv = lax.dot_general(pb, vt.astype(BF16), (((1,), (1,)), ((), ())), preferred_element_type=F32)
            outs[g][h] = pv + pnb * vn
            lses[g][h] = m + jnp.log(den)
            new_refs[2 * g][h] = jnp.where(last, col_ref[g, 0, h], pltpu.roll(kt, window - 1, axis=1))
            new_refs[2 * g + 1][h] = jnp.where(last, col_ref[g, 1, h], pltpu.roll(vt, window - 1, axis=1))
    for h in range(HEADS_PER_GROUP):
        m = jnp.maximum(jnp.maximum(lses[0][h], lses[1][h]), lses[2][h])
        es = [jnp.exp(lses[g][h] - m) for g in range(n_grp)]
        num = es[0] * outs[0][h] + es[1] * outs[1][h] + es[2] * outs[2][h]
        o_ref[h:h + 1, :] = num / (es[0] + es[1] + es[2])


def attn_sample(qkv5, caches_t):
    bsz = qkv5.shape[0]
    cols = qkv5[:, :, 1:3].reshape(bsz, len(ATT_GROUPS), 2, HEADS_PER_GROUP, HEAD_DIM, 1)
    in_specs = [pl.BlockSpec((None, 3, 3, HEADS_PER_GROUP, HEAD_DIM), lambda b: (b, 0, 0, 0, 0)),
                pl.BlockSpec((None, 3, 2, HEADS_PER_GROUP, HEAD_DIM, 1), lambda b: (b, 0, 0, 0, 0, 0))]
    cache_specs = [pl.BlockSpec((None,) + c.shape[1:], lambda b: (b, 0, 0, 0)) for c in caches_t]
    return pl.pallas_call(
        _attn_sample_kernel,
        grid=(bsz,),
        in_specs=in_specs + cache_specs,
        out_specs=[pl.BlockSpec((None, HEADS_PER_GROUP, HEAD_DIM), lambda b: (b, 0, 0))] + cache_specs,
        out_shape=[jax.ShapeDtypeStruct((bsz, HEADS_PER_GROUP, HEAD_DIM), F32)]
                  + [jax.ShapeDtypeStruct(c.shape, F32) for c in caches_t],
        compiler_params=_cparams(("arbitrary",)),
        name="attn_sample",
    )(qkv5, cols, *caches_t)


SSM_KBLK = SSM_WIDTH // LANES
SSM_KSTATE = SSM_NSTATE // SSM_KBLK


def _ssm_tail(y, u, dsk, wglu, bglu):
    y = y + dsk * u
    z = jax.nn.gelu(y)
    gl = jnp.dot(z.astype(BF16), wglu, preferred_element_type=F32) + bglu
    return z * jax.nn.sigmoid(gl)


def _ssm_prompt_kernel(u_ref, wb_ref, wc_ref, are_ref, aim_ref, dsk_ref, wglu_ref, bglu_ref,
                       o_ref, st_ref, uperm, bu, sbf, yperm, ynat, *, bsz, tt):
    rows = bsz * tt

    @pl.when(pl.program_id(0) == 0)
    def _():
        st_ref[...] = jnp.zeros_like(st_ref)

    for b in range(bsz):
        ub = u_ref[b]
        for k in range(SSM_KBLK):
            uperm[k, pl.ds(b, tt, stride=bsz), :] = ub[:, k * LANES:(k + 1) * LANES]

    for k in range(SSM_KBLK):
        bu[k] = jnp.dot(uperm[k].astype(BF16), wb_ref[k], preferred_element_type=F32)

    for k in range(SSM_KBLK):
        ar = jnp.broadcast_to(are_ref[k], (bsz, SSM_KSTATE))
        ai = jnp.broadcast_to(aim_ref[k], (bsz, SSM_KSTATE))
        st = st_ref[k]

        def step(t, carry, k=k, ar=ar, ai=ai):
            re, im = carry
            r0 = pl.multiple_of(t * bsz, bsz)
            b_t = bu[k, pl.ds(r0, bsz), :]
            nre = ar * re - ai * im + b_t[:, :SSM_KSTATE]
            nim = ar * im + ai * re + b_t[:, SSM_KSTATE:]
            sbf[k, pl.ds(r0, bsz), :] = jnp.concatenate([nre, nim], axis=1).astype(BF16)
            return nre, nim

        re, im = lax.fori_loop(0, tt, step, (st[:, :SSM_KSTATE], st[:, SSM_KSTATE:]), unroll=2)
        st_ref[k] = jnp.concatenate([re, im], axis=1)

    for k in range(SSM_KBLK):
        yperm[k] = jnp.dot(sbf[k], wc_ref[k], preferred_element_type=F32)

    for b in range(bsz):
        for k in range(SSM_KBLK):
            ynat[b * tt:(b + 1) * tt, k * LANES:(k + 1) * LANES] = yperm[k, pl.ds(b, tt, stride=bsz), :]

    u2 = u_ref[...].reshape(rows, SSM_WIDTH)
    out = _ssm_tail(ynat[...], u2, dsk_ref[...], wglu_ref[...], bglu_ref[...])
    o_ref[...] = out.reshape(bsz, tt, SSM_WIDTH).astype(o_ref.dtype)


def ssm_prompt(u, prm, tt=32):
    bsz, seq, _ = u.shape
    rows = bsz * tt
    const = lambda shape: pl.BlockSpec(shape, lambda i: (0,) * len(shape))
    return pl.pallas_call(
        functools.partial(_ssm_prompt_kernel, bsz=bsz, tt=tt),
        grid=(seq // tt,),
        in_specs=[
            pl.BlockSpec((bsz, tt, SSM_WIDTH), lambda i: (0, i, 0)),
            const((SSM_KBLK, LANES, 2 * SSM_KSTATE)),
            const((SSM_KBLK, 2 * SSM_KSTATE, LANES)),
            const((SSM_KBLK, 1, SSM_KSTATE)),
            const((SSM_KBLK, 1, SSM_KSTATE)),
            const((1, SSM_WIDTH)),
            const((SSM_WIDTH, SSM_WIDTH)),
            const((1, SSM_WIDTH)),
        ],
        out_specs=[
            pl.BlockSpec((bsz, tt, SSM_WIDTH), lambda i: (0, i, 0)),
            const((SSM_KBLK, bsz, 2 * SSM_KSTATE)),
        ],
        out_shape=[
            jax.ShapeDtypeStruct((bsz, seq, SSM_WIDTH), BF16),
            jax.ShapeDtypeStruct((SSM_KBLK, bsz, 2 * SSM_KSTATE), F32),
        ],
        scratch_shapes=[
            pltpu.VMEM((SSM_KBLK, rows, LANES), F32),
            pltpu.VMEM((SSM_KBLK, rows, 2 * SSM_KSTATE), F32),
            pltpu.VMEM((SSM_KBLK, rows, 2 * SSM_KSTATE), BF16),
            pltpu.VMEM((SSM_KBLK, rows, LANES), F32),
            pltpu.VMEM((rows, SSM_WIDTH), F32),
        ],
        compiler_params=_cparams(("arbitrary",)),
        name="ssm_prompt",
    )(u, prm["wb"], prm["wc"], prm["a_re"], prm["a_im"], prm["dsk"], prm["wglu"], prm["bglu"])


def _ssm_sample_kernel(u_ref, h0_ref, wb_ref, wc_ref, are_ref, aim_ref, dsk_ref, wglu_ref, bglu_ref,
                       o_ref, st_ref):
    u = u_ref[...]
    ub = u.astype(BF16)
    ys = []
    for k in range(SSM_KBLK):
        b_t = jnp.dot(ub[:, k * LANES:(k + 1) * LANES], wb_ref[k], preferred_element_type=F32)
        h0 = h0_ref[k]
        re, im = h0[:, :SSM_KSTATE], h0[:, SSM_KSTATE:]
        ar, ai = are_ref[k], aim_ref[k]
        nre = ar * re - ai * im + b_t[:, :SSM_KSTATE]
        nim = ar * im + ai * re + b_t[:, SSM_KSTATE:]
        s = jnp.concatenate([nre, nim], axis=1)
        st_ref[k] = s
        ys.append(jnp.dot(s.astype(BF16), wc_ref[k], preferred_element_type=F32))
    y = jnp.concatenate(ys, axis=1)
    o_ref[...] = _ssm_tail(y, u, dsk_ref[...], wglu_ref[...], bglu_ref[...])


def ssm_sample(u, h0, prm):
    bsz = u.shape[0]
    return pl.pallas_call(
        _ssm_sample_kernel,
        out_shape=[jax.ShapeDtypeStruct((bsz, SSM_WIDTH), F32),
                   jax.ShapeDtypeStruct((SSM_KBLK, bsz, 2 * SSM_KSTATE), F32)],
        compiler_params=pltpu.CompilerParams(vmem_limit_bytes=VMEM_LIMIT),
        name="ssm_sample",
    )(u, h0, prm["wb"], prm["wc"], prm["a_re"], prm["a_im"], prm["dsk"], prm["wglu"], prm["bglu"])


def ssm_params(lam_re, lam_im, log_dt, b_re, b_im, c_re, c_im, d_skip, w_glu, b_glu):
    dt = jnp.exp(log_dt)[:, None]
    mag = jnp.exp(lam_re * dt)
    ab_re = mag * jnp.cos(lam_im * dt)
    ab_im = mag * jnp.sin(lam_im * dt)
    den = lam_re * lam_re + lam_im * lam_im
    nr = ab_re - 1.0
    ni = ab_im
    z_re = ((nr * lam_re + ni * lam_im) / den)[..., None]
    z_im = ((ni * lam_re - nr * lam_im) / den)[..., None]
    bb_re = z_re * b_re - z_im * b_im
    bb_im = z_re * b_im + z_im * b_re
    gpb = SSM_GROUPS // SSM_KBLK
    eye = jnp.eye(gpb, dtype=F32)

    def in_mat(bb):
        bbk = bb.reshape(SSM_KBLK, gpb, SSM_STATE, SSM_GROUP_CH)
        m = jnp.einsum("kgpc,gh->kgchp", bbk, eye)
        return m.reshape(SSM_KBLK, LANES, SSM_KSTATE)

    def out_mat(c):
        ck = c.reshape(SSM_KBLK, gpb, SSM_GROUP_CH, SSM_STATE)
        m = jnp.einsum("kgcp,gh->kgphc", ck, eye)
        return m.reshape(SSM_KBLK, SSM_KSTATE, LANES)

    wb = jnp.concatenate([in_mat(bb_re), in_mat(bb_im)], axis=2).astype(BF16)
    wc = jnp.concatenate([out_mat(c_re), -out_mat(c_im)], axis=1).astype(BF16)
    return dict(
        wb=wb, wc=wc,
        a_re=ab_re.reshape(SSM_KBLK, 1, SSM_KSTATE), a_im=ab_im.reshape(SSM_KBLK, 1, SSM_KSTATE),
        dsk=d_skip.reshape(1, SSM_WIDTH), wglu=w_glu.astype(BF16), bglu=b_glu.reshape(1, SSM_WIDTH))


def _state_to_blocks(h):
    bsz = h.shape[0]
    return h.reshape(bsz, SSM_KBLK, SSM_KSTATE).transpose(1, 0, 2)


def _blocks_to_state(s):
    bsz = s.shape[1]
    return s.transpose(1, 0, 2).reshape(bsz, SSM_GROUPS, SSM_STATE)


def _mem_attn_kernel(q_ref, k_ref, v_ref, o_ref):
    scale = MEM_HEAD_DIM ** -0.5
    for h in range(MEM_HEADS):
        sl = slice(h * MEM_HEAD_DIM, (h + 1) * MEM_HEAD_DIM)
        q = q_ref[:, sl].astype(BF16)
        k = k_ref[:, sl].astype(BF16)
        v = v_ref[:, sl].astype(BF16)
        s = lax.dot_general(q, k, (((1,), (1,)), ((), ())), preferred_element_type=F32) * scale
        m = jnp.max(s, axis=-1, keepdims=True)
        p = jnp.exp(s - m)
        den = jnp.sum(p, axis=-1, keepdims=True)
        o_ref[:, sl] = jnp.dot((p / den).astype(BF16), v, preferred_element_type=F32).astype(o_ref.dtype)


def mem_attention(q, mk, mv, tl, out_dtype=F32):
    bsz, seq, _ = q.shape
    tl = min(tl, seq)
    return pl.pallas_call(
        _mem_attn_kernel,
        grid=(bsz, seq // tl),
        in_specs=[
            pl.BlockSpec((None, tl, MEM_WIDTH), lambda b, i: (b, i, 0)),
            pl.BlockSpec((None, MEM_TOKENS, MEM_WIDTH), lambda b, i: (b, 0, 0)),
            pl.BlockSpec((None, MEM_TOKENS, MEM_WIDTH), lambda b, i: (b, 0, 0)),
        ],
        out_specs=pl.BlockSpec((None, tl, MEM_WIDTH), lambda b, i: (b, i, 0)),
        out_shape=jax.ShapeDtypeStruct((bsz, seq, MEM_WIDTH), out_dtype),
        compiler_params=_cparams(("arbitrary", "arbitrary")),
        name="mem_attention",
    )(q, mk, mv)


def _route(logits):
    lane = lax.broadcasted_iota(jnp.int32, logits.shape, 1)
    lanef = lane.astype(F32)
    none = float(ROUTER_COLS)
    neg = -jnp.inf
    gl = jnp.where(lane < MOE_GROUPS, logits, neg)
    gmax = jnp.max(gl, axis=-1, keepdims=True)
    grp = jnp.min(jnp.where(gl == gmax, lanef, none), axis=-1, keepdims=True)
    p_grp = 1.0 / jnp.sum(jnp.exp(gl - gmax), axis=-1, keepdims=True)
    lo = MOE_GROUPS + grp * EXPERTS_PER_GROUP
    el = jnp.where(jnp.logical_and(lanef >= lo, lanef < lo + EXPERTS_PER_GROUP), logits, neg)
    v1 = jnp.max(el, axis=-1, keepdims=True)
    i1 = jnp.min(jnp.where(el == v1, lanef, none), axis=-1, keepdims=True)
    el2 = jnp.where(lanef == i1, neg, el)
    v2 = jnp.max(el2, axis=-1, keepdims=True)
    i2 = jnp.min(jnp.where(el2 == v2, lanef, none), axis=-1, keepdims=True)
    t = jnp.exp(v2 - v1)
    g1 = p_grp / (1.0 + t)
    g2 = g1 * t
    out = jnp.where(lane == 0, g1, 0.0)
    out = jnp.where(lane == 1, g2, out)
    out = jnp.where(lane == 2, i1 - MOE_GROUPS, out)
    return jnp.where(lane == 3, i2 - MOE_GROUPS, out)


def _merge_kernel(att_ref, ssm_ref, mem_ref, gate_ref, x_ref, wa_ref, ws_ref, wm_ref, wo_ref,
                  n2_ref, wr_ref, br_ref, h_ref, hn_ref, lg_ref, *, tm):
    a = jnp.dot(att_ref[...].astype(BF16), wa_ref[...], preferred_element_type=F32)
    merged = gate_ref[:, 0:D_MODEL] * a
    s = jnp.dot(ssm_ref[...].astype(BF16), ws_ref[...], preferred_element_type=F32)
    merged = merged + gate_ref[:, D_MODEL:2 * D_MODEL] * s
    m = jnp.dot(mem_ref[...].astype(BF16), wm_ref[...], preferred_element_type=F32)
    merged = merged + gate_ref[:, 2 * D_MODEL:3 * D_MODEL] * m
    h = x_ref[...] + jnp.dot(merged.astype(BF16), wo_ref[...], preferred_element_type=F32)
    h_ref[...] = h
    hn = _rms(h, n2_ref[...])
    for j in range(D_MODEL // LANES):
        hn_ref[pl.ds(j, tm, stride=SUBLANES), :] = hn[:, j * LANES:(j + 1) * LANES]
    logits = jnp.dot(hn.astype(BF16), wr_ref[...], preferred_element_type=F32) + br_ref[...]
    lg_ref[...] = _route(logits)


def merge(att, ssm, mem, gates, x, wts, tm):
    n = x.shape[0]
    tm = min(tm, n)
    row = lambda w: pl.BlockSpec((tm, w), lambda i: (i, 0))
    const = lambda a: pl.BlockSpec(a.shape, lambda i: (0, 0))
    ws = [wts["wa"], wts["ws"], wts["wm"], wts["wo"], wts["n2"], wts["wr"], wts["br"]]
    return pl.pallas_call(
        functools.partial(_merge_kernel, tm=tm),
        grid=(n // tm,),
        in_specs=[row(ATT_OUT), row(SSM_WIDTH), row(MEM_WIDTH), row(N_BRANCHES * D_MODEL), row(D_MODEL)]
                 + [const(w) for w in ws],
        out_specs=[row(D_MODEL), pl.BlockSpec((tm * SUBLANES, LANES), lambda i: (i, 0)), row(ROUTER_COLS)],
        out_shape=[jax.ShapeDtypeStruct((n, D_MODEL), F32),
                   jax.ShapeDtypeStruct((n * SUBLANES, LANES), F32),
                   jax.ShapeDtypeStruct((n, ROUTER_COLS), F32)],
        compiler_params=_cparams(("arbitrary",)),
        name="merge",
    )(att, ssm, mem, gates, x, *ws)


def _moe_kernel(be_ref, idx_hbm, x_hbm, w1_ref, w3_ref, w2_ref, out_hbm,
                idx_smem, xbuf, xb, ybuf, w1b, w3b, w2b, gsem, ssem, isem, *, bm):
    blk = pl.program_id(0)
    nblk = pl.num_programs(0)
    slot = blk % 2
    tiles = D_MODEL // LANES

    def idx_copy(b, s):
        return pltpu.make_async_copy(idx_hbm.at[b], idx_smem.at[s], isem.at[s])

    def row_loop(s, table_off, start_copy):
        def body(c, carry):
            flat = c * SUBLANES + table_off
            srow = lax.shift_right_logical(flat, 7)
            scol = jnp.bitwise_and(flat, LANES - 1)
            for j in range(SUBLANES):
                ent = idx_smem[s, srow, scol + j]
                r0 = pl.multiple_of((c * SUBLANES + j) * SUBLANES, SUBLANES)
                start_copy(ent, r0, j % 2)
            return carry
        lax.fori_loop(0, bm // SUBLANES, body, 0)

    def issue_gather(s):
        def start(tok, r0, prio):
            pltpu.make_async_copy(x_hbm.at[tok], xbuf.at[pl.ds(r0, SUBLANES)], gsem).start(priority=prio)
        row_loop(s, 0, start)

    def issue_scatter(s):
        def start(pid, r0, prio):
            pltpu.make_async_copy(ybuf.at[pl.ds(r0, SUBLANES)], out_hbm.at[pid], ssem).start(priority=prio)
        row_loop(s, bm, start)

    def wait_rows(sem, buf):
        pltpu.make_async_copy(buf, buf, sem).wait()

    @pl.when(blk == 0)
    def _():
        c = idx_copy(0, 0)
        c.start()
        c.wait()
        issue_gather(0)

    @pl.when(blk + 1 < nblk)
    def _():
        idx_copy(blk + 1, 1 - slot).start()

    prev = jnp.maximum(blk - 1, 0)
    changed = jnp.logical_or(blk == 0, be_ref[blk] != be_ref[prev])

    @pl.when(changed)
    def _():
        w1b[...] = w1_ref[...].astype(BF16)
        w3b[...] = w3_ref[...].astype(BF16)
        w2b[...] = w2_ref[...].astype(BF16)

    wait_rows(gsem, xbuf)
    for j in range(tiles):
        xb[:, j * LANES:(j + 1) * LANES] = xbuf[pl.ds(j, bm, stride=SUBLANES), :].astype(BF16)

    @pl.when(blk + 1 < nblk)
    def _():
        idx_copy(blk + 1, 1 - slot).wait()
        issue_gather(1 - slot)

    x = xb[...]
    h1 = jnp.dot(x, w1b[...], preferred_element_type=F32)
    h3 = jnp.dot(x, w3b[...], preferred_element_type=F32)
    act = (jax.nn.silu(h1) * h3).astype(BF16)
    y = jnp.dot(act, w2b[...], preferred_element_type=F32)

    @pl.when(blk > 0)
    def _():
        wait_rows(ssem, ybuf)

    for j in range(tiles):
        ybuf[pl.ds(j, bm, stride=SUBLANES), :] = y[:, j * LANES:(j + 1) * LANES]
    issue_scatter(slot)

    @pl.when(blk == nblk - 1)
    def _():
        wait_rows(ssem, ybuf)


def moe_experts(block_expert, idx, hn3, w1, w3, w2, n_out_rows, bm):
    n_blocks = idx.shape[0]
    grid_spec = pltpu.PrefetchScalarGridSpec(
        num_scalar_prefetch=1,
        grid=(n_blocks,),
        in_specs=[
            pl.BlockSpec(memory_space=pl.ANY),
            pl.BlockSpec(memory_space=pl.ANY),
            pl.BlockSpec((None, D_MODEL, D_EXPERT), lambda b, be: (be[b], 0, 0)),
            pl.BlockSpec((None, D_MODEL, D_EXPERT), lambda b, be: (be[b], 0, 0)),
            pl.BlockSpec((None, D_EXPERT, D_MODEL), lambda b, be: (be[b], 0, 0)),
        ],
        out_specs=pl.BlockSpec(memory_space=pl.ANY),
        scratch_shapes=[
            pltpu.SMEM((2, SUBLANES, LANES), jnp.int32),
            pltpu.VMEM((bm * SUBLANES, LANES), F32),
            pltpu.VMEM((bm, D_MODEL), BF16),
            pltpu.VMEM((bm * SUBLANES, LANES), F32),
            pltpu.VMEM((D_MODEL, D_EXPERT), BF16),
            pltpu.VMEM((D_MODEL, D_EXPERT), BF16),
            pltpu.VMEM((D_EXPERT, D_MODEL), BF16),
            pltpu.SemaphoreType.DMA,
            pltpu.SemaphoreType.DMA,
            pltpu.SemaphoreType.DMA((2,)),
        ],
    )
    return pl.pallas_call(
        functools.partial(_moe_kernel, bm=bm),
        grid_spec=grid_spec,
        out_shape=jax.ShapeDtypeStruct((n_out_rows, SUBLANES, LANES), F32),
        compiler_params=_cparams(("arbitrary",)),
        name="moe_experts",
    )(block_expert, idx, hn3, w1, w3, w2)


def route_tables(route, bm):
    n_tok = route.shape[0]
    n_pairs = n_tok * TOP_K
    e_flat = route[:, 2:2 + TOP_K].astype(jnp.int32).reshape(n_pairs)
    experts = jnp.arange(N_EXPERTS, dtype=jnp.int32)
    counts = jnp.sum(e_flat[:, None] == experts[None, :], axis=0, dtype=jnp.int32)
    padded = (counts + bm - 1) // bm * bm
    pad_ends = jnp.cumsum(padded)
    n_blocks = -(-n_pairs // bm) + N_EXPERTS
    row0 = jnp.arange(n_blocks, dtype=jnp.int32) * bm
    block_expert = jnp.minimum(jnp.sum(pad_ends[None, :] <= row0[:, None], axis=1, dtype=jnp.int32),
                               N_EXPERTS - 1)
    fill_id = jnp.arange(n_blocks * bm - n_pairs, dtype=jnp.int32)
    fill_e = jnp.minimum(fill_id // bm, N_EXPERTS - 1)
    fill_used = jnp.logical_and(fill_id < N_EXPERTS * bm, fill_id % bm < (padded - counts)[fill_e])
    keys = jnp.concatenate([e_flat * 2, jnp.where(fill_used, fill_e * 2 + 1, 2 * N_EXPERTS)])
    vals = jnp.concatenate([jnp.arange(n_pairs, dtype=jnp.int32), jnp.full(fill_id.shape, -1, jnp.int32)])
    _, win = lax.sort((keys, vals), num_keys=1, is_stable=True)
    win = win.reshape(n_blocks, bm)
    pos = jnp.arange(bm, dtype=jnp.int32)[None, :]
    valid = win >= 0
    tok = win // TOP_K
    src = jnp.where(valid, tok, 0)
    dst = jnp.where(valid, (win % TOP_K) * n_tok + tok, n_pairs + pos)
    fill = jnp.zeros((n_blocks, SUBLANES * LANES - 2 * bm), jnp.int32)
    idx = jnp.concatenate([src, dst, fill], axis=1).reshape(n_blocks, SUBLANES, LANES)
    return block_expert, idx, n_pairs + bm


def _final_kernel(h_ref, p0_ref, p1_ref, gw_ref, nf_ref, y_ref, *, tm):
    tiles = D_MODEL // LANES
    p0 = jnp.concatenate([p0_ref[pl.ds(j, tm, stride=SUBLANES), :] for j in range(tiles)], axis=1)
    p1 = jnp.concatenate([p1_ref[pl.ds(j, tm, stride=SUBLANES), :] for j in range(tiles)], axis=1)
    gw = gw_ref[...]
    h = h_ref[...] + (gw[:, 0:1] * p0 + gw[:, 1:2] * p1)
    y_ref[...] = _rms(h, nf_ref[...])


def final(h, pairs2d, gate_pad, norm_f, tm):
    n = h.shape[0]
    tm = min(tm, n)
    nb = n // tm
    row = lambda w: pl.BlockSpec((tm, w), lambda i: (i, 0))
    return pl.pallas_call(
        functools.partial(_final_kernel, tm=tm),
        grid=(nb,),
        in_specs=[
            row(D_MODEL),
            pl.BlockSpec((tm * SUBLANES, LANES), lambda i: (i, 0)),
            pl.BlockSpec((tm * SUBLANES, LANES), lambda i: (nb + i, 0)),
            row(LANES),
            pl.BlockSpec((1, D_MODEL), lambda i: (0, 0)),
        ],
        out_specs=row(D_MODEL),
        out_shape=jax.ShapeDtypeStruct((n, D_MODEL), F32),
        compiler_params=_cparams(("arbitrary",)),
        name="final",
    )(h, pairs2d, pairs2d, gate_pad, norm_f.reshape(1, D_MODEL))


def _alibi_slopes():
    h = jnp.arange(1, N_ATT_HEADS + 1, dtype=F32)
    return jnp.exp2(-8.0 * h / N_ATT_HEADS)


def _token_stage(att, ssm, mem, gates, x2, wts, w1, w3, w2, norm_f, tm, bm):
    n = x2.shape[0]
    h, hn2d, route = merge(att, ssm, mem, gates, x2, wts, tm)
    block_expert, idx, out_rows = route_tables(route, bm)
    pairs = moe_experts(block_expert, idx, hn2d.reshape(n, SUBLANES, LANES), w1, w3, w2, out_rows, bm)
    return final(h, pairs.reshape(out_rows * SUBLANES, LANES), route, norm_f, tm)


def kernel(x_prompt, x_sample, cache_swa0_k, cache_swa0_v, cache_swa1_k, cache_swa1_v, cache_swa2_k, cache_swa2_v, cache_mem_k, cache_mem_v, state_ssm_re, state_ssm_im, mem_prompt, norm1, w_in, lam_re, lam_im, log_dt, ssm_b_re, ssm_b_im, ssm_c_re, ssm_c_im, ssm_d, w_glu, b_glu, w_att_out, w_ssm_out, w_mem_out, w_o, norm_mem, w_mk, w_mv, norm2, w_grp, b_grp, w_exp, b_exp, w1, w3, w2, norm_f):
    assert norm1.shape[0] == 1, "single-layer trunk"
    bsz, seq, _ = x_prompt.shape
    sbz = x_sample.shape[0]
    n_p = bsz * seq
    caches = (cache_swa0_k[0], cache_swa0_v[0], cache_swa1_k[0], cache_swa1_v[0], cache_swa2_k[0], cache_swa2_v[0])

    w_in_b = w_in[0].astype(BF16)
    w_mkv_b = jnp.concatenate([w_mk[0], w_mv[0]], axis=1).astype(BF16)
    wr = jnp.zeros((D_MODEL, ROUTER_COLS), F32)
    wr = wr.at[:, :MOE_GROUPS].set(w_grp[0]).at[:, MOE_GROUPS:MOE_GROUPS + N_EXPERTS].set(w_exp[0])
    br = jnp.zeros((1, ROUTER_COLS), F32)
    br = br.at[0, :MOE_GROUPS].set(b_grp[0]).at[0, MOE_GROUPS:MOE_GROUPS + N_EXPERTS].set(b_exp[0])
    wts = dict(wa=w_att_out[0].astype(BF16), ws=w_ssm_out[0].astype(BF16), wm=w_mem_out[0].astype(BF16),
               wo=w_o[0].astype(BF16), n2=norm2[0].reshape(1, D_MODEL), wr=wr.astype(BF16), br=br)
    prm = ssm_params(lam_re[0], lam_im[0], log_dt[0], ssm_b_re[0], ssm_b_im[0], ssm_c_re[0], ssm_c_im[0],
                     ssm_d[0], w_glu[0], b_glu[0])
    slopes = _alibi_slopes()
    splits = (QKV_COLS, SSM_WIDTH, MEM_WIDTH, N_BRANCHES * D_MODEL)
    sig = (False, False, False, True)

    x2 = x_prompt.reshape(n_p, D_MODEL)
    qkv, u, mq, gates = norm_matmul(x2, norm1[0], w_in_b, splits, sig, tm=256, dtypes=(F32, F32, BF16, F32))
    qkv3 = qkv.reshape(bsz, seq, QKV_COLS)
    att = attn_prompt(qkv3, slopes)
    ssm, st = ssm_prompt(u.reshape(bsz, seq, SSM_WIDTH), prm)
    mk, mv = norm_matmul(mem_prompt.reshape(bsz * MEM_TOKENS, D_MODEL), norm_mem[0], w_mkv_b,
                         (MEM_WIDTH, MEM_WIDTH), (False, False), tm=256)
    mk3 = mk.reshape(bsz, MEM_TOKENS, MEM_WIDTH)
    mv3 = mv.reshape(bsz, MEM_TOKENS, MEM_WIDTH)
    mem = mem_attention(mq.reshape(bsz, seq, MEM_WIDTH), mk3, mv3, tl=512, out_dtype=BF16)
    y_p = _token_stage(att.reshape(n_p, ATT_OUT), ssm.reshape(n_p, SSM_WIDTH), mem.reshape(n_p, MEM_WIDTH),
                       gates, x2, wts, w1[0], w3[0], w2[0], norm_f, tm=256, bm=256)

    xs2 = x_sample.reshape(sbz, D_MODEL)
    qkv_s, u_s, mq_s, gates_s = norm_matmul(xs2, norm1[0], w_in_b, splits, sig, tm=sbz)
    qkv5 = qkv_s.reshape(sbz, len(ATT_GROUPS), 3, HEADS_PER_GROUP, HEAD_DIM)
    att_s, *new_caches = attn_sample(qkv5, [jnp.transpose(c, (0, 2, 3, 1)) for c in caches])
    att_s = att_s.reshape(sbz, ATT_OUT)
    h0 = jnp.concatenate([_state_to_blocks(state_ssm_re[0]), _state_to_blocks(state_ssm_im[0])], axis=2)
    ssm_s, st_s = ssm_sample(u_s, h0, prm)
    cmk = cache_mem_k[0].reshape(sbz, MEM_TOKENS, MEM_WIDTH)
    cmv = cache_mem_v[0].reshape(sbz, MEM_TOKENS, MEM_WIDTH)
    mem_s = mem_attention(mq_s.reshape(sbz, 1, MEM_WIDTH), cmk, cmv, tl=1).reshape(sbz, MEM_WIDTH)
    y_s = _token_stage(att_s, ssm_s, mem_s, gates_s, xs2, wts, w1[0], w3[0], w2[0], norm_f, tm=sbz, bm=16)

    outs = [y_p.reshape(bsz, seq, D_MODEL), y_s.reshape(sbz, 1, D_MODEL)]
    for g, (window, _) in enumerate(ATT_GROUPS):
        keep = min(window, seq)
        for part in (1, 2):
            c0 = g * QKV_GROUP + part * ATT_OUT
            outs.append(qkv3[:, seq - keep:, c0:c0 + ATT_OUT].reshape(1, bsz, keep, HEADS_PER_GROUP, HEAD_DIM))
    outs.append(mk3.reshape(1, bsz, MEM_TOKENS, MEM_HEADS, MEM_HEAD_DIM))
    outs.append(mv3.reshape(1, bsz, MEM_TOKENS, MEM_HEADS, MEM_HEAD_DIM))
    outs.append(_blocks_to_state(st[:, :, :SSM_KSTATE])[None])
    outs.append(_blocks_to_state(st[:, :, SSM_KSTATE:])[None])
    outs.extend(jnp.transpose(c, (0, 3, 1, 2))[None] for c in new_caches)
    outs.append(_blocks_to_state(st_s[:, :, :SSM_KSTATE])[None])
    outs.append(_blocks_to_state(st_s[:, :, SSM_KSTATE:])[None])
    return tuple(outs)
```

```python
import functools
import math

import jax
import jax.numpy as jnp
from jax import lax
from jax.experimental import pallas as pl
from jax.experimental.pallas import tpu as pltpu

F32 = jnp.float32
BF16 = jnp.bfloat16

D_MODEL = 1024
ATT_GROUPS = ((128, 1), (512, 4), (2048, 16))
HEADS_PER_GROUP = 4
HEAD_DIM = 64
N_ATT_HEADS = len(ATT_GROUPS) * HEADS_PER_GROUP
ATT_OUT = HEADS_PER_GROUP * HEAD_DIM
QKV_GROUP = 3 * ATT_OUT
QKV_COLS = len(ATT_GROUPS) * QKV_GROUP
SSM_WIDTH = D_MODEL // 2
SSM_GROUP_CH = 16
SSM_GROUPS = SSM_WIDTH // SSM_GROUP_CH
SSM_STATE = 64
SSM_NSTATE = SSM_GROUPS * SSM_STATE
MEM_TOKENS = 256
MEM_HEADS = 4
MEM_HEAD_DIM = D_MODEL // 8
MEM_WIDTH = MEM_HEADS * MEM_HEAD_DIM
N_BRANCHES = 3
MOE_GROUPS = 4
EXPERTS_PER_GROUP = 8
N_EXPERTS = MOE_GROUPS * EXPERTS_PER_GROUP
TOP_K = 2
D_EXPERT = D_MODEL // 2
RMS_EPS = 1e-6

LANES = 128
SUBLANES = 8
Q_TILE = 128
ROUTER_COLS = 128
NEG_BIG = -1e30
VMEM_LIMIT = 56 * 1024 * 1024


def _cparams(sem):
    return pltpu.CompilerParams(dimension_semantics=sem, vmem_limit_bytes=VMEM_LIMIT)


def _mm(a, b, precise, dims=None):
    if precise:
        a, b, kw = a.astype(F32), b.astype(F32), dict(precision=lax.Precision.HIGHEST)
    else:
        a, b, kw = a.astype(BF16), b.astype(BF16), {}
    if dims is None:
        return jnp.dot(a, b, preferred_element_type=F32, **kw)
    return lax.dot_general(a, b, dims, preferred_element_type=F32, **kw)


NT_DIMS = (((1,), (1,)), ((), ()))


def _rms(x, g):
    ms = jnp.mean(x * x, axis=-1, keepdims=True)
    return (x * lax.rsqrt(ms + RMS_EPS)) * g


def _norm_matmul_kernel(x_ref, g_ref, w_ref, *out_refs, splits, sigmoid, chunk, precise):
    xb = _rms(x_ref[...], g_ref[...])
    if not precise:
        xb = xb.astype(BF16)
    c0 = 0
    for o_ref, n, sg in zip(out_refs, splits, sigmoid):
        for j in range(0, n, chunk):
            w = min(chunk, n - j)
            y = _mm(xb, w_ref[:, c0 + j:c0 + j + w], precise)
            if sg:
                y = jax.nn.sigmoid(y)
            o_ref[:, j:j + w] = y.astype(o_ref.dtype)
        c0 += n


def norm_matmul(x, g, w, splits, sigmoid, tm, dtypes=None, precise=False):
    n, d = x.shape
    tm = min(tm, n)
    kern = functools.partial(_norm_matmul_kernel, splits=tuple(splits), sigmoid=tuple(sigmoid), chunk=512,
                             precise=precise)
    return pl.pallas_call(
        kern,
        grid=(n // tm,),
        in_specs=[
            pl.BlockSpec((tm, d), lambda i: (i, 0)),
            pl.BlockSpec((1, d), lambda i: (0, 0)),
            pl.BlockSpec((d, w.shape[1]), lambda i: (0, 0), pipeline_mode=pl.Buffered(1)),
        ],
        out_specs=[pl.BlockSpec((tm, s), lambda i: (i, 0)) for s in splits],
        out_shape=[jax.ShapeDtypeStruct((n, s), dt) for s, dt in zip(splits, dtypes or (F32,) * len(splits))],
        compiler_params=_cparams(("arbitrary",)),
        name="norm_matmul",
    )(x, g.reshape(1, d), w)


def _attn_prompt_kernel(slopes_ref, *refs, seq):
    qkv_refs = refs[:9]
    o_ref = refs[9]
    o_scr, l_scr = refs[10], refs[11]
    hp = pl.program_id(1)
    scale = HEAD_DIM ** -0.5
    row = lax.broadcasted_iota(jnp.int32, (Q_TILE, 2 * Q_TILE), 0)
    col = lax.broadcasted_iota(jnp.int32, (Q_TILE, 2 * Q_TILE), 1)
    dist = (row + Q_TILE - col)
    in_window = jnp.logical_and(dist >= 0, dist <= Q_TILE)
    distf = dist.astype(F32)
    lane = lax.broadcasted_iota(jnp.int32, (Q_TILE, LANES), 1)
    head_a = lane < HEAD_DIM

    for g, (_, dil) in enumerate(ATT_GROUPS):
        q_ref, k_ref, v_ref = qkv_refs[3 * g:3 * g + 3]
        sub_len = seq // dil
        n_blk = sub_len // Q_TILE
        slope_a = slopes_ref[g * HEADS_PER_GROUP + 2 * hp] * float(dil)
        slope_b = slopes_ref[g * HEADS_PER_GROUP + 2 * hp + 1] * float(dil)

        def rows(start):
            if dil == 1:
                return pl.ds(start, Q_TILE)
            return pl.ds(start, Q_TILE, stride=dil)

        def body(it, carry, q_ref=q_ref, k_ref=k_ref, v_ref=v_ref, n_blk=n_blk, dil=dil,
                 slope_a=slope_a, slope_b=slope_b, rows=rows, g=g):
            r = it // n_blk
            blk = it % n_blk
            prev = jnp.maximum(blk - 1, 0)
            q_start = r + dil * Q_TILE * blk
            p_start = r + dil * Q_TILE * prev
            q = q_ref[rows(q_start), :]
            k2 = jnp.concatenate([k_ref[rows(p_start), :], k_ref[rows(q_start), :]], axis=0).astype(BF16)
            v2 = jnp.concatenate([v_ref[rows(p_start), :], v_ref[rows(q_start), :]], axis=0).astype(BF16)
            valid = jnp.logical_and(in_window, jnp.logical_or(col >= Q_TILE, blk > 0))
            outs, lses = [], []
            for is_a, slope in ((True, slope_a), (False, slope_b)):
                hmask = head_a if is_a else jnp.logical_not(head_a)
                qh = jnp.where(hmask, q, 0.0).astype(BF16)
                s = lax.dot_general(qh, k2, NT_DIMS, preferred_element_type=F32)
                s = s * scale - slope * distf
                s = jnp.where(valid, s, NEG_BIG)
                m = jnp.max(s, axis=-1, keepdims=True)
                p = jnp.exp(s - m)
                den = jnp.sum(p, axis=-1, keepdims=True)
                pv = jnp.dot(p.astype(BF16), v2, preferred_element_type=F32)
                outs.append(pv / den)
                lses.append(m + jnp.log(den))
            o_scr[g, rows(q_start), :] = jnp.where(head_a, outs[0], outs[1])
            l_scr[g, rows(q_start), :] = jnp.where(head_a, lses[0], lses[1])
            return carry

        lax.fori_loop(0, dil * n_blk, body, 0, unroll=4)

    def combine(c, carry):
        sl = pl.ds(pl.multiple_of(c * 256, 256), 256)
        l0, l1, l2 = l_scr[0, sl, :], l_scr[1, sl, :], l_scr[2, sl, :]
        m = jnp.maximum(jnp.maximum(l0, l1), l2)
        e0, e1, e2 = jnp.exp(l0 - m), jnp.exp(l1 - m), jnp.exp(l2 - m)
        tot = e0 + e1 + e2
        att = (e0 * o_scr[0, sl, :] + e1 * o_scr[1, sl, :] + e2 * o_scr[2, sl, :]) / tot
        o_ref[sl, :] = att.astype(o_ref.dtype)
        return carry

    lax.fori_loop(0, seq // 256, combine, 0)


def attn_prompt(qkv, slopes):
    bsz, seq, _ = qkv.shape
    in_specs = []
    for g in range(len(ATT_GROUPS)):
        for part in range(3):
            base = (g * QKV_GROUP + part * ATT_OUT) // LANES
            in_specs.append(pl.BlockSpec((None, seq, LANES),
                                         lambda b, h, sl, base=base: (b, 0, base + h)))
    grid_spec = pltpu.PrefetchScalarGridSpec(
        num_scalar_prefetch=1,
        grid=(bsz, ATT_OUT // LANES),
        in_specs=in_specs,
        out_specs=pl.BlockSpec((None, seq, LANES), lambda b, h, sl: (b, 0, h)),
        scratch_shapes=[pltpu.VMEM((3, seq, LANES), F32), pltpu.VMEM((3, seq, LANES), F32)],
    )
    return pl.pallas_call(
        functools.partial(_attn_prompt_kernel, seq=seq),
        grid_spec=grid_spec,
        out_shape=jax.ShapeDtypeStruct((bsz, seq, ATT_OUT), BF16),
        compiler_params=_cparams(("arbitrary", "arbitrary")),
        name="attn_prompt",
    )(slopes, *([qkv] * 9))


def _attn_sample_kernel(qkv_ref, col_ref, *refs):
    cache_refs = refs[:6]
    o_ref = refs[6]
    new_refs = refs[7:13]
    scale = HEAD_DIM ** -0.5
    n_grp = len(ATT_GROUPS)
    outs = [[None] * HEADS_PER_GROUP for _ in range(n_grp)]
    lses = [[None] * HEADS_PER_GROUP for _ in range(n_grp)]
    for g, (window, dil) in enumerate(ATT_GROUPS):
        lane = lax.broadcasted_iota(jnp.int32, (1, window), 1)
        on_grid = jnp.bitwise_and(lane, dil - 1) == 0
        dist = (window - lane).astype(F32)
        last = lax.broadcasted_iota(jnp.int32, (HEAD_DIM, window), 1) == window - 1
        for h in range(HEADS_PER_GROUP):
            slope = 2.0 ** (-8.0 * (g * HEADS_PER_GROUP + h + 1) / N_ATT_HEADS)
            q = qkv_ref[g, 0][h:h + 1, :]
            kn = qkv_ref[g, 1][h:h + 1, :]
            vn = qkv_ref[g, 2][h:h + 1, :]
            kt = cache_refs[2 * g][h]
            vt = cache_refs[2 * g + 1][h]
            s = _mm(q, kt, True) * scale - slope * dist
            s = jnp.where(on_grid, s, NEG_BIG)
            sn = jnp.sum(q * kn, axis=-1, keepdims=True) * scale
            m = jnp.maximum(jnp.max(s, axis=-1, keepdims=True), sn)
            p = jnp.exp(s - m)
            pn = jnp.exp(sn - m)
            den = jnp.sum(p, axis=-1, keepdims=True) + pn
            outs[g][h] = (_mm(p, vt, True, NT_DIMS) + pn * vn) / den
            lses[g][h] = m + jnp.log(den)
            new_refs[2 * g][h] = jnp.where(last, col_ref[g, 0, h], pltpu.roll(kt, window - 1, axis=1))
            new_refs[2 * g + 1][h] = jnp.where(last, col_ref[g, 1, h], pltpu.roll(vt, window - 1, axis=1))
    for h in range(HEADS_PER_GROUP):
        m = jnp.maximum(jnp.maximum(lses[0][h], lses[1][h]), lses[2][h])
        es = [jnp.exp(lses[g][h] - m) for g in range(n_grp)]
        num = es[0] * outs[0][h] + es[1] * outs[1][h] + es[2] * outs[2][h]
        o_ref[h:h + 1, :] = num / (es[0] + es[1] + es[2])


def attn_sample(qkv5, caches_t):
    bsz = qkv5.shape[0]
    cols = qkv5[:, :, 1:3].reshape(bsz, len(ATT_GROUPS), 2, HEADS_PER_GROUP, HEAD_DIM, 1)
    in_specs = [pl.BlockSpec((None, 3, 3, HEADS_PER_GROUP, HEAD_DIM), lambda b: (b, 0, 0, 0, 0)),
                pl.BlockSpec((None, 3, 2, HEADS_PER_GROUP, HEAD_DIM, 1), lambda b: (b, 0, 0, 0, 0, 0))]
    cache_specs = [pl.BlockSpec((None,) + c.shape[1:], lambda b: (b, 0, 0, 0)) for c in caches_t]
    return pl.pallas_call(
        _attn_sample_kernel,
        grid=(bsz,),
        in_specs=in_specs + cache_specs,
        out_specs=[pl.BlockSpec((None, HEADS_PER_GROUP, HEAD_DIM), lambda b: (b, 0, 0))] + cache_specs,
        out_shape=[jax.ShapeDtypeStruct((bsz, HEADS_PER_GROUP, HEAD_DIM), F32)]
                  + [jax.ShapeDtypeStruct(c.shape, F32) for c in caches_t],
        compiler_params=_cparams(("arbitrary",)),
        name="attn_sample",
    )(qkv5, cols, *caches_t)


SSM_KBLK = SSM_WIDTH // LANES
SSM_KSTATE = SSM_NSTATE // SSM_KBLK


def _ssm_tail(y, u, dsk, wglu, bglu, precise=False):
    y = y + dsk * u
    z = jax.nn.gelu(y)
    gl = _mm(z, wglu, precise) + bglu
    return z * jax.nn.sigmoid(gl)


def _ssm_prompt_kernel(u_ref, wb_ref, wc_ref, are_ref, aim_ref, dsk_ref, wglu_ref, bglu_ref,
                       o_ref, st_ref, uperm, bu, sbf, yperm, ynat, *, bsz, tt):
    rows = bsz * tt

    @pl.when(pl.program_id(0) == 0)
    def _():
        st_ref[...] = jnp.zeros_like(st_ref)

    for b in range(bsz):
        ub = u_ref[b]
        for k in range(SSM_KBLK):
            uperm[k, pl.ds(b, tt, stride=bsz), :] = ub[:, k * LANES:(k + 1) * LANES]

    for k in range(SSM_KBLK):
        bu[k] = jnp.dot(uperm[k].astype(BF16), wb_ref[k], preferred_element_type=F32)

    for k in range(SSM_KBLK):
        ar = jnp.broadcast_to(are_ref[k], (bsz, SSM_KSTATE))
        ai = jnp.broadcast_to(aim_ref[k], (bsz, SSM_KSTATE))
        st = st_ref[k]

        def step(t, carry, k=k, ar=ar, ai=ai):
            re, im = carry
            r0 = pl.multiple_of(t * bsz, bsz)
            b_t = bu[k, pl.ds(r0, bsz), :]
            nre = ar * re - ai * im + b_t[:, :SSM_KSTATE]
            nim = ar * im + ai * re + b_t[:, SSM_KSTATE:]
            sbf[k, pl.ds(r0, bsz), :] = jnp.concatenate([nre, nim], axis=1).astype(BF16)
            return nre, nim

        re, im = lax.fori_loop(0, tt, step, (st[:, :SSM_KSTATE], st[:, SSM_KSTATE:]), unroll=2)
        st_ref[k] = jnp.concatenate([re, im], axis=1)

    for k in range(SSM_KBLK):
        yperm[k] = jnp.dot(sbf[k], wc_ref[k], preferred_element_type=F32)

    for b in range(bsz):
        for k in range(SSM_KBLK):
            ynat[b * tt:(b + 1) * tt, k * LANES:(k + 1) * LANES] = yperm[k, pl.ds(b, tt, stride=bsz), :]

    u2 = u_ref[...].reshape(rows, SSM_WIDTH)
    out = _ssm_tail(ynat[...], u2, dsk_ref[...], wglu_ref[...], bglu_ref[...])
    o_ref[...] = out.reshape(bsz, tt, SSM_WIDTH).astype(o_ref.dtype)


def ssm_prompt(u, prm, tt=32):
    bsz, seq, _ = u.shape
    rows = bsz * tt
    const = lambda shape: pl.BlockSpec(shape, lambda i: (0,) * len(shape))
    return pl.pallas_call(
        functools.partial(_ssm_prompt_kernel, bsz=bsz, tt=tt),
        grid=(seq // tt,),
        in_specs=[
            pl.BlockSpec((bsz, tt, SSM_WIDTH), lambda i: (0, i, 0)),
            const((SSM_KBLK, LANES, 2 * SSM_KSTATE)),
            const((SSM_KBLK, 2 * SSM_KSTATE, LANES)),
            const((SSM_KBLK, 1, SSM_KSTATE)),
            const((SSM_KBLK, 1, SSM_KSTATE)),
            const((1, SSM_WIDTH)),
            const((SSM_WIDTH, SSM_WIDTH)),
            const((1, SSM_WIDTH)),
        ],
        out_specs=[
            pl.BlockSpec((bsz, tt, SSM_WIDTH), lambda i: (0, i, 0)),
            const((SSM_KBLK, bsz, 2 * SSM_KSTATE)),
        ],
        out_shape=[
            jax.ShapeDtypeStruct((bsz, seq, SSM_WIDTH), BF16),
            jax.ShapeDtypeStruct((SSM_KBLK, bsz, 2 * SSM_KSTATE), F32),
        ],
        scratch_shapes=[
            pltpu.VMEM((SSM_KBLK, rows, LANES), F32),
            pltpu.VMEM((SSM_KBLK, rows, 2 * SSM_KSTATE), F32),
            pltpu.VMEM((SSM_KBLK, rows, 2 * SSM_KSTATE), BF16),
            pltpu.VMEM((SSM_KBLK, rows, LANES), F32),
            pltpu.VMEM((rows, SSM_WIDTH), F32),
        ],
        compiler_params=_cparams(("arbitrary",)),
        name="ssm_prompt",
    )(u, prm["wb"], prm["wc"], prm["a_re"], prm["a_im"], prm["dsk"], prm["wglu"], prm["bglu"])


def _ssm_sample_kernel(u_ref, h0_ref, wb_ref, wc_ref, are_ref, aim_ref, dsk_ref, wglu_ref, bglu_ref,
                       o_ref, st_ref):
    u = u_ref[...]
    ys = []
    for k in range(SSM_KBLK):
        b_t = _mm(u[:, k * LANES:(k + 1) * LANES], wb_ref[k], True)
        h0 = h0_ref[k]
        re, im = h0[:, :SSM_KSTATE], h0[:, SSM_KSTATE:]
        ar, ai = are_ref[k], aim_ref[k]
        nre = ar * re - ai * im + b_t[:, :SSM_KSTATE]
        nim = ar * im + ai * re + b_t[:, SSM_KSTATE:]
        s = jnp.concatenate([nre, nim], axis=1)
        st_ref[k] = s
        ys.append(_mm(s, wc_ref[k], True))
    y = jnp.concatenate(ys, axis=1)
    o_ref[...] = _ssm_tail(y, u, dsk_ref[...], wglu_ref[...], bglu_ref[...], precise=True)


def ssm_sample(u, h0, prm):
    bsz = u.shape[0]
    return pl.pallas_call(
        _ssm_sample_kernel,
        out_shape=[jax.ShapeDtypeStruct((bsz, SSM_WIDTH), F32),
                   jax.ShapeDtypeStruct((SSM_KBLK, bsz, 2 * SSM_KSTATE), F32)],
        compiler_params=pltpu.CompilerParams(vmem_limit_bytes=VMEM_LIMIT),
        name="ssm_sample",
    )(u, h0, prm["wb_f32"], prm["wc_f32"], prm["a_re"], prm["a_im"], prm["dsk"], prm["wglu_f32"], prm["bglu"])


def ssm_params(lam_re, lam_im, log_dt, b_re, b_im, c_re, c_im, d_skip, w_glu, b_glu):
    dt = jnp.exp(log_dt)[:, None]
    mag = jnp.exp(lam_re * dt)
    ab_re = mag * jnp.cos(lam_im * dt)
    ab_im = mag * jnp.sin(lam_im * dt)
    den = lam_re * lam_re + lam_im * lam_im
    nr = ab_re - 1.0
    ni = ab_im
    z_re = ((nr * lam_re + ni * lam_im) / den)[..., None]
    z_im = ((ni * lam_re - nr * lam_im) / den)[..., None]
    bb_re = z_re * b_re - z_im * b_im
    bb_im = z_re * b_im + z_im * b_re
    gpb = SSM_GROUPS // SSM_KBLK
    eye = jnp.eye(gpb, dtype=F32)

    def in_mat(bb):
        bbk = bb.reshape(SSM_KBLK, gpb, SSM_STATE, SSM_GROUP_CH)
        m = jnp.einsum("kgpc,gh->kgchp", bbk, eye)
        return m.reshape(SSM_KBLK, LANES, SSM_KSTATE)

    def out_mat(c):
        ck = c.reshape(SSM_KBLK, gpb, SSM_GROUP_CH, SSM_STATE)
        m = jnp.einsum("kgcp,gh->kgphc", ck, eye)
        return m.reshape(SSM_KBLK, SSM_KSTATE, LANES)

    wb = jnp.concatenate([in_mat(bb_re), in_mat(bb_im)], axis=2)
    wc = jnp.concatenate([out_mat(c_re), -out_mat(c_im)], axis=1)
    return dict(
        wb=wb.astype(BF16), wc=wc.astype(BF16), wb_f32=wb, wc_f32=wc, wglu_f32=w_glu,
        a_re=ab_re.reshape(SSM_KBLK, 1, SSM_KSTATE), a_im=ab_im.reshape(SSM_KBLK, 1, SSM_KSTATE),
        dsk=d_skip.reshape(1, SSM_WIDTH), wglu=w_glu.astype(BF16), bglu=b_glu.reshape(1, SSM_WIDTH))


def _state_to_blocks(h):
    bsz = h.shape[0]
    return h.reshape(bsz, SSM_KBLK, SSM_KSTATE).transpose(1, 0, 2)


def _blocks_to_state(s):
    bsz = s.shape[1]
    return s.transpose(1, 0, 2).reshape(bsz, SSM_GROUPS, SSM_STATE)


def _mem_attn_kernel(q_ref, k_ref, v_ref, o_ref, *, precise):
    scale = MEM_HEAD_DIM ** -0.5
    for h in range(MEM_HEADS):
        sl = slice(h * MEM_HEAD_DIM, (h + 1) * MEM_HEAD_DIM)
        s = _mm(q_ref[:, sl], k_ref[:, sl], precise, NT_DIMS) * scale
        m = jnp.max(s, axis=-1, keepdims=True)
        p = jnp.exp(s - m)
        den = jnp.sum(p, axis=-1, keepdims=True)
        o_ref[:, sl] = _mm(p / den, v_ref[:, sl], precise).astype(o_ref.dtype)


def mem_attention(q, mk, mv, tl, out_dtype=F32, precise=False):
    bsz, seq, _ = q.shape
    tl = min(tl, seq)
    return pl.pallas_call(
        functools.partial(_mem_attn_kernel, precise=precise),
        grid=(bsz, seq // tl),
        in_specs=[
            pl.BlockSpec((None, tl, MEM_WIDTH), lambda b, i: (b, i, 0)),
            pl.BlockSpec((None, MEM_TOKENS, MEM_WIDTH), lambda b, i: (b, 0, 0)),
            pl.BlockSpec((None, MEM_TOKENS, MEM_WIDTH), lambda b, i: (b, 0, 0)),
        ],
        out_specs=pl.BlockSpec((None, tl, MEM_WIDTH), lambda b, i: (b, i, 0)),
        out_shape=jax.ShapeDtypeStruct((bsz, seq, MEM_WIDTH), out_dtype),
        compiler_params=_cparams(("arbitrary", "arbitrary")),
        name="mem_attention",
    )(q, mk, mv)


def _route(logits):
    lane = lax.broadcasted_iota(jnp.int32, logits.shape, 1)
    lanef = lane.astype(F32)
    none = float(ROUTER_COLS)
    neg = -jnp.inf
    gl = jnp.where(lane < MOE_GROUPS, logits, neg)
    gmax = jnp.max(gl, axis=-1, keepdims=True)
    grp = jnp.min(jnp.where(gl == gmax, lanef, none), axis=-1, keepdims=True)
    p_grp = 1.0 / jnp.sum(jnp.exp(gl - gmax), axis=-1, keepdims=True)
    lo = MOE_GROUPS + grp * EXPERTS_PER_GROUP
    el = jnp.where(jnp.logical_and(lanef >= lo, lanef < lo + EXPERTS_PER_GROUP), logits, neg)
    v1 = jnp.max(el, axis=-1, keepdims=True)
    i1 = jnp.min(jnp.where(el == v1, lanef, none), axis=-1, keepdims=True)
    el2 = jnp.where(lanef == i1, neg, el)
    v2 = jnp.max(el2, axis=-1, keepdims=True)
    i2 = jnp.min(jnp.where(el2 == v2, lanef, none), axis=-1, keepdims=True)
    t = jnp.exp(v2 - v1)
    g1 = p_grp / (1.0 + t)
    g2 = g1 * t
    out = jnp.where(lane == 0, g1, 0.0)
    out = jnp.where(lane == 1, g2, out)
    out = jnp.where(lane == 2, i1 - MOE_GROUPS, out)
    return jnp.where(lane == 3, i2 - MOE_GROUPS, out)


def _merge_kernel(att_ref, ssm_ref, mem_ref, gate_ref, x_ref, wa_ref, ws_ref, wm_ref, wo_ref,
                  n2_ref, wr_ref, br_ref, h_ref, hn_ref, lg_ref, *, tm, precise):
    a = _mm(att_ref[...], wa_ref[...], precise)
    merged = gate_ref[:, 0:D_MODEL] * a
    s = _mm(ssm_ref[...], ws_ref[...], precise)
    merged = merged + gate_ref[:, D_MODEL:2 * D_MODEL] * s
    m = _mm(mem_ref[...], wm_ref[...], precise)
    merged = merged + gate_ref[:, 2 * D_MODEL:3 * D_MODEL] * m
    h = x_ref[...] + _mm(merged, wo_ref[...], precise)
    h_ref[...] = h
    hn = _rms(h, n2_ref[...])
    for j in range(D_MODEL // LANES):
        hn_ref[pl.ds(j, tm, stride=SUBLANES), :] = hn[:, j * LANES:(j + 1) * LANES]
    logits = _mm(hn, wr_ref[...], precise) + br_ref[...]
    lg_ref[...] = _route(logits)


def merge(att, ssm, mem, gates, x, wts, tm, precise=False):
    n = x.shape[0]
    tm = min(tm, n)
    row = lambda w: pl.BlockSpec((tm, w), lambda i: (i, 0))
    const = lambda a: pl.BlockSpec(a.shape, lambda i: (0, 0))
    ws = [wts["wa"], wts["ws"], wts["wm"], wts["wo"], wts["n2"], wts["wr"], wts["br"]]
    return pl.pallas_call(
        functools.partial(_merge_kernel, tm=tm, precise=precise),
        grid=(n // tm,),
        in_specs=[row(ATT_OUT), row(SSM_WIDTH), row(MEM_WIDTH), row(N_BRANCHES * D_MODEL), row(D_MODEL)]
                 + [const(w) for w in ws],
        out_specs=[row(D_MODEL), pl.BlockSpec((tm * SUBLANES, LANES), lambda i: (i, 0)), row(ROUTER_COLS)],
        out_shape=[jax.ShapeDtypeStruct((n, D_MODEL), F32),
                   jax.ShapeDtypeStruct((n * SUBLANES, LANES), F32),
                   jax.ShapeDtypeStruct((n, ROUTER_COLS), F32)],
        compiler_params=_cparams(("arbitrary",)),
        name="merge",
    )(att, ssm, mem, gates, x, *ws)


TABLE_WORDS = SUBLANES * LANES


def _table_entry(smem, slot, flat):
    return smem[slot, lax.shift_right_logical(flat, 7), jnp.bitwise_and(flat, LANES - 1)]


def _dispatch_kernel(pos_hbm, hn_ref, xs_hbm, pos_smem, zero, sem, psem, *, tm, n_pad):
    i = pl.program_id(0)
    slot = i % 2

    def pos_copy(b, s):
        return pltpu.make_async_copy(pos_hbm.at[b], pos_smem.at[s], psem.at[s])

    @pl.when(i == 0)
    def _():
        pos_copy(0, 0).start()

    @pl.when(i + 1 < pl.num_programs(0))
    def _():
        pos_copy(i + 1, 1 - slot).start()

    zero[...] = jnp.zeros_like(zero)
    pos_copy(i, slot).wait()

    def body(c, carry):
        for j in range(SUBLANES):
            tok = c * SUBLANES + j
            src = hn_ref.at[pl.ds(pl.multiple_of(tok * SUBLANES, SUBLANES), SUBLANES)]
            for k in range(TOP_K):
                row = _table_entry(pos_smem, slot, c * SUBLANES + k * tm + j)
                pltpu.make_async_copy(src, xs_hbm.at[row], sem).start(priority=(j + k) % 2)
        return carry

    lax.fori_loop(0, tm // SUBLANES, body, 0)

    def pad_body(c, carry):
        for j in range(SUBLANES):
            row = _table_entry(pos_smem, slot, TOP_K * tm + c * SUBLANES + j)
            pltpu.make_async_copy(zero, xs_hbm.at[row], sem).start(priority=j % 2)
        return carry

    lax.fori_loop(0, n_pad // SUBLANES, pad_body, 0)
    whole, part = divmod(n_pad, tm)
    for _ in range(TOP_K + whole):
        pltpu.make_async_copy(hn_ref, hn_ref, sem).wait()
    if part:
        piece = hn_ref.at[pl.ds(0, part * SUBLANES)]
        pltpu.make_async_copy(piece, piece, sem).wait()


def moe_dispatch(pos, hn2d, n_rows, tm, n_pad):
    n_tiles = pos.shape[0]
    return pl.pallas_call(
        functools.partial(_dispatch_kernel, tm=tm, n_pad=n_pad),
        grid=(n_tiles,),
        in_specs=[pl.BlockSpec(memory_space=pl.ANY),
                  pl.BlockSpec((tm * SUBLANES, LANES), lambda i: (i, 0))],
        out_specs=pl.BlockSpec(memory_space=pl.ANY),
        out_shape=jax.ShapeDtypeStruct((n_rows, SUBLANES, LANES), F32),
        scratch_shapes=[pltpu.SMEM((2, SUBLANES, LANES), jnp.int32),
                        pltpu.VMEM((SUBLANES, LANES), F32),
                        pltpu.SemaphoreType.DMA, pltpu.SemaphoreType.DMA((2,))],
        compiler_params=_cparams(("arbitrary",)),
        name="moe_dispatch",
    )(pos, hn2d)


def _moe_kernel(be_ref, dst_hbm, xs_ref, w1_ref, w3_ref, w2_ref, out_hbm,
                dst_smem, ybuf, w1b, w3b, w2b, ssem, isem, *, bm):
    blk = pl.program_id(0)
    nblk = pl.num_programs(0)
    slot = blk % 2
    tiles = D_MODEL // LANES

    def dst_copy(b, s):
        return pltpu.make_async_copy(dst_hbm.at[b], dst_smem.at[s], isem.at[s])

    def wait_scatter():
        pltpu.make_async_copy(ybuf, ybuf, ssem).wait()

    @pl.when(blk == 0)
    def _():
        dst_copy(0, 0).start()

    @pl.when(blk + 1 < nblk)
    def _():
        dst_copy(blk + 1, 1 - slot).start()

    prev = jnp.maximum(blk - 1, 0)
    changed = jnp.logical_or(blk == 0, be_ref[blk] != be_ref[prev])

    @pl.when(changed)
    def _():
        w1b[...] = w1_ref[...].astype(BF16)
        w3b[...] = w3_ref[...].astype(BF16)
        w2b[...] = w2_ref[...].astype(BF16)

    x = jnp.concatenate([xs_ref[pl.ds(j, bm, stride=SUBLANES), :].astype(BF16) for j in range(tiles)], axis=1)
    h1 = jnp.dot(x, w1b[...], preferred_element_type=F32)
    h3 = jnp.dot(x, w3b[...], preferred_element_type=F32)
    act = (jax.nn.silu(h1) * h3).astype(BF16)
    y = jnp.dot(act, w2b[...], preferred_element_type=F32)

    @pl.when(blk > 0)
    def _():
        wait_scatter()

    for j in range(tiles):
        ybuf[pl.ds(j, bm, stride=SUBLANES), :] = y[:, j * LANES:(j + 1) * LANES]
    dst_copy(blk, slot).wait()

    def body(c, carry):
        for j in range(SUBLANES):
            row = _table_entry(dst_smem, slot, c * SUBLANES + j)
            r0 = pl.multiple_of((c * SUBLANES + j) * SUBLANES, SUBLANES)
            pltpu.make_async_copy(ybuf.at[pl.ds(r0, SUBLANES)], out_hbm.at[row], ssem).start(priority=j % 2)
        return carry

    lax.fori_loop(0, bm // SUBLANES, body, 0)

    @pl.when(blk == nblk - 1)
    def _():
        wait_scatter()


def moe_experts(block_expert, dst, xs2d, w1, w3, w2, n_out_rows, bm):
    n_blocks = dst.shape[0]
    grid_spec = pltpu.PrefetchScalarGridSpec(
        num_scalar_prefetch=1,
        grid=(n_blocks,),
        in_specs=[
            pl.BlockSpec(memory_space=pl.ANY),
            pl.BlockSpec((bm * SUBLANES, LANES), lambda b, be: (b, 0)),
            pl.BlockSpec((None, D_MODEL, D_EXPERT), lambda b, be: (be[b], 0, 0)),
            pl.BlockSpec((None, D_MODEL, D_EXPERT), lambda b, be: (be[b], 0, 0)),
            pl.BlockSpec((None, D_EXPERT, D_MODEL), lambda b, be: (be[b], 0, 0)),
        ],
        out_specs=pl.BlockSpec(memory_space=pl.ANY),
        scratch_shapes=[
            pltpu.SMEM((2, SUBLANES, LANES), jnp.int32),
            pltpu.VMEM((bm * SUBLANES, LANES), F32),
            pltpu.VMEM((D_MODEL, D_EXPERT), BF16),
            pltpu.VMEM((D_MODEL, D_EXPERT), BF16),
            pltpu.VMEM((D_EXPERT, D_MODEL), BF16),
            pltpu.SemaphoreType.DMA,
            pltpu.SemaphoreType.DMA((2,)),
        ],
    )
    return pl.pallas_call(
        functools.partial(_moe_kernel, bm=bm),
        grid_spec=grid_spec,
        out_shape=jax.ShapeDtypeStruct((n_out_rows, SUBLANES, LANES), F32),
        compiler_params=_cparams(("arbitrary",)),
        name="moe_experts",
    )(block_expert, dst, xs2d, w1, w3, w2)


def _as_tables(cols, width):
    fill = jnp.zeros((cols.shape[0], TABLE_WORDS - width), jnp.int32)
    return jnp.concatenate([cols, fill], axis=1).reshape(cols.shape[0], SUBLANES, LANES)


def route_tables(route, bm, tm):
    n_tok = route.shape[0]
    n_pairs = n_tok * TOP_K
    e_flat = route[:, 2:2 + TOP_K].astype(jnp.int32).reshape(n_pairs)
    experts = jnp.arange(N_EXPERTS, dtype=jnp.int32)
    counts = jnp.sum(e_flat[:, None] == experts[None, :], axis=0, dtype=jnp.int32)
    padded = (counts + bm - 1) // bm * bm
    pad_ends = jnp.cumsum(padded)
    n_blocks = -(-n_pairs // bm) + N_EXPERTS
    n_rows = n_blocks * bm
    row0 = jnp.arange(n_blocks, dtype=jnp.int32) * bm
    block_expert = jnp.minimum(jnp.sum(pad_ends[None, :] <= row0[:, None], axis=1, dtype=jnp.int32),
                               N_EXPERTS - 1)
    n_fill = n_rows - n_pairs
    fill_id = jnp.arange(n_fill, dtype=jnp.int32)
    fill_e = jnp.minimum(fill_id // bm, N_EXPERTS - 1)
    fill_used = jnp.logical_and(fill_id < N_EXPERTS * bm, fill_id % bm < (padded - counts)[fill_e])
    keys = jnp.concatenate([e_flat * 2, jnp.where(fill_used, fill_e * 2 + 1, 2 * N_EXPERTS)])
    vals = jnp.concatenate([jnp.arange(n_pairs, dtype=jnp.int32), jnp.full((n_fill,), -1, jnp.int32)])
    _, pair_at = lax.sort((keys, vals), num_keys=1, is_stable=True)
    _, row_of = lax.sort((pair_at, jnp.arange(n_rows, dtype=jnp.int32)), num_keys=1)
    n_tiles = n_tok // tm
    n_pad = n_fill // n_tiles
    assert n_pad * n_tiles == n_fill and n_pad % SUBLANES == 0
    pair_row = row_of[n_fill:].reshape(n_tok, TOP_K)
    pos = jnp.concatenate([pair_row[:, k].reshape(n_tiles, tm) for k in range(TOP_K)]
                          + [row_of[:n_fill].reshape(n_tiles, n_pad)], axis=1)
    pair_at = pair_at.reshape(n_blocks, bm)
    valid = pair_at >= 0
    trash = n_pairs + jnp.arange(bm, dtype=jnp.int32)[None, :]
    dst = jnp.where(valid, (pair_at % TOP_K) * n_tok + pair_at // TOP_K, trash)
    return block_expert, _as_tables(dst, bm), _as_tables(pos, TOP_K * tm + n_pad), n_pad, n_rows, n_pairs + bm


def _final_kernel(h_ref, p0_ref, p1_ref, gw_ref, nf_ref, y_ref, *, tm):
    tiles = D_MODEL // LANES
    p0 = jnp.concatenate([p0_ref[pl.ds(j, tm, stride=SUBLANES), :] for j in range(tiles)], axis=1)
    p1 = jnp.concatenate([p1_ref[pl.ds(j, tm, stride=SUBLANES), :] for j in range(tiles)], axis=1)
    gw = gw_ref[...]
    h = h_ref[...] + (gw[:, 0:1] * p0 + gw[:, 1:2] * p1)
    y_ref[...] = _rms(h, nf_ref[...])


def final(h, pairs2d, gate_pad, norm_f, tm):
    n = h.shape[0]
    tm = min(tm, n)
    nb = n // tm
    row = lambda w: pl.BlockSpec((tm, w), lambda i: (i, 0))
    return pl.pallas_call(
        functools.partial(_final_kernel, tm=tm),
        grid=(nb,),
        in_specs=[
            row(D_MODEL),
            pl.BlockSpec((tm * SUBLANES, LANES), lambda i: (i, 0)),
            pl.BlockSpec((tm * SUBLANES, LANES), lambda i: (nb + i, 0)),
            row(LANES),
            pl.BlockSpec((1, D_MODEL), lambda i: (0, 0)),
        ],
        out_specs=row(D_MODEL),
        out_shape=jax.ShapeDtypeStruct((n, D_MODEL), F32),
        compiler_params=_cparams(("arbitrary",)),
        name="final",
    )(h, pairs2d, pairs2d, gate_pad, norm_f.reshape(1, D_MODEL))


def _alibi_slopes():
    h = jnp.arange(1, N_ATT_HEADS + 1, dtype=F32)
    return jnp.exp2(-8.0 * h / N_ATT_HEADS)


def _token_stage(att, ssm, mem, gates, x2, wts, w1, w3, w2, norm_f, tm, bm, precise=False):
    n = x2.shape[0]
    h, hn2d, route = merge(att, ssm, mem, gates, x2, wts, tm, precise)
    tm = min(tm, n)
    block_expert, dst, pos, n_pad, n_rows, out_rows = route_tables(route, bm, tm)
    xs = moe_dispatch(pos, hn2d, n_rows, tm, n_pad)
    pairs = moe_experts(block_expert, dst, xs.reshape(n_rows * SUBLANES, LANES), w1, w3, w2, out_rows, bm)
    return final(h, pairs.reshape(out_rows * SUBLANES, LANES), route, norm_f, tm)


def kernel(x_prompt, x_sample, cache_swa0_k, cache_swa0_v, cache_swa1_k, cache_swa1_v, cache_swa2_k, cache_swa2_v, cache_mem_k, cache_mem_v, state_ssm_re, state_ssm_im, mem_prompt, norm1, w_in, lam_re, lam_im, log_dt, ssm_b_re, ssm_b_im, ssm_c_re, ssm_c_im, ssm_d, w_glu, b_glu, w_att_out, w_ssm_out, w_mem_out, w_o, norm_mem, w_mk, w_mv, norm2, w_grp, b_grp, w_exp, b_exp, w1, w3, w2, norm_f):
    assert norm1.shape[0] == 1, "single-layer trunk"
    bsz, seq, _ = x_prompt.shape
    sbz = x_sample.shape[0]
    n_p = bsz * seq
    caches = (cache_swa0_k[0], cache_swa0_v[0], cache_swa1_k[0], cache_swa1_v[0], cache_swa2_k[0], cache_swa2_v[0])

    w_in_b = w_in[0].astype(BF16)
    w_mkv_b = jnp.concatenate([w_mk[0], w_mv[0]], axis=1).astype(BF16)
    wr = jnp.zeros((D_MODEL, ROUTER_COLS), F32)
    wr = wr.at[:, :MOE_GROUPS].set(w_grp[0]).at[:, MOE_GROUPS:MOE_GROUPS + N_EXPERTS].set(w_exp[0])
    br = jnp.zeros((1, ROUTER_COLS), F32)
    br = br.at[0, :MOE_GROUPS].set(b_grp[0]).at[0, MOE_GROUPS:MOE_GROUPS + N_EXPERTS].set(b_exp[0])
    wts_f32 = dict(wa=w_att_out[0], ws=w_ssm_out[0], wm=w_mem_out[0], wo=w_o[0],
                   n2=norm2[0].reshape(1, D_MODEL), wr=wr, br=br)
    wts = {k: (v.astype(BF16) if k in ("wa", "ws", "wm", "wo", "wr") else v) for k, v in wts_f32.items()}
    prm = ssm_params(lam_re[0], lam_im[0], log_dt[0], ssm_b_re[0], ssm_b_im[0], ssm_c_re[0], ssm_c_im[0],
                     ssm_d[0], w_glu[0], b_glu[0])
    slopes = _alibi_slopes()
    splits = (QKV_COLS, SSM_WIDTH, MEM_WIDTH, N_BRANCHES * D_MODEL)
    sig = (False, False, False, True)

    x2 = x_prompt.reshape(n_p, D_MODEL)
    qkv, u, mq, gates = norm_matmul(x2, norm1[0], w_in_b, splits, sig, tm=256, dtypes=(F32, F32, BF16, F32))
    qkv3 = qkv.reshape(bsz, seq, QKV_COLS)
    att = attn_prompt(qkv3, slopes)
    ssm, st = ssm_prompt(u.reshape(bsz, seq, SSM_WIDTH), prm)
    mk, mv = norm_matmul(mem_prompt.reshape(bsz * MEM_TOKENS, D_MODEL), norm_mem[0], w_mkv_b,
                         (MEM_WIDTH, MEM_WIDTH), (False, False), tm=256)
    mk3 = mk.reshape(bsz, MEM_TOKENS, MEM_WIDTH)
    mv3 = mv.reshape(bsz, MEM_TOKENS, MEM_WIDTH)
    mem = mem_attention(mq.reshape(bsz, seq, MEM_WIDTH), mk3, mv3, tl=512, out_dtype=BF16)
    y_p = _token_stage(att.reshape(n_p, ATT_OUT), ssm.reshape(n_p, SSM_WIDTH), mem.reshape(n_p, MEM_WIDTH),
                       gates, x2, wts, w1[0], w3[0], w2[0], norm_f, tm=256, bm=256)

    xs2 = x_sample.reshape(sbz, D_MODEL)
    qkv_s, u_s, mq_s, gates_s = norm_matmul(xs2, norm1[0], w_in[0], splits, sig, tm=sbz, precise=True)
    qkv5 = qkv_s.reshape(sbz, len(ATT_GROUPS), 3, HEADS_PER_GROUP, HEAD_DIM)
    att_s, *new_caches = attn_sample(qkv5, [jnp.transpose(c, (0, 2, 3, 1)) for c in caches])
    att_s = att_s.reshape(sbz, ATT_OUT)
    h0 = jnp.concatenate([_state_to_blocks(state_ssm_re[0]), _state_to_blocks(state_ssm_im[0])], axis=2)
    ssm_s, st_s = ssm_sample(u_s, h0, prm)
    cmk = cache_mem_k[0].reshape(sbz, MEM_TOKENS, MEM_WIDTH)
    cmv = cache_mem_v[0].reshape(sbz, MEM_TOKENS, MEM_WIDTH)
    mem_s = mem_attention(mq_s.reshape(sbz, 1, MEM_WIDTH), cmk, cmv, tl=1, precise=True).reshape(sbz, MEM_WIDTH)
    y_s = _token_stage(att_s, ssm_s, mem_s, gates_s, xs2, wts_f32, w1[0], w3[0], w2[0], norm_f, tm=sbz, bm=16,
                       precise=True)

    outs = [y_p.reshape(bsz, seq, D_MODEL), y_s.reshape(sbz, 1, D_MODEL)]
    for g, (window, _) in enumerate(ATT_GROUPS):
        keep = min(window, seq)
        for part in (1, 2):
            c0 = g * QKV_GROUP + part * ATT_OUT
            outs.append(qkv3[:, seq - keep:, c0:c0 + ATT_OUT].reshape(1, bsz, keep, HEADS_PER_GROUP, HEAD_DIM))
    outs.append(mk3.reshape(1, bsz, MEM_TOKENS, MEM_HEADS, MEM_HEAD_DIM))
    outs.append(mv3.reshape(1, bsz, MEM_TOKENS, MEM_HEADS, MEM_HEAD_DIM))
    outs.append(_blocks_to_state(st[:, :, :SSM_KSTATE])[None])
    outs.append(_blocks_to_state(st[:, :, SSM_KSTATE:])[None])
    outs.extend(jnp.transpose(c, (0, 3, 1, 2))[None] for c in new_caches)
    outs.append(_blocks_to_state(st_s[:, :, :SSM_KSTATE])[None])
    outs.append(_blocks_to_state(st_s[:, :, SSM_KSTATE:])[None])
    return tuple(outs)
```

```python
import functools
import math

import jax
import jax.numpy as jnp
from jax import lax
from jax.experimental import pallas as pl
from jax.experimental.pallas import tpu as pltpu

F32 = jnp.float32
BF16 = jnp.bfloat16

D_MODEL = 1024
ATT_GROUPS = ((128, 1), (512, 4), (2048, 16))
HEADS_PER_GROUP = 4
HEAD_DIM = 64
N_ATT_HEADS = len(ATT_GROUPS) * HEADS_PER_GROUP
ATT_OUT = HEADS_PER_GROUP * HEAD_DIM
QKV_GROUP = 3 * ATT_OUT
QKV_COLS = len(ATT_GROUPS) * QKV_GROUP
SSM_WIDTH = D_MODEL // 2
SSM_GROUP_CH = 16
SSM_GROUPS = SSM_WIDTH // SSM_GROUP_CH
SSM_STATE = 64
SSM_NSTATE = SSM_GROUPS * SSM_STATE
MEM_TOKENS = 256
MEM_HEADS = 4
MEM_HEAD_DIM = D_MODEL // 8
MEM_WIDTH = MEM_HEADS * MEM_HEAD_DIM
N_BRANCHES = 3
MOE_GROUPS = 4
EXPERTS_PER_GROUP = 8
N_EXPERTS = MOE_GROUPS * EXPERTS_PER_GROUP
TOP_K = 2
D_EXPERT = D_MODEL // 2
RMS_EPS = 1e-6

LANES = 128
SUBLANES = 8
Q_TILE = 128
ROUTER_COLS = 128
NEG_BIG = -1e30
VMEM_LIMIT = 56 * 1024 * 1024


def _cparams(sem):
    return pltpu.CompilerParams(dimension_semantics=sem, vmem_limit_bytes=VMEM_LIMIT)


def _mm(a, b, precise, dims=None):
    if precise:
        a, b, kw = a.astype(F32), b.astype(F32), dict(precision=lax.Precision.HIGHEST)
    else:
        a, b, kw = a.astype(BF16), b.astype(BF16), {}
    if dims is None:
        return jnp.dot(a, b, preferred_element_type=F32, **kw)
    return lax.dot_general(a, b, dims, preferred_element_type=F32, **kw)


NT_DIMS = (((1,), (1,)), ((), ()))


def _rms(x, g):
    ms = jnp.mean(x * x, axis=-1, keepdims=True)
    return (x * lax.rsqrt(ms + RMS_EPS)) * g


def _norm_matmul_kernel(x_ref, g_ref, w_ref, *out_refs, splits, sigmoid, chunk, precise):
    xb = _rms(x_ref[...], g_ref[...])
    if not precise:
        xb = xb.astype(BF16)
    c0 = 0
    for o_ref, n, sg in zip(out_refs, splits, sigmoid):
        for j in range(0, n, chunk):
            w = min(chunk, n - j)
            y = _mm(xb, w_ref[:, c0 + j:c0 + j + w], precise)
            if sg:
                y = jax.nn.sigmoid(y)
            o_ref[:, j:j + w] = y.astype(o_ref.dtype)
        c0 += n


def norm_matmul(x, g, w, splits, sigmoid, tm, dtypes=None, precise=False):
    n, d = x.shape
    tm = min(tm, n)
    kern = functools.partial(_norm_matmul_kernel, splits=tuple(splits), sigmoid=tuple(sigmoid), chunk=512,
                             precise=precise)
    return pl.pallas_call(
        kern,
        grid=(n // tm,),
        in_specs=[
            pl.BlockSpec((tm, d), lambda i: (i, 0)),
            pl.BlockSpec((1, d), lambda i: (0, 0)),
            pl.BlockSpec((d, w.shape[1]), lambda i: (0, 0), pipeline_mode=pl.Buffered(1)),
        ],
        out_specs=[pl.BlockSpec((tm, s), lambda i: (i, 0)) for s in splits],
        out_shape=[jax.ShapeDtypeStruct((n, s), dt) for s, dt in zip(splits, dtypes or (F32,) * len(splits))],
        compiler_params=_cparams(("arbitrary",)),
        name="norm_matmul",
    )(x, g.reshape(1, d), w)


def _attn_prompt_kernel(slopes_ref, *refs, seq):
    qkv_refs = refs[:9]
    o_ref = refs[9]
    o_scr, l_scr = refs[10], refs[11]
    hp = pl.program_id(1)
    scale = HEAD_DIM ** -0.5
    row = lax.broadcasted_iota(jnp.int32, (Q_TILE, 2 * Q_TILE), 0)
    col = lax.broadcasted_iota(jnp.int32, (Q_TILE, 2 * Q_TILE), 1)
    dist = (row + Q_TILE - col)
    in_window = jnp.logical_and(dist >= 0, dist <= Q_TILE)
    distf = dist.astype(F32)
    lane = lax.broadcasted_iota(jnp.int32, (Q_TILE, LANES), 1)
    head_a = lane < HEAD_DIM

    for g, (_, dil) in enumerate(ATT_GROUPS):
        q_ref, k_ref, v_ref = qkv_refs[3 * g:3 * g + 3]
        sub_len = seq // dil
        n_blk = sub_len // Q_TILE
        slope_a = slopes_ref[g * HEADS_PER_GROUP + 2 * hp] * float(dil)
        slope_b = slopes_ref[g * HEADS_PER_GROUP + 2 * hp + 1] * float(dil)

        def rows(start):
            if dil == 1:
                return pl.ds(start, Q_TILE)
            return pl.ds(start, Q_TILE, stride=dil)

        def body(it, carry, q_ref=q_ref, k_ref=k_ref, v_ref=v_ref, n_blk=n_blk, dil=dil,
                 slope_a=slope_a, slope_b=slope_b, rows=rows, g=g):
            r = it // n_blk
            blk = it % n_blk
            prev = jnp.maximum(blk - 1, 0)
            q_start = r + dil * Q_TILE * blk
            p_start = r + dil * Q_TILE * prev
            q = q_ref[rows(q_start), :]
            k2 = jnp.concatenate([k_ref[rows(p_start), :], k_ref[rows(q_start), :]], axis=0).astype(BF16)
            v2 = jnp.concatenate([v_ref[rows(p_start), :], v_ref[rows(q_start), :]], axis=0).astype(BF16)
            valid = jnp.logical_and(in_window, jnp.logical_or(col >= Q_TILE, blk > 0))
            outs, lses = [], []
            for is_a, slope in ((True, slope_a), (False, slope_b)):
                hmask = head_a if is_a else jnp.logical_not(head_a)
                qh = jnp.where(hmask, q, 0.0).astype(BF16)
                s = lax.dot_general(qh, k2, NT_DIMS, preferred_element_type=F32)
                s = s * scale - slope * distf
                s = jnp.where(valid, s, NEG_BIG)
                m = jnp.max(s, axis=-1, keepdims=True)
                p = jnp.exp(s - m)
                den = jnp.sum(p, axis=-1, keepdims=True)
                pv = jnp.dot(p.astype(BF16), v2, preferred_element_type=F32)
                outs.append(pv / den)
                lses.append(m + jnp.log(den))
            o_scr[g, rows(q_start), :] = jnp.where(head_a, outs[0], outs[1])
            l_scr[g, rows(q_start), :] = jnp.where(head_a, lses[0], lses[1])
            return carry

        lax.fori_loop(0, dil * n_blk, body, 0, unroll=8)

    def combine(c, carry):
        sl = pl.ds(pl.multiple_of(c * 256, 256), 256)
        l0, l1, l2 = l_scr[0, sl, :], l_scr[1, sl, :], l_scr[2, sl, :]
        m = jnp.maximum(jnp.maximum(l0, l1), l2)
        e0, e1, e2 = jnp.exp(l0 - m), jnp.exp(l1 - m), jnp.exp(l2 - m)
        tot = e0 + e1 + e2
        att = (e0 * o_scr[0, sl, :] + e1 * o_scr[1, sl, :] + e2 * o_scr[2, sl, :]) / tot
        o_ref[sl, :] = att.astype(o_ref.dtype)
        return carry

    lax.fori_loop(0, seq // 256, combine, 0)


def attn_prompt(qkv, slopes):
    bsz, seq, _ = qkv.shape
    in_specs = []
    for g in range(len(ATT_GROUPS)):
        for part in range(3):
            base = (g * QKV_GROUP + part * ATT_OUT) // LANES
            in_specs.append(pl.BlockSpec((None, seq, LANES),
                                         lambda b, h, sl, base=base: (b, 0, base + h)))
    grid_spec = pltpu.PrefetchScalarGridSpec(
        num_scalar_prefetch=1,
        grid=(bsz, ATT_OUT // LANES),
        in_specs=in_specs,
        out_specs=pl.BlockSpec((None, seq, LANES), lambda b, h, sl: (b, 0, h)),
        scratch_shapes=[pltpu.VMEM((3, seq, LANES), F32), pltpu.VMEM((3, seq, LANES), F32)],
    )
    return pl.pallas_call(
        functools.partial(_attn_prompt_kernel, seq=seq),
        grid_spec=grid_spec,
        out_shape=jax.ShapeDtypeStruct((bsz, seq, ATT_OUT), BF16),
        compiler_params=_cparams(("arbitrary", "arbitrary")),
        name="attn_prompt",
    )(slopes, *([qkv] * 9))


def _attn_sample_kernel(qkv_ref, col_ref, *refs):
    cache_refs = refs[:6]
    o_ref = refs[6]
    new_refs = refs[7:13]
    scale = HEAD_DIM ** -0.5
    n_grp = len(ATT_GROUPS)
    outs = [[None] * HEADS_PER_GROUP for _ in range(n_grp)]
    lses = [[None] * HEADS_PER_GROUP for _ in range(n_grp)]
    for g, (window, dil) in enumerate(ATT_GROUPS):
        lane = lax.broadcasted_iota(jnp.int32, (1, window), 1)
        on_grid = jnp.bitwise_and(lane, dil - 1) == 0
        dist = (window - lane).astype(F32)
        last = lax.broadcasted_iota(jnp.int32, (HEAD_DIM, window), 1) == window - 1
        for h in range(HEADS_PER_GROUP):
            slope = 2.0 ** (-8.0 * (g * HEADS_PER_GROUP + h + 1) / N_ATT_HEADS)
            q = qkv_ref[g, 0][h:h + 1, :]
            kn = qkv_ref[g, 1][h:h + 1, :]
            vn = qkv_ref[g, 2][h:h + 1, :]
            kt = cache_refs[2 * g][h]
            vt = cache_refs[2 * g + 1][h]
            s = _mm(q, kt, True) * scale - slope * dist
            s = jnp.where(on_grid, s, NEG_BIG)
            sn = jnp.sum(q * kn, axis=-1, keepdims=True) * scale
            m = jnp.maximum(jnp.max(s, axis=-1, keepdims=True), sn)
            p = jnp.exp(s - m)
            pn = jnp.exp(sn - m)
            den = jnp.sum(p, axis=-1, keepdims=True) + pn
            outs[g][h] = (_mm(p, vt, True, NT_DIMS) + pn * vn) / den
            lses[g][h] = m + jnp.log(den)
            new_refs[2 * g][h] = jnp.where(last, col_ref[g, 0, h], pltpu.roll(kt, window - 1, axis=1))
            new_refs[2 * g + 1][h] = jnp.where(last, col_ref[g, 1, h], pltpu.roll(vt, window - 1, axis=1))
    for h in range(HEADS_PER_GROUP):
        m = jnp.maximum(jnp.maximum(lses[0][h], lses[1][h]), lses[2][h])
        es = [jnp.exp(lses[g][h] - m) for g in range(n_grp)]
        num = es[0] * outs[0][h] + es[1] * outs[1][h] + es[2] * outs[2][h]
        o_ref[h:h + 1, :] = num / (es[0] + es[1] + es[2])


def attn_sample(qkv5, caches_t):
    bsz = qkv5.shape[0]
    cols = qkv5[:, :, 1:3].reshape(bsz, len(ATT_GROUPS), 2, HEADS_PER_GROUP, HEAD_DIM, 1)
    in_specs = [pl.BlockSpec((None, 3, 3, HEADS_PER_GROUP, HEAD_DIM), lambda b: (b, 0, 0, 0, 0)),
                pl.BlockSpec((None, 3, 2, HEADS_PER_GROUP, HEAD_DIM, 1), lambda b: (b, 0, 0, 0, 0, 0))]
    cache_specs = [pl.BlockSpec((None,) + c.shape[1:], lambda b: (b, 0, 0, 0)) for c in caches_t]
    return pl.pallas_call(
        _attn_sample_kernel,
        grid=(bsz,),
        in_specs=in_specs + cache_specs,
        out_specs=[pl.BlockSpec((None, HEADS_PER_GROUP, HEAD_DIM), lambda b: (b, 0, 0))] + cache_specs,
        out_shape=[jax.ShapeDtypeStruct((bsz, HEADS_PER_GROUP, HEAD_DIM), F32)]
                  + [jax.ShapeDtypeStruct(c.shape, F32) for c in caches_t],
        compiler_params=_cparams(("arbitrary",)),
        name="attn_sample",
    )(qkv5, cols, *caches_t)


SSM_KBLK = SSM_WIDTH // LANES
SSM_KSTATE = SSM_NSTATE // SSM_KBLK


def _ssm_tail(y, u, dsk, wglu, bglu, precise=False):
    y = y + dsk * u
    z = jax.nn.gelu(y)
    gl = _mm(z, wglu, precise) + bglu
    return z * jax.nn.sigmoid(gl)


def _ssm_prompt_kernel(u_ref, wb_ref, wc_ref, are_ref, aim_ref, dsk_ref, wglu_ref, bglu_ref,
                       o_ref, st_ref, uperm, bu, sbf, yperm, ynat, *, bsz, tt):
    rows = bsz * tt

    @pl.when(pl.program_id(0) == 0)
    def _():
        st_ref[...] = jnp.zeros_like(st_ref)

    for b in range(bsz):
        ub = u_ref[b]
        for k in range(SSM_KBLK):
            uperm[k, pl.ds(b, tt, stride=bsz), :] = ub[:, k * LANES:(k + 1) * LANES]

    for k in range(SSM_KBLK):
        bu[k] = jnp.dot(uperm[k].astype(BF16), wb_ref[k], preferred_element_type=F32)

    for k in range(SSM_KBLK):
        ar = jnp.broadcast_to(are_ref[k], (bsz, SSM_KSTATE))
        ai = jnp.broadcast_to(aim_ref[k], (bsz, SSM_KSTATE))
        st = st_ref[k]

        def step(t, carry, k=k, ar=ar, ai=ai):
            re, im = carry
            r0 = pl.multiple_of(t * bsz, bsz)
            b_t = bu[k, pl.ds(r0, bsz), :]
            nre = ar * re - ai * im + b_t[:, :SSM_KSTATE]
            nim = ar * im + ai * re + b_t[:, SSM_KSTATE:]
            sbf[k, pl.ds(r0, bsz), :] = jnp.concatenate([nre, nim], axis=1).astype(BF16)
            return nre, nim

        re, im = lax.fori_loop(0, tt, step, (st[:, :SSM_KSTATE], st[:, SSM_KSTATE:]), unroll=2)
        st_ref[k] = jnp.concatenate([re, im], axis=1)

    for k in range(SSM_KBLK):
        yperm[k] = jnp.dot(sbf[k], wc_ref[k], preferred_element_type=F32)

    for b in range(bsz):
        for k in range(SSM_KBLK):
            ynat[b * tt:(b + 1) * tt, k * LANES:(k + 1) * LANES] = yperm[k, pl.ds(b, tt, stride=bsz), :]

    u2 = u_ref[...].reshape(rows, SSM_WIDTH)
    out = _ssm_tail(ynat[...], u2, dsk_ref[...], wglu_ref[...], bglu_ref[...])
    o_ref[...] = out.reshape(bsz, tt, SSM_WIDTH).astype(o_ref.dtype)


def ssm_prompt(u, prm, tt=32):
    bsz, seq, _ = u.shape
    rows = bsz * tt
    const = lambda shape: pl.BlockSpec(shape, lambda i: (0,) * len(shape))
    return pl.pallas_call(
        functools.partial(_ssm_prompt_kernel, bsz=bsz, tt=tt),
        grid=(seq // tt,),
        in_specs=[
            pl.BlockSpec((bsz, tt, SSM_WIDTH), lambda i: (0, i, 0)),
            const((SSM_KBLK, LANES, 2 * SSM_KSTATE)),
            const((SSM_KBLK, 2 * SSM_KSTATE, LANES)),
            const((SSM_KBLK, 1, SSM_KSTATE)),
            const((SSM_KBLK, 1, SSM_KSTATE)),
            const((1, SSM_WIDTH)),
            const((SSM_WIDTH, SSM_WIDTH)),
            const((1, SSM_WIDTH)),
        ],
        out_specs=[
            pl.BlockSpec((bsz, tt, SSM_WIDTH), lambda i: (0, i, 0)),
            const((SSM_KBLK, bsz, 2 * SSM_KSTATE)),
        ],
        out_shape=[
            jax.ShapeDtypeStruct((bsz, seq, SSM_WIDTH), BF16),
            jax.ShapeDtypeStruct((SSM_KBLK, bsz, 2 * SSM_KSTATE), F32),
        ],
        scratch_shapes=[
            pltpu.VMEM((SSM_KBLK, rows, LANES), F32),
            pltpu.VMEM((SSM_KBLK, rows, 2 * SSM_KSTATE), F32),
            pltpu.VMEM((SSM_KBLK, rows, 2 * SSM_KSTATE), BF16),
            pltpu.VMEM((SSM_KBLK, rows, LANES), F32),
            pltpu.VMEM((rows, SSM_WIDTH), F32),
        ],
        compiler_params=_cparams(("arbitrary",)),
        name="ssm_prompt",
    )(u, prm["wb"], prm["wc"], prm["a_re"], prm["a_im"], prm["dsk"], prm["wglu"], prm["bglu"])


def _ssm_sample_kernel(u_ref, h0_ref, wb_ref, wc_ref, are_ref, aim_ref, dsk_ref, wglu_ref, bglu_ref,
                       o_ref, st_ref):
    u = u_ref[...]
    ys = []
    for k in range(SSM_KBLK):
        b_t = _mm(u[:, k * LANES:(k + 1) * LANES], wb_ref[k], True)
        h0 = h0_ref[k]
        re, im = h0[:, :SSM_KSTATE], h0[:, SSM_KSTATE:]
        ar, ai = are_ref[k], aim_ref[k]
        nre = ar * re - ai * im + b_t[:, :SSM_KSTATE]
        nim = ar * im + ai * re + b_t[:, SSM_KSTATE:]
        s = jnp.concatenate([nre, nim], axis=1)
        st_ref[k] = s
        ys.append(_mm(s, wc_ref[k], True))
    y = jnp.concatenate(ys, axis=1)
    o_ref[...] = _ssm_tail(y, u, dsk_ref[...], wglu_ref[...], bglu_ref[...], precise=True)


def ssm_sample(u, h0, prm):
    bsz = u.shape[0]
    return pl.pallas_call(
        _ssm_sample_kernel,
        out_shape=[jax.ShapeDtypeStruct((bsz, SSM_WIDTH), F32),
                   jax.ShapeDtypeStruct((SSM_KBLK, bsz, 2 * SSM_KSTATE), F32)],
        compiler_params=pltpu.CompilerParams(vmem_limit_bytes=VMEM_LIMIT),
        name="ssm_sample",
    )(u, h0, prm["wb_f32"], prm["wc_f32"], prm["a_re"], prm["a_im"], prm["dsk"], prm["wglu_f32"], prm["bglu"])


def ssm_params(lam_re, lam_im, log_dt, b_re, b_im, c_re, c_im, d_skip, w_glu, b_glu):
    dt = jnp.exp(log_dt)[:, None]
    mag = jnp.exp(lam_re * dt)
    ab_re = mag * jnp.cos(lam_im * dt)
    ab_im = mag * jnp.sin(lam_im * dt)
    den = lam_re * lam_re + lam_im * lam_im
    nr = ab_re - 1.0
    ni = ab_im
    z_re = ((nr * lam_re + ni * lam_im) / den)[..., None]
    z_im = ((ni * lam_re - nr * lam_im) / den)[..., None]
    bb_re = z_re * b_re - z_im * b_im
    bb_im = z_re * b_im + z_im * b_re
    gpb = SSM_GROUPS // SSM_KBLK
    eye = jnp.eye(gpb, dtype=F32)

    def in_mat(bb):
        bbk = bb.reshape(SSM_KBLK, gpb, SSM_STATE, SSM_GROUP_CH)
        m = jnp.einsum("kgpc,gh->kgchp", bbk, eye)
        return m.reshape(SSM_KBLK, LANES, SSM_KSTATE)

    def out_mat(c):
        ck = c.reshape(SSM_KBLK, gpb, SSM_GROUP_CH, SSM_STATE)
        m = jnp.einsum("kgcp,gh->kgphc", ck, eye)
        return m.reshape(SSM_KBLK, SSM_KSTATE, LANES)

    wb = jnp.concatenate([in_mat(bb_re), in_mat(bb_im)], axis=2)
    wc = jnp.concatenate([out_mat(c_re), -out_mat(c_im)], axis=1)
    return dict(
        wb=wb.astype(BF16), wc=wc.astype(BF16), wb_f32=wb, wc_f32=wc, wglu_f32=w_glu,
        a_re=ab_re.reshape(SSM_KBLK, 1, SSM_KSTATE), a_im=ab_im.reshape(SSM_KBLK, 1, SSM_KSTATE),
        dsk=d_skip.reshape(1, SSM_WIDTH), wglu=w_glu.astype(BF16), bglu=b_glu.reshape(1, SSM_WIDTH))


def _state_to_blocks(h):
    bsz = h.shape[0]
    return h.reshape(bsz, SSM_KBLK, SSM_KSTATE).transpose(1, 0, 2)


def _blocks_to_state(s):
    bsz = s.shape[1]
    return s.transpose(1, 0, 2).reshape(bsz, SSM_GROUPS, SSM_STATE)


def _mem_attn_kernel(q_ref, k_ref, v_ref, o_ref, *, precise):
    scale = MEM_HEAD_DIM ** -0.5
    for h in range(MEM_HEADS):
        sl = slice(h * MEM_HEAD_DIM, (h + 1) * MEM_HEAD_DIM)
        s = _mm(q_ref[:, sl], k_ref[:, sl], precise, NT_DIMS) * scale
        m = jnp.max(s, axis=-1, keepdims=True)
        p = jnp.exp(s - m)
        den = jnp.sum(p, axis=-1, keepdims=True)
        o_ref[:, sl] = _mm(p / den, v_ref[:, sl], precise).astype(o_ref.dtype)


def mem_attention(q, mk, mv, tl, out_dtype=F32, precise=False):
    bsz, seq, _ = q.shape
    tl = min(tl, seq)
    return pl.pallas_call(
        functools.partial(_mem_attn_kernel, precise=precise),
        grid=(bsz, seq // tl),
        in_specs=[
            pl.BlockSpec((None, tl, MEM_WIDTH), lambda b, i: (b, i, 0)),
            pl.BlockSpec((None, MEM_TOKENS, MEM_WIDTH), lambda b, i: (b, 0, 0)),
            pl.BlockSpec((None, MEM_TOKENS, MEM_WIDTH), lambda b, i: (b, 0, 0)),
        ],
        out_specs=pl.BlockSpec((None, tl, MEM_WIDTH), lambda b, i: (b, i, 0)),
        out_shape=jax.ShapeDtypeStruct((bsz, seq, MEM_WIDTH), out_dtype),
        compiler_params=_cparams(("arbitrary", "arbitrary")),
        name="mem_attention",
    )(q, mk, mv)


def _route(logits):
    lane = lax.broadcasted_iota(jnp.int32, logits.shape, 1)
    lanef = lane.astype(F32)
    none = float(ROUTER_COLS)
    neg = -jnp.inf
    gl = jnp.where(lane < MOE_GROUPS, logits, neg)
    gmax = jnp.max(gl, axis=-1, keepdims=True)
    grp = jnp.min(jnp.where(gl == gmax, lanef, none), axis=-1, keepdims=True)
    p_grp = 1.0 / jnp.sum(jnp.exp(gl - gmax), axis=-1, keepdims=True)
    lo = MOE_GROUPS + grp * EXPERTS_PER_GROUP
    el = jnp.where(jnp.logical_and(lanef >= lo, lanef < lo + EXPERTS_PER_GROUP), logits, neg)
    v1 = jnp.max(el, axis=-1, keepdims=True)
    i1 = jnp.min(jnp.where(el == v1, lanef, none), axis=-1, keepdims=True)
    el2 = jnp.where(lanef == i1, neg, el)
    v2 = jnp.max(el2, axis=-1, keepdims=True)
    i2 = jnp.min(jnp.where(el2 == v2, lanef, none), axis=-1, keepdims=True)
    t = jnp.exp(v2 - v1)
    g1 = p_grp / (1.0 + t)
    g2 = g1 * t
    out = jnp.where(lane == 0, g1, 0.0)
    out = jnp.where(lane == 1, g2, out)
    out = jnp.where(lane == 2, i1 - MOE_GROUPS, out)
    return jnp.where(lane == 3, i2 - MOE_GROUPS, out)


def _merge_kernel(att_ref, ssm_ref, mem_ref, gate_ref, x_ref, wa_ref, ws_ref, wm_ref, wo_ref,
                  n2_ref, wr_ref, br_ref, h_ref, hn_ref, lg_ref, *, tm, precise):
    a = _mm(att_ref[...], wa_ref[...], precise)
    merged = gate_ref[:, 0:D_MODEL] * a
    s = _mm(ssm_ref[...], ws_ref[...], precise)
    merged = merged + gate_ref[:, D_MODEL:2 * D_MODEL] * s
    m = _mm(mem_ref[...], wm_ref[...], precise)
    merged = merged + gate_ref[:, 2 * D_MODEL:3 * D_MODEL] * m
    h = x_ref[...] + _mm(merged, wo_ref[...], precise)
    h_ref[...] = h
    hn = _rms(h, n2_ref[...])
    for j in range(D_MODEL // LANES):
        hn_ref[pl.ds(j, tm, stride=SUBLANES), :] = hn[:, j * LANES:(j + 1) * LANES]
    logits = _mm(hn, wr_ref[...], precise) + br_ref[...]
    lg_ref[...] = _route(logits)


def merge(att, ssm, mem, gates, x, wts, tm, precise=False):
    n = x.shape[0]
    tm = min(tm, n)
    row = lambda w: pl.BlockSpec((tm, w), lambda i: (i, 0))
    const = lambda a: pl.BlockSpec(a.shape, lambda i: (0, 0))
    ws = [wts["wa"], wts["ws"], wts["wm"], wts["wo"], wts["n2"], wts["wr"], wts["br"]]
    return pl.pallas_call(
        functools.partial(_merge_kernel, tm=tm, precise=precise),
        grid=(n // tm,),
        in_specs=[row(ATT_OUT), row(SSM_WIDTH), row(MEM_WIDTH), row(N_BRANCHES * D_MODEL), row(D_MODEL)]
                 + [const(w) for w in ws],
        out_specs=[row(D_MODEL), pl.BlockSpec((tm * SUBLANES, LANES), lambda i: (i, 0)), row(ROUTER_COLS)],
        out_shape=[jax.ShapeDtypeStruct((n, D_MODEL), F32),
                   jax.ShapeDtypeStruct((n * SUBLANES, LANES), F32),
                   jax.ShapeDtypeStruct((n, ROUTER_COLS), F32)],
        compiler_params=_cparams(("arbitrary",)),
        name="merge",
    )(att, ssm, mem, gates, x, *ws)


TABLE_WORDS = SUBLANES * LANES


def _row_dma_loop(n, off, smem, slot, start):
    inner = min(n, LANES)
    assert n % inner == 0

    def body(r, carry):
        for j in range(inner):
            f = off + j
            start(smem[slot, f // LANES + r, f % LANES], r * inner + j, j % 2)
        return carry

    if n == inner:
        body(0, 0)
    else:
        lax.fori_loop(0, n // inner, body, 0)


def _tile_rows(i):
    start = i * SUBLANES
    return pl.ds(start if isinstance(start, int) else pl.multiple_of(start, SUBLANES), SUBLANES)


def _dispatch_kernel(pos_hbm, hn_ref, xs_hbm, pos_smem, zero, sem, psem, *, tm, n_pad):
    i = pl.program_id(0)
    slot = i % 2

    def pos_copy(b, s):
        return pltpu.make_async_copy(pos_hbm.at[b], pos_smem.at[s], psem.at[s])

    @pl.when(i == 0)
    def _():
        pos_copy(0, 0).start()

    @pl.when(i + 1 < pl.num_programs(0))
    def _():
        pos_copy(i + 1, 1 - slot).start()

    zero[...] = jnp.zeros_like(zero)
    pos_copy(i, slot).wait()

    def send_token(row, tok, prio):
        pltpu.make_async_copy(hn_ref.at[_tile_rows(tok)], xs_hbm.at[row], sem).start(priority=prio)

    def send_zero(row, _, prio):
        pltpu.make_async_copy(zero, xs_hbm.at[row], sem).start(priority=prio)

    for k in range(TOP_K):
        _row_dma_loop(tm, k * tm, pos_smem, slot, send_token)
    _row_dma_loop(n_pad, TOP_K * tm, pos_smem, slot, send_zero)
    whole, part = divmod(n_pad, tm)
    for _ in range(TOP_K + whole):
        pltpu.make_async_copy(hn_ref, hn_ref, sem).wait()
    if part:
        piece = hn_ref.at[pl.ds(0, part * SUBLANES)]
        pltpu.make_async_copy(piece, piece, sem).wait()


def moe_dispatch(pos, hn2d, n_rows, tm, n_pad):
    n_tiles = pos.shape[0]
    return pl.pallas_call(
        functools.partial(_dispatch_kernel, tm=tm, n_pad=n_pad),
        grid=(n_tiles,),
        in_specs=[pl.BlockSpec(memory_space=pl.ANY),
                  pl.BlockSpec((tm * SUBLANES, LANES), lambda i: (i, 0))],
        out_specs=pl.BlockSpec(memory_space=pl.ANY),
        out_shape=jax.ShapeDtypeStruct((n_rows, SUBLANES, LANES), F32),
        scratch_shapes=[pltpu.SMEM((2, SUBLANES, LANES), jnp.int32),
                        pltpu.VMEM((SUBLANES, LANES), F32),
                        pltpu.SemaphoreType.DMA, pltpu.SemaphoreType.DMA((2,))],
        compiler_params=_cparams(("arbitrary",)),
        name="moe_dispatch",
    )(pos, hn2d)


def _moe_kernel(be_ref, dst_hbm, xs_ref, w1_ref, w3_ref, w2_ref, out_hbm,
                dst_smem, ybuf, w1b, w3b, w2b, ssem, isem, *, bm):
    blk = pl.program_id(0)
    nblk = pl.num_programs(0)
    slot = blk % 2
    tiles = D_MODEL // LANES

    def dst_copy(b, s):
        return pltpu.make_async_copy(dst_hbm.at[b], dst_smem.at[s], isem.at[s])

    def wait_scatter():
        pltpu.make_async_copy(ybuf, ybuf, ssem).wait()

    @pl.when(blk == 0)
    def _():
        dst_copy(0, 0).start()

    @pl.when(blk + 1 < nblk)
    def _():
        dst_copy(blk + 1, 1 - slot).start()

    prev = jnp.maximum(blk - 1, 0)
    changed = jnp.logical_or(blk == 0, be_ref[blk] != be_ref[prev])

    @pl.when(changed)
    def _():
        w1b[...] = w1_ref[...].astype(BF16)
        w3b[...] = w3_ref[...].astype(BF16)
        w2b[...] = w2_ref[...].astype(BF16)

    x = jnp.concatenate([xs_ref[pl.ds(j, bm, stride=SUBLANES), :].astype(BF16) for j in range(tiles)], axis=1)
    h1 = jnp.dot(x, w1b[...], preferred_element_type=F32)
    h3 = jnp.dot(x, w3b[...], preferred_element_type=F32)
    act = (jax.nn.silu(h1) * h3).astype(BF16)
    y = jnp.dot(act, w2b[...], preferred_element_type=F32)

    @pl.when(blk > 0)
    def _():
        wait_scatter()

    for j in range(tiles):
        ybuf[pl.ds(j, bm, stride=SUBLANES), :] = y[:, j * LANES:(j + 1) * LANES]
    dst_copy(blk, slot).wait()

    def send_row(row, i, prio):
        pltpu.make_async_copy(ybuf.at[_tile_rows(i)], out_hbm.at[row], ssem).start(priority=prio)

    _row_dma_loop(bm, 0, dst_smem, slot, send_row)

    @pl.when(blk == nblk - 1)
    def _():
        wait_scatter()


def moe_experts(block_expert, dst, xs2d, w1, w3, w2, n_out_rows, bm):
    n_blocks = dst.shape[0]
    grid_spec = pltpu.PrefetchScalarGridSpec(
        num_scalar_prefetch=1,
        grid=(n_blocks,),
        in_specs=[
            pl.BlockSpec(memory_space=pl.ANY),
            pl.BlockSpec((bm * SUBLANES, LANES), lambda b, be: (b, 0)),
            pl.BlockSpec((None, D_MODEL, D_EXPERT), lambda b, be: (be[b], 0, 0)),
            pl.BlockSpec((None, D_MODEL, D_EXPERT), lambda b, be: (be[b], 0, 0)),
            pl.BlockSpec((None, D_EXPERT, D_MODEL), lambda b, be: (be[b], 0, 0)),
        ],
        out_specs=pl.BlockSpec(memory_space=pl.ANY),
        scratch_shapes=[
            pltpu.SMEM((2, SUBLANES, LANES), jnp.int32),
            pltpu.VMEM((bm * SUBLANES, LANES), F32),
            pltpu.VMEM((D_MODEL, D_EXPERT), BF16),
            pltpu.VMEM((D_MODEL, D_EXPERT), BF16),
            pltpu.VMEM((D_EXPERT, D_MODEL), BF16),
            pltpu.SemaphoreType.DMA,
            pltpu.SemaphoreType.DMA((2,)),
        ],
    )
    return pl.pallas_call(
        functools.partial(_moe_kernel, bm=bm),
        grid_spec=grid_spec,
        out_shape=jax.ShapeDtypeStruct((n_out_rows, SUBLANES, LANES), F32),
        compiler_params=_cparams(("arbitrary",)),
        name="moe_experts",
    )(block_expert, dst, xs2d, w1, w3, w2)


def _as_tables(cols, width):
    fill = jnp.zeros((cols.shape[0], TABLE_WORDS - width), jnp.int32)
    return jnp.concatenate([cols, fill], axis=1).reshape(cols.shape[0], SUBLANES, LANES)


def route_tables(route, bm, tm):
    n_tok = route.shape[0]
    n_pairs = n_tok * TOP_K
    e_flat = route[:, 2:2 + TOP_K].astype(jnp.int32).reshape(n_pairs)
    experts = jnp.arange(N_EXPERTS, dtype=jnp.int32)
    counts = jnp.sum(e_flat[:, None] == experts[None, :], axis=0, dtype=jnp.int32)
    padded = (counts + bm - 1) // bm * bm
    pad_ends = jnp.cumsum(padded)
    n_blocks = -(-n_pairs // bm) + N_EXPERTS
    n_rows = n_blocks * bm
    row0 = jnp.arange(n_blocks, dtype=jnp.int32) * bm
    block_expert = jnp.minimum(jnp.sum(pad_ends[None, :] <= row0[:, None], axis=1, dtype=jnp.int32),
                               N_EXPERTS - 1)
    n_fill = n_rows - n_pairs
    fill_id = jnp.arange(n_fill, dtype=jnp.int32)
    fill_e = jnp.minimum(fill_id // bm, N_EXPERTS - 1)
    fill_used = jnp.logical_and(fill_id < N_EXPERTS * bm, fill_id % bm < (padded - counts)[fill_e])
    keys = jnp.concatenate([e_flat * 2, jnp.where(fill_used, fill_e * 2 + 1, 2 * N_EXPERTS)])
    vals = jnp.concatenate([jnp.arange(n_pairs, dtype=jnp.int32), jnp.full((n_fill,), -1, jnp.int32)])
    _, pair_at = lax.sort((keys, vals), num_keys=1, is_stable=True)
    _, row_of = lax.sort((pair_at, jnp.arange(n_rows, dtype=jnp.int32)), num_keys=1)
    n_tiles = n_tok // tm
    n_pad = n_fill // n_tiles
    assert n_pad * n_tiles == n_fill and n_pad % SUBLANES == 0
    pair_row = row_of[n_fill:].reshape(n_tok, TOP_K)
    pos = jnp.concatenate([pair_row[:, k].reshape(n_tiles, tm) for k in range(TOP_K)]
                          + [row_of[:n_fill].reshape(n_tiles, n_pad)], axis=1)
    pair_at = pair_at.reshape(n_blocks, bm)
    valid = pair_at >= 0
    trash = n_pairs + jnp.arange(bm, dtype=jnp.int32)[None, :]
    dst = jnp.where(valid, (pair_at % TOP_K) * n_tok + pair_at // TOP_K, trash)
    return block_expert, _as_tables(dst, bm), _as_tables(pos, TOP_K * tm + n_pad), n_pad, n_rows, n_pairs + bm


def _final_kernel(h_ref, p0_ref, p1_ref, gw_ref, nf_ref, y_ref, *, tm):
    tiles = D_MODEL // LANES
    p0 = jnp.concatenate([p0_ref[pl.ds(j, tm, stride=SUBLANES), :] for j in range(tiles)], axis=1)
    p1 = jnp.concatenate([p1_ref[pl.ds(j, tm, stride=SUBLANES), :] for j in range(tiles)], axis=1)
    gw = gw_ref[...]
    h = h_ref[...] + (gw[:, 0:1] * p0 + gw[:, 1:2] * p1)
    y_ref[...] = _rms(h, nf_ref[...])


def final(h, pairs2d, gate_pad, norm_f, tm):
    n = h.shape[0]
    tm = min(tm, n)
    nb = n // tm
    row = lambda w: pl.BlockSpec((tm, w), lambda i: (i, 0))
    return pl.pallas_call(
        functools.partial(_final_kernel, tm=tm),
        grid=(nb,),
        in_specs=[
            row(D_MODEL),
            pl.BlockSpec((tm * SUBLANES, LANES), lambda i: (i, 0)),
            pl.BlockSpec((tm * SUBLANES, LANES), lambda i: (nb + i, 0)),
            row(LANES),
            pl.BlockSpec((1, D_MODEL), lambda i: (0, 0)),
        ],
        out_specs=row(D_MODEL),
        out_shape=jax.ShapeDtypeStruct((n, D_MODEL), F32),
        compiler_params=_cparams(("arbitrary",)),
        name="final",
    )(h, pairs2d, pairs2d, gate_pad, norm_f.reshape(1, D_MODEL))


def _alibi_slopes():
    h = jnp.arange(1, N_ATT_HEADS + 1, dtype=F32)
    return jnp.exp2(-8.0 * h / N_ATT_HEADS)


def _token_stage(att, ssm, mem, gates, x2, wts, w1, w3, w2, norm_f, tm, bm, precise=False):
    n = x2.shape[0]
    h, hn2d, route = merge(att, ssm, mem, gates, x2, wts, tm, precise)
    tm = min(tm, n)
    block_expert, dst, pos, n_pad, n_rows, out_rows = route_tables(route, bm, tm)
    xs = moe_dispatch(pos, hn2d, n_rows, tm, n_pad)
    pairs = moe_experts(block_expert, dst, xs.reshape(n_rows * SUBLANES, LANES), w1, w3, w2, out_rows, bm)
    return final(h, pairs.reshape(out_rows * SUBLANES, LANES), route, norm_f, tm)


def kernel(x_prompt, x_sample, cache_swa0_k, cache_swa0_v, cache_swa1_k, cache_swa1_v, cache_swa2_k, cache_swa2_v, cache_mem_k, cache_mem_v, state_ssm_re, state_ssm_im, mem_prompt, norm1, w_in, lam_re, lam_im, log_dt, ssm_b_re, ssm_b_im, ssm_c_re, ssm_c_im, ssm_d, w_glu, b_glu, w_att_out, w_ssm_out, w_mem_out, w_o, norm_mem, w_mk, w_mv, norm2, w_grp, b_grp, w_exp, b_exp, w1, w3, w2, norm_f):
    assert norm1.shape[0] == 1, "single-layer trunk"
    bsz, seq, _ = x_prompt.shape
    sbz = x_sample.shape[0]
    n_p = bsz * seq
    caches = (cache_swa0_k[0], cache_swa0_v[0], cache_swa1_k[0], cache_swa1_v[0], cache_swa2_k[0], cache_swa2_v[0])

    w_in_b = w_in[0].astype(BF16)
    w_mkv_b = jnp.concatenate([w_mk[0], w_mv[0]], axis=1).astype(BF16)
    wr = jnp.zeros((D_MODEL, ROUTER_COLS), F32)
    wr = wr.at[:, :MOE_GROUPS].set(w_grp[0]).at[:, MOE_GROUPS:MOE_GROUPS + N_EXPERTS].set(w_exp[0])
    br = jnp.zeros((1, ROUTER_COLS), F32)
    br = br.at[0, :MOE_GROUPS].set(b_grp[0]).at[0, MOE_GROUPS:MOE_GROUPS + N_EXPERTS].set(b_exp[0])
    wts_f32 = dict(wa=w_att_out[0], ws=w_ssm_out[0], wm=w_mem_out[0], wo=w_o[0],
                   n2=norm2[0].reshape(1, D_MODEL), wr=wr, br=br)
    wts = {k: (v.astype(BF16) if k in ("wa", "ws", "wm", "wo", "wr") else v) for k, v in wts_f32.items()}
    prm = ssm_params(lam_re[0], lam_im[0], log_dt[0], ssm_b_re[0], ssm_b_im[0], ssm_c_re[0], ssm_c_im[0],
                     ssm_d[0], w_glu[0], b_glu[0])
    slopes = _alibi_slopes()
    splits = (QKV_COLS, SSM_WIDTH, MEM_WIDTH, N_BRANCHES * D_MODEL)
    sig = (False, False, False, True)

    x2 = x_prompt.reshape(n_p, D_MODEL)
    qkv, u, mq, gates = norm_matmul(x2, norm1[0], w_in_b, splits, sig, tm=512, dtypes=(F32, F32, BF16, F32))
    qkv3 = qkv.reshape(bsz, seq, QKV_COLS)
    att = attn_prompt(qkv3, slopes)
    ssm, st = ssm_prompt(u.reshape(bsz, seq, SSM_WIDTH), prm)
    mk, mv = norm_matmul(mem_prompt.reshape(bsz * MEM_TOKENS, D_MODEL), norm_mem[0], w_mkv_b,
                         (MEM_WIDTH, MEM_WIDTH), (False, False), tm=256)
    mk3 = mk.reshape(bsz, MEM_TOKENS, MEM_WIDTH)
    mv3 = mv.reshape(bsz, MEM_TOKENS, MEM_WIDTH)
    mem = mem_attention(mq.reshape(bsz, seq, MEM_WIDTH), mk3, mv3, tl=512, out_dtype=BF16)
    y_p = _token_stage(att.reshape(n_p, ATT_OUT), ssm.reshape(n_p, SSM_WIDTH), mem.reshape(n_p, MEM_WIDTH),
                       gates, x2, wts, w1[0], w3[0], w2[0], norm_f, tm=256, bm=256)

    xs2 = x_sample.reshape(sbz, D_MODEL)
    qkv_s, u_s, mq_s, gates_s = norm_matmul(xs2, norm1[0], w_in[0], splits, sig, tm=sbz, precise=True)
    qkv5 = qkv_s.reshape(sbz, len(ATT_GROUPS), 3, HEADS_PER_GROUP, HEAD_DIM)
    att_s, *new_caches = attn_sample(qkv5, [jnp.transpose(c, (0, 2, 3, 1)) for c in caches])
    att_s = att_s.reshape(sbz, ATT_OUT)
    h0 = jnp.concatenate([_state_to_blocks(state_ssm_re[0]), _state_to_blocks(state_ssm_im[0])], axis=2)
    ssm_s, st_s = ssm_sample(u_s, h0, prm)
    cmk = cache_mem_k[0].reshape(sbz, MEM_TOKENS, MEM_WIDTH)
    cmv = cache_mem_v[0].reshape(sbz, MEM_TOKENS, MEM_WIDTH)
    mem_s = mem_attention(mq_s.reshape(sbz, 1, MEM_WIDTH), cmk, cmv, tl=1, precise=True).reshape(sbz, MEM_WIDTH)
    y_s = _token_stage(att_s, ssm_s, mem_s, gates_s, xs2, wts_f32, w1[0], w3[0], w2[0], norm_f, tm=sbz, bm=16,
                       precise=True)

    outs = [y_p.reshape(bsz, seq, D_MODEL), y_s.reshape(sbz, 1, D_MODEL)]
    for g, (window, _) in enumerate(ATT_GROUPS):
        keep = min(window, seq)
        for part in (1, 2):
            c0 = g * QKV_GROUP + part * ATT_OUT
            outs.append(qkv3[:, seq - keep:, c0:c0 + ATT_OUT].reshape(1, bsz, keep, HEADS_PER_GROUP, HEAD_DIM))
    outs.append(mk3.reshape(1, bsz, MEM_TOKENS, MEM_HEADS, MEM_HEAD_DIM))
    outs.append(mv3.reshape(1, bsz, MEM_TOKENS, MEM_HEADS, MEM_HEAD_DIM))
    outs.append(_blocks_to_state(st[:, :, :SSM_KSTATE])[None])
    outs.append(_blocks_to_state(st[:, :, SSM_KSTATE:])[None])
    outs.extend(jnp.transpose(c, (0, 3, 1, 2))[None] for c in new_caches)
    outs.append(_blocks_to_state(st_s[:, :, :SSM_KSTATE])[None])
    outs.append(_blocks_to_state(st_s[:, :, SSM_KSTATE:])[None])
    return tuple(outs)
```

```python
import functools
import math

import jax
import jax.numpy as jnp
from jax import lax
from jax.experimental import pallas as pl
from jax.experimental.pallas import tpu as pltpu

F32 = jnp.float32
BF16 = jnp.bfloat16

D_MODEL = 1024
ATT_GROUPS = ((128, 1), (512, 4), (2048, 16))
HEADS_PER_GROUP = 4
HEAD_DIM = 64
N_ATT_HEADS = len(ATT_GROUPS) * HEADS_PER_GROUP
ATT_OUT = HEADS_PER_GROUP * HEAD_DIM
QKV_GROUP = 3 * ATT_OUT
QKV_COLS = len(ATT_GROUPS) * QKV_GROUP
SSM_WIDTH = D_MODEL // 2
SSM_GROUP_CH = 16
SSM_GROUPS = SSM_WIDTH // SSM_GROUP_CH
SSM_STATE = 64
SSM_NSTATE = SSM_GROUPS * SSM_STATE
MEM_TOKENS = 256
MEM_HEADS = 4
MEM_HEAD_DIM = D_MODEL // 8
MEM_WIDTH = MEM_HEADS * MEM_HEAD_DIM
N_BRANCHES = 3
MOE_GROUPS = 4
EXPERTS_PER_GROUP = 8
N_EXPERTS = MOE_GROUPS * EXPERTS_PER_GROUP
TOP_K = 2
D_EXPERT = D_MODEL // 2
RMS_EPS = 1e-6

LANES = 128
SUBLANES = 8
Q_TILE = 128
ROUTER_COLS = 128
NEG_BIG = -1e30
VMEM_LIMIT = 56 * 1024 * 1024


def _cparams(sem):
    return pltpu.CompilerParams(dimension_semantics=sem, vmem_limit_bytes=VMEM_LIMIT)


def _mm(a, b, precise, dims=None):
    if precise:
        a, b, kw = a.astype(F32), b.astype(F32), dict(precision=lax.Precision.HIGHEST)
    else:
        a, b, kw = a.astype(BF16), b.astype(BF16), {}
    if dims is None:
        return jnp.dot(a, b, preferred_element_type=F32, **kw)
    return lax.dot_general(a, b, dims, preferred_element_type=F32, **kw)


NT_DIMS = (((1,), (1,)), ((), ()))


def _rms(x, g):
    ms = jnp.mean(x * x, axis=-1, keepdims=True)
    return (x * lax.rsqrt(ms + RMS_EPS)) * g


def _norm_matmul_kernel(x_ref, g_ref, w_ref, *out_refs, splits, sigmoid, chunk, precise):
    xb = _rms(x_ref[...], g_ref[...])
    if not precise:
        xb = xb.astype(BF16)
    c0 = 0
    for o_ref, n, sg in zip(out_refs, splits, sigmoid):
        for j in range(0, n, chunk):
            w = min(chunk, n - j)
            y = _mm(xb, w_ref[:, c0 + j:c0 + j + w], precise)
            if sg:
                y = jax.nn.sigmoid(y)
            o_ref[:, j:j + w] = y.astype(o_ref.dtype)
        c0 += n


def norm_matmul(x, g, w, splits, sigmoid, tm, dtypes=None, precise=False):
    n, d = x.shape
    tm = min(tm, n)
    kern = functools.partial(_norm_matmul_kernel, splits=tuple(splits), sigmoid=tuple(sigmoid), chunk=512,
                             precise=precise)
    return pl.pallas_call(
        kern,
        grid=(n // tm,),
        in_specs=[
            pl.BlockSpec((tm, d), lambda i: (i, 0)),
            pl.BlockSpec((1, d), lambda i: (0, 0)),
            pl.BlockSpec((d, w.shape[1]), lambda i: (0, 0), pipeline_mode=pl.Buffered(1)),
        ],
        out_specs=[pl.BlockSpec((tm, s), lambda i: (i, 0)) for s in splits],
        out_shape=[jax.ShapeDtypeStruct((n, s), dt) for s, dt in zip(splits, dtypes or (F32,) * len(splits))],
        compiler_params=_cparams(("arbitrary",)),
        name="norm_matmul",
    )(x, g.reshape(1, d), w)


def _attn_prompt_kernel(slopes_ref, *refs, seq):
    qkv_refs = refs[:9]
    o_ref = refs[9]
    o_scr, l_scr = refs[10], refs[11]
    hp = pl.program_id(1)
    scale = HEAD_DIM ** -0.5
    row = lax.broadcasted_iota(jnp.int32, (Q_TILE, 2 * Q_TILE), 0)
    col = lax.broadcasted_iota(jnp.int32, (Q_TILE, 2 * Q_TILE), 1)
    dist = (row + Q_TILE - col)
    in_window = jnp.logical_and(dist >= 0, dist <= Q_TILE)
    distf = dist.astype(F32)
    lane = lax.broadcasted_iota(jnp.int32, (Q_TILE, LANES), 1)
    head_a = lane < HEAD_DIM

    for g, (_, dil) in enumerate(ATT_GROUPS):
        q_ref, k_ref, v_ref = qkv_refs[3 * g:3 * g + 3]
        sub_len = seq // dil
        n_blk = sub_len // Q_TILE
        slope_a = slopes_ref[g * HEADS_PER_GROUP + 2 * hp] * float(dil)
        slope_b = slopes_ref[g * HEADS_PER_GROUP + 2 * hp + 1] * float(dil)

        def rows(start):
            if dil == 1:
                return pl.ds(start, Q_TILE)
            return pl.ds(start, Q_TILE, stride=dil)

        def body(it, carry, q_ref=q_ref, k_ref=k_ref, v_ref=v_ref, n_blk=n_blk, dil=dil,
                 slope_a=slope_a, slope_b=slope_b, rows=rows, g=g):
            r = it // n_blk
            blk = it % n_blk
            prev = jnp.maximum(blk - 1, 0)
            q_start = r + dil * Q_TILE * blk
            p_start = r + dil * Q_TILE * prev
            q = q_ref[rows(q_start), :]
            k2 = jnp.concatenate([k_ref[rows(p_start), :], k_ref[rows(q_start), :]], axis=0).astype(BF16)
            v2 = jnp.concatenate([v_ref[rows(p_start), :], v_ref[rows(q_start), :]], axis=0).astype(BF16)
            valid = jnp.logical_and(in_window, jnp.logical_or(col >= Q_TILE, blk > 0))
            outs, lses = [], []
            for is_a, slope in ((True, slope_a), (False, slope_b)):
                hmask = head_a if is_a else jnp.logical_not(head_a)
                qh = jnp.where(hmask, q, 0.0).astype(BF16)
                s = lax.dot_general(qh, k2, NT_DIMS, preferred_element_type=F32)
                s = s * scale - slope * distf
                s = jnp.where(valid, s, NEG_BIG)
                m = jnp.max(s, axis=-1, keepdims=True)
                p = jnp.exp(s - m)
                den = jnp.sum(p, axis=-1, keepdims=True)
                pv = jnp.dot(p.astype(BF16), v2, preferred_element_type=F32)
                outs.append(pv / den)
                lses.append(m + jnp.log(den))
            o_scr[g, rows(q_start), :] = jnp.where(head_a, outs[0], outs[1])
            l_scr[g, rows(q_start), :] = jnp.where(head_a, lses[0], lses[1])
            return carry

        lax.fori_loop(0, dil * n_blk, body, 0, unroll=8)

    def combine(c, carry):
        sl = pl.ds(pl.multiple_of(c * 256, 256), 256)
        l0, l1, l2 = l_scr[0, sl, :], l_scr[1, sl, :], l_scr[2, sl, :]
        m = jnp.maximum(jnp.maximum(l0, l1), l2)
        e0, e1, e2 = jnp.exp(l0 - m), jnp.exp(l1 - m), jnp.exp(l2 - m)
        tot = e0 + e1 + e2
        att = (e0 * o_scr[0, sl, :] + e1 * o_scr[1, sl, :] + e2 * o_scr[2, sl, :]) / tot
        o_ref[sl, :] = att.astype(o_ref.dtype)
        return carry

    lax.fori_loop(0, seq // 256, combine, 0)


def attn_prompt(qkv, slopes):
    bsz, seq, _ = qkv.shape
    in_specs = []
    for g in range(len(ATT_GROUPS)):
        for part in range(3):
            base = (g * QKV_GROUP + part * ATT_OUT) // LANES
            in_specs.append(pl.BlockSpec((None, seq, LANES),
                                         lambda b, h, sl, base=base: (b, 0, base + h)))
    grid_spec = pltpu.PrefetchScalarGridSpec(
        num_scalar_prefetch=1,
        grid=(bsz, ATT_OUT // LANES),
        in_specs=in_specs,
        out_specs=pl.BlockSpec((None, seq, LANES), lambda b, h, sl: (b, 0, h)),
        scratch_shapes=[pltpu.VMEM((3, seq, LANES), F32), pltpu.VMEM((3, seq, LANES), F32)],
    )
    return pl.pallas_call(
        functools.partial(_attn_prompt_kernel, seq=seq),
        grid_spec=grid_spec,
        out_shape=jax.ShapeDtypeStruct((bsz, seq, ATT_OUT), BF16),
        compiler_params=_cparams(("arbitrary", "arbitrary")),
        name="attn_prompt",
    )(slopes, *([qkv] * 9))


def _attn_sample_kernel(qkv_ref, col_ref, *refs):
    cache_refs = refs[:6]
    o_ref = refs[6]
    new_refs = refs[7:13]
    scale = HEAD_DIM ** -0.5
    n_grp = len(ATT_GROUPS)
    outs = [[None] * HEADS_PER_GROUP for _ in range(n_grp)]
    lses = [[None] * HEADS_PER_GROUP for _ in range(n_grp)]
    for g, (window, dil) in enumerate(ATT_GROUPS):
        lane = lax.broadcasted_iota(jnp.int32, (1, window), 1)
        on_grid = jnp.bitwise_and(lane, dil - 1) == 0
        dist = (window - lane).astype(F32)
        last = lax.broadcasted_iota(jnp.int32, (HEAD_DIM, window), 1) == window - 1
        for h in range(HEADS_PER_GROUP):
            slope = 2.0 ** (-8.0 * (g * HEADS_PER_GROUP + h + 1) / N_ATT_HEADS)
            q = qkv_ref[g, 0][h:h + 1, :]
            kn = qkv_ref[g, 1][h:h + 1, :]
            vn = qkv_ref[g, 2][h:h + 1, :]
            kt = cache_refs[2 * g][h]
            vt = cache_refs[2 * g + 1][h]
            s = _mm(q, kt, True) * scale - slope * dist
            s = jnp.where(on_grid, s, NEG_BIG)
            sn = jnp.sum(q * kn, axis=-1, keepdims=True) * scale
            m = jnp.maximum(jnp.max(s, axis=-1, keepdims=True), sn)
            p = jnp.exp(s - m)
            pn = jnp.exp(sn - m)
            den = jnp.sum(p, axis=-1, keepdims=True) + pn
            outs[g][h] = (_mm(p, vt, True, NT_DIMS) + pn * vn) / den
            lses[g][h] = m + jnp.log(den)
            new_refs[2 * g][h] = jnp.where(last, col_ref[g, 0, h], pltpu.roll(kt, window - 1, axis=1))
            new_refs[2 * g + 1][h] = jnp.where(last, col_ref[g, 1, h], pltpu.roll(vt, window - 1, axis=1))
    for h in range(HEADS_PER_GROUP):
        m = jnp.maximum(jnp.maximum(lses[0][h], lses[1][h]), lses[2][h])
        es = [jnp.exp(lses[g][h] - m) for g in range(n_grp)]
        num = es[0] * outs[0][h] + es[1] * outs[1][h] + es[2] * outs[2][h]
        o_ref[h:h + 1, :] = num / (es[0] + es[1] + es[2])


def attn_sample(qkv5, caches_t):
    bsz = qkv5.shape[0]
    cols = qkv5[:, :, 1:3].reshape(bsz, len(ATT_GROUPS), 2, HEADS_PER_GROUP, HEAD_DIM, 1)
    in_specs = [pl.BlockSpec((None, 3, 3, HEADS_PER_GROUP, HEAD_DIM), lambda b: (b, 0, 0, 0, 0)),
                pl.BlockSpec((None, 3, 2, HEADS_PER_GROUP, HEAD_DIM, 1), lambda b: (b, 0, 0, 0, 0, 0))]
    cache_specs = [pl.BlockSpec((None,) + c.shape[1:], lambda b: (b, 0, 0, 0)) for c in caches_t]
    return pl.pallas_call(
        _attn_sample_kernel,
        grid=(bsz,),
        in_specs=in_specs + cache_specs,
        out_specs=[pl.BlockSpec((None, HEADS_PER_GROUP, HEAD_DIM), lambda b: (b, 0, 0))] + cache_specs,
        out_shape=[jax.ShapeDtypeStruct((bsz, HEADS_PER_GROUP, HEAD_DIM), F32)]
                  + [jax.ShapeDtypeStruct(c.shape, F32) for c in caches_t],
        compiler_params=_cparams(("arbitrary",)),
        name="attn_sample",
    )(qkv5, cols, *caches_t)


SSM_KBLK = SSM_WIDTH // LANES
SSM_KSTATE = SSM_NSTATE // SSM_KBLK


def _ssm_tail(y, u, dsk, wglu, bglu, precise=False):
    y = y + dsk * u
    z = jax.nn.gelu(y)
    gl = _mm(z, wglu, precise) + bglu
    return z * jax.nn.sigmoid(gl)


def _ssm_prompt_kernel(u_ref, wb_ref, wc_ref, are_ref, aim_ref, dsk_ref, wglu_ref, bglu_ref,
                       o_ref, st_ref, uperm, bu, sbf, yperm, ynat, *, bsz, tt):
    rows = bsz * tt

    @pl.when(pl.program_id(0) == 0)
    def _():
        st_ref[...] = jnp.zeros_like(st_ref)

    for b in range(bsz):
        ub = u_ref[b]
        for k in range(SSM_KBLK):
            uperm[k, pl.ds(b, tt, stride=bsz), :] = ub[:, k * LANES:(k + 1) * LANES]

    for k in range(SSM_KBLK):
        bu[k] = jnp.dot(uperm[k].astype(BF16), wb_ref[k], preferred_element_type=F32)

    for k in range(SSM_KBLK):
        ar = jnp.broadcast_to(are_ref[k], (bsz, SSM_KSTATE))
        ai = jnp.broadcast_to(aim_ref[k], (bsz, SSM_KSTATE))
        st = st_ref[k]

        def step(t, carry, k=k, ar=ar, ai=ai):
            re, im = carry
            r0 = pl.multiple_of(t * bsz, bsz)
            b_t = bu[k, pl.ds(r0, bsz), :]
            nre = ar * re - ai * im + b_t[:, :SSM_KSTATE]
            nim = ar * im + ai * re + b_t[:, SSM_KSTATE:]
            sbf[k, pl.ds(r0, bsz), :] = jnp.concatenate([nre, nim], axis=1).astype(BF16)
            return nre, nim

        re, im = lax.fori_loop(0, tt, step, (st[:, :SSM_KSTATE], st[:, SSM_KSTATE:]), unroll=2)
        st_ref[k] = jnp.concatenate([re, im], axis=1)

    for k in range(SSM_KBLK):
        yperm[k] = jnp.dot(sbf[k], wc_ref[k], preferred_element_type=F32)

    for b in range(bsz):
        for k in range(SSM_KBLK):
            ynat[b * tt:(b + 1) * tt, k * LANES:(k + 1) * LANES] = yperm[k, pl.ds(b, tt, stride=bsz), :]

    u2 = u_ref[...].reshape(rows, SSM_WIDTH)
    out = _ssm_tail(ynat[...], u2, dsk_ref[...], wglu_ref[...], bglu_ref[...])
    o_ref[...] = out.reshape(bsz, tt, SSM_WIDTH).astype(o_ref.dtype)


def ssm_prompt(u, prm, tt=32):
    bsz, seq, _ = u.shape
    rows = bsz * tt
    const = lambda shape: pl.BlockSpec(shape, lambda i: (0,) * len(shape))
    return pl.pallas_call(
        functools.partial(_ssm_prompt_kernel, bsz=bsz, tt=tt),
        grid=(seq // tt,),
        in_specs=[
            pl.BlockSpec((bsz, tt, SSM_WIDTH), lambda i: (0, i, 0)),
            const((SSM_KBLK, LANES, 2 * SSM_KSTATE)),
            const((SSM_KBLK, 2 * SSM_KSTATE, LANES)),
            const((SSM_KBLK, 1, SSM_KSTATE)),
            const((SSM_KBLK, 1, SSM_KSTATE)),
            const((1, SSM_WIDTH)),
            const((SSM_WIDTH, SSM_WIDTH)),
            const((1, SSM_WIDTH)),
        ],
        out_specs=[
            pl.BlockSpec((bsz, tt, SSM_WIDTH), lambda i: (0, i, 0)),
            const((SSM_KBLK, bsz, 2 * SSM_KSTATE)),
        ],
        out_shape=[
            jax.ShapeDtypeStruct((bsz, seq, SSM_WIDTH), BF16),
            jax.ShapeDtypeStruct((SSM_KBLK, bsz, 2 * SSM_KSTATE), F32),
        ],
        scratch_shapes=[
            pltpu.VMEM((SSM_KBLK, rows, LANES), F32),
            pltpu.VMEM((SSM_KBLK, rows, 2 * SSM_KSTATE), F32),
            pltpu.VMEM((SSM_KBLK, rows, 2 * SSM_KSTATE), BF16),
            pltpu.VMEM((SSM_KBLK, rows, LANES), F32),
            pltpu.VMEM((rows, SSM_WIDTH), F32),
        ],
        compiler_params=_cparams(("arbitrary",)),
        name="ssm_prompt",
    )(u, prm["wb"], prm["wc"], prm["a_re"], prm["a_im"], prm["dsk"], prm["wglu"], prm["bglu"])


def _ssm_sample_kernel(u_ref, h0_ref, wb_ref, wc_ref, are_ref, aim_ref, dsk_ref, wglu_ref, bglu_ref,
                       o_ref, st_ref):
    u = u_ref[...]
    ys = []
    for k in range(SSM_KBLK):
        b_t = _mm(u[:, k * LANES:(k + 1) * LANES], wb_ref[k], True)
        h0 = h0_ref[k]
        re, im = h0[:, :SSM_KSTATE], h0[:, SSM_KSTATE:]
        ar, ai = are_ref[k], aim_ref[k]
        nre = ar * re - ai * im + b_t[:, :SSM_KSTATE]
        nim = ar * im + ai * re + b_t[:, SSM_KSTATE:]
        s = jnp.concatenate([nre, nim], axis=1)
        st_ref[k] = s
        ys.append(_mm(s, wc_ref[k], True))
    y = jnp.concatenate(ys, axis=1)
    o_ref[...] = _ssm_tail(y, u, dsk_ref[...], wglu_ref[...], bglu_ref[...], precise=True)


def ssm_sample(u, h0, prm):
    bsz = u.shape[0]
    return pl.pallas_call(
        _ssm_sample_kernel,
        out_shape=[jax.ShapeDtypeStruct((bsz, SSM_WIDTH), F32),
                   jax.ShapeDtypeStruct((SSM_KBLK, bsz, 2 * SSM_KSTATE), F32)],
        compiler_params=pltpu.CompilerParams(vmem_limit_bytes=VMEM_LIMIT),
        name="ssm_sample",
    )(u, h0, prm["wb_f32"], prm["wc_f32"], prm["a_re"], prm["a_im"], prm["dsk"], prm["wglu_f32"], prm["bglu"])


def ssm_params(lam_re, lam_im, log_dt, b_re, b_im, c_re, c_im, d_skip, w_glu, b_glu):
    dt = jnp.exp(log_dt)[:, None]
    mag = jnp.exp(lam_re * dt)
    ab_re = mag * jnp.cos(lam_im * dt)
    ab_im = mag * jnp.sin(lam_im * dt)
    den = lam_re * lam_re + lam_im * lam_im
    nr = ab_re - 1.0
    ni = ab_im
    z_re = ((nr * lam_re + ni * lam_im) / den)[..., None]
    z_im = ((ni * lam_re - nr * lam_im) / den)[..., None]
    bb_re = z_re * b_re - z_im * b_im
    bb_im = z_re * b_im + z_im * b_re
    gpb = SSM_GROUPS // SSM_KBLK
    eye = jnp.eye(gpb, dtype=F32)

    def in_mat(bb):
        bbk = bb.reshape(SSM_KBLK, gpb, SSM_STATE, SSM_GROUP_CH)
        m = jnp.einsum("kgpc,gh->kgchp", bbk, eye)
        return m.reshape(SSM_KBLK, LANES, SSM_KSTATE)

    def out_mat(c):
        ck = c.reshape(SSM_KBLK, gpb, SSM_GROUP_CH, SSM_STATE)
        m = jnp.einsum("kgcp,gh->kgphc", ck, eye)
        return m.reshape(SSM_KBLK, SSM_KSTATE, LANES)

    wb = jnp.concatenate([in_mat(bb_re), in_mat(bb_im)], axis=2)
    wc = jnp.concatenate([out_mat(c_re), -out_mat(c_im)], axis=1)
    return dict(
        wb=wb.astype(BF16), wc=wc.astype(BF16), wb_f32=wb, wc_f32=wc, wglu_f32=w_glu,
        a_re=ab_re.reshape(SSM_KBLK, 1, SSM_KSTATE), a_im=ab_im.reshape(SSM_KBLK, 1, SSM_KSTATE),
        dsk=d_skip.reshape(1, SSM_WIDTH), wglu=w_glu.astype(BF16), bglu=b_glu.reshape(1, SSM_WIDTH))


def _state_to_blocks(h):
    bsz = h.shape[0]
    return h.reshape(bsz, SSM_KBLK, SSM_KSTATE).transpose(1, 0, 2)


def _blocks_to_state(s):
    bsz = s.shape[1]
    return s.transpose(1, 0, 2).reshape(bsz, SSM_GROUPS, SSM_STATE)


def _mem_attn_kernel(q_ref, k_ref, v_ref, o_ref, *, precise):
    scale = MEM_HEAD_DIM ** -0.5
    for h in range(MEM_HEADS):
        sl = slice(h * MEM_HEAD_DIM, (h + 1) * MEM_HEAD_DIM)
        s = _mm(q_ref[:, sl], k_ref[:, sl], precise, NT_DIMS) * scale
        m = jnp.max(s, axis=-1, keepdims=True)
        p = jnp.exp(s - m)
        den = jnp.sum(p, axis=-1, keepdims=True)
        o_ref[:, sl] = _mm(p / den, v_ref[:, sl], precise).astype(o_ref.dtype)


def mem_attention(q, mk, mv, tl, out_dtype=F32, precise=False):
    bsz, seq, _ = q.shape
    tl = min(tl, seq)
    return pl.pallas_call(
        functools.partial(_mem_attn_kernel, precise=precise),
        grid=(bsz, seq // tl),
        in_specs=[
            pl.BlockSpec((None, tl, MEM_WIDTH), lambda b, i: (b, i, 0)),
            pl.BlockSpec((None, MEM_TOKENS, MEM_WIDTH), lambda b, i: (b, 0, 0)),
            pl.BlockSpec((None, MEM_TOKENS, MEM_WIDTH), lambda b, i: (b, 0, 0)),
        ],
        out_specs=pl.BlockSpec((None, tl, MEM_WIDTH), lambda b, i: (b, i, 0)),
        out_shape=jax.ShapeDtypeStruct((bsz, seq, MEM_WIDTH), out_dtype),
        compiler_params=_cparams(("arbitrary", "arbitrary")),
        name="mem_attention",
    )(q, mk, mv)


def _route(logits):
    lane = lax.broadcasted_iota(jnp.int32, logits.shape, 1)
    lanef = lane.astype(F32)
    none = float(ROUTER_COLS)
    neg = -jnp.inf
    gl = jnp.where(lane < MOE_GROUPS, logits, neg)
    gmax = jnp.max(gl, axis=-1, keepdims=True)
    grp = jnp.min(jnp.where(gl == gmax, lanef, none), axis=-1, keepdims=True)
    p_grp = 1.0 / jnp.sum(jnp.exp(gl - gmax), axis=-1, keepdims=True)
    lo = MOE_GROUPS + grp * EXPERTS_PER_GROUP
    el = jnp.where(jnp.logical_and(lanef >= lo, lanef < lo + EXPERTS_PER_GROUP), logits, neg)
    v1 = jnp.max(el, axis=-1, keepdims=True)
    i1 = jnp.min(jnp.where(el == v1, lanef, none), axis=-1, keepdims=True)
    el2 = jnp.where(lanef == i1, neg, el)
    v2 = jnp.max(el2, axis=-1, keepdims=True)
    i2 = jnp.min(jnp.where(el2 == v2, lanef, none), axis=-1, keepdims=True)
    t = jnp.exp(v2 - v1)
    g1 = p_grp / (1.0 + t)
    g2 = g1 * t
    out = jnp.where(lane == 0, g1, 0.0)
    out = jnp.where(lane == 1, g2, out)
    out = jnp.where(lane == 2, i1 - MOE_GROUPS, out)
    return jnp.where(lane == 3, i2 - MOE_GROUPS, out)


def _merge_kernel(att_ref, ssm_ref, mem_ref, gate_ref, x_ref, wa_ref, ws_ref, wm_ref, wo_ref,
                  n2_ref, wr_ref, br_ref, h_ref, hn_ref, lg_ref, *, tm, precise):
    a = _mm(att_ref[...], wa_ref[...], precise)
    merged = gate_ref[:, 0:D_MODEL] * a
    s = _mm(ssm_ref[...], ws_ref[...], precise)
    merged = merged + gate_ref[:, D_MODEL:2 * D_MODEL] * s
    m = _mm(mem_ref[...], wm_ref[...], precise)
    merged = merged + gate_ref[:, 2 * D_MODEL:3 * D_MODEL] * m
    h = x_ref[...] + _mm(merged, wo_ref[...], precise)
    h_ref[...] = h
    hn = _rms(h, n2_ref[...])
    hn_ref[...] = hn
    logits = _mm(hn, wr_ref[...], precise) + br_ref[...]
    lg_ref[...] = _route(logits)


def merge(att, ssm, mem, gates, x, wts, tm, precise=False):
    n = x.shape[0]
    tm = min(tm, n)
    row = lambda w: pl.BlockSpec((tm, w), lambda i: (i, 0))
    const = lambda a: pl.BlockSpec(a.shape, lambda i: (0, 0))
    ws = [wts["wa"], wts["ws"], wts["wm"], wts["wo"], wts["n2"], wts["wr"], wts["br"]]
    return pl.pallas_call(
        functools.partial(_merge_kernel, tm=tm, precise=precise),
        grid=(n // tm,),
        in_specs=[row(ATT_OUT), row(SSM_WIDTH), row(MEM_WIDTH), row(N_BRANCHES * D_MODEL), row(D_MODEL)]
                 + [const(w) for w in ws],
        out_specs=[row(D_MODEL), row(D_MODEL), row(ROUTER_COLS)],
        out_shape=[jax.ShapeDtypeStruct((n, D_MODEL), F32),
                   jax.ShapeDtypeStruct((n, D_MODEL), F32),
                   jax.ShapeDtypeStruct((n, ROUTER_COLS), F32)],
        compiler_params=_cparams(("arbitrary",)),
        name="merge",
    )(att, ssm, mem, gates, x, *ws)


TABLE_WORDS = SUBLANES * LANES


def _row_dma_loop(n, off, smem, slot, start):
    inner = min(n, LANES)
    assert n % inner == 0

    def body(r, carry):
        for j in range(inner):
            f = off + j
            start(smem[slot, f // LANES + r, f % LANES], r * inner + j, j % 2)
        return carry

    if n == inner:
        body(0, 0)
    else:
        lax.fori_loop(0, n // inner, body, 0)


def _row(ref, i):
    return ref.at[pl.ds(i, 1), :]


def _dispatch_kernel(pos_hbm, hn_ref, xs_hbm, pos_smem, zero, sem, psem, *, tm, n_pad):
    i = pl.program_id(0)
    slot = i % 2

    def pos_copy(b, s):
        return pltpu.make_async_copy(pos_hbm.at[b], pos_smem.at[s], psem.at[s])

    @pl.when(i == 0)
    def _():
        pos_copy(0, 0).start()

    @pl.when(i + 1 < pl.num_programs(0))
    def _():
        pos_copy(i + 1, 1 - slot).start()

    zero[...] = jnp.zeros_like(zero)
    pos_copy(i, slot).wait()

    def send_token(row, tok, prio):
        pltpu.make_async_copy(_row(hn_ref, tok), _row(xs_hbm, row), sem).start(priority=prio)

    def send_zero(row, _, prio):
        pltpu.make_async_copy(zero, _row(xs_hbm, row), sem).start(priority=prio)

    for k in range(TOP_K):
        _row_dma_loop(tm, k * tm, pos_smem, slot, send_token)
    _row_dma_loop(n_pad, TOP_K * tm, pos_smem, slot, send_zero)
    whole, part = divmod(n_pad, tm)
    for _ in range(TOP_K + whole):
        pltpu.make_async_copy(hn_ref, hn_ref, sem).wait()
    if part:
        piece = hn_ref.at[pl.ds(0, part), :]
        pltpu.make_async_copy(piece, piece, sem).wait()


def moe_dispatch(pos, hn, n_rows, tm, n_pad):
    n_tiles = pos.shape[0]
    return pl.pallas_call(
        functools.partial(_dispatch_kernel, tm=tm, n_pad=n_pad),
        grid=(n_tiles,),
        in_specs=[pl.BlockSpec(memory_space=pl.ANY),
                  pl.BlockSpec((tm, D_MODEL), lambda i: (i, 0))],
        out_specs=pl.BlockSpec(memory_space=pl.ANY),
        out_shape=jax.ShapeDtypeStruct((n_rows, D_MODEL), F32),
        scratch_shapes=[pltpu.SMEM((2, SUBLANES, LANES), jnp.int32),
                        pltpu.VMEM((1, D_MODEL), F32),
                        pltpu.SemaphoreType.DMA, pltpu.SemaphoreType.DMA((2,))],
        compiler_params=_cparams(("arbitrary",)),
        name="moe_dispatch",
    )(pos, hn)


def _moe_kernel(be_ref, dst_hbm, xs_ref, w1_ref, w3_ref, w2_ref, out_hbm,
                dst_smem, ybuf, w1b, w3b, w2b, ssem, isem, *, bm):
    blk = pl.program_id(0)
    nblk = pl.num_programs(0)
    slot = blk % 2

    def dst_copy(b, s):
        return pltpu.make_async_copy(dst_hbm.at[b], dst_smem.at[s], isem.at[s])

    def wait_scatter():
        pltpu.make_async_copy(ybuf, ybuf, ssem).wait()

    @pl.when(blk == 0)
    def _():
        dst_copy(0, 0).start()

    @pl.when(blk + 1 < nblk)
    def _():
        dst_copy(blk + 1, 1 - slot).start()

    prev = jnp.maximum(blk - 1, 0)
    changed = jnp.logical_or(blk == 0, be_ref[blk] != be_ref[prev])

    @pl.when(changed)
    def _():
        w1b[...] = w1_ref[...].astype(BF16)
        w3b[...] = w3_ref[...].astype(BF16)
        w2b[...] = w2_ref[...].astype(BF16)

    x = xs_ref[...].astype(BF16)
    h1 = jnp.dot(x, w1b[...], preferred_element_type=F32)
    h3 = jnp.dot(x, w3b[...], preferred_element_type=F32)
    act = (jax.nn.silu(h1) * h3).astype(BF16)
    y = jnp.dot(act, w2b[...], preferred_element_type=F32)

    @pl.when(blk > 0)
    def _():
        wait_scatter()

    ybuf[...] = y
    dst_copy(blk, slot).wait()

    def send_row(row, i, prio):
        pltpu.make_async_copy(_row(ybuf, i), _row(out_hbm, row), ssem).start(priority=prio)

    _row_dma_loop(bm, 0, dst_smem, slot, send_row)

    @pl.when(blk == nblk - 1)
    def _():
        wait_scatter()


def moe_experts(block_expert, dst, xs, w1, w3, w2, n_out_rows, bm):
    n_blocks = dst.shape[0]
    grid_spec = pltpu.PrefetchScalarGridSpec(
        num_scalar_prefetch=1,
        grid=(n_blocks,),
        in_specs=[
            pl.BlockSpec(memory_space=pl.ANY),
            pl.BlockSpec((bm, D_MODEL), lambda b, be: (b, 0)),
            pl.BlockSpec((None, D_MODEL, D_EXPERT), lambda b, be: (be[b], 0, 0)),
            pl.BlockSpec((None, D_MODEL, D_EXPERT), lambda b, be: (be[b], 0, 0)),
            pl.BlockSpec((None, D_EXPERT, D_MODEL), lambda b, be: (be[b], 0, 0)),
        ],
        out_specs=pl.BlockSpec(memory_space=pl.ANY),
        scratch_shapes=[
            pltpu.SMEM((2, SUBLANES, LANES), jnp.int32),
            pltpu.VMEM((bm, D_MODEL), F32),
            pltpu.VMEM((D_MODEL, D_EXPERT), BF16),
            pltpu.VMEM((D_MODEL, D_EXPERT), BF16),
            pltpu.VMEM((D_EXPERT, D_MODEL), BF16),
            pltpu.SemaphoreType.DMA,
            pltpu.SemaphoreType.DMA((2,)),
        ],
    )
    return pl.pallas_call(
        functools.partial(_moe_kernel, bm=bm),
        grid_spec=grid_spec,
        out_shape=jax.ShapeDtypeStruct((n_out_rows, D_MODEL), F32),
        compiler_params=_cparams(("arbitrary",)),
        name="moe_experts",
    )(block_expert, dst, xs, w1, w3, w2)


def _as_tables(cols, width):
    fill = jnp.zeros((cols.shape[0], TABLE_WORDS - width), jnp.int32)
    return jnp.concatenate([cols, fill], axis=1).reshape(cols.shape[0], SUBLANES, LANES)


def route_tables(route, bm, tm):
    n_tok = route.shape[0]
    n_pairs = n_tok * TOP_K
    e_flat = route[:, 2:2 + TOP_K].astype(jnp.int32).reshape(n_pairs)
    experts = jnp.arange(N_EXPERTS, dtype=jnp.int32)
    counts = jnp.sum(e_flat[:, None] == experts[None, :], axis=0, dtype=jnp.int32)
    padded = (counts + bm - 1) // bm * bm
    pad_ends = jnp.cumsum(padded)
    n_blocks = -(-n_pairs // bm) + N_EXPERTS
    n_rows = n_blocks * bm
    row0 = jnp.arange(n_blocks, dtype=jnp.int32) * bm
    block_expert = jnp.minimum(jnp.sum(pad_ends[None, :] <= row0[:, None], axis=1, dtype=jnp.int32),
                               N_EXPERTS - 1)
    n_fill = n_rows - n_pairs
    fill_id = jnp.arange(n_fill, dtype=jnp.int32)
    fill_e = jnp.minimum(fill_id // bm, N_EXPERTS - 1)
    fill_used = jnp.logical_and(fill_id < N_EXPERTS * bm, fill_id % bm < (padded - counts)[fill_e])
    keys = jnp.concatenate([e_flat * 2, jnp.where(fill_used, fill_e * 2 + 1, 2 * N_EXPERTS)])
    vals = jnp.concatenate([jnp.arange(n_pairs, dtype=jnp.int32), jnp.full((n_fill,), -1, jnp.int32)])
    _, pair_at = lax.sort((keys, vals), num_keys=1, is_stable=True)
    _, row_of = lax.sort((pair_at, jnp.arange(n_rows, dtype=jnp.int32)), num_keys=1)
    n_tiles = n_tok // tm
    n_pad = n_fill // n_tiles
    assert n_pad * n_tiles == n_fill and n_pad % SUBLANES == 0
    pair_row = row_of[n_fill:].reshape(n_tok, TOP_K)
    pos = jnp.concatenate([pair_row[:, k].reshape(n_tiles, tm) for k in range(TOP_K)]
                          + [row_of[:n_fill].reshape(n_tiles, n_pad)], axis=1)
    pair_at = pair_at.reshape(n_blocks, bm)
    valid = pair_at >= 0
    trash = n_pairs + jnp.arange(bm, dtype=jnp.int32)[None, :]
    dst = jnp.where(valid, (pair_at % TOP_K) * n_tok + pair_at // TOP_K, trash)
    return block_expert, _as_tables(dst, bm), _as_tables(pos, TOP_K * tm + n_pad), n_pad, n_rows, n_pairs + bm


def _final_kernel(h_ref, p0_ref, p1_ref, gw_ref, nf_ref, y_ref):
    gw = gw_ref[...]
    h = h_ref[...] + (gw[:, 0:1] * p0_ref[...] + gw[:, 1:2] * p1_ref[...])
    y_ref[...] = _rms(h, nf_ref[...])


def final(h, pairs, gate_pad, norm_f, tm):
    n = h.shape[0]
    tm = min(tm, n)
    nb = n // tm
    row = lambda w: pl.BlockSpec((tm, w), lambda i: (i, 0))
    return pl.pallas_call(
        _final_kernel,
        grid=(nb,),
        in_specs=[
            row(D_MODEL),
            row(D_MODEL),
            pl.BlockSpec((tm, D_MODEL), lambda i: (nb + i, 0)),
            row(LANES),
            pl.BlockSpec((1, D_MODEL), lambda i: (0, 0)),
        ],
        out_specs=row(D_MODEL),
        out_shape=jax.ShapeDtypeStruct((n, D_MODEL), F32),
        compiler_params=_cparams(("arbitrary",)),
        name="final",
    )(h, pairs, pairs, gate_pad, norm_f.reshape(1, D_MODEL))


def _alibi_slopes():
    h = jnp.arange(1, N_ATT_HEADS + 1, dtype=F32)
    return jnp.exp2(-8.0 * h / N_ATT_HEADS)


def _token_stage(att, ssm, mem, gates, x2, wts, w1, w3, w2, norm_f, tm, bm, precise=False):
    n = x2.shape[0]
    h, hn, route = merge(att, ssm, mem, gates, x2, wts, tm, precise)
    tm = min(tm, n)
    block_expert, dst, pos, n_pad, n_rows, out_rows = route_tables(route, bm, tm)
    xs = moe_dispatch(pos, hn, n_rows, tm, n_pad)
    pairs = moe_experts(block_expert, dst, xs, w1, w3, w2, out_rows, bm)
    return final(h, pairs, route, norm_f, tm)


def kernel(x_prompt, x_sample, cache_swa0_k, cache_swa0_v, cache_swa1_k, cache_swa1_v, cache_swa2_k, cache_swa2_v, cache_mem_k, cache_mem_v, state_ssm_re, state_ssm_im, mem_prompt, norm1, w_in, lam_re, lam_im, log_dt, ssm_b_re, ssm_b_im, ssm_c_re, ssm_c_im, ssm_d, w_glu, b_glu, w_att_out, w_ssm_out, w_mem_out, w_o, norm_mem, w_mk, w_mv, norm2, w_grp, b_grp, w_exp, b_exp, w1, w3, w2, norm_f):
    assert norm1.shape[0] == 1, "single-layer trunk"
    bsz, seq, _ = x_prompt.shape
    sbz = x_sample.shape[0]
    n_p = bsz * seq
    caches = (cache_swa0_k[0], cache_swa0_v[0], cache_swa1_k[0], cache_swa1_v[0], cache_swa2_k[0], cache_swa2_v[0])

    w_in_b = w_in[0].astype(BF16)
    w_mkv_b = jnp.concatenate([w_mk[0], w_mv[0]], axis=1).astype(BF16)
    wr = jnp.zeros((D_MODEL, ROUTER_COLS), F32)
    wr = wr.at[:, :MOE_GROUPS].set(w_grp[0]).at[:, MOE_GROUPS:MOE_GROUPS + N_EXPERTS].set(w_exp[0])
    br = jnp.zeros((1, ROUTER_COLS), F32)
    br = br.at[0, :MOE_GROUPS].set(b_grp[0]).at[0, MOE_GROUPS:MOE_GROUPS + N_EXPERTS].set(b_exp[0])
    wts_f32 = dict(wa=w_att_out[0], ws=w_ssm_out[0], wm=w_mem_out[0], wo=w_o[0],
                   n2=norm2[0].reshape(1, D_MODEL), wr=wr, br=br)
    wts = {k: (v.astype(BF16) if k in ("wa", "ws", "wm", "wo", "wr") else v) for k, v in wts_f32.items()}
    prm = ssm_params(lam_re[0], lam_im[0], log_dt[0], ssm_b_re[0], ssm_b_im[0], ssm_c_re[0], ssm_c_im[0],
                     ssm_d[0], w_glu[0], b_glu[0])
    slopes = _alibi_slopes()
    splits = (QKV_COLS, SSM_WIDTH, MEM_WIDTH, N_BRANCHES * D_MODEL)
    sig = (False, False, False, True)

    x2 = x_prompt.reshape(n_p, D_MODEL)
    qkv, u, mq, gates = norm_matmul(x2, norm1[0], w_in_b, splits, sig, tm=256, dtypes=(F32, F32, BF16, F32))
    qkv3 = qkv.reshape(bsz, seq, QKV_COLS)
    att = attn_prompt(qkv3, slopes)
    ssm, st = ssm_prompt(u.reshape(bsz, seq, SSM_WIDTH), prm)
    mk, mv = norm_matmul(mem_prompt.reshape(bsz * MEM_TOKENS, D_MODEL), norm_mem[0], w_mkv_b,
                         (MEM_WIDTH, MEM_WIDTH), (False, False), tm=256)
    mk3 = mk.reshape(bsz, MEM_TOKENS, MEM_WIDTH)
    mv3 = mv.reshape(bsz, MEM_TOKENS, MEM_WIDTH)
    mem = mem_attention(mq.reshape(bsz, seq, MEM_WIDTH), mk3, mv3, tl=512, out_dtype=BF16)
    y_p = _token_stage(att.reshape(n_p, ATT_OUT), ssm.reshape(n_p, SSM_WIDTH), mem.reshape(n_p, MEM_WIDTH),
                       gates, x2, wts, w1[0], w3[0], w2[0], norm_f, tm=256, bm=256)

    xs2 = x_sample.reshape(sbz, D_MODEL)
    qkv_s, u_s, mq_s, gates_s = norm_matmul(xs2, norm1[0], w_in[0], splits, sig, tm=sbz, precise=True)
    qkv5 = qkv_s.reshape(sbz, len(ATT_GROUPS), 3, HEADS_PER_GROUP, HEAD_DIM)
    att_s, *new_caches = attn_sample(qkv5, [jnp.transpose(c, (0, 2, 3, 1)) for c in caches])
    att_s = att_s.reshape(sbz, ATT_OUT)
    h0 = jnp.concatenate([_state_to_blocks(state_ssm_re[0]), _state_to_blocks(state_ssm_im[0])], axis=2)
    ssm_s, st_s = ssm_sample(u_s, h0, prm)
    cmk = cache_mem_k[0].reshape(sbz, MEM_TOKENS, MEM_WIDTH)
    cmv = cache_mem_v[0].reshape(sbz, MEM_TOKENS, MEM_WIDTH)
    mem_s = mem_attention(mq_s.reshape(sbz, 1, MEM_WIDTH), cmk, cmv, tl=1, precise=True).reshape(sbz, MEM_WIDTH)
    y_s = _token_stage(att_s, ssm_s, mem_s, gates_s, xs2, wts_f32, w1[0], w3[0], w2[0], norm_f, tm=sbz, bm=16,
                       precise=True)

    outs = [y_p.reshape(bsz, seq, D_MODEL), y_s.reshape(sbz, 1, D_MODEL)]
    for g, (window, _) in enumerate(ATT_GROUPS):
        keep = min(window, seq)
        for part in (1, 2):
            c0 = g * QKV_GROUP + part * ATT_OUT
            outs.append(qkv3[:, seq - keep:, c0:c0 + ATT_OUT].reshape(1, bsz, keep, HEADS_PER_GROUP, HEAD_DIM))
    outs.append(mk3.reshape(1, bsz, MEM_TOKENS, MEM_HEADS, MEM_HEAD_DIM))
    outs.append(mv3.reshape(1, bsz, MEM_TOKENS, MEM_HEADS, MEM_HEAD_DIM))
    outs.append(_blocks_to_state(st[:, :, :SSM_KSTATE])[None])
    outs.append(_blocks_to_state(st[:, :, SSM_KSTATE:])[None])
    outs.extend(jnp.transpose(c, (0, 3, 1, 2))[None] for c in new_caches)
    outs.append(_blocks_to_state(st_s[:, :, :SSM_KSTATE])[None])
    outs.append(_blocks_to_state(st_s[:, :, SSM_KSTATE:])[None])
    return tuple(outs)
```

```python
import functools
import math

import jax
import jax.numpy as jnp
from jax import lax
from jax.experimental import pallas as pl
from jax.experimental.pallas import tpu as pltpu

F32 = jnp.float32
BF16 = jnp.bfloat16

D_MODEL = 1024
ATT_GROUPS = ((128, 1), (512, 4), (2048, 16))
HEADS_PER_GROUP = 4
HEAD_DIM = 64
N_ATT_HEADS = len(ATT_GROUPS) * HEADS_PER_GROUP
ATT_OUT = HEADS_PER_GROUP * HEAD_DIM
QKV_GROUP = 3 * ATT_OUT
QKV_COLS = len(ATT_GROUPS) * QKV_GROUP
SSM_WIDTH = D_MODEL // 2
SSM_GROUP_CH = 16
SSM_GROUPS = SSM_WIDTH // SSM_GROUP_CH
SSM_STATE = 64
SSM_NSTATE = SSM_GROUPS * SSM_STATE
MEM_TOKENS = 256
MEM_HEADS = 4
MEM_HEAD_DIM = D_MODEL // 8
MEM_WIDTH = MEM_HEADS * MEM_HEAD_DIM
N_BRANCHES = 3
MOE_GROUPS = 4
EXPERTS_PER_GROUP = 8
N_EXPERTS = MOE_GROUPS * EXPERTS_PER_GROUP
TOP_K = 2
D_EXPERT = D_MODEL // 2
RMS_EPS = 1e-6

LANES = 128
SUBLANES = 8
Q_TILE = 128
ROUTER_COLS = 128
NEG_BIG = -1e30
VMEM_LIMIT = 56 * 1024 * 1024


def _cparams(sem):
    return pltpu.CompilerParams(dimension_semantics=sem, vmem_limit_bytes=VMEM_LIMIT)


def _mm(a, b, precise, dims=None):
    if precise:
        a, b, kw = a.astype(F32), b.astype(F32), dict(precision=lax.Precision.HIGHEST)
    else:
        a, b, kw = a.astype(BF16), b.astype(BF16), {}
    if dims is None:
        return jnp.dot(a, b, preferred_element_type=F32, **kw)
    return lax.dot_general(a, b, dims, preferred_element_type=F32, **kw)


NT_DIMS = (((1,), (1,)), ((), ()))


def _rms(x, g):
    ms = jnp.mean(x * x, axis=-1, keepdims=True)
    return (x * lax.rsqrt(ms + RMS_EPS)) * g


def _norm_matmul_kernel(x_ref, g_ref, w_ref, *out_refs, splits, sigmoid, chunk, precise):
    xb = _rms(x_ref[...], g_ref[...])
    if not precise:
        xb = xb.astype(BF16)
    c0 = 0
    for o_ref, n, sg in zip(out_refs, splits, sigmoid):
        for j in range(0, n, chunk):
            w = min(chunk, n - j)
            y = _mm(xb, w_ref[:, c0 + j:c0 + j + w], precise)
            if sg:
                y = jax.nn.sigmoid(y)
            o_ref[:, j:j + w] = y.astype(o_ref.dtype)
        c0 += n


def norm_matmul(x, g, w, splits, sigmoid, tm, dtypes=None, precise=False):
    n, d = x.shape
    tm = min(tm, n)
    kern = functools.partial(_norm_matmul_kernel, splits=tuple(splits), sigmoid=tuple(sigmoid), chunk=512,
                             precise=precise)
    return pl.pallas_call(
        kern,
        grid=(n // tm,),
        in_specs=[
            pl.BlockSpec((tm, d), lambda i: (i, 0)),
            pl.BlockSpec((1, d), lambda i: (0, 0)),
            pl.BlockSpec((d, w.shape[1]), lambda i: (0, 0), pipeline_mode=pl.Buffered(1)),
        ],
        out_specs=[pl.BlockSpec((tm, s), lambda i: (i, 0)) for s in splits],
        out_shape=[jax.ShapeDtypeStruct((n, s), dt) for s, dt in zip(splits, dtypes or (F32,) * len(splits))],
        compiler_params=_cparams(("arbitrary",)),
        name="norm_matmul",
    )(x, g.reshape(1, d), w)


def _attn_prompt_kernel(slopes_ref, *refs, seq):
    qkv_refs = refs[:9]
    o_ref = refs[9]
    o_scr, l_scr = refs[10], refs[11]
    hp = pl.program_id(1)
    scale = HEAD_DIM ** -0.5
    lane = lax.broadcasted_iota(jnp.int32, (Q_TILE, LANES), 1)
    head_a = lane < HEAD_DIM

    for g, (_, dil) in enumerate(ATT_GROUPS):
        q_ref, k_ref, v_ref = qkv_refs[3 * g:3 * g + 3]
        sub_len = seq // dil
        n_blk = sub_len // Q_TILE
        slope_a = slopes_ref[g * HEADS_PER_GROUP + 2 * hp] * float(dil)
        slope_b = slopes_ref[g * HEADS_PER_GROUP + 2 * hp + 1] * float(dil)
        n_prev = 0 if n_blk == 1 else Q_TILE
        row = lax.broadcasted_iota(jnp.int32, (Q_TILE, n_prev + Q_TILE), 0)
        col = lax.broadcasted_iota(jnp.int32, (Q_TILE, n_prev + Q_TILE), 1)
        dist = row + n_prev - col
        in_window = jnp.logical_and(dist >= 0, dist <= Q_TILE)
        distf = dist.astype(F32)

        def rows(start):
            if dil == 1:
                return pl.ds(start, Q_TILE)
            return pl.ds(start, Q_TILE, stride=dil)

        def body(it, carry, q_ref=q_ref, k_ref=k_ref, v_ref=v_ref, n_blk=n_blk, dil=dil,
                 slope_a=slope_a, slope_b=slope_b, rows=rows, g=g, n_prev=n_prev, col=col,
                 in_window=in_window, distf=distf):
            r = it // n_blk
            blk = it % n_blk
            prev = jnp.maximum(blk - 1, 0)
            q_start = r + dil * Q_TILE * blk
            p_start = r + dil * Q_TILE * prev
            q = q_ref[rows(q_start), :]
            if n_prev:
                k2 = jnp.concatenate([k_ref[rows(p_start), :], k_ref[rows(q_start), :]], axis=0).astype(BF16)
                v2 = jnp.concatenate([v_ref[rows(p_start), :], v_ref[rows(q_start), :]], axis=0).astype(BF16)
                valid = jnp.logical_and(in_window, jnp.logical_or(col >= Q_TILE, blk > 0))
            else:
                k2 = k_ref[rows(q_start), :].astype(BF16)
                v2 = v_ref[rows(q_start), :].astype(BF16)
                valid = in_window
            outs, lses = [], []
            for is_a, slope in ((True, slope_a), (False, slope_b)):
                hmask = head_a if is_a else jnp.logical_not(head_a)
                qh = jnp.where(hmask, q, 0.0).astype(BF16)
                s = lax.dot_general(qh, k2, NT_DIMS, preferred_element_type=F32)
                s = s * scale - slope * distf
                s = jnp.where(valid, s, NEG_BIG)
                m = jnp.max(s, axis=-1, keepdims=True)
                p = jnp.exp(s - m)
                den = jnp.sum(p, axis=-1, keepdims=True)
                pv = jnp.dot(p.astype(BF16), v2, preferred_element_type=F32)
                outs.append(pv / den)
                lses.append(m + jnp.log(den))
            o_scr[g, rows(q_start), :] = jnp.where(head_a, outs[0], outs[1])
            l_scr[g, rows(q_start), :] = jnp.where(head_a, lses[0], lses[1])
            return carry

        lax.fori_loop(0, dil * n_blk, body, 0, unroll=8)

    def combine(c, carry):
        sl = pl.ds(pl.multiple_of(c * 256, 256), 256)
        l0, l1, l2 = l_scr[0, sl, :], l_scr[1, sl, :], l_scr[2, sl, :]
        m = jnp.maximum(jnp.maximum(l0, l1), l2)
        e0, e1, e2 = jnp.exp(l0 - m), jnp.exp(l1 - m), jnp.exp(l2 - m)
        tot = e0 + e1 + e2
        att = (e0 * o_scr[0, sl, :] + e1 * o_scr[1, sl, :] + e2 * o_scr[2, sl, :]) / tot
        o_ref[sl, :] = att.astype(o_ref.dtype)
        return carry

    lax.fori_loop(0, seq // 256, combine, 0)


def attn_prompt(qkv, slopes):
    bsz, seq, _ = qkv.shape
    in_specs = []
    for g in range(len(ATT_GROUPS)):
        for part in range(3):
            base = (g * QKV_GROUP + part * ATT_OUT) // LANES
            in_specs.append(pl.BlockSpec((None, seq, LANES),
                                         lambda b, h, sl, base=base: (b, 0, base + h)))
    grid_spec = pltpu.PrefetchScalarGridSpec(
        num_scalar_prefetch=1,
        grid=(bsz, ATT_OUT // LANES),
        in_specs=in_specs,
        out_specs=pl.BlockSpec((None, seq, LANES), lambda b, h, sl: (b, 0, h)),
        scratch_shapes=[pltpu.VMEM((3, seq, LANES), F32), pltpu.VMEM((3, seq, LANES), F32)],
    )
    return pl.pallas_call(
        functools.partial(_attn_prompt_kernel, seq=seq),
        grid_spec=grid_spec,
        out_shape=jax.ShapeDtypeStruct((bsz, seq, ATT_OUT), BF16),
        compiler_params=_cparams(("arbitrary", "arbitrary")),
        name="attn_prompt",
    )(slopes, *([qkv] * 9))


def _attn_sample_kernel(qkv_ref, col_ref, *refs):
    cache_refs = refs[:6]
    o_ref = refs[6]
    new_refs = refs[7:13]
    scale = HEAD_DIM ** -0.5
    n_grp = len(ATT_GROUPS)
    outs = [[None] * HEADS_PER_GROUP for _ in range(n_grp)]
    lses = [[None] * HEADS_PER_GROUP for _ in range(n_grp)]
    for g, (window, dil) in enumerate(ATT_GROUPS):
        lane = lax.broadcasted_iota(jnp.int32, (1, window), 1)
        on_grid = jnp.bitwise_and(lane, dil - 1) == 0
        dist = (window - lane).astype(F32)
        last = lax.broadcasted_iota(jnp.int32, (HEAD_DIM, window), 1) == window - 1
        for h in range(HEADS_PER_GROUP):
            slope = 2.0 ** (-8.0 * (g * HEADS_PER_GROUP + h + 1) / N_ATT_HEADS)
            q = qkv_ref[g, 0][h:h + 1, :]
            kn = qkv_ref[g, 1][h:h + 1, :]
            vn = qkv_ref[g, 2][h:h + 1, :]
            kt = cache_refs[2 * g][h]
            vt = cache_refs[2 * g + 1][h]
            s = _mm(q, kt, True) * scale - slope * dist
            s = jnp.where(on_grid, s, NEG_BIG)
            sn = jnp.sum(q * kn, axis=-1, keepdims=True) * scale
            m = jnp.maximum(jnp.max(s, axis=-1, keepdims=True), sn)
            p = jnp.exp(s - m)
            pn = jnp.exp(sn - m)
            den = jnp.sum(p, axis=-1, keepdims=True) + pn
            outs[g][h] = (_mm(p, vt, True, NT_DIMS) + pn * vn) / den
            lses[g][h] = m + jnp.log(den)
            new_refs[2 * g][h] = jnp.where(last, col_ref[g, 0, h], pltpu.roll(kt, window - 1, axis=1))
            new_refs[2 * g + 1][h] = jnp.where(last, col_ref[g, 1, h], pltpu.roll(vt, window - 1, axis=1))
    for h in range(HEADS_PER_GROUP):
        m = jnp.maximum(jnp.maximum(lses[0][h], lses[1][h]), lses[2][h])
        es = [jnp.exp(lses[g][h] - m) for g in range(n_grp)]
        num = es[0] * outs[0][h] + es[1] * outs[1][h] + es[2] * outs[2][h]
        o_ref[h:h + 1, :] = num / (es[0] + es[1] + es[2])


def attn_sample(qkv5, caches_t):
    bsz = qkv5.shape[0]
    cols = qkv5[:, :, 1:3].reshape(bsz, len(ATT_GROUPS), 2, HEADS_PER_GROUP, HEAD_DIM, 1)
    in_specs = [pl.BlockSpec((None, 3, 3, HEADS_PER_GROUP, HEAD_DIM), lambda b: (b, 0, 0, 0, 0)),
                pl.BlockSpec((None, 3, 2, HEADS_PER_GROUP, HEAD_DIM, 1), lambda b: (b, 0, 0, 0, 0, 0))]
    cache_specs = [pl.BlockSpec((None,) + c.shape[1:], lambda b: (b, 0, 0, 0)) for c in caches_t]
    return pl.pallas_call(
        _attn_sample_kernel,
        grid=(bsz,),
        in_specs=in_specs + cache_specs,
        out_specs=[pl.BlockSpec((None, HEADS_PER_GROUP, HEAD_DIM), lambda b: (b, 0, 0))] + cache_specs,
        out_shape=[jax.ShapeDtypeStruct((bsz, HEADS_PER_GROUP, HEAD_DIM), F32)]
                  + [jax.ShapeDtypeStruct(c.shape, F32) for c in caches_t],
        compiler_params=_cparams(("arbitrary",)),
        name="attn_sample",
    )(qkv5, cols, *caches_t)


SSM_KBLK = SSM_WIDTH // LANES
SSM_KSTATE = SSM_NSTATE // SSM_KBLK


def _ssm_tail(y, u, dsk, wglu, bglu, precise=False):
    y = y + dsk * u
    z = jax.nn.gelu(y)
    gl = _mm(z, wglu, precise) + bglu
    return z * jax.nn.sigmoid(gl)


def _ssm_prompt_kernel(u_ref, wb_ref, wc_ref, are_ref, aim_ref, dsk_ref, wglu_ref, bglu_ref,
                       o_ref, st_ref, uperm, bu, sbf, yperm, ynat, *, bsz, tt):
    rows = bsz * tt

    @pl.when(pl.program_id(0) == 0)
    def _():
        st_ref[...] = jnp.zeros_like(st_ref)

    for b in range(bsz):
        ub = u_ref[b]
        for k in range(SSM_KBLK):
            uperm[k, pl.ds(b, tt, stride=bsz), :] = ub[:, k * LANES:(k + 1) * LANES]

    for k in range(SSM_KBLK):
        bu[k] = jnp.dot(uperm[k].astype(BF16), wb_ref[k], preferred_element_type=F32)

    for k in range(SSM_KBLK):
        ar = jnp.broadcast_to(are_ref[k], (bsz, SSM_KSTATE))
        ai = jnp.broadcast_to(aim_ref[k], (bsz, SSM_KSTATE))
        st = st_ref[k]

        def step(t, carry, k=k, ar=ar, ai=ai):
            re, im = carry
            r0 = pl.multiple_of(t * bsz, bsz)
            b_t = bu[k, pl.ds(r0, bsz), :]
            nre = ar * re - ai * im + b_t[:, :SSM_KSTATE]
            nim = ar * im + ai * re + b_t[:, SSM_KSTATE:]
            sbf[k, pl.ds(r0, bsz), :] = jnp.concatenate([nre, nim], axis=1).astype(BF16)
            return nre, nim

        re, im = lax.fori_loop(0, tt, step, (st[:, :SSM_KSTATE], st[:, SSM_KSTATE:]), unroll=2)
        st_ref[k] = jnp.concatenate([re, im], axis=1)

    for k in range(SSM_KBLK):
        yperm[k] = jnp.dot(sbf[k], wc_ref[k], preferred_element_type=F32)

    for b in range(bsz):
        for k in range(SSM_KBLK):
            ynat[b * tt:(b + 1) * tt, k * LANES:(k + 1) * LANES] = yperm[k, pl.ds(b, tt, stride=bsz), :]

    u2 = u_ref[...].reshape(rows, SSM_WIDTH)
    out = _ssm_tail(ynat[...], u2, dsk_ref[...], wglu_ref[...], bglu_ref[...])
    o_ref[...] = out.reshape(bsz, tt, SSM_WIDTH).astype(o_ref.dtype)


def ssm_prompt(u, prm, tt=32):
    bsz, seq, _ = u.shape
    rows = bsz * tt
    const = lambda shape: pl.BlockSpec(shape, lambda i: (0,) * len(shape))
    return pl.pallas_call(
        functools.partial(_ssm_prompt_kernel, bsz=bsz, tt=tt),
        grid=(seq // tt,),
        in_specs=[
            pl.BlockSpec((bsz, tt, SSM_WIDTH), lambda i: (0, i, 0)),
            const((SSM_KBLK, LANES, 2 * SSM_KSTATE)),
            const((SSM_KBLK, 2 * SSM_KSTATE, LANES)),
            const((SSM_KBLK, 1, SSM_KSTATE)),
            const((SSM_KBLK, 1, SSM_KSTATE)),
            const((1, SSM_WIDTH)),
            const((SSM_WIDTH, SSM_WIDTH)),
            const((1, SSM_WIDTH)),
        ],
        out_specs=[
            pl.BlockSpec((bsz, tt, SSM_WIDTH), lambda i: (0, i, 0)),
            const((SSM_KBLK, bsz, 2 * SSM_KSTATE)),
        ],
        out_shape=[
            jax.ShapeDtypeStruct((bsz, seq, SSM_WIDTH), BF16),
            jax.ShapeDtypeStruct((SSM_KBLK, bsz, 2 * SSM_KSTATE), F32),
        ],
        scratch_shapes=[
            pltpu.VMEM((SSM_KBLK, rows, LANES), F32),
            pltpu.VMEM((SSM_KBLK, rows, 2 * SSM_KSTATE), F32),
            pltpu.VMEM((SSM_KBLK, rows, 2 * SSM_KSTATE), BF16),
            pltpu.VMEM((SSM_KBLK, rows, LANES), F32),
            pltpu.VMEM((rows, SSM_WIDTH), F32),
        ],
        compiler_params=_cparams(("arbitrary",)),
        name="ssm_prompt",
    )(u, prm["wb"], prm["wc"], prm["a_re"], prm["a_im"], prm["dsk"], prm["wglu"], prm["bglu"])


def _ssm_sample_kernel(u_ref, h0_ref, wb_ref, wc_ref, are_ref, aim_ref, dsk_ref, wglu_ref, bglu_ref,
                       o_ref, st_ref):
    u = u_ref[...]
    ys = []
    for k in range(SSM_KBLK):
        b_t = _mm(u[:, k * LANES:(k + 1) * LANES], wb_ref[k], True)
        h0 = h0_ref[k]
        re, im = h0[:, :SSM_KSTATE], h0[:, SSM_KSTATE:]
        ar, ai = are_ref[k], aim_ref[k]
        nre = ar * re - ai * im + b_t[:, :SSM_KSTATE]
        nim = ar * im + ai * re + b_t[:, SSM_KSTATE:]
        s = jnp.concatenate([nre, nim], axis=1)
        st_ref[k] = s
        ys.append(_mm(s, wc_ref[k], True))
    y = jnp.concatenate(ys, axis=1)
    o_ref[...] = _ssm_tail(y, u, dsk_ref[...], wglu_ref[...], bglu_ref[...], precise=True)


def ssm_sample(u, h0, prm):
    bsz = u.shape[0]
    return pl.pallas_call(
        _ssm_sample_kernel,
        out_shape=[jax.ShapeDtypeStruct((bsz, SSM_WIDTH), F32),
                   jax.ShapeDtypeStruct((SSM_KBLK, bsz, 2 * SSM_KSTATE), F32)],
        compiler_params=pltpu.CompilerParams(vmem_limit_bytes=VMEM_LIMIT),
        name="ssm_sample",
    )(u, h0, prm["wb_f32"], prm["wc_f32"], prm["a_re"], prm["a_im"], prm["dsk"], prm["wglu_f32"], prm["bglu"])


def ssm_params(lam_re, lam_im, log_dt, b_re, b_im, c_re, c_im, d_skip, w_glu, b_glu):
    dt = jnp.exp(log_dt)[:, None]
    mag = jnp.exp(lam_re * dt)
    ab_re = mag * jnp.cos(lam_im * dt)
    ab_im = mag * jnp.sin(lam_im * dt)
    den = lam_re * lam_re + lam_im * lam_im
    nr = ab_re - 1.0
    ni = ab_im
    z_re = ((nr * lam_re + ni * lam_im) / den)[..., None]
    z_im = ((ni * lam_re - nr * lam_im) / den)[..., None]
    bb_re = z_re * b_re - z_im * b_im
    bb_im = z_re * b_im + z_im * b_re
    gpb = SSM_GROUPS // SSM_KBLK
    eye = jnp.eye(gpb, dtype=F32)

    def in_mat(bb):
        bbk = bb.reshape(SSM_KBLK, gpb, SSM_STATE, SSM_GROUP_CH)
        m = jnp.einsum("kgpc,gh->kgchp", bbk, eye)
        return m.reshape(SSM_KBLK, LANES, SSM_KSTATE)

    def out_mat(c):
        ck = c.reshape(SSM_KBLK, gpb, SSM_GROUP_CH, SSM_STATE)
        m = jnp.einsum("kgcp,gh->kgphc", ck, eye)
        return m.reshape(SSM_KBLK, SSM_KSTATE, LANES)

    wb = jnp.concatenate([in_mat(bb_re), in_mat(bb_im)], axis=2)
    wc = jnp.concatenate([out_mat(c_re), -out_mat(c_im)], axis=1)
    return dict(
        wb=wb.astype(BF16), wc=wc.astype(BF16), wb_f32=wb, wc_f32=wc, wglu_f32=w_glu,
        a_re=ab_re.reshape(SSM_KBLK, 1, SSM_KSTATE), a_im=ab_im.reshape(SSM_KBLK, 1, SSM_KSTATE),
        dsk=d_skip.reshape(1, SSM_WIDTH), wglu=w_glu.astype(BF16), bglu=b_glu.reshape(1, SSM_WIDTH))


def _state_to_blocks(h):
    bsz = h.shape[0]
    return h.reshape(bsz, SSM_KBLK, SSM_KSTATE).transpose(1, 0, 2)


def _blocks_to_state(s):
    bsz = s.shape[1]
    return s.transpose(1, 0, 2).reshape(bsz, SSM_GROUPS, SSM_STATE)


def _mem_attn_kernel(q_ref, k_ref, v_ref, o_ref, *, precise):
    scale = MEM_HEAD_DIM ** -0.5
    for h in range(MEM_HEADS):
        sl = slice(h * MEM_HEAD_DIM, (h + 1) * MEM_HEAD_DIM)
        s = _mm(q_ref[:, sl], k_ref[:, sl], precise, NT_DIMS) * scale
        m = jnp.max(s, axis=-1, keepdims=True)
        p = jnp.exp(s - m)
        den = jnp.sum(p, axis=-1, keepdims=True)
        o_ref[:, sl] = _mm(p / den, v_ref[:, sl], precise).astype(o_ref.dtype)


def mem_attention(q, mk, mv, tl, out_dtype=F32, precise=False):
    bsz, seq, _ = q.shape
    tl = min(tl, seq)
    return pl.pallas_call(
        functools.partial(_mem_attn_kernel, precise=precise),
        grid=(bsz, seq // tl),
        in_specs=[
            pl.BlockSpec((None, tl, MEM_WIDTH), lambda b, i: (b, i, 0)),
            pl.BlockSpec((None, MEM_TOKENS, MEM_WIDTH), lambda b, i: (b, 0, 0)),
            pl.BlockSpec((None, MEM_TOKENS, MEM_WIDTH), lambda b, i: (b, 0, 0)),
        ],
        out_specs=pl.BlockSpec((None, tl, MEM_WIDTH), lambda b, i: (b, i, 0)),
        out_shape=jax.ShapeDtypeStruct((bsz, seq, MEM_WIDTH), out_dtype),
        compiler_params=_cparams(("arbitrary", "arbitrary")),
        name="mem_attention",
    )(q, mk, mv)


def _route(logits):
    lane = lax.broadcasted_iota(jnp.int32, logits.shape, 1)
    lanef = lane.astype(F32)
    none = float(ROUTER_COLS)
    neg = -jnp.inf
    gl = jnp.where(lane < MOE_GROUPS, logits, neg)
    gmax = jnp.max(gl, axis=-1, keepdims=True)
    grp = jnp.min(jnp.where(gl == gmax, lanef, none), axis=-1, keepdims=True)
    p_grp = 1.0 / jnp.sum(jnp.exp(gl - gmax), axis=-1, keepdims=True)
    lo = MOE_GROUPS + grp * EXPERTS_PER_GROUP
    el = jnp.where(jnp.logical_and(lanef >= lo, lanef < lo + EXPERTS_PER_GROUP), logits, neg)
    v1 = jnp.max(el, axis=-1, keepdims=True)
    i1 = jnp.min(jnp.where(el == v1, lanef, none), axis=-1, keepdims=True)
    el2 = jnp.where(lanef == i1, neg, el)
    v2 = jnp.max(el2, axis=-1, keepdims=True)
    i2 = jnp.min(jnp.where(el2 == v2, lanef, none), axis=-1, keepdims=True)
    t = jnp.exp(v2 - v1)
    g1 = p_grp / (1.0 + t)
    g2 = g1 * t
    out = jnp.where(lane == 0, g1, 0.0)
    out = jnp.where(lane == 1, g2, out)
    out = jnp.where(lane == 2, i1 - MOE_GROUPS, out)
    return jnp.where(lane == 3, i2 - MOE_GROUPS, out)


def _merge_kernel(att_ref, ssm_ref, mem_ref, gate_ref, x_ref, wa_ref, ws_ref, wm_ref, wo_ref,
                  n2_ref, wr_ref, br_ref, h_ref, hn_ref, lg_ref, *, tm, precise):
    a = _mm(att_ref[...], wa_ref[...], precise)
    merged = gate_ref[:, 0:D_MODEL] * a
    s = _mm(ssm_ref[...], ws_ref[...], precise)
    merged = merged + gate_ref[:, D_MODEL:2 * D_MODEL] * s
    m = _mm(mem_ref[...], wm_ref[...], precise)
    merged = merged + gate_ref[:, 2 * D_MODEL:3 * D_MODEL] * m
    h = x_ref[...] + _mm(merged, wo_ref[...], precise)
    h_ref[...] = h
    hn = _rms(h, n2_ref[...])
    for j in range(D_MODEL // LANES):
        hn_ref[pl.ds(j, tm, stride=SUBLANES), :] = hn[:, j * LANES:(j + 1) * LANES]
    logits = _mm(hn, wr_ref[...], precise) + br_ref[...]
    lg_ref[...] = _route(logits)


def merge(att, ssm, mem, gates, x, wts, tm, precise=False):
    n = x.shape[0]
    tm = min(tm, n)
    row = lambda w: pl.BlockSpec((tm, w), lambda i: (i, 0))
    const = lambda a: pl.BlockSpec(a.shape, lambda i: (0, 0))
    ws = [wts["wa"], wts["ws"], wts["wm"], wts["wo"], wts["n2"], wts["wr"], wts["br"]]
    return pl.pallas_call(
        functools.partial(_merge_kernel, tm=tm, precise=precise),
        grid=(n // tm,),
        in_specs=[row(ATT_OUT), row(SSM_WIDTH), row(MEM_WIDTH), row(N_BRANCHES * D_MODEL), row(D_MODEL)]
                 + [const(w) for w in ws],
        out_specs=[row(D_MODEL), pl.BlockSpec((tm * SUBLANES, LANES), lambda i: (i, 0)), row(ROUTER_COLS)],
        out_shape=[jax.ShapeDtypeStruct((n, D_MODEL), F32),
                   jax.ShapeDtypeStruct((n * SUBLANES, LANES), F32),
                   jax.ShapeDtypeStruct((n, ROUTER_COLS), F32)],
        compiler_params=_cparams(("arbitrary",)),
        name="merge",
    )(att, ssm, mem, gates, x, *ws)


TABLE_WORDS = SUBLANES * LANES


def _row_dma_loop(n, off, smem, slot, start):
    inner = min(n, LANES)
    assert n % inner == 0

    def body(r, carry):
        for j in range(inner):
            f = off + j
            start(smem[slot, f // LANES + r, f % LANES], r * inner + j, j % 2)
        return carry

    if n == inner:
        body(0, 0)
    else:
        lax.fori_loop(0, n // inner, body, 0)


def _tile_rows(i):
    start = i * SUBLANES
    return pl.ds(start if isinstance(start, int) else pl.multiple_of(start, SUBLANES), SUBLANES)


def _dispatch_kernel(pos_hbm, hn_ref, xs_hbm, pos_smem, zero, sem, psem, *, tm, n_pad):
    i = pl.program_id(0)
    slot = i % 2

    def pos_copy(b, s):
        return pltpu.make_async_copy(pos_hbm.at[b], pos_smem.at[s], psem.at[s])

    @pl.when(i == 0)
    def _():
        pos_copy(0, 0).start()

    @pl.when(i + 1 < pl.num_programs(0))
    def _():
        pos_copy(i + 1, 1 - slot).start()

    zero[...] = jnp.zeros_like(zero)
    pos_copy(i, slot).wait()

    def send_token(row, tok, prio):
        pltpu.make_async_copy(hn_ref.at[_tile_rows(tok)], xs_hbm.at[row], sem).start(priority=prio)

    def send_zero(row, _, prio):
        pltpu.make_async_copy(zero, xs_hbm.at[row], sem).start(priority=prio)

    for k in range(TOP_K):
        _row_dma_loop(tm, k * tm, pos_smem, slot, send_token)
    _row_dma_loop(n_pad, TOP_K * tm, pos_smem, slot, send_zero)
    whole, part = divmod(n_pad, tm)
    for _ in range(TOP_K + whole):
        pltpu.make_async_copy(hn_ref, hn_ref, sem).wait()
    if part:
        piece = hn_ref.at[pl.ds(0, part * SUBLANES)]
        pltpu.make_async_copy(piece, piece, sem).wait()


def moe_dispatch(pos, hn2d, n_rows, tm, n_pad):
    n_tiles = pos.shape[0]
    return pl.pallas_call(
        functools.partial(_dispatch_kernel, tm=tm, n_pad=n_pad),
        grid=(n_tiles,),
        in_specs=[pl.BlockSpec(memory_space=pl.ANY),
                  pl.BlockSpec((tm * SUBLANES, LANES), lambda i: (i, 0))],
        out_specs=pl.BlockSpec(memory_space=pl.ANY),
        out_shape=jax.ShapeDtypeStruct((n_rows, SUBLANES, LANES), F32),
        scratch_shapes=[pltpu.SMEM((2, SUBLANES, LANES), jnp.int32),
                        pltpu.VMEM((SUBLANES, LANES), F32),
                        pltpu.SemaphoreType.DMA, pltpu.SemaphoreType.DMA((2,))],
        compiler_params=_cparams(("arbitrary",)),
        name="moe_dispatch",
    )(pos, hn2d)


def _moe_kernel(be_ref, dst_hbm, xs_ref, w1_ref, w3_ref, w2_ref, out_hbm,
                dst_smem, ybuf, w1b, w3b, w2b, ssem, isem, *, bm):
    blk = pl.program_id(0)
    nblk = pl.num_programs(0)
    slot = blk % 2
    tiles = D_MODEL // LANES

    def dst_copy(b, s):
        return pltpu.make_async_copy(dst_hbm.at[b], dst_smem.at[s], isem.at[s])

    def wait_scatter():
        pltpu.make_async_copy(ybuf, ybuf, ssem).wait()

    @pl.when(blk == 0)
    def _():
        dst_copy(0, 0).start()

    @pl.when(blk + 1 < nblk)
    def _():
        dst_copy(blk + 1, 1 - slot).start()

    prev = jnp.maximum(blk - 1, 0)
    changed = jnp.logical_or(blk == 0, be_ref[blk] != be_ref[prev])

    @pl.when(changed)
    def _():
        w1b[...] = w1_ref[...].astype(BF16)
        w3b[...] = w3_ref[...].astype(BF16)
        w2b[...] = w2_ref[...].astype(BF16)

    x = jnp.concatenate([xs_ref[pl.ds(j, bm, stride=SUBLANES), :].astype(BF16) for j in range(tiles)], axis=1)
    h1 = jnp.dot(x, w1b[...], preferred_element_type=F32)
    h3 = jnp.dot(x, w3b[...], preferred_element_type=F32)
    act = (jax.nn.silu(h1) * h3).astype(BF16)
    y = jnp.dot(act, w2b[...], preferred_element_type=F32)

    @pl.when(blk > 0)
    def _():
        wait_scatter()

    for j in range(tiles):
        ybuf[pl.ds(j, bm, stride=SUBLANES), :] = y[:, j * LANES:(j + 1) * LANES]
    dst_copy(blk, slot).wait()

    def send_row(row, i, prio):
        pltpu.make_async_copy(ybuf.at[_tile_rows(i)], out_hbm.at[row], ssem).start(priority=prio)

    _row_dma_loop(bm, 0, dst_smem, slot, send_row)

    @pl.when(blk == nblk - 1)
    def _():
        wait_scatter()


def moe_experts(block_expert, dst, xs2d, w1, w3, w2, n_out_rows, bm):
    n_blocks = dst.shape[0]
    grid_spec = pltpu.PrefetchScalarGridSpec(
        num_scalar_prefetch=1,
        grid=(n_blocks,),
        in_specs=[
            pl.BlockSpec(memory_space=pl.ANY),
            pl.BlockSpec((bm * SUBLANES, LANES), lambda b, be: (b, 0)),
            pl.BlockSpec((None, D_MODEL, D_EXPERT), lambda b, be: (be[b], 0, 0)),
            pl.BlockSpec((None, D_MODEL, D_EXPERT), lambda b, be: (be[b], 0, 0)),
            pl.BlockSpec((None, D_EXPERT, D_MODEL), lambda b, be: (be[b], 0, 0)),
        ],
        out_specs=pl.BlockSpec(memory_space=pl.ANY),
        scratch_shapes=[
            pltpu.SMEM((2, SUBLANES, LANES), jnp.int32),
            pltpu.VMEM((bm * SUBLANES, LANES), F32),
            pltpu.VMEM((D_MODEL, D_EXPERT), BF16),
            pltpu.VMEM((D_MODEL, D_EXPERT), BF16),
            pltpu.VMEM((D_EXPERT, D_MODEL), BF16),
            pltpu.SemaphoreType.DMA,
            pltpu.SemaphoreType.DMA((2,)),
        ],
    )
    return pl.pallas_call(
        functools.partial(_moe_kernel, bm=bm),
        grid_spec=grid_spec,
        out_shape=jax.ShapeDtypeStruct((n_out_rows, SUBLANES, LANES), F32),
        compiler_params=_cparams(("arbitrary",)),
        name="moe_experts",
    )(block_expert, dst, xs2d, w1, w3, w2)


def _as_tables(cols, width):
    fill = jnp.zeros((cols.shape[0], TABLE_WORDS - width), jnp.int32)
    return jnp.concatenate([cols, fill], axis=1).reshape(cols.shape[0], SUBLANES, LANES)


def route_tables(route, bm, tm):
    n_tok = route.shape[0]
    n_pairs = n_tok * TOP_K
    e_flat = route[:, 2:2 + TOP_K].astype(jnp.int32).reshape(n_pairs)
    experts = jnp.arange(N_EXPERTS, dtype=jnp.int32)
    counts = jnp.sum(e_flat[:, None] == experts[None, :], axis=0, dtype=jnp.int32)
    padded = (counts + bm - 1) // bm * bm
    pad_ends = jnp.cumsum(padded)
    n_blocks = -(-n_pairs // bm) + N_EXPERTS
    n_rows = n_blocks * bm
    row0 = jnp.arange(n_blocks, dtype=jnp.int32) * bm
    block_expert = jnp.minimum(jnp.sum(pad_ends[None, :] <= row0[:, None], axis=1, dtype=jnp.int32),
                               N_EXPERTS - 1)
    n_fill = n_rows - n_pairs
    fill_id = jnp.arange(n_fill, dtype=jnp.int32)
    fill_e = jnp.minimum(fill_id // bm, N_EXPERTS - 1)
    fill_used = jnp.logical_and(fill_id < N_EXPERTS * bm, fill_id % bm < (padded - counts)[fill_e])
    keys = jnp.concatenate([e_flat * 2, jnp.where(fill_used, fill_e * 2 + 1, 2 * N_EXPERTS)])
    vals = jnp.concatenate([jnp.arange(n_pairs, dtype=jnp.int32), jnp.full((n_fill,), -1, jnp.int32)])
    _, pair_at = lax.sort((keys, vals), num_keys=1, is_stable=True)
    _, row_of = lax.sort((pair_at, jnp.arange(n_rows, dtype=jnp.int32)), num_keys=1)
    n_tiles = n_tok // tm
    n_pad = n_fill // n_tiles
    assert n_pad * n_tiles == n_fill and n_pad % SUBLANES == 0
    pair_row = row_of[n_fill:].reshape(n_tok, TOP_K)
    pos = jnp.concatenate([pair_row[:, k].reshape(n_tiles, tm) for k in range(TOP_K)]
                          + [row_of[:n_fill].reshape(n_tiles, n_pad)], axis=1)
    pair_at = pair_at.reshape(n_blocks, bm)
    valid = pair_at >= 0
    trash = n_pairs + jnp.arange(bm, dtype=jnp.int32)[None, :]
    dst = jnp.where(valid, (pair_at % TOP_K) * n_tok + pair_at // TOP_K, trash)
    return block_expert, _as_tables(dst, bm), _as_tables(pos, TOP_K * tm + n_pad), n_pad, n_rows, n_pairs + bm


def _final_kernel(h_ref, p0_ref, p1_ref, gw_ref, nf_ref, y_ref, *, tm):
    tiles = D_MODEL // LANES
    p0 = jnp.concatenate([p0_ref[pl.ds(j, tm, stride=SUBLANES), :] for j in range(tiles)], axis=1)
    p1 = jnp.concatenate([p1_ref[pl.ds(j, tm, stride=SUBLANES), :] for j in range(tiles)], axis=1)
    gw = gw_ref[...]
    h = h_ref[...] + (gw[:, 0:1] * p0 + gw[:, 1:2] * p1)
    y_ref[...] = _rms(h, nf_ref[...])


def final(h, pairs2d, gate_pad, norm_f, tm):
    n = h.shape[0]
    tm = min(tm, n)
    nb = n // tm
    row = lambda w: pl.BlockSpec((tm, w), lambda i: (i, 0))
    return pl.pallas_call(
        functools.partial(_final_kernel, tm=tm),
        grid=(nb,),
        in_specs=[
            row(D_MODEL),
            pl.BlockSpec((tm * SUBLANES, LANES), lambda i: (i, 0)),
            pl.BlockSpec((tm * SUBLANES, LANES), lambda i: (nb + i, 0)),
            row(LANES),
            pl.BlockSpec((1, D_MODEL), lambda i: (0, 0)),
        ],
        out_specs=row(D_MODEL),
        out_shape=jax.ShapeDtypeStruct((n, D_MODEL), F32),
        compiler_params=_cparams(("arbitrary",)),
        name="final",
    )(h, pairs2d, pairs2d, gate_pad, norm_f.reshape(1, D_MODEL))


def _alibi_slopes():
    h = jnp.arange(1, N_ATT_HEADS + 1, dtype=F32)
    return jnp.exp2(-8.0 * h / N_ATT_HEADS)


def _token_stage(att, ssm, mem, gates, x2, wts, w1, w3, w2, norm_f, tm, bm, precise=False):
    n = x2.shape[0]
    h, hn2d, route = merge(att, ssm, mem, gates, x2, wts, tm, precise)
    tm = min(tm, n)
    block_expert, dst, pos, n_pad, n_rows, out_rows = route_tables(route, bm, tm)
    xs = moe_dispatch(pos, hn2d, n_rows, tm, n_pad)
    pairs = moe_experts(block_expert, dst, xs.reshape(n_rows * SUBLANES, LANES), w1, w3, w2, out_rows, bm)
    return final(h, pairs.reshape(out_rows * SUBLANES, LANES), route, norm_f, tm)


def kernel(x_prompt, x_sample, cache_swa0_k, cache_swa0_v, cache_swa1_k, cache_swa1_v, cache_swa2_k, cache_swa2_v, cache_mem_k, cache_mem_v, state_ssm_re, state_ssm_im, mem_prompt, norm1, w_in, lam_re, lam_im, log_dt, ssm_b_re, ssm_b_im, ssm_c_re, ssm_c_im, ssm_d, w_glu, b_glu, w_att_out, w_ssm_out, w_mem_out, w_o, norm_mem, w_mk, w_mv, norm2, w_grp, b_grp, w_exp, b_exp, w1, w3, w2, norm_f):
    assert norm1.shape[0] == 1, "single-layer trunk"
    bsz, seq, _ = x_prompt.shape
    sbz = x_sample.shape[0]
    n_p = bsz * seq
    caches = (cache_swa0_k[0], cache_swa0_v[0], cache_swa1_k[0], cache_swa1_v[0], cache_swa2_k[0], cache_swa2_v[0])

    w_in_b = w_in[0].astype(BF16)
    w_mkv_b = jnp.concatenate([w_mk[0], w_mv[0]], axis=1).astype(BF16)
    wr = jnp.zeros((D_MODEL, ROUTER_COLS), F32)
    wr = wr.at[:, :MOE_GROUPS].set(w_grp[0]).at[:, MOE_GROUPS:MOE_GROUPS + N_EXPERTS].set(w_exp[0])
    br = jnp.zeros((1, ROUTER_COLS), F32)
    br = br.at[0, :MOE_GROUPS].set(b_grp[0]).at[0, MOE_GROUPS:MOE_GROUPS + N_EXPERTS].set(b_exp[0])
    wts_f32 = dict(wa=w_att_out[0], ws=w_ssm_out[0], wm=w_mem_out[0], wo=w_o[0],
                   n2=norm2[0].reshape(1, D_MODEL), wr=wr, br=br)
    wts = {k: (v.astype(BF16) if k in ("wa", "ws", "wm", "wo", "wr") else v) for k, v in wts_f32.items()}
    prm = ssm_params(lam_re[0], lam_im[0], log_dt[0], ssm_b_re[0], ssm_b_im[0], ssm_c_re[0], ssm_c_im[0],
                     ssm_d[0], w_glu[0], b_glu[0])
    slopes = _alibi_slopes()
    splits = (QKV_COLS, SSM_WIDTH, MEM_WIDTH, N_BRANCHES * D_MODEL)
    sig = (False, False, False, True)

    x2 = x_prompt.reshape(n_p, D_MODEL)
    qkv, u, mq, gates = norm_matmul(x2, norm1[0], w_in_b, splits, sig, tm=256, dtypes=(F32, F32, BF16, BF16))
    qkv3 = qkv.reshape(bsz, seq, QKV_COLS)
    att = attn_prompt(qkv3, slopes)
    ssm, st = ssm_prompt(u.reshape(bsz, seq, SSM_WIDTH), prm)
    mk, mv = norm_matmul(mem_prompt.reshape(bsz * MEM_TOKENS, D_MODEL), norm_mem[0], w_mkv_b,
                         (MEM_WIDTH, MEM_WIDTH), (False, False), tm=256)
    mk3 = mk.reshape(bsz, MEM_TOKENS, MEM_WIDTH)
    mv3 = mv.reshape(bsz, MEM_TOKENS, MEM_WIDTH)
    mem = mem_attention(mq.reshape(bsz, seq, MEM_WIDTH), mk3, mv3, tl=512, out_dtype=BF16)
    y_p = _token_stage(att.reshape(n_p, ATT_OUT), ssm.reshape(n_p, SSM_WIDTH), mem.reshape(n_p, MEM_WIDTH),
                       gates, x2, wts, w1[0], w3[0], w2[0], norm_f, tm=256, bm=256)

    xs2 = x_sample.reshape(sbz, D_MODEL)
    qkv_s, u_s, mq_s, gates_s = norm_matmul(xs2, norm1[0], w_in[0], splits, sig, tm=sbz, precise=True)
    qkv5 = qkv_s.reshape(sbz, len(ATT_GROUPS), 3, HEADS_PER_GROUP, HEAD_DIM)
    att_s, *new_caches = attn_sample(qkv5, [jnp.transpose(c, (0, 2, 3, 1)) for c in caches])
    att_s = att_s.reshape(sbz, ATT_OUT)
    h0 = jnp.concatenate([_state_to_blocks(state_ssm_re[0]), _state_to_blocks(state_ssm_im[0])], axis=2)
    ssm_s, st_s = ssm_sample(u_s, h0, prm)
    cmk = cache_mem_k[0].reshape(sbz, MEM_TOKENS, MEM_WIDTH)
    cmv = cache_mem_v[0].reshape(sbz, MEM_TOKENS, MEM_WIDTH)
    mem_s = mem_attention(mq_s.reshape(sbz, 1, MEM_WIDTH), cmk, cmv, tl=1, precise=True).reshape(sbz, MEM_WIDTH)
    y_s = _token_stage(att_s, ssm_s, mem_s, gates_s, xs2, wts_f32, w1[0], w3[0], w2[0], norm_f, tm=sbz, bm=16,
                       precise=True)

    outs = [y_p.reshape(bsz, seq, D_MODEL), y_s.reshape(sbz, 1, D_MODEL)]
    for g, (window, _) in enumerate(ATT_GROUPS):
        keep = min(window, seq)
        for part in (1, 2):
            c0 = g * QKV_GROUP + part * ATT_OUT
            outs.append(qkv3[:, seq - keep:, c0:c0 + ATT_OUT].reshape(1, bsz, keep, HEADS_PER_GROUP, HEAD_DIM))
    outs.append(mk3.reshape(1, bsz, MEM_TOKENS, MEM_HEADS, MEM_HEAD_DIM))
    outs.append(mv3.reshape(1, bsz, MEM_TOKENS, MEM_HEADS, MEM_HEAD_DIM))
    outs.append(_blocks_to_state(st[:, :, :SSM_KSTATE])[None])
    outs.append(_blocks_to_state(st[:, :, SSM_KSTATE:])[None])
    outs.extend(jnp.transpose(c, (0, 3, 1, 2))[None] for c in new_caches)
    outs.append(_blocks_to_state(st_s[:, :, :SSM_KSTATE])[None])
    outs.append(_blocks_to_state(st_s[:, :, SSM_KSTATE:])[None])
    return tuple(outs)
```

```python
import functools
import math

import jax
import jax.numpy as jnp
from jax import lax
from jax.experimental import pallas as pl
from jax.experimental.pallas import tpu as pltpu

F32 = jnp.float32
BF16 = jnp.bfloat16

D_MODEL = 1024
ATT_GROUPS = ((128, 1), (512, 4), (2048, 16))
HEADS_PER_GROUP = 4
HEAD_DIM = 64
N_ATT_HEADS = len(ATT_GROUPS) * HEADS_PER_GROUP
ATT_OUT = HEADS_PER_GROUP * HEAD_DIM
QKV_GROUP = 3 * ATT_OUT
QKV_COLS = len(ATT_GROUPS) * QKV_GROUP
SSM_WIDTH = D_MODEL // 2
SSM_GROUP_CH = 16
SSM_GROUPS = SSM_WIDTH // SSM_GROUP_CH
SSM_STATE = 64
SSM_NSTATE = SSM_GROUPS * SSM_STATE
MEM_TOKENS = 256
MEM_HEADS = 4
MEM_HEAD_DIM = D_MODEL // 8
MEM_WIDTH = MEM_HEADS * MEM_HEAD_DIM
N_BRANCHES = 3
MOE_GROUPS = 4
EXPERTS_PER_GROUP = 8
N_EXPERTS = MOE_GROUPS * EXPERTS_PER_GROUP
TOP_K = 2
D_EXPERT = D_MODEL // 2
RMS_EPS = 1e-6

LANES = 128
SUBLANES = 8
Q_TILE = 128
ROUTER_COLS = 128
NEG_BIG = -1e30
VMEM_LIMIT = 56 * 1024 * 1024


def _cparams(sem):
    return pltpu.CompilerParams(dimension_semantics=sem, vmem_limit_bytes=VMEM_LIMIT)


def _mm(a, b, precise, dims=None):
    if precise:
        a, b, kw = a.astype(F32), b.astype(F32), dict(precision=lax.Precision.HIGHEST)
    else:
        a, b, kw = a.astype(BF16), b.astype(BF16), {}
    if dims is None:
        return jnp.dot(a, b, preferred_element_type=F32, **kw)
    return lax.dot_general(a, b, dims, preferred_element_type=F32, **kw)


NT_DIMS = (((1,), (1,)), ((), ()))


def _rms(x, g):
    ms = jnp.mean(x * x, axis=-1, keepdims=True)
    return (x * lax.rsqrt(ms + RMS_EPS)) * g


def _norm_matmul_kernel(x_ref, g_ref, w_ref, *out_refs, splits, sigmoid, chunk, precise):
    xb = _rms(x_ref[...], g_ref[...])
    if not precise:
        xb = xb.astype(BF16)
    c0 = 0
    for o_ref, n, sg in zip(out_refs, splits, sigmoid):
        for j in range(0, n, chunk):
            w = min(chunk, n - j)
            y = _mm(xb, w_ref[:, c0 + j:c0 + j + w], precise)
            if sg:
                y = jax.nn.sigmoid(y)
            o_ref[:, j:j + w] = y.astype(o_ref.dtype)
        c0 += n


def norm_matmul(x, g, w, splits, sigmoid, tm, dtypes=None, precise=False):
    n, d = x.shape
    tm = min(tm, n)
    kern = functools.partial(_norm_matmul_kernel, splits=tuple(splits), sigmoid=tuple(sigmoid), chunk=512,
                             precise=precise)
    return pl.pallas_call(
        kern,
        grid=(n // tm,),
        in_specs=[
            pl.BlockSpec((tm, d), lambda i: (i, 0)),
            pl.BlockSpec((1, d), lambda i: (0, 0)),
            pl.BlockSpec((d, w.shape[1]), lambda i: (0, 0), pipeline_mode=pl.Buffered(1)),
        ],
        out_specs=[pl.BlockSpec((tm, s), lambda i: (i, 0)) for s in splits],
        out_shape=[jax.ShapeDtypeStruct((n, s), dt) for s, dt in zip(splits, dtypes or (F32,) * len(splits))],
        compiler_params=_cparams(("arbitrary",)),
        name="norm_matmul",
    )(x, g.reshape(1, d), w)


def _attn_prompt_kernel(slopes_ref, *refs, seq):
    qkv_refs = refs[:9]
    o_ref = refs[9]
    o_scr, l_scr = refs[10], refs[11]
    hp = pl.program_id(1)
    scale = HEAD_DIM ** -0.5
    row = lax.broadcasted_iota(jnp.int32, (Q_TILE, 2 * Q_TILE), 0)
    col = lax.broadcasted_iota(jnp.int32, (Q_TILE, 2 * Q_TILE), 1)
    dist = (row + Q_TILE - col)
    in_window = jnp.logical_and(dist >= 0, dist <= Q_TILE)
    distf = dist.astype(F32)
    lane = lax.broadcasted_iota(jnp.int32, (Q_TILE, LANES), 1)
    head_a = lane < HEAD_DIM

    for g, (_, dil) in enumerate(ATT_GROUPS):
        q_ref, k_ref, v_ref = qkv_refs[3 * g:3 * g + 3]
        sub_len = seq // dil
        n_blk = sub_len // Q_TILE
        slope_a = slopes_ref[g * HEADS_PER_GROUP + 2 * hp] * float(dil)
        slope_b = slopes_ref[g * HEADS_PER_GROUP + 2 * hp + 1] * float(dil)

        def rows(start):
            if dil == 1:
                return pl.ds(start, Q_TILE)
            return pl.ds(start, Q_TILE, stride=dil)

        def body(it, carry, q_ref=q_ref, k_ref=k_ref, v_ref=v_ref, n_blk=n_blk, dil=dil,
                 slope_a=slope_a, slope_b=slope_b, rows=rows, g=g):
            r = it // n_blk
            blk = it % n_blk
            prev = jnp.maximum(blk - 1, 0)
            q_start = r + dil * Q_TILE * blk
            p_start = r + dil * Q_TILE * prev
            q = q_ref[rows(q_start), :]
            k2 = jnp.concatenate([k_ref[rows(p_start), :], k_ref[rows(q_start), :]], axis=0).astype(BF16)
            v2 = jnp.concatenate([v_ref[rows(p_start), :], v_ref[rows(q_start), :]], axis=0).astype(BF16)
            valid = jnp.logical_and(in_window, jnp.logical_or(col >= Q_TILE, blk > 0))
            outs, lses = [], []
            for is_a, slope in ((True, slope_a), (False, slope_b)):
                hmask = head_a if is_a else jnp.logical_not(head_a)
                qh = jnp.where(hmask, q, 0.0).astype(BF16)
                s = lax.dot_general(qh, k2, NT_DIMS, preferred_element_type=F32)
                s = s * scale - slope * distf
                s = jnp.where(valid, s, NEG_BIG)
                m = jnp.max(s, axis=-1, keepdims=True)
                p = jnp.exp(s - m)
                den = jnp.sum(p, axis=-1, keepdims=True)
                pv = jnp.dot(p.astype(BF16), v2, preferred_element_type=F32)
                outs.append(pv / den)
                lses.append(m + jnp.log(den))
            o_scr[g, rows(q_start), :] = jnp.where(head_a, outs[0], outs[1])
            l_scr[g, rows(q_start), :] = jnp.where(head_a, lses[0], lses[1])
            return carry

        lax.fori_loop(0, dil * n_blk, body, 0, unroll=8)

    def combine(c, carry):
        sl = pl.ds(pl.multiple_of(c * 256, 256), 256)
        l0, l1, l2 = l_scr[0, sl, :], l_scr[1, sl, :], l_scr[2, sl, :]
        m = jnp.maximum(jnp.maximum(l0, l1), l2)
        e0, e1, e2 = jnp.exp(l0 - m), jnp.exp(l1 - m), jnp.exp(l2 - m)
        tot = e0 + e1 + e2
        att = (e0 * o_scr[0, sl, :] + e1 * o_scr[1, sl, :] + e2 * o_scr[2, sl, :]) / tot
        o_ref[sl, :] = att.astype(o_ref.dtype)
        return carry

    lax.fori_loop(0, seq // 256, combine, 0)


def attn_prompt(qkv, slopes):
    bsz, seq, _ = qkv.shape
    in_specs = []
    for g in range(len(ATT_GROUPS)):
        for part in range(3):
            base = (g * QKV_GROUP + part * ATT_OUT) // LANES
            in_specs.append(pl.BlockSpec((None, seq, LANES),
                                         lambda b, h, sl, base=base: (b, 0, base + h)))
    grid_spec = pltpu.PrefetchScalarGridSpec(
        num_scalar_prefetch=1,
        grid=(bsz, ATT_OUT // LANES),
        in_specs=in_specs,
        out_specs=pl.BlockSpec((None, seq, LANES), lambda b, h, sl: (b, 0, h)),
        scratch_shapes=[pltpu.VMEM((3, seq, LANES), F32), pltpu.VMEM((3, seq, LANES), F32)],
    )
    return pl.pallas_call(
        functools.partial(_attn_prompt_kernel, seq=seq),
        grid_spec=grid_spec,
        out_shape=jax.ShapeDtypeStruct((bsz, seq, ATT_OUT), BF16),
        compiler_params=_cparams(("arbitrary", "arbitrary")),
        name="attn_prompt",
    )(slopes, *([qkv] * 9))


def _attn_sample_kernel(qkv_ref, col_ref, *refs):
    cache_refs = refs[:6]
    o_ref = refs[6]
    new_refs = refs[7:13]
    scale = HEAD_DIM ** -0.5
    n_grp = len(ATT_GROUPS)
    outs = [[None] * HEADS_PER_GROUP for _ in range(n_grp)]
    lses = [[None] * HEADS_PER_GROUP for _ in range(n_grp)]
    for g, (window, dil) in enumerate(ATT_GROUPS):
        lane = lax.broadcasted_iota(jnp.int32, (1, window), 1)
        on_grid = jnp.bitwise_and(lane, dil - 1) == 0
        dist = (window - lane).astype(F32)
        last = lax.broadcasted_iota(jnp.int32, (HEAD_DIM, window), 1) == window - 1
        for h in range(HEADS_PER_GROUP):
            slope = 2.0 ** (-8.0 * (g * HEADS_PER_GROUP + h + 1) / N_ATT_HEADS)
            q = qkv_ref[g, 0][h:h + 1, :]
            kn = qkv_ref[g, 1][h:h + 1, :]
            vn = qkv_ref[g, 2][h:h + 1, :]
            kt = cache_refs[2 * g][h]
            vt = cache_refs[2 * g + 1][h]
            s = _mm(q, kt, True) * scale - slope * dist
            s = jnp.where(on_grid, s, NEG_BIG)
            sn = jnp.sum(q * kn, axis=-1, keepdims=True) * scale
            m = jnp.maximum(jnp.max(s, axis=-1, keepdims=True), sn)
            p = jnp.exp(s - m)
            pn = jnp.exp(sn - m)
            den = jnp.sum(p, axis=-1, keepdims=True) + pn
            outs[g][h] = (_mm(p, vt, True, NT_DIMS) + pn * vn) / den
            lses[g][h] = m + jnp.log(den)
            new_refs[2 * g][h] = jnp.where(last, col_ref[g, 0, h], pltpu.roll(kt, window - 1, axis=1))
            new_refs[2 * g + 1][h] = jnp.where(last, col_ref[g, 1, h], pltpu.roll(vt, window - 1, axis=1))
    for h in range(HEADS_PER_GROUP):
        m = jnp.maximum(jnp.maximum(lses[0][h], lses[1][h]), lses[2][h])
        es = [jnp.exp(lses[g][h] - m) for g in range(n_grp)]
        num = es[0] * outs[0][h] + es[1] * outs[1][h] + es[2] * outs[2][h]
        o_ref[h:h + 1, :] = num / (es[0] + es[1] + es[2])


def attn_sample(qkv5, caches_t):
    bsz = qkv5.shape[0]
    cols = qkv5[:, :, 1:3].reshape(bsz, len(ATT_GROUPS), 2, HEADS_PER_GROUP, HEAD_DIM, 1)
    in_specs = [pl.BlockSpec((None, 3, 3, HEADS_PER_GROUP, HEAD_DIM), lambda b: (b, 0, 0, 0, 0)),
                pl.BlockSpec((None, 3, 2, HEADS_PER_GROUP, HEAD_DIM, 1), lambda b: (b, 0, 0, 0, 0, 0))]
    cache_specs = [pl.BlockSpec((None,) + c.shape[1:], lambda b: (b, 0, 0, 0)) for c in caches_t]
    return pl.pallas_call(
        _attn_sample_kernel,
        grid=(bsz,),
        in_specs=in_specs + cache_specs,
        out_specs=[pl.BlockSpec((None, HEADS_PER_GROUP, HEAD_DIM), lambda b: (b, 0, 0))] + cache_specs,
        out_shape=[jax.ShapeDtypeStruct((bsz, HEADS_PER_GROUP, HEAD_DIM), F32)]
                  + [jax.ShapeDtypeStruct(c.shape, F32) for c in caches_t],
        compiler_params=_cparams(("arbitrary",)),
        name="attn_sample",
    )(qkv5, cols, *caches_t)


SSM_KBLK = SSM_WIDTH // LANES
SSM_KSTATE = SSM_NSTATE // SSM_KBLK


def _ssm_tail(y, u, dsk, wglu, bglu, precise=False):
    y = y + dsk * u
    z = jax.nn.gelu(y)
    gl = _mm(z, wglu, precise) + bglu
    return z * jax.nn.sigmoid(gl)


def _ssm_prompt_kernel(u_ref, wb_ref, wc_ref, are_ref, aim_ref, dsk_ref, wglu_ref, bglu_ref,
                       o_ref, st_ref, uperm, bu, sbf, yperm, ynat, *, bsz, tt):
    rows = bsz * tt

    @pl.when(pl.program_id(0) == 0)
    def _():
        st_ref[...] = jnp.zeros_like(st_ref)

    for b in range(bsz):
        ub = u_ref[b]
        for k in range(SSM_KBLK):
            uperm[k, pl.ds(b, tt, stride=bsz), :] = ub[:, k * LANES:(k + 1) * LANES]

    for k in range(SSM_KBLK):
        bu[k] = jnp.dot(uperm[k].astype(BF16), wb_ref[k], preferred_element_type=F32)

    for k in range(SSM_KBLK):
        ar = jnp.broadcast_to(are_ref[k], (bsz, SSM_KSTATE))
        ai = jnp.broadcast_to(aim_ref[k], (bsz, SSM_KSTATE))
        st = st_ref[k]

        def step(t, carry, k=k, ar=ar, ai=ai):
            re, im = carry
            r0 = pl.multiple_of(t * bsz, bsz)
            b_t = bu[k, pl.ds(r0, bsz), :]
            nre = ar * re - ai * im + b_t[:, :SSM_KSTATE]
            nim = ar * im + ai * re + b_t[:, SSM_KSTATE:]
            sbf[k, pl.ds(r0, bsz), :] = jnp.concatenate([nre, nim], axis=1).astype(BF16)
            return nre, nim

        re, im = lax.fori_loop(0, tt, step, (st[:, :SSM_KSTATE], st[:, SSM_KSTATE:]), unroll=2)
        st_ref[k] = jnp.concatenate([re, im], axis=1)

    for k in range(SSM_KBLK):
        yperm[k] = jnp.dot(sbf[k], wc_ref[k], preferred_element_type=F32)

    for b in range(bsz):
        for k in range(SSM_KBLK):
            ynat[b * tt:(b + 1) * tt, k * LANES:(k + 1) * LANES] = yperm[k, pl.ds(b, tt, stride=bsz), :]

    u2 = u_ref[...].reshape(rows, SSM_WIDTH)
    out = _ssm_tail(ynat[...], u2, dsk_ref[...], wglu_ref[...], bglu_ref[...])
    o_ref[...] = out.reshape(bsz, tt, SSM_WIDTH).astype(o_ref.dtype)


def ssm_prompt(u, prm, tt=32):
    bsz, seq, _ = u.shape
    rows = bsz * tt
    const = lambda shape: pl.BlockSpec(shape, lambda i: (0,) * len(shape))
    return pl.pallas_call(
        functools.partial(_ssm_prompt_kernel, bsz=bsz, tt=tt),
        grid=(seq // tt,),
        in_specs=[
            pl.BlockSpec((bsz, tt, SSM_WIDTH), lambda i: (0, i, 0)),
            const((SSM_KBLK, LANES, 2 * SSM_KSTATE)),
            const((SSM_KBLK, 2 * SSM_KSTATE, LANES)),
            const((SSM_KBLK, 1, SSM_KSTATE)),
            const((SSM_KBLK, 1, SSM_KSTATE)),
            const((1, SSM_WIDTH)),
            const((SSM_WIDTH, SSM_WIDTH)),
            const((1, SSM_WIDTH)),
        ],
        out_specs=[
            pl.BlockSpec((bsz, tt, SSM_WIDTH), lambda i: (0, i, 0)),
            const((SSM_KBLK, bsz, 2 * SSM_KSTATE)),
        ],
        out_shape=[
            jax.ShapeDtypeStruct((bsz, seq, SSM_WIDTH), BF16),
            jax.ShapeDtypeStruct((SSM_KBLK, bsz, 2 * SSM_KSTATE), F32),
        ],
        scratch_shapes=[
            pltpu.VMEM((SSM_KBLK, rows, LANES), F32),
            pltpu.VMEM((SSM_KBLK, rows, 2 * SSM_KSTATE), F32),
            pltpu.VMEM((SSM_KBLK, rows, 2 * SSM_KSTATE), BF16),
            pltpu.VMEM((SSM_KBLK, rows, LANES), F32),
            pltpu.VMEM((rows, SSM_WIDTH), F32),
        ],
        compiler_params=_cparams(("arbitrary",)),
        name="ssm_prompt",
    )(u, prm["wb"], prm["wc"], prm["a_re"], prm["a_im"], prm["dsk"], prm["wglu"], prm["bglu"])


def _ssm_sample_kernel(u_ref, h0_ref, wb_ref, wc_ref, are_ref, aim_ref, dsk_ref, wglu_ref, bglu_ref,
                       o_ref, st_ref):
    u = u_ref[...]
    ys = []
    for k in range(SSM_KBLK):
        b_t = _mm(u[:, k * LANES:(k + 1) * LANES], wb_ref[k], True)
        h0 = h0_ref[k]
        re, im = h0[:, :SSM_KSTATE], h0[:, SSM_KSTATE:]
        ar, ai = are_ref[k], aim_ref[k]
        nre = ar * re - ai * im + b_t[:, :SSM_KSTATE]
        nim = ar * im + ai * re + b_t[:, SSM_KSTATE:]
        s = jnp.concatenate([nre, nim], axis=1)
        st_ref[k] = s
        ys.append(_mm(s, wc_ref[k], True))
    y = jnp.concatenate(ys, axis=1)
    o_ref[...] = _ssm_tail(y, u, dsk_ref[...], wglu_ref[...], bglu_ref[...], precise=True)


def ssm_sample(u, h0, prm):
    bsz = u.shape[0]
    return pl.pallas_call(
        _ssm_sample_kernel,
        out_shape=[jax.ShapeDtypeStruct((bsz, SSM_WIDTH), F32),
                   jax.ShapeDtypeStruct((SSM_KBLK, bsz, 2 * SSM_KSTATE), F32)],
        compiler_params=pltpu.CompilerParams(vmem_limit_bytes=VMEM_LIMIT),
        name="ssm_sample",
    )(u, h0, prm["wb_f32"], prm["wc_f32"], prm["a_re"], prm["a_im"], prm["dsk"], prm["wglu_f32"], prm["bglu"])


def ssm_params(lam_re, lam_im, log_dt, b_re, b_im, c_re, c_im, d_skip, w_glu, b_glu):
    dt = jnp.exp(log_dt)[:, None]
    mag = jnp.exp(lam_re * dt)
    ab_re = mag * jnp.cos(lam_im * dt)
    ab_im = mag * jnp.sin(lam_im * dt)
    den = lam_re * lam_re + lam_im * lam_im
    nr = ab_re - 1.0
    ni = ab_im
    z_re = ((nr * lam_re + ni * lam_im) / den)[..., None]
    z_im = ((ni * lam_re - nr * lam_im) / den)[..., None]
    bb_re = z_re * b_re - z_im * b_im
    bb_im = z_re * b_im + z_im * b_re
    gpb = SSM_GROUPS // SSM_KBLK
    eye = jnp.eye(gpb, dtype=F32)

    def in_mat(bb):
        bbk = bb.reshape(SSM_KBLK, gpb, SSM_STATE, SSM_GROUP_CH)
        m = jnp.einsum("kgpc,gh->kgchp", bbk, eye)
        return m.reshape(SSM_KBLK, LANES, SSM_KSTATE)

    def out_mat(c):
        ck = c.reshape(SSM_KBLK, gpb, SSM_GROUP_CH, SSM_STATE)
        m = jnp.einsum("kgcp,gh->kgphc", ck, eye)
        return m.reshape(SSM_KBLK, SSM_KSTATE, LANES)

    wb = jnp.concatenate([in_mat(bb_re), in_mat(bb_im)], axis=2)
    wc = jnp.concatenate([out_mat(c_re), -out_mat(c_im)], axis=1)
    return dict(
        wb=wb.astype(BF16), wc=wc.astype(BF16), wb_f32=wb, wc_f32=wc, wglu_f32=w_glu,
        a_re=ab_re.reshape(SSM_KBLK, 1, SSM_KSTATE), a_im=ab_im.reshape(SSM_KBLK, 1, SSM_KSTATE),
        dsk=d_skip.reshape(1, SSM_WIDTH), wglu=w_glu.astype(BF16), bglu=b_glu.reshape(1, SSM_WIDTH))


def _state_to_blocks(h):
    bsz = h.shape[0]
    return h.reshape(bsz, SSM_KBLK, SSM_KSTATE).transpose(1, 0, 2)


def _blocks_to_state(s):
    bsz = s.shape[1]
    return s.transpose(1, 0, 2).reshape(bsz, SSM_GROUPS, SSM_STATE)


def _mem_attn_kernel(q_ref, k_ref, v_ref, o_ref, *, precise):
    scale = MEM_HEAD_DIM ** -0.5
    for h in range(MEM_HEADS):
        sl = slice(h * MEM_HEAD_DIM, (h + 1) * MEM_HEAD_DIM)
        s = _mm(q_ref[:, sl], k_ref[:, sl], precise, NT_DIMS) * scale
        m = jnp.max(s, axis=-1, keepdims=True)
        p = jnp.exp(s - m)
        den = jnp.sum(p, axis=-1, keepdims=True)
        o_ref[:, sl] = _mm(p / den, v_ref[:, sl], precise).astype(o_ref.dtype)


def mem_attention(q, mk, mv, tl, out_dtype=F32, precise=False):
    bsz, seq, _ = q.shape
    tl = min(tl, seq)
    return pl.pallas_call(
        functools.partial(_mem_attn_kernel, precise=precise),
        grid=(bsz, seq // tl),
        in_specs=[
            pl.BlockSpec((None, tl, MEM_WIDTH), lambda b, i: (b, i, 0)),
            pl.BlockSpec((None, MEM_TOKENS, MEM_WIDTH), lambda b, i: (b, 0, 0)),
            pl.BlockSpec((None, MEM_TOKENS, MEM_WIDTH), lambda b, i: (b, 0, 0)),
        ],
        out_specs=pl.BlockSpec((None, tl, MEM_WIDTH), lambda b, i: (b, i, 0)),
        out_shape=jax.ShapeDtypeStruct((bsz, seq, MEM_WIDTH), out_dtype),
        compiler_params=_cparams(("arbitrary", "arbitrary")),
        name="mem_attention",
    )(q, mk, mv)


def _route(logits):
    lane = lax.broadcasted_iota(jnp.int32, logits.shape, 1)
    lanef = lane.astype(F32)
    none = float(ROUTER_COLS)
    neg = -jnp.inf
    gl = jnp.where(lane < MOE_GROUPS, logits, neg)
    gmax = jnp.max(gl, axis=-1, keepdims=True)
    grp = jnp.min(jnp.where(gl == gmax, lanef, none), axis=-1, keepdims=True)
    p_grp = 1.0 / jnp.sum(jnp.exp(gl - gmax), axis=-1, keepdims=True)
    lo = MOE_GROUPS + grp * EXPERTS_PER_GROUP
    el = jnp.where(jnp.logical_and(lanef >= lo, lanef < lo + EXPERTS_PER_GROUP), logits, neg)
    v1 = jnp.max(el, axis=-1, keepdims=True)
    i1 = jnp.min(jnp.where(el == v1, lanef, none), axis=-1, keepdims=True)
    el2 = jnp.where(lanef == i1, neg, el)
    v2 = jnp.max(el2, axis=-1, keepdims=True)
    i2 = jnp.min(jnp.where(el2 == v2, lanef, none), axis=-1, keepdims=True)
    t = jnp.exp(v2 - v1)
    g1 = p_grp / (1.0 + t)
    g2 = g1 * t
    out = jnp.where(lane == 0, g1, 0.0)
    out = jnp.where(lane == 1, g2, out)
    out = jnp.where(lane == 2, i1 - MOE_GROUPS, out)
    return jnp.where(lane == 3, i2 - MOE_GROUPS, out)


def _merge_kernel(att_ref, ssm_ref, mem_ref, gate_ref, x_ref, wa_ref, ws_ref, wm_ref, wo_ref,
                  n2_ref, wr_ref, br_ref, h_ref, hn_ref, lg_ref, *, tm, precise):
    a = _mm(att_ref[...], wa_ref[...], precise)
    merged = gate_ref[:, 0:D_MODEL] * a
    s = _mm(ssm_ref[...], ws_ref[...], precise)
    merged = merged + gate_ref[:, D_MODEL:2 * D_MODEL] * s
    m = _mm(mem_ref[...], wm_ref[...], precise)
    merged = merged + gate_ref[:, 2 * D_MODEL:3 * D_MODEL] * m
    h = x_ref[...] + _mm(merged, wo_ref[...], precise)
    h_ref[...] = h
    hn = _rms(h, n2_ref[...])
    for j in range(D_MODEL // LANES):
        hn_ref[pl.ds(j, tm, stride=SUBLANES), :] = hn[:, j * LANES:(j + 1) * LANES]
    logits = _mm(hn, wr_ref[...], precise) + br_ref[...]
    lg_ref[...] = _route(logits)


def merge(att, ssm, mem, gates, x, wts, tm, precise=False):
    n = x.shape[0]
    tm = min(tm, n)
    row = lambda w: pl.BlockSpec((tm, w), lambda i: (i, 0))
    const = lambda a: pl.BlockSpec(a.shape, lambda i: (0, 0))
    ws = [wts["wa"], wts["ws"], wts["wm"], wts["wo"], wts["n2"], wts["wr"], wts["br"]]
    return pl.pallas_call(
        functools.partial(_merge_kernel, tm=tm, precise=precise),
        grid=(n // tm,),
        in_specs=[row(ATT_OUT), row(SSM_WIDTH), row(MEM_WIDTH), row(N_BRANCHES * D_MODEL), row(D_MODEL)]
                 + [const(w) for w in ws],
        out_specs=[row(D_MODEL), pl.BlockSpec((tm * SUBLANES, LANES), lambda i: (i, 0)), row(ROUTER_COLS)],
        out_shape=[jax.ShapeDtypeStruct((n, D_MODEL), F32),
                   jax.ShapeDtypeStruct((n * SUBLANES, LANES), F32),
                   jax.ShapeDtypeStruct((n, ROUTER_COLS), F32)],
        compiler_params=_cparams(("arbitrary",)),
        name="merge",
    )(att, ssm, mem, gates, x, *ws)


TABLE_WORDS = SUBLANES * LANES


def _row_dma_loop(n, off, smem, slot, start):
    inner = min(n, LANES)
    assert n % inner == 0

    def body(r, carry):
        for j in range(inner):
            f = off + j
            start(smem[slot, f // LANES + r, f % LANES], r * inner + j, j % 2)
        return carry

    if n == inner:
        body(0, 0)
    else:
        lax.fori_loop(0, n // inner, body, 0)


def _tile_rows(i):
    start = i * SUBLANES
    return pl.ds(start if isinstance(start, int) else pl.multiple_of(start, SUBLANES), SUBLANES)


def _dispatch_kernel(pos_hbm, hn_ref, xs_hbm, pos_smem, zero, sem, psem, *, tm, n_pad):
    i = pl.program_id(0)
    slot = i % 2

    def pos_copy(b, s):
        return pltpu.make_async_copy(pos_hbm.at[b], pos_smem.at[s], psem.at[s])

    @pl.when(i == 0)
    def _():
        pos_copy(0, 0).start()

    @pl.when(i + 1 < pl.num_programs(0))
    def _():
        pos_copy(i + 1, 1 - slot).start()

    zero[...] = jnp.zeros_like(zero)
    pos_copy(i, slot).wait()

    def send_token(row, tok, prio):
        pltpu.make_async_copy(hn_ref.at[_tile_rows(tok)], xs_hbm.at[row], sem).start(priority=prio)

    def send_zero(row, _, prio):
        pltpu.make_async_copy(zero, xs_hbm.at[row], sem).start(priority=prio)

    for k in range(TOP_K):
        _row_dma_loop(tm, k * tm, pos_smem, slot, send_token)
    _row_dma_loop(n_pad, TOP_K * tm, pos_smem, slot, send_zero)
    whole, part = divmod(n_pad, tm)
    for _ in range(TOP_K + whole):
        pltpu.make_async_copy(hn_ref, hn_ref, sem).wait()
    if part:
        piece = hn_ref.at[pl.ds(0, part * SUBLANES)]
        pltpu.make_async_copy(piece, piece, sem).wait()


def moe_dispatch(pos, hn2d, n_rows, tm, n_pad):
    n_tiles = pos.shape[0]
    return pl.pallas_call(
        functools.partial(_dispatch_kernel, tm=tm, n_pad=n_pad),
        grid=(n_tiles,),
        in_specs=[pl.BlockSpec(memory_space=pl.ANY),
                  pl.BlockSpec((tm * SUBLANES, LANES), lambda i: (i, 0))],
        out_specs=pl.BlockSpec(memory_space=pl.ANY),
        out_shape=jax.ShapeDtypeStruct((n_rows, SUBLANES, LANES), F32),
        scratch_shapes=[pltpu.SMEM((2, SUBLANES, LANES), jnp.int32),
                        pltpu.VMEM((SUBLANES, LANES), F32),
                        pltpu.SemaphoreType.DMA, pltpu.SemaphoreType.DMA((2,))],
        compiler_params=_cparams(("arbitrary",)),
        name="moe_dispatch",
    )(pos, hn2d)


def _moe_kernel(be_ref, nu_ref, dst_hbm, xs_ref, w1_ref, w3_ref, w2_ref, out_hbm,
                dst_smem, ybuf, w1b, w3b, w2b, ssem, isem, *, bm):
    blk = pl.program_id(0)
    nblk = pl.num_programs(0)
    slot = blk % 2
    tiles = D_MODEL // LANES

    def dst_copy(b, s):
        return pltpu.make_async_copy(dst_hbm.at[b], dst_smem.at[s], isem.at[s])

    def wait_scatter():
        pltpu.make_async_copy(ybuf, ybuf, ssem).wait()

    @pl.when(blk == 0)
    def _():
        dst_copy(0, 0).start()

    @pl.when(blk + 1 < nblk)
    def _():
        dst_copy(blk + 1, 1 - slot).start()

    prev = jnp.maximum(blk - 1, 0)
    changed = jnp.logical_or(blk == 0, be_ref[blk] != be_ref[prev])
    active = jnp.logical_or(blk == 0, blk < nu_ref[0])

    @pl.when(jnp.logical_and(active, changed))
    def _():
        w1b[...] = w1_ref[...].astype(BF16)
        w3b[...] = w3_ref[...].astype(BF16)
        w2b[...] = w2_ref[...].astype(BF16)

    @pl.when(active)
    def _():
        x = jnp.concatenate([xs_ref[pl.ds(j, bm, stride=SUBLANES), :].astype(BF16) for j in range(tiles)], axis=1)
        h1 = jnp.dot(x, w1b[...], preferred_element_type=F32)
        h3 = jnp.dot(x, w3b[...], preferred_element_type=F32)
        act = (jax.nn.silu(h1) * h3).astype(BF16)
        y = jnp.dot(act, w2b[...], preferred_element_type=F32)

        @pl.when(blk > 0)
        def _():
            wait_scatter()

        for j in range(tiles):
            ybuf[pl.ds(j, bm, stride=SUBLANES), :] = y[:, j * LANES:(j + 1) * LANES]

    @pl.when(jnp.logical_not(active))
    def _():
        wait_scatter()

    dst_copy(blk, slot).wait()

    def send_row(row, i, prio):
        pltpu.make_async_copy(ybuf.at[_tile_rows(i)], out_hbm.at[row], ssem).start(priority=prio)

    _row_dma_loop(bm, 0, dst_smem, slot, send_row)

    @pl.when(blk == nblk - 1)
    def _():
        wait_scatter()


def moe_experts(block_expert, n_used, dst, xs2d, w1, w3, w2, n_out_rows, bm):
    n_blocks = dst.shape[0]
    grid_spec = pltpu.PrefetchScalarGridSpec(
        num_scalar_prefetch=2,
        grid=(n_blocks,),
        in_specs=[
            pl.BlockSpec(memory_space=pl.ANY),
            pl.BlockSpec((bm * SUBLANES, LANES), lambda b, be, nu: (b, 0)),
            pl.BlockSpec((None, D_MODEL, D_EXPERT), lambda b, be, nu: (be[b], 0, 0)),
            pl.BlockSpec((None, D_MODEL, D_EXPERT), lambda b, be, nu: (be[b], 0, 0)),
            pl.BlockSpec((None, D_EXPERT, D_MODEL), lambda b, be, nu: (be[b], 0, 0)),
        ],
        out_specs=pl.BlockSpec(memory_space=pl.ANY),
        scratch_shapes=[
            pltpu.SMEM((2, SUBLANES, LANES), jnp.int32),
            pltpu.VMEM((bm * SUBLANES, LANES), F32),
            pltpu.VMEM((D_MODEL, D_EXPERT), BF16),
            pltpu.VMEM((D_MODEL, D_EXPERT), BF16),
            pltpu.VMEM((D_EXPERT, D_MODEL), BF16),
            pltpu.SemaphoreType.DMA,
            pltpu.SemaphoreType.DMA((2,)),
        ],
    )
    return pl.pallas_call(
        functools.partial(_moe_kernel, bm=bm),
        grid_spec=grid_spec,
        out_shape=jax.ShapeDtypeStruct((n_out_rows, SUBLANES, LANES), F32),
        compiler_params=_cparams(("arbitrary",)),
        name="moe_experts",
    )(block_expert, n_used, dst, xs2d, w1, w3, w2)


def _as_tables(cols, width):
    fill = jnp.zeros((cols.shape[0], TABLE_WORDS - width), jnp.int32)
    return jnp.concatenate([cols, fill], axis=1).reshape(cols.shape[0], SUBLANES, LANES)


def route_tables(route, bm, tm):
    n_tok = route.shape[0]
    n_pairs = n_tok * TOP_K
    e_flat = route[:, 2:2 + TOP_K].astype(jnp.int32).reshape(n_pairs)
    experts = jnp.arange(N_EXPERTS, dtype=jnp.int32)
    counts = jnp.sum(e_flat[:, None] == experts[None, :], axis=0, dtype=jnp.int32)
    padded = (counts + bm - 1) // bm * bm
    pad_ends = jnp.cumsum(padded)
    n_blocks = -(-n_pairs // bm) + N_EXPERTS
    n_rows = n_blocks * bm
    row0 = jnp.arange(n_blocks, dtype=jnp.int32) * bm
    block_expert = jnp.minimum(jnp.sum(pad_ends[None, :] <= row0[:, None], axis=1, dtype=jnp.int32),
                               N_EXPERTS - 1)
    n_fill = n_rows - n_pairs
    fill_id = jnp.arange(n_fill, dtype=jnp.int32)
    fill_e = jnp.minimum(fill_id // bm, N_EXPERTS - 1)
    fill_used = jnp.logical_and(fill_id < N_EXPERTS * bm, fill_id % bm < (padded - counts)[fill_e])
    keys = jnp.concatenate([e_flat * 2, jnp.where(fill_used, fill_e * 2 + 1, 2 * N_EXPERTS)])
    vals = jnp.concatenate([jnp.arange(n_pairs, dtype=jnp.int32), jnp.full((n_fill,), -1, jnp.int32)])
    _, pair_at = lax.sort((keys, vals), num_keys=1, is_stable=True)
    _, row_of = lax.sort((pair_at, jnp.arange(n_rows, dtype=jnp.int32)), num_keys=1)
    n_tiles = n_tok // tm
    n_pad = n_fill // n_tiles
    assert n_pad * n_tiles == n_fill and n_pad % SUBLANES == 0
    pair_row = row_of[n_fill:].reshape(n_tok, TOP_K)
    pos = jnp.concatenate([pair_row[:, k].reshape(n_tiles, tm) for k in range(TOP_K)]
                          + [row_of[:n_fill].reshape(n_tiles, n_pad)], axis=1)
    pair_at = pair_at.reshape(n_blocks, bm)
    valid = pair_at >= 0
    trash = n_pairs + jnp.arange(bm, dtype=jnp.int32)[None, :]
    dst = jnp.where(valid, (pair_at % TOP_K) * n_tok + pair_at // TOP_K, trash)
    n_used = (pad_ends[-1:] // bm).astype(jnp.int32)
    return (block_expert, n_used, _as_tables(dst, bm), _as_tables(pos, TOP_K * tm + n_pad), n_pad, n_rows,
            n_pairs + bm)


def _final_kernel(h_ref, p0_ref, p1_ref, gw_ref, nf_ref, y_ref, *, tm):
    tiles = D_MODEL // LANES
    p0 = jnp.concatenate([p0_ref[pl.ds(j, tm, stride=SUBLANES), :] for j in range(tiles)], axis=1)
    p1 = jnp.concatenate([p1_ref[pl.ds(j, tm, stride=SUBLANES), :] for j in range(tiles)], axis=1)
    gw = gw_ref[...]
    h = h_ref[...] + (gw[:, 0:1] * p0 + gw[:, 1:2] * p1)
    y_ref[...] = _rms(h, nf_ref[...])


def final(h, pairs2d, gate_pad, norm_f, tm):
    n = h.shape[0]
    tm = min(tm, n)
    nb = n // tm
    row = lambda w: pl.BlockSpec((tm, w), lambda i: (i, 0))
    return pl.pallas_call(
        functools.partial(_final_kernel, tm=tm),
        grid=(nb,),
        in_specs=[
            row(D_MODEL),
            pl.BlockSpec((tm * SUBLANES, LANES), lambda i: (i, 0)),
            pl.BlockSpec((tm * SUBLANES, LANES), lambda i: (nb + i, 0)),
            row(LANES),
            pl.BlockSpec((1, D_MODEL), lambda i: (0, 0)),
        ],
        out_specs=row(D_MODEL),
        out_shape=jax.ShapeDtypeStruct((n, D_MODEL), F32),
        compiler_params=_cparams(("arbitrary",)),
        name="final",
    )(h, pairs2d, pairs2d, gate_pad, norm_f.reshape(1, D_MODEL))


def _alibi_slopes():
    h = jnp.arange(1, N_ATT_HEADS + 1, dtype=F32)
    return jnp.exp2(-8.0 * h / N_ATT_HEADS)


def _token_stage(att, ssm, mem, gates, x2, wts, w1, w3, w2, norm_f, tm, bm, precise=False):
    n = x2.shape[0]
    h, hn2d, route = merge(att, ssm, mem, gates, x2, wts, tm, precise)
    tm = min(tm, n)
    block_expert, n_used, dst, pos, n_pad, n_rows, out_rows = route_tables(route, bm, tm)
    xs = moe_dispatch(pos, hn2d, n_rows, tm, n_pad)
    pairs = moe_experts(block_expert, n_used, dst, xs.reshape(n_rows * SUBLANES, LANES), w1, w3, w2, out_rows, bm)
    return final(h, pairs.reshape(out_rows * SUBLANES, LANES), route, norm_f, tm)


def kernel(x_prompt, x_sample, cache_swa0_k, cache_swa0_v, cache_swa1_k, cache_swa1_v, cache_swa2_k, cache_swa2_v, cache_mem_k, cache_mem_v, state_ssm_re, state_ssm_im, mem_prompt, norm1, w_in, lam_re, lam_im, log_dt, ssm_b_re, ssm_b_im, ssm_c_re, ssm_c_im, ssm_d, w_glu, b_glu, w_att_out, w_ssm_out, w_mem_out, w_o, norm_mem, w_mk, w_mv, norm2, w_grp, b_grp, w_exp, b_exp, w1, w3, w2, norm_f):
    assert norm1.shape[0] == 1, "single-layer trunk"
    bsz, seq, _ = x_prompt.shape
    sbz = x_sample.shape[0]
    n_p = bsz * seq
    caches = (cache_swa0_k[0], cache_swa0_v[0], cache_swa1_k[0], cache_swa1_v[0], cache_swa2_k[0], cache_swa2_v[0])

    w_in_b = w_in[0].astype(BF16)
    w_mkv_b = jnp.concatenate([w_mk[0], w_mv[0]], axis=1).astype(BF16)
    wr = jnp.zeros((D_MODEL, ROUTER_COLS), F32)
    wr = wr.at[:, :MOE_GROUPS].set(w_grp[0]).at[:, MOE_GROUPS:MOE_GROUPS + N_EXPERTS].set(w_exp[0])
    br = jnp.zeros((1, ROUTER_COLS), F32)
    br = br.at[0, :MOE_GROUPS].set(b_grp[0]).at[0, MOE_GROUPS:MOE_GROUPS + N_EXPERTS].set(b_exp[0])
    wts_f32 = dict(wa=w_att_out[0], ws=w_ssm_out[0], wm=w_mem_out[0], wo=w_o[0],
                   n2=norm2[0].reshape(1, D_MODEL), wr=wr, br=br)
    wts = {k: (v.astype(BF16) if k in ("wa", "ws", "wm", "wo", "wr") else v) for k, v in wts_f32.items()}
    prm = ssm_params(lam_re[0], lam_im[0], log_dt[0], ssm_b_re[0], ssm_b_im[0], ssm_c_re[0], ssm_c_im[0],
                     ssm_d[0], w_glu[0], b_glu[0])
    slopes = _alibi_slopes()
    splits = (QKV_COLS, SSM_WIDTH, MEM_WIDTH, N_BRANCHES * D_MODEL)
    sig = (False, False, False, True)

    x2 = x_prompt.reshape(n_p, D_MODEL)
    qkv, u, mq, gates = norm_matmul(x2, norm1[0], w_in_b, splits, sig, tm=256, dtypes=(F32, F32, BF16, BF16))
    qkv3 = qkv.reshape(bsz, seq, QKV_COLS)
    att = attn_prompt(qkv3, slopes)
    ssm, st = ssm_prompt(u.reshape(bsz, seq, SSM_WIDTH), prm)
    mk, mv = norm_matmul(mem_prompt.reshape(bsz * MEM_TOKENS, D_MODEL), norm_mem[0], w_mkv_b,
                         (MEM_WIDTH, MEM_WIDTH), (False, False), tm=256)
    mk3 = mk.reshape(bsz, MEM_TOKENS, MEM_WIDTH)
    mv3 = mv.reshape(bsz, MEM_TOKENS, MEM_WIDTH)
    mem = mem_attention(mq.reshape(bsz, seq, MEM_WIDTH), mk3, mv3, tl=512, out_dtype=BF16)
    y_p = _token_stage(att.reshape(n_p, ATT_OUT), ssm.reshape(n_p, SSM_WIDTH), mem.reshape(n_p, MEM_WIDTH),
                       gates, x2, wts, w1[0], w3[0], w2[0], norm_f, tm=256, bm=256)

    xs2 = x_sample.reshape(sbz, D_MODEL)
    qkv_s, u_s, mq_s, gates_s = norm_matmul(xs2, norm1[0], w_in[0], splits, sig, tm=sbz, precise=True)
    qkv5 = qkv_s.reshape(sbz, len(ATT_GROUPS), 3, HEADS_PER_GROUP, HEAD_DIM)
    att_s, *new_caches = attn_sample(qkv5, [jnp.transpose(c, (0, 2, 3, 1)) for c in caches])
    att_s = att_s.reshape(sbz, ATT_OUT)
    h0 = jnp.concatenate([_state_to_blocks(state_ssm_re[0]), _state_to_blocks(state_ssm_im[0])], axis=2)
    ssm_s, st_s = ssm_sample(u_s, h0, prm)
    cmk = cache_mem_k[0].reshape(sbz, MEM_TOKENS, MEM_WIDTH)
    cmv = cache_mem_v[0].reshape(sbz, MEM_TOKENS, MEM_WIDTH)
    mem_s = mem_attention(mq_s.reshape(sbz, 1, MEM_WIDTH), cmk, cmv, tl=1, precise=True).reshape(sbz, MEM_WIDTH)
    y_s = _token_stage(att_s, ssm_s, mem_s, gates_s, xs2, wts_f32, w1[0], w3[0], w2[0], norm_f, tm=sbz, bm=16,
                       precise=True)

    outs = [y_p.reshape(bsz, seq, D_MODEL), y_s.reshape(sbz, 1, D_MODEL)]
    for g, (window, _) in enumerate(ATT_GROUPS):
        keep = min(window, seq)
        for part in (1, 2):
            c0 = g * QKV_GROUP + part * ATT_OUT
            outs.append(qkv3[:, seq - keep:, c0:c0 + ATT_OUT].reshape(1, bsz, keep, HEADS_PER_GROUP, HEAD_DIM))
    outs.append(mk3.reshape(1, bsz, MEM_TOKENS, MEM_HEADS, MEM_HEAD_DIM))
    outs.append(mv3.reshape(1, bsz, MEM_TOKENS, MEM_HEADS, MEM_HEAD_DIM))
    outs.append(_blocks_to_state(st[:, :, :SSM_KSTATE])[None])
    outs.append(_blocks_to_state(st[:, :, SSM_KSTATE:])[None])
    outs.extend(jnp.transpose(c, (0, 3, 1, 2))[None] for c in new_caches)
    outs.append(_blocks_to_state(st_s[:, :, :SSM_KSTATE])[None])
    outs.append(_blocks_to_state(st_s[:, :, SSM_KSTATE:])[None])
    return tuple(outs)
```

```python
import functools
import math

import jax
import jax.numpy as jnp
from jax import lax
from jax.experimental import pallas as pl
from jax.experimental.pallas import tpu as pltpu

F32 = jnp.float32
BF16 = jnp.bfloat16

D_MODEL = 1024
ATT_GROUPS = ((128, 1), (512, 4), (2048, 16))
HEADS_PER_GROUP = 4
HEAD_DIM = 64
N_ATT_HEADS = len(ATT_GROUPS) * HEADS_PER_GROUP
ATT_OUT = HEADS_PER_GROUP * HEAD_DIM
QKV_GROUP = 3 * ATT_OUT
QKV_COLS = len(ATT_GROUPS) * QKV_GROUP
SSM_WIDTH = D_MODEL // 2
SSM_GROUP_CH = 16
SSM_GROUPS = SSM_WIDTH // SSM_GROUP_CH
SSM_STATE = 64
SSM_NSTATE = SSM_GROUPS * SSM_STATE
MEM_TOKENS = 256
MEM_HEADS = 4
MEM_HEAD_DIM = D_MODEL // 8
MEM_WIDTH = MEM_HEADS * MEM_HEAD_DIM
N_BRANCHES = 3
MOE_GROUPS = 4
EXPERTS_PER_GROUP = 8
N_EXPERTS = MOE_GROUPS * EXPERTS_PER_GROUP
TOP_K = 2
D_EXPERT = D_MODEL // 2
RMS_EPS = 1e-6

LANES = 128
SUBLANES = 8
Q_TILE = 128
ROUTER_COLS = 128
NEG_BIG = -1e30
VMEM_LIMIT = 56 * 1024 * 1024


def _cparams(sem):
    return pltpu.CompilerParams(dimension_semantics=sem, vmem_limit_bytes=VMEM_LIMIT)


def _mm(a, b, precise, dims=None):
    if precise:
        a, b, kw = a.astype(F32), b.astype(F32), dict(precision=lax.Precision.HIGHEST)
    else:
        a, b, kw = a.astype(BF16), b.astype(BF16), {}
    if dims is None:
        return jnp.dot(a, b, preferred_element_type=F32, **kw)
    return lax.dot_general(a, b, dims, preferred_element_type=F32, **kw)


NT_DIMS = (((1,), (1,)), ((), ()))


def _rms(x, g):
    ms = jnp.mean(x * x, axis=-1, keepdims=True)
    return (x * lax.rsqrt(ms + RMS_EPS)) * g


def _norm_matmul_kernel(x_ref, g_ref, w_ref, *out_refs, splits, sigmoid, chunk, precise):
    xb = _rms(x_ref[...], g_ref[...])
    if not precise:
        xb = xb.astype(BF16)
    c0 = 0
    for o_ref, n, sg in zip(out_refs, splits, sigmoid):
        for j in range(0, n, chunk):
            w = min(chunk, n - j)
            y = _mm(xb, w_ref[:, c0 + j:c0 + j + w], precise)
            if sg:
                y = jax.nn.sigmoid(y)
            o_ref[:, j:j + w] = y.astype(o_ref.dtype)
        c0 += n


def norm_matmul(x, g, w, splits, sigmoid, tm, dtypes=None, precise=False):
    n, d = x.shape
    tm = min(tm, n)
    kern = functools.partial(_norm_matmul_kernel, splits=tuple(splits), sigmoid=tuple(sigmoid), chunk=512,
                             precise=precise)
    return pl.pallas_call(
        kern,
        grid=(n // tm,),
        in_specs=[
            pl.BlockSpec((tm, d), lambda i: (i, 0)),
            pl.BlockSpec((1, d), lambda i: (0, 0)),
            pl.BlockSpec((d, w.shape[1]), lambda i: (0, 0), pipeline_mode=pl.Buffered(1)),
        ],
        out_specs=[pl.BlockSpec((tm, s), lambda i: (i, 0)) for s in splits],
        out_shape=[jax.ShapeDtypeStruct((n, s), dt) for s, dt in zip(splits, dtypes or (F32,) * len(splits))],
        compiler_params=_cparams(("arbitrary",)),
        name="norm_matmul",
    )(x, g.reshape(1, d), w)


def _attn_prompt_kernel(slopes_ref, *refs, seq):
    qkv_refs = refs[:9]
    o_ref = refs[9]
    o_scr, l_scr = refs[10], refs[11]
    hp = pl.program_id(1)
    scale = HEAD_DIM ** -0.5
    row = lax.broadcasted_iota(jnp.int32, (Q_TILE, 2 * Q_TILE), 0)
    col = lax.broadcasted_iota(jnp.int32, (Q_TILE, 2 * Q_TILE), 1)
    dist = (row + Q_TILE - col)
    in_window = jnp.logical_and(dist >= 0, dist <= Q_TILE)
    distf = dist.astype(F32)
    lane = lax.broadcasted_iota(jnp.int32, (Q_TILE, LANES), 1)
    head_a = lane < HEAD_DIM

    for g, (_, dil) in enumerate(ATT_GROUPS):
        q_ref, k_ref, v_ref = qkv_refs[3 * g:3 * g + 3]
        sub_len = seq // dil
        n_blk = sub_len // Q_TILE
        slope_a = slopes_ref[g * HEADS_PER_GROUP + 2 * hp] * float(dil)
        slope_b = slopes_ref[g * HEADS_PER_GROUP + 2 * hp + 1] * float(dil)

        def rows(start):
            if dil == 1:
                return pl.ds(start, Q_TILE)
            return pl.ds(start, Q_TILE, stride=dil)

        def body(it, carry, q_ref=q_ref, k_ref=k_ref, v_ref=v_ref, n_blk=n_blk, dil=dil,
                 slope_a=slope_a, slope_b=slope_b, rows=rows, g=g):
            r = it // n_blk
            blk = it % n_blk
            prev = jnp.maximum(blk - 1, 0)
            q_start = r + dil * Q_TILE * blk
            p_start = r + dil * Q_TILE * prev
            q = q_ref[rows(q_start), :]
            k2 = jnp.concatenate([k_ref[rows(p_start), :], k_ref[rows(q_start), :]], axis=0).astype(BF16)
            v2 = jnp.concatenate([v_ref[rows(p_start), :], v_ref[rows(q_start), :]], axis=0).astype(BF16)
            valid = jnp.logical_and(in_window, jnp.logical_or(col >= Q_TILE, blk > 0))
            outs, lses = [], []
            for is_a, slope in ((True, slope_a), (False, slope_b)):
                hmask = head_a if is_a else jnp.logical_not(head_a)
                qh = jnp.where(hmask, q, 0.0).astype(BF16)
                s = lax.dot_general(qh, k2, NT_DIMS, preferred_element_type=F32)
                s = s * scale - slope * distf
                s = jnp.where(valid, s, NEG_BIG)
                m = jnp.max(s, axis=-1, keepdims=True)
                p = jnp.exp(s - m)
                den = jnp.sum(p, axis=-1, keepdims=True)
                pv = jnp.dot(p.astype(BF16), v2, preferred_element_type=F32)
                outs.append(pv / den)
                lses.append(m + jnp.log(den))
            o_scr[g, rows(q_start), :] = jnp.where(head_a, outs[0], outs[1])
            l_scr[g, rows(q_start), :] = jnp.where(head_a, lses[0], lses[1])
            return carry

        lax.fori_loop(0, dil * n_blk, body, 0, unroll=8)

    def combine(c, carry):
        sl = pl.ds(pl.multiple_of(c * 256, 256), 256)
        l0, l1, l2 = l_scr[0, sl, :], l_scr[1, sl, :], l_scr[2, sl, :]
        m = jnp.maximum(jnp.maximum(l0, l1), l2)
        e0, e1, e2 = jnp.exp(l0 - m), jnp.exp(l1 - m), jnp.exp(l2 - m)
        tot = e0 + e1 + e2
        att = (e0 * o_scr[0, sl, :] + e1 * o_scr[1, sl, :] + e2 * o_scr[2, sl, :]) / tot
        o_ref[sl, :] = att.astype(o_ref.dtype)
        return carry

    lax.fori_loop(0, seq // 256, combine, 0)


def attn_prompt(qkv, slopes):
    bsz, seq, _ = qkv.shape
    in_specs = []
    for g in range(len(ATT_GROUPS)):
        for part in range(3):
            base = (g * QKV_GROUP + part * ATT_OUT) // LANES
            in_specs.append(pl.BlockSpec((None, seq, LANES),
                                         lambda b, h, sl, base=base: (b, 0, base + h)))
    grid_spec = pltpu.PrefetchScalarGridSpec(
        num_scalar_prefetch=1,
        grid=(bsz, ATT_OUT // LANES),
        in_specs=in_specs,
        out_specs=pl.BlockSpec((None, seq, LANES), lambda b, h, sl: (b, 0, h)),
        scratch_shapes=[pltpu.VMEM((3, seq, LANES), F32), pltpu.VMEM((3, seq, LANES), F32)],
    )
    return pl.pallas_call(
        functools.partial(_attn_prompt_kernel, seq=seq),
        grid_spec=grid_spec,
        out_shape=jax.ShapeDtypeStruct((bsz, seq, ATT_OUT), BF16),
        compiler_params=_cparams(("arbitrary", "arbitrary")),
        name="attn_prompt",
    )(slopes, *([qkv] * 9))


def _attn_sample_kernel(qkv_ref, col_ref, *refs):
    cache_refs = refs[:6]
    o_ref = refs[6]
    new_refs = refs[7:13]
    scale = HEAD_DIM ** -0.5
    n_grp = len(ATT_GROUPS)
    outs = [[None] * HEADS_PER_GROUP for _ in range(n_grp)]
    lses = [[None] * HEADS_PER_GROUP for _ in range(n_grp)]
    for g, (window, dil) in enumerate(ATT_GROUPS):
        lane = lax.broadcasted_iota(jnp.int32, (1, window), 1)
        on_grid = jnp.bitwise_and(lane, dil - 1) == 0
        dist = (window - lane).astype(F32)
        last = lax.broadcasted_iota(jnp.int32, (HEAD_DIM, window), 1) == window - 1
        for h in range(HEADS_PER_GROUP):
            slope = 2.0 ** (-8.0 * (g * HEADS_PER_GROUP + h + 1) / N_ATT_HEADS)
            q = qkv_ref[g, 0][h:h + 1, :]
            kn = qkv_ref[g, 1][h:h + 1, :]
            vn = qkv_ref[g, 2][h:h + 1, :]
            kt = cache_refs[2 * g][h]
            vt = cache_refs[2 * g + 1][h]
            s = _mm(q, kt, True) * scale - slope * dist
            s = jnp.where(on_grid, s, NEG_BIG)
            sn = jnp.sum(q * kn, axis=-1, keepdims=True) * scale
            m = jnp.maximum(jnp.max(s, axis=-1, keepdims=True), sn)
            p = jnp.exp(s - m)
            pn = jnp.exp(sn - m)
            den = jnp.sum(p, axis=-1, keepdims=True) + pn
            outs[g][h] = (_mm(p, vt, True, NT_DIMS) + pn * vn) / den
            lses[g][h] = m + jnp.log(den)
            new_refs[2 * g][h] = jnp.where(last, col_ref[g, 0, h], pltpu.roll(kt, window - 1, axis=1))
            new_refs[2 * g + 1][h] = jnp.where(last, col_ref[g, 1, h], pltpu.roll(vt, window - 1, axis=1))
    for h in range(HEADS_PER_GROUP):
        m = jnp.maximum(jnp.maximum(lses[0][h], lses[1][h]), lses[2][h])
        es = [jnp.exp(lses[g][h] - m) for g in range(n_grp)]
        num = es[0] * outs[0][h] + es[1] * outs[1][h] + es[2] * outs[2][h]
        o_ref[h:h + 1, :] = num / (es[0] + es[1] + es[2])


def attn_sample(qkv5, caches_t):
    bsz = qkv5.shape[0]
    cols = qkv5[:, :, 1:3].reshape(bsz, len(ATT_GROUPS), 2, HEADS_PER_GROUP, HEAD_DIM, 1)
    in_specs = [pl.BlockSpec((None, 3, 3, HEADS_PER_GROUP, HEAD_DIM), lambda b: (b, 0, 0, 0, 0)),
                pl.BlockSpec((None, 3, 2, HEADS_PER_GROUP, HEAD_DIM, 1), lambda b: (b, 0, 0, 0, 0, 0))]
    cache_specs = [pl.BlockSpec((None,) + c.shape[1:], lambda b: (b, 0, 0, 0)) for c in caches_t]
    return pl.pallas_call(
        _attn_sample_kernel,
        grid=(bsz,),
        in_specs=in_specs + cache_specs,
        out_specs=[pl.BlockSpec((None, HEADS_PER_GROUP, HEAD_DIM), lambda b: (b, 0, 0))] + cache_specs,
        out_shape=[jax.ShapeDtypeStruct((bsz, HEADS_PER_GROUP, HEAD_DIM), F32)]
                  + [jax.ShapeDtypeStruct(c.shape, F32) for c in caches_t],
        compiler_params=_cparams(("arbitrary",)),
        name="attn_sample",
    )(qkv5, cols, *caches_t)


SSM_KBLK = SSM_WIDTH // LANES
SSM_KSTATE = SSM_NSTATE // SSM_KBLK


def _ssm_tail(y, u, dsk, wglu, bglu, precise=False):
    y = y + dsk * u
    z = jax.nn.gelu(y)
    gl = _mm(z, wglu, precise) + bglu
    return z * jax.nn.sigmoid(gl)


def _ssm_prompt_kernel(u_ref, wb_ref, wc_ref, are_ref, aim_ref, dsk_ref, wglu_ref, bglu_ref,
                       o_ref, st_ref, uperm, bu, sbf, yperm, ynat, *, bsz, tt):
    rows = bsz * tt

    @pl.when(pl.program_id(0) == 0)
    def _():
        st_ref[...] = jnp.zeros_like(st_ref)

    for b in range(bsz):
        ub = u_ref[b]
        for k in range(SSM_KBLK):
            uperm[k, pl.ds(b, tt, stride=bsz), :] = ub[:, k * LANES:(k + 1) * LANES]

    for k in range(SSM_KBLK):
        bu[k] = jnp.dot(uperm[k].astype(BF16), wb_ref[k], preferred_element_type=F32)

    for k in range(SSM_KBLK):
        ar = jnp.broadcast_to(are_ref[k], (bsz, SSM_KSTATE))
        ai = jnp.broadcast_to(aim_ref[k], (bsz, SSM_KSTATE))
        st = st_ref[k]

        def step(t, carry, k=k, ar=ar, ai=ai):
            re, im = carry
            r0 = pl.multiple_of(t * bsz, bsz)
            b_t = bu[k, pl.ds(r0, bsz), :]
            nre = ar * re - ai * im + b_t[:, :SSM_KSTATE]
            nim = ar * im + ai * re + b_t[:, SSM_KSTATE:]
            sbf[k, pl.ds(r0, bsz), :] = jnp.concatenate([nre, nim], axis=1).astype(BF16)
            return nre, nim

        re, im = lax.fori_loop(0, tt, step, (st[:, :SSM_KSTATE], st[:, SSM_KSTATE:]), unroll=2)
        st_ref[k] = jnp.concatenate([re, im], axis=1)

    for k in range(SSM_KBLK):
        yperm[k] = jnp.dot(sbf[k], wc_ref[k], preferred_element_type=F32)

    for b in range(bsz):
        for k in range(SSM_KBLK):
            ynat[b * tt:(b + 1) * tt, k * LANES:(k + 1) * LANES] = yperm[k, pl.ds(b, tt, stride=bsz), :]

    u2 = u_ref[...].reshape(rows, SSM_WIDTH)
    out = _ssm_tail(ynat[...], u2, dsk_ref[...], wglu_ref[...], bglu_ref[...])
    o_ref[...] = out.reshape(bsz, tt, SSM_WIDTH).astype(o_ref.dtype)


def ssm_prompt(u, prm, tt=32):
    bsz, seq, _ = u.shape
    rows = bsz * tt
    const = lambda shape: pl.BlockSpec(shape, lambda i: (0,) * len(shape))
    return pl.pallas_call(
        functools.partial(_ssm_prompt_kernel, bsz=bsz, tt=tt),
        grid=(seq // tt,),
        in_specs=[
            pl.BlockSpec((bsz, tt, SSM_WIDTH), lambda i: (0, i, 0)),
            const((SSM_KBLK, LANES, 2 * SSM_KSTATE)),
            const((SSM_KBLK, 2 * SSM_KSTATE, LANES)),
            const((SSM_KBLK, 1, SSM_KSTATE)),
            const((SSM_KBLK, 1, SSM_KSTATE)),
            const((1, SSM_WIDTH)),
            const((SSM_WIDTH, SSM_WIDTH)),
            const((1, SSM_WIDTH)),
        ],
        out_specs=[
            pl.BlockSpec((bsz, tt, SSM_WIDTH), lambda i: (0, i, 0)),
            const((SSM_KBLK, bsz, 2 * SSM_KSTATE)),
        ],
        out_shape=[
            jax.ShapeDtypeStruct((bsz, seq, SSM_WIDTH), BF16),
            jax.ShapeDtypeStruct((SSM_KBLK, bsz, 2 * SSM_KSTATE), F32),
        ],
        scratch_shapes=[
            pltpu.VMEM((SSM_KBLK, rows, LANES), F32),
            pltpu.VMEM((SSM_KBLK, rows, 2 * SSM_KSTATE), F32),
            pltpu.VMEM((SSM_KBLK, rows, 2 * SSM_KSTATE), BF16),
            pltpu.VMEM((SSM_KBLK, rows, LANES), F32),
            pltpu.VMEM((rows, SSM_WIDTH), F32),
        ],
        compiler_params=_cparams(("arbitrary",)),
        name="ssm_prompt",
    )(u, prm["wb"], prm["wc"], prm["a_re"], prm["a_im"], prm["dsk"], prm["wglu"], prm["bglu"])


def _ssm_sample_kernel(u_ref, h0_ref, wb_ref, wc_ref, are_ref, aim_ref, dsk_ref, wglu_ref, bglu_ref,
                       o_ref, st_ref):
    u = u_ref[...]
    ys = []
    for k in range(SSM_KBLK):
        b_t = _mm(u[:, k * LANES:(k + 1) * LANES], wb_ref[k], True)
        h0 = h0_ref[k]
        re, im = h0[:, :SSM_KSTATE], h0[:, SSM_KSTATE:]
        ar, ai = are_ref[k], aim_ref[k]
        nre = ar * re - ai * im + b_t[:, :SSM_KSTATE]
        nim = ar * im + ai * re + b_t[:, SSM_KSTATE:]
        s = jnp.concatenate([nre, nim], axis=1)
        st_ref[k] = s
        ys.append(_mm(s, wc_ref[k], True))
    y = jnp.concatenate(ys, axis=1)
    o_ref[...] = _ssm_tail(y, u, dsk_ref[...], wglu_ref[...], bglu_ref[...], precise=True)


def ssm_sample(u, h0, prm):
    bsz = u.shape[0]
    return pl.pallas_call(
        _ssm_sample_kernel,
        out_shape=[jax.ShapeDtypeStruct((bsz, SSM_WIDTH), F32),
                   jax.ShapeDtypeStruct((SSM_KBLK, bsz, 2 * SSM_KSTATE), F32)],
        compiler_params=pltpu.CompilerParams(vmem_limit_bytes=VMEM_LIMIT),
        name="ssm_sample",
    )(u, h0, prm["wb_f32"], prm["wc_f32"], prm["a_re"], prm["a_im"], prm["dsk"], prm["wglu_f32"], prm["bglu"])


def ssm_params(lam_re, lam_im, log_dt, b_re, b_im, c_re, c_im, d_skip, w_glu, b_glu):
    dt = jnp.exp(log_dt)[:, None]
    mag = jnp.exp(lam_re * dt)
    ab_re = mag * jnp.cos(lam_im * dt)
    ab_im = mag * jnp.sin(lam_im * dt)
    den = lam_re * lam_re + lam_im * lam_im
    nr = ab_re - 1.0
    ni = ab_im
    z_re = ((nr * lam_re + ni * lam_im) / den)[..., None]
    z_im = ((ni * lam_re - nr * lam_im) / den)[..., None]
    bb_re = z_re * b_re - z_im * b_im
    bb_im = z_re * b_im + z_im * b_re
    gpb = SSM_GROUPS // SSM_KBLK
    eye = jnp.eye(gpb, dtype=F32)

    def in_mat(bb):
        bbk = bb.reshape(SSM_KBLK, gpb, SSM_STATE, SSM_GROUP_CH)
        m = jnp.einsum("kgpc,gh->kgchp", bbk, eye)
        return m.reshape(SSM_KBLK, LANES, SSM_KSTATE)

    def out_mat(c):
        ck = c.reshape(SSM_KBLK, gpb, SSM_GROUP_CH, SSM_STATE)
        m = jnp.einsum("kgcp,gh->kgphc", ck, eye)
        return m.reshape(SSM_KBLK, SSM_KSTATE, LANES)

    wb = jnp.concatenate([in_mat(bb_re), in_mat(bb_im)], axis=2)
    wc = jnp.concatenate([out_mat(c_re), -out_mat(c_im)], axis=1)
    return dict(
        wb=wb.astype(BF16), wc=wc.astype(BF16), wb_f32=wb, wc_f32=wc, wglu_f32=w_glu,
        a_re=ab_re.reshape(SSM_KBLK, 1, SSM_KSTATE), a_im=ab_im.reshape(SSM_KBLK, 1, SSM_KSTATE),
        dsk=d_skip.reshape(1, SSM_WIDTH), wglu=w_glu.astype(BF16), bglu=b_glu.reshape(1, SSM_WIDTH))


def _state_to_blocks(h):
    bsz = h.shape[0]
    return h.reshape(bsz, SSM_KBLK, SSM_KSTATE).transpose(1, 0, 2)


def _blocks_to_state(s):
    bsz = s.shape[1]
    return s.transpose(1, 0, 2).reshape(bsz, SSM_GROUPS, SSM_STATE)


def _mem_attn_kernel(q_ref, k_ref, v_ref, o_ref, *, precise):
    scale = MEM_HEAD_DIM ** -0.5
    for h in range(MEM_HEADS):
        sl = slice(h * MEM_HEAD_DIM, (h + 1) * MEM_HEAD_DIM)
        s = _mm(q_ref[:, sl], k_ref[:, sl], precise, NT_DIMS) * scale
        m = jnp.max(s, axis=-1, keepdims=True)
        p = jnp.exp(s - m)
        den = jnp.sum(p, axis=-1, keepdims=True)
        o_ref[:, sl] = _mm(p / den, v_ref[:, sl], precise).astype(o_ref.dtype)


def mem_attention(q, mk, mv, tl, out_dtype=F32, precise=False):
    bsz, seq, _ = q.shape
    tl = min(tl, seq)
    return pl.pallas_call(
        functools.partial(_mem_attn_kernel, precise=precise),
        grid=(bsz, seq // tl),
        in_specs=[
            pl.BlockSpec((None, tl, MEM_WIDTH), lambda b, i: (b, i, 0)),
            pl.BlockSpec((None, MEM_TOKENS, MEM_WIDTH), lambda b, i: (b, 0, 0)),
            pl.BlockSpec((None, MEM_TOKENS, MEM_WIDTH), lambda b, i: (b, 0, 0)),
        ],
        out_specs=pl.BlockSpec((None, tl, MEM_WIDTH), lambda b, i: (b, i, 0)),
        out_shape=jax.ShapeDtypeStruct((bsz, seq, MEM_WIDTH), out_dtype),
        compiler_params=_cparams(("arbitrary", "arbitrary")),
        name="mem_attention",
    )(q, mk, mv)


def _route(logits, seen_ref):
    lane = lax.broadcasted_iota(jnp.int32, logits.shape, 1)
    lanef = lane.astype(F32)
    none = float(ROUTER_COLS)
    neg = -jnp.inf
    gl = jnp.where(lane < MOE_GROUPS, logits, neg)
    gmax = jnp.max(gl, axis=-1, keepdims=True)
    grp = jnp.min(jnp.where(gl == gmax, lanef, none), axis=-1, keepdims=True)
    p_grp = 1.0 / jnp.sum(jnp.exp(gl - gmax), axis=-1, keepdims=True)
    lo = MOE_GROUPS + grp * EXPERTS_PER_GROUP
    el = jnp.where(jnp.logical_and(lanef >= lo, lanef < lo + EXPERTS_PER_GROUP), logits, neg)
    v1 = jnp.max(el, axis=-1, keepdims=True)
    i1 = jnp.min(jnp.where(el == v1, lanef, none), axis=-1, keepdims=True)
    el2 = jnp.where(lanef == i1, neg, el)
    v2 = jnp.max(el2, axis=-1, keepdims=True)
    i2 = jnp.min(jnp.where(el2 == v2, lanef, none), axis=-1, keepdims=True)
    t = jnp.exp(v2 - v1)
    g1 = p_grp / (1.0 + t)
    g2 = g1 * t
    e1 = i1 - MOE_GROUPS
    e2 = i2 - MOE_GROUPS
    oh1 = (lanef == e1).astype(F32)
    oh2 = (lanef == e2).astype(F32)
    both = oh1 + oh2
    tm = logits.shape[0]
    earlier = (lax.broadcasted_iota(jnp.int32, (tm, tm), 1) < lax.broadcasted_iota(jnp.int32, (tm, tm), 0))
    before = jnp.dot(earlier.astype(BF16), both.astype(BF16), preferred_element_type=F32) + seen_ref[...]
    r1 = jnp.sum(oh1 * before, axis=-1, keepdims=True)
    r2 = jnp.sum(oh2 * before, axis=-1, keepdims=True)
    seen_ref[...] += jnp.sum(both, axis=0, keepdims=True)
    out = jnp.where(lane == 0, g1, 0.0)
    out = jnp.where(lane == 1, g2, out)
    out = jnp.where(lane == 2, e1, out)
    out = jnp.where(lane == 3, e2, out)
    out = jnp.where(lane == 4, r1, out)
    return jnp.where(lane == 5, r2, out)


def _merge_kernel(att_ref, ssm_ref, mem_ref, gate_ref, x_ref, wa_ref, ws_ref, wm_ref, wo_ref,
                  n2_ref, wr_ref, br_ref, h_ref, hn_ref, lg_ref, seen_ref, *, tm, precise):
    @pl.when(pl.program_id(0) == 0)
    def _():
        seen_ref[...] = jnp.zeros_like(seen_ref)

    a = _mm(att_ref[...], wa_ref[...], precise)
    merged = gate_ref[:, 0:D_MODEL] * a
    s = _mm(ssm_ref[...], ws_ref[...], precise)
    merged = merged + gate_ref[:, D_MODEL:2 * D_MODEL] * s
    m = _mm(mem_ref[...], wm_ref[...], precise)
    merged = merged + gate_ref[:, 2 * D_MODEL:3 * D_MODEL] * m
    h = x_ref[...] + _mm(merged, wo_ref[...], precise)
    h_ref[...] = h
    hn = _rms(h, n2_ref[...])
    for j in range(D_MODEL // LANES):
        hn_ref[pl.ds(j, tm, stride=SUBLANES), :] = hn[:, j * LANES:(j + 1) * LANES]
    logits = _mm(hn, wr_ref[...], precise) + br_ref[...]
    lg_ref[...] = _route(logits, seen_ref)


def merge(att, ssm, mem, gates, x, wts, tm, precise=False):
    n = x.shape[0]
    tm = min(tm, n)
    row = lambda w: pl.BlockSpec((tm, w), lambda i: (i, 0))
    const = lambda a: pl.BlockSpec(a.shape, lambda i: (0, 0))
    ws = [wts["wa"], wts["ws"], wts["wm"], wts["wo"], wts["n2"], wts["wr"], wts["br"]]
    return pl.pallas_call(
        functools.partial(_merge_kernel, tm=tm, precise=precise),
        grid=(n // tm,),
        in_specs=[row(ATT_OUT), row(SSM_WIDTH), row(MEM_WIDTH), row(N_BRANCHES * D_MODEL), row(D_MODEL)]
                 + [const(w) for w in ws],
        out_specs=[row(D_MODEL), pl.BlockSpec((tm * SUBLANES, LANES), lambda i: (i, 0)), row(ROUTER_COLS)],
        out_shape=[jax.ShapeDtypeStruct((n, D_MODEL), F32),
                   jax.ShapeDtypeStruct((n * SUBLANES, LANES), F32),
                   jax.ShapeDtypeStruct((n, ROUTER_COLS), F32)],
        scratch_shapes=[pltpu.VMEM((1, ROUTER_COLS), F32)],
        compiler_params=_cparams(("arbitrary",)),
        name="merge",
    )(att, ssm, mem, gates, x, *ws)


TABLE_WORDS = SUBLANES * LANES


def _row_dma_loop(n, off, smem, slot, start):
    inner = min(n, LANES)
    assert n % inner == 0

    def body(r, carry):
        for j in range(inner):
            f = off + j
            start(smem[slot, f // LANES + r, f % LANES], r * inner + j, j % 2)
        return carry

    if n == inner:
        body(0, 0)
    else:
        lax.fori_loop(0, n // inner, body, 0)


def _tile_rows(i):
    start = i * SUBLANES
    return pl.ds(start if isinstance(start, int) else pl.multiple_of(start, SUBLANES), SUBLANES)


def _dispatch_kernel(pos_hbm, hn_ref, xs_hbm, pos_smem, zero, sem, psem, *, tm, n_pad):
    i = pl.program_id(0)
    slot = i % 2

    def pos_copy(b, s):
        return pltpu.make_async_copy(pos_hbm.at[b], pos_smem.at[s], psem.at[s])

    @pl.when(i == 0)
    def _():
        pos_copy(0, 0).start()

    @pl.when(i + 1 < pl.num_programs(0))
    def _():
        pos_copy(i + 1, 1 - slot).start()

    zero[...] = jnp.zeros_like(zero)
    pos_copy(i, slot).wait()

    def send_token(row, tok, prio):
        pltpu.make_async_copy(hn_ref.at[_tile_rows(tok)], xs_hbm.at[row], sem).start(priority=prio)

    def send_zero(row, _, prio):
        pltpu.make_async_copy(zero, xs_hbm.at[row], sem).start(priority=prio)

    for k in range(TOP_K):
        _row_dma_loop(tm, k * tm, pos_smem, slot, send_token)
    _row_dma_loop(n_pad, TOP_K * tm, pos_smem, slot, send_zero)
    whole, part = divmod(n_pad, tm)
    for _ in range(TOP_K + whole):
        pltpu.make_async_copy(hn_ref, hn_ref, sem).wait()
    if part:
        piece = hn_ref.at[pl.ds(0, part * SUBLANES)]
        pltpu.make_async_copy(piece, piece, sem).wait()


def moe_dispatch(pos, hn2d, n_rows, tm, n_pad):
    n_tiles = pos.shape[0]
    return pl.pallas_call(
        functools.partial(_dispatch_kernel, tm=tm, n_pad=n_pad),
        grid=(n_tiles,),
        in_specs=[pl.BlockSpec(memory_space=pl.ANY),
                  pl.BlockSpec((tm * SUBLANES, LANES), lambda i: (i, 0))],
        out_specs=pl.BlockSpec(memory_space=pl.ANY),
        out_shape=jax.ShapeDtypeStruct((n_rows, SUBLANES, LANES), F32),
        scratch_shapes=[pltpu.SMEM((2, SUBLANES, LANES), jnp.int32),
                        pltpu.VMEM((SUBLANES, LANES), F32),
                        pltpu.SemaphoreType.DMA, pltpu.SemaphoreType.DMA((2,))],
        compiler_params=_cparams(("arbitrary",)),
        name="moe_dispatch",
    )(pos, hn2d)


def _moe_kernel(be_ref, nu_ref, dst_hbm, xs_ref, w1_ref, w3_ref, w2_ref, out_hbm,
                dst_smem, ybuf, w1b, w3b, w2b, ssem, isem, *, bm):
    blk = pl.program_id(0)
    nblk = pl.num_programs(0)
    slot = blk % 2
    tiles = D_MODEL // LANES

    def dst_copy(b, s):
        return pltpu.make_async_copy(dst_hbm.at[b], dst_smem.at[s], isem.at[s])

    def wait_scatter():
        pltpu.make_async_copy(ybuf, ybuf, ssem).wait()

    @pl.when(blk == 0)
    def _():
        dst_copy(0, 0).start()

    @pl.when(blk + 1 < nblk)
    def _():
        dst_copy(blk + 1, 1 - slot).start()

    prev = jnp.maximum(blk - 1, 0)
    changed = jnp.logical_or(blk == 0, be_ref[blk] != be_ref[prev])
    active = jnp.logical_or(blk == 0, blk < nu_ref[0])

    @pl.when(jnp.logical_and(active, changed))
    def _():
        w1b[...] = w1_ref[...].astype(BF16)
        w3b[...] = w3_ref[...].astype(BF16)
        w2b[...] = w2_ref[...].astype(BF16)

    @pl.when(active)
    def _():
        x = jnp.concatenate([xs_ref[pl.ds(j, bm, stride=SUBLANES), :].astype(BF16) for j in range(tiles)], axis=1)
        h1 = jnp.dot(x, w1b[...], preferred_element_type=F32)
        h3 = jnp.dot(x, w3b[...], preferred_element_type=F32)
        act = (jax.nn.silu(h1) * h3).astype(BF16)
        y = jnp.dot(act, w2b[...], preferred_element_type=F32)

        @pl.when(blk > 0)
        def _():
            wait_scatter()

        for j in range(tiles):
            ybuf[pl.ds(j, bm, stride=SUBLANES), :] = y[:, j * LANES:(j + 1) * LANES]

    @pl.when(jnp.logical_not(active))
    def _():
        wait_scatter()

    dst_copy(blk, slot).wait()

    def send_row(row, i, prio):
        pltpu.make_async_copy(ybuf.at[_tile_rows(i)], out_hbm.at[row], ssem).start(priority=prio)

    _row_dma_loop(bm, 0, dst_smem, slot, send_row)

    @pl.when(blk == nblk - 1)
    def _():
        wait_scatter()


def moe_experts(block_expert, n_used, dst, xs2d, w1, w3, w2, n_out_rows, bm):
    n_blocks = dst.shape[0]
    grid_spec = pltpu.PrefetchScalarGridSpec(
        num_scalar_prefetch=2,
        grid=(n_blocks,),
        in_specs=[
            pl.BlockSpec(memory_space=pl.ANY),
            pl.BlockSpec((bm * SUBLANES, LANES), lambda b, be, nu: (b, 0)),
            pl.BlockSpec((None, D_MODEL, D_EXPERT), lambda b, be, nu: (be[b], 0, 0)),
            pl.BlockSpec((None, D_MODEL, D_EXPERT), lambda b, be, nu: (be[b], 0, 0)),
            pl.BlockSpec((None, D_EXPERT, D_MODEL), lambda b, be, nu: (be[b], 0, 0)),
        ],
        out_specs=pl.BlockSpec(memory_space=pl.ANY),
        scratch_shapes=[
            pltpu.SMEM((2, SUBLANES, LANES), jnp.int32),
            pltpu.VMEM((bm * SUBLANES, LANES), F32),
            pltpu.VMEM((D_MODEL, D_EXPERT), BF16),
            pltpu.VMEM((D_MODEL, D_EXPERT), BF16),
            pltpu.VMEM((D_EXPERT, D_MODEL), BF16),
            pltpu.SemaphoreType.DMA,
            pltpu.SemaphoreType.DMA((2,)),
        ],
    )
    return pl.pallas_call(
        functools.partial(_moe_kernel, bm=bm),
        grid_spec=grid_spec,
        out_shape=jax.ShapeDtypeStruct((n_out_rows, SUBLANES, LANES), F32),
        compiler_params=_cparams(("arbitrary",)),
        name="moe_experts",
    )(block_expert, n_used, dst, xs2d, w1, w3, w2)


def _as_tables(cols, width):
    fill = jnp.zeros((cols.shape[0], TABLE_WORDS - width), jnp.int32)
    return jnp.concatenate([cols, fill], axis=1).reshape(cols.shape[0], SUBLANES, LANES)


def route_tables(route, bm, tm):
    n_tok = route.shape[0]
    n_pairs = n_tok * TOP_K
    e_pair = route[:, 2:2 + TOP_K].astype(jnp.int32)
    rank = route[:, 2 + TOP_K:2 + 2 * TOP_K].astype(jnp.int32)
    e_flat = e_pair.reshape(n_pairs)
    experts = jnp.arange(N_EXPERTS, dtype=jnp.int32)
    counts = jnp.sum(e_flat[:, None] == experts[None, :], axis=0, dtype=jnp.int32)
    padded = (counts + bm - 1) // bm * bm
    pad_ends = jnp.cumsum(padded)
    n_blocks = -(-n_pairs // bm) + N_EXPERTS
    n_rows = n_blocks * bm
    row0 = jnp.arange(n_blocks, dtype=jnp.int32) * bm
    block_expert = jnp.minimum(jnp.sum(pad_ends[None, :] <= row0[:, None], axis=1, dtype=jnp.int32),
                               N_EXPERTS - 1)
    n_fill = n_rows - n_pairs
    fill_id = jnp.arange(n_fill, dtype=jnp.int32)
    fill_e = jnp.minimum(fill_id // bm, N_EXPERTS - 1)
    fill_used = jnp.logical_and(fill_id < N_EXPERTS * bm, fill_id % bm < (padded - counts)[fill_e])
    keys = jnp.concatenate([e_flat * 2, jnp.where(fill_used, fill_e * 2 + 1, 2 * N_EXPERTS)])
    vals = jnp.concatenate([jnp.arange(n_pairs, dtype=jnp.int32), jnp.full((n_fill,), -1, jnp.int32)])
    _, pair_at = lax.sort((keys, vals), num_keys=1, is_stable=True)
    pad_starts = pad_ends - padded
    pair_row = pad_starts[e_pair] + rank
    n_used_before = jnp.cumsum(fill_used.astype(jnp.int32)) - fill_used.astype(jnp.int32)
    pad_row = jnp.where(fill_used, (pad_starts + counts)[fill_e] + fill_id % bm,
                        pad_ends[-1] + fill_id - n_used_before)
    n_tiles = n_tok // tm
    n_pad = n_fill // n_tiles
    assert n_pad * n_tiles == n_fill and n_pad % SUBLANES == 0
    pos = jnp.concatenate([pair_row[:, k].reshape(n_tiles, tm) for k in range(TOP_K)]
                          + [pad_row.reshape(n_tiles, n_pad)], axis=1)
    pair_at = pair_at.reshape(n_blocks, bm)
    valid = pair_at >= 0
    trash = n_pairs + jnp.arange(bm, dtype=jnp.int32)[None, :]
    dst = jnp.where(valid, (pair_at % TOP_K) * n_tok + pair_at // TOP_K, trash)
    n_used = (pad_ends[-1:] // bm).astype(jnp.int32)
    return (block_expert, n_used, _as_tables(dst, bm), _as_tables(pos, TOP_K * tm + n_pad), n_pad, n_rows,
            n_pairs + bm)


def _final_kernel(h_ref, p0_ref, p1_ref, gw_ref, nf_ref, y_ref, *, tm):
    tiles = D_MODEL // LANES
    p0 = jnp.concatenate([p0_ref[pl.ds(j, tm, stride=SUBLANES), :] for j in range(tiles)], axis=1)
    p1 = jnp.concatenate([p1_ref[pl.ds(j, tm, stride=SUBLANES), :] for j in range(tiles)], axis=1)
    gw = gw_ref[...]
    h = h_ref[...] + (gw[:, 0:1] * p0 + gw[:, 1:2] * p1)
    y_ref[...] = _rms(h, nf_ref[...])


def final(h, pairs2d, gate_pad, norm_f, tm):
    n = h.shape[0]
    tm = min(tm, n)
    nb = n // tm
    row = lambda w: pl.BlockSpec((tm, w), lambda i: (i, 0))
    return pl.pallas_call(
        functools.partial(_final_kernel, tm=tm),
        grid=(nb,),
        in_specs=[
            row(D_MODEL),
            pl.BlockSpec((tm * SUBLANES, LANES), lambda i: (i, 0)),
            pl.BlockSpec((tm * SUBLANES, LANES), lambda i: (nb + i, 0)),
            row(LANES),
            pl.BlockSpec((1, D_MODEL), lambda i: (0, 0)),
        ],
        out_specs=row(D_MODEL),
        out_shape=jax.ShapeDtypeStruct((n, D_MODEL), F32),
        compiler_params=_cparams(("arbitrary",)),
        name="final",
    )(h, pairs2d, pairs2d, gate_pad, norm_f.reshape(1, D_MODEL))


def _alibi_slopes():
    h = jnp.arange(1, N_ATT_HEADS + 1, dtype=F32)
    return jnp.exp2(-8.0 * h / N_ATT_HEADS)


def _token_stage(att, ssm, mem, gates, x2, wts, w1, w3, w2, norm_f, tm, bm, precise=False):
    n = x2.shape[0]
    h, hn2d, route = merge(att, ssm, mem, gates, x2, wts, tm, precise)
    tm = min(tm, n)
    block_expert, n_used, dst, pos, n_pad, n_rows, out_rows = route_tables(route, bm, tm)
    xs = moe_dispatch(pos, hn2d, n_rows, tm, n_pad)
    pairs = moe_experts(block_expert, n_used, dst, xs.reshape(n_rows * SUBLANES, LANES), w1, w3, w2, out_rows, bm)
    return final(h, pairs.reshape(out_rows * SUBLANES, LANES), route, norm_f, tm)


def kernel(x_prompt, x_sample, cache_swa0_k, cache_swa0_v, cache_swa1_k, cache_swa1_v, cache_swa2_k, cache_swa2_v, cache_mem_k, cache_mem_v, state_ssm_re, state_ssm_im, mem_prompt, norm1, w_in, lam_re, lam_im, log_dt, ssm_b_re, ssm_b_im, ssm_c_re, ssm_c_im, ssm_d, w_glu, b_glu, w_att_out, w_ssm_out, w_mem_out, w_o, norm_mem, w_mk, w_mv, norm2, w_grp, b_grp, w_exp, b_exp, w1, w3, w2, norm_f):
    assert norm1.shape[0] == 1, "single-layer trunk"
    bsz, seq, _ = x_prompt.shape
    sbz = x_sample.shape[0]
    n_p = bsz * seq
    caches = (cache_swa0_k[0], cache_swa0_v[0], cache_swa1_k[0], cache_swa1_v[0], cache_swa2_k[0], cache_swa2_v[0])

    w_in_b = w_in[0].astype(BF16)
    w_mkv_b = jnp.concatenate([w_mk[0], w_mv[0]], axis=1).astype(BF16)
    wr = jnp.zeros((D_MODEL, ROUTER_COLS), F32)
    wr = wr.at[:, :MOE_GROUPS].set(w_grp[0]).at[:, MOE_GROUPS:MOE_GROUPS + N_EXPERTS].set(w_exp[0])
    br = jnp.zeros((1, ROUTER_COLS), F32)
    br = br.at[0, :MOE_GROUPS].set(b_grp[0]).at[0, MOE_GROUPS:MOE_GROUPS + N_EXPERTS].set(b_exp[0])
    wts_f32 = dict(wa=w_att_out[0], ws=w_ssm_out[0], wm=w_mem_out[0], wo=w_o[0],
                   n2=norm2[0].reshape(1, D_MODEL), wr=wr, br=br)
    wts = {k: (v.astype(BF16) if k in ("wa", "ws", "wm", "wo", "wr") else v) for k, v in wts_f32.items()}
    prm = ssm_params(lam_re[0], lam_im[0], log_dt[0], ssm_b_re[0], ssm_b_im[0], ssm_c_re[0], ssm_c_im[0],
                     ssm_d[0], w_glu[0], b_glu[0])
    slopes = _alibi_slopes()
    splits = (QKV_COLS, SSM_WIDTH, MEM_WIDTH, N_BRANCHES * D_MODEL)
    sig = (False, False, False, True)

    x2 = x_prompt.reshape(n_p, D_MODEL)
    qkv, u, mq, gates = norm_matmul(x2, norm1[0], w_in_b, splits, sig, tm=256, dtypes=(F32, F32, BF16, BF16))
    qkv3 = qkv.reshape(bsz, seq, QKV_COLS)
    att = attn_prompt(qkv3, slopes)
    ssm, st = ssm_prompt(u.reshape(bsz, seq, SSM_WIDTH), prm)
    mk, mv = norm_matmul(mem_prompt.reshape(bsz * MEM_TOKENS, D_MODEL), norm_mem[0], w_mkv_b,
                         (MEM_WIDTH, MEM_WIDTH), (False, False), tm=256)
    mk3 = mk.reshape(bsz, MEM_TOKENS, MEM_WIDTH)
    mv3 = mv.reshape(bsz, MEM_TOKENS, MEM_WIDTH)
    mem = mem_attention(mq.reshape(bsz, seq, MEM_WIDTH), mk3, mv3, tl=512, out_dtype=BF16)
    y_p = _token_stage(att.reshape(n_p, ATT_OUT), ssm.reshape(n_p, SSM_WIDTH), mem.reshape(n_p, MEM_WIDTH),
                       gates, x2, wts, w1[0], w3[0], w2[0], norm_f, tm=256, bm=256)

    xs2 = x_sample.reshape(sbz, D_MODEL)
    qkv_s, u_s, mq_s, gates_s = norm_matmul(xs2, norm1[0], w_in[0], splits, sig, tm=sbz, precise=True)
    qkv5 = qkv_s.reshape(sbz, len(ATT_GROUPS), 3, HEADS_PER_GROUP, HEAD_DIM)
    att_s, *new_caches = attn_sample(qkv5, [jnp.transpose(c, (0, 2, 3, 1)) for c in caches])
    att_s = att_s.reshape(sbz, ATT_OUT)
    h0 = jnp.concatenate([_state_to_blocks(state_ssm_re[0]), _state_to_blocks(state_ssm_im[0])], axis=2)
    ssm_s, st_s = ssm_sample(u_s, h0, prm)
    cmk = cache_mem_k[0].reshape(sbz, MEM_TOKENS, MEM_WIDTH)
    cmv = cache_mem_v[0].reshape(sbz, MEM_TOKENS, MEM_WIDTH)
    mem_s = mem_attention(mq_s.reshape(sbz, 1, MEM_WIDTH), cmk, cmv, tl=1, precise=True).reshape(sbz, MEM_WIDTH)
    y_s = _token_stage(att_s, ssm_s, mem_s, gates_s, xs2, wts_f32, w1[0], w3[0], w2[0], norm_f, tm=sbz, bm=16,
                       precise=True)

    outs = [y_p.reshape(bsz, seq, D_MODEL), y_s.reshape(sbz, 1, D_MODEL)]
    for g, (window, _) in enumerate(ATT_GROUPS):
        keep = min(window, seq)
        for part in (1, 2):
            c0 = g * QKV_GROUP + part * ATT_OUT
            outs.append(qkv3[:, seq - keep:, c0:c0 + ATT_OUT].reshape(1, bsz, keep, HEADS_PER_GROUP, HEAD_DIM))
    outs.append(mk3.reshape(1, bsz, MEM_TOKENS, MEM_HEADS, MEM_HEAD_DIM))
    outs.append(mv3.reshape(1, bsz, MEM_TOKENS, MEM_HEADS, MEM_HEAD_DIM))
    outs.append(_blocks_to_state(st[:, :, :SSM_KSTATE])[None])
    outs.append(_blocks_to_state(st[:, :, SSM_KSTATE:])[None])
    outs.extend(jnp.transpose(c, (0, 3, 1, 2))[None] for c in new_caches)
    outs.append(_blocks_to_state(st_s[:, :, :SSM_KSTATE])[None])
    outs.append(_blocks_to_state(st_s[:, :, SSM_KSTATE:])[None])
    return tuple(outs)
```

```python
import functools
import math

import jax
import jax.numpy as jnp
from jax import lax
from jax.experimental import pallas as pl
from jax.experimental.pallas import tpu as pltpu

F32 = jnp.float32
BF16 = jnp.bfloat16

D_MODEL = 1024
ATT_GROUPS = ((128, 1), (512, 4), (2048, 16))
HEADS_PER_GROUP = 4
HEAD_DIM = 64
N_ATT_HEADS = len(ATT_GROUPS) * HEADS_PER_GROUP
ATT_OUT = HEADS_PER_GROUP * HEAD_DIM
QKV_GROUP = 3 * ATT_OUT
QKV_COLS = len(ATT_GROUPS) * QKV_GROUP
SSM_WIDTH = D_MODEL // 2
SSM_GROUP_CH = 16
SSM_GROUPS = SSM_WIDTH // SSM_GROUP_CH
SSM_STATE = 64
SSM_NSTATE = SSM_GROUPS * SSM_STATE
MEM_TOKENS = 256
MEM_HEADS = 4
MEM_HEAD_DIM = D_MODEL // 8
MEM_WIDTH = MEM_HEADS * MEM_HEAD_DIM
N_BRANCHES = 3
MOE_GROUPS = 4
EXPERTS_PER_GROUP = 8
N_EXPERTS = MOE_GROUPS * EXPERTS_PER_GROUP
TOP_K = 2
D_EXPERT = D_MODEL // 2
RMS_EPS = 1e-6

LANES = 128
SUBLANES = 8
Q_TILE = 128
ROUTER_COLS = 128
NEG_BIG = -1e30
VMEM_LIMIT = 56 * 1024 * 1024


def _cparams(sem):
    return pltpu.CompilerParams(dimension_semantics=sem, vmem_limit_bytes=VMEM_LIMIT)


def _mm(a, b, precise, dims=None):
    if precise:
        a, b, kw = a.astype(F32), b.astype(F32), dict(precision=lax.Precision.HIGHEST)
    else:
        a, b, kw = a.astype(BF16), b.astype(BF16), {}
    if dims is None:
        return jnp.dot(a, b, preferred_element_type=F32, **kw)
    return lax.dot_general(a, b, dims, preferred_element_type=F32, **kw)


NT_DIMS = (((1,), (1,)), ((), ()))


def _rms(x, g):
    ms = jnp.mean(x * x, axis=-1, keepdims=True)
    return (x * lax.rsqrt(ms + RMS_EPS)) * g


def _norm_matmul_kernel(x_ref, g_ref, w_ref, *out_refs, splits, sigmoid, chunk, precise):
    xb = _rms(x_ref[...], g_ref[...])
    if not precise:
        xb = xb.astype(BF16)
    c0 = 0
    for o_ref, n, sg in zip(out_refs, splits, sigmoid):
        for j in range(0, n, chunk):
            w = min(chunk, n - j)
            y = _mm(xb, w_ref[:, c0 + j:c0 + j + w], precise)
            if sg:
                y = jax.nn.sigmoid(y)
            o_ref[:, j:j + w] = y.astype(o_ref.dtype)
        c0 += n


def norm_matmul(x, g, w, splits, sigmoid, tm, dtypes=None, precise=False):
    n, d = x.shape
    tm = min(tm, n)
    kern = functools.partial(_norm_matmul_kernel, splits=tuple(splits), sigmoid=tuple(sigmoid), chunk=512,
                             precise=precise)
    return pl.pallas_call(
        kern,
        grid=(n // tm,),
        in_specs=[
            pl.BlockSpec((tm, d), lambda i: (i, 0)),
            pl.BlockSpec((1, d), lambda i: (0, 0)),
            pl.BlockSpec((d, w.shape[1]), lambda i: (0, 0), pipeline_mode=pl.Buffered(1)),
        ],
        out_specs=[pl.BlockSpec((tm, s), lambda i: (i, 0)) for s in splits],
        out_shape=[jax.ShapeDtypeStruct((n, s), dt) for s, dt in zip(splits, dtypes or (F32,) * len(splits))],
        compiler_params=_cparams(("arbitrary",)),
        name="norm_matmul",
    )(x, g.reshape(1, d), w)


def _attn_prompt_kernel(slopes_ref, *refs, seq):
    qkv_refs = refs[:9]
    o_ref = refs[9]
    o_scr, l_scr = refs[10], refs[11]
    hp = pl.program_id(1)
    scale = HEAD_DIM ** -0.5
    row = lax.broadcasted_iota(jnp.int32, (Q_TILE, 2 * Q_TILE), 0)
    col = lax.broadcasted_iota(jnp.int32, (Q_TILE, 2 * Q_TILE), 1)
    dist = (row + Q_TILE - col)
    in_window = jnp.logical_and(dist >= 0, dist <= Q_TILE)
    distf = dist.astype(F32)
    lane = lax.broadcasted_iota(jnp.int32, (Q_TILE, LANES), 1)
    head_a = lane < HEAD_DIM

    for g, (_, dil) in enumerate(ATT_GROUPS):
        q_ref, k_ref, v_ref = qkv_refs[3 * g:3 * g + 3]
        sub_len = seq // dil
        n_blk = sub_len // Q_TILE
        slope_a = slopes_ref[g * HEADS_PER_GROUP + 2 * hp] * float(dil)
        slope_b = slopes_ref[g * HEADS_PER_GROUP + 2 * hp + 1] * float(dil)

        def rows(start):
            if dil == 1:
                return pl.ds(start, Q_TILE)
            return pl.ds(start, Q_TILE, stride=dil)

        def body(it, carry, q_ref=q_ref, k_ref=k_ref, v_ref=v_ref, n_blk=n_blk, dil=dil,
                 slope_a=slope_a, slope_b=slope_b, rows=rows, g=g):
            r = it // n_blk
            blk = it % n_blk
            prev = jnp.maximum(blk - 1, 0)
            q_start = r + dil * Q_TILE * blk
            p_start = r + dil * Q_TILE * prev
            q = q_ref[rows(q_start), :]
            k2 = jnp.concatenate([k_ref[rows(p_start), :], k_ref[rows(q_start), :]], axis=0).astype(BF16)
            v2 = jnp.concatenate([v_ref[rows(p_start), :], v_ref[rows(q_start), :]], axis=0).astype(BF16)
            valid = jnp.logical_and(in_window, jnp.logical_or(col >= Q_TILE, blk > 0))
            outs, lses = [], []
            for is_a, slope in ((True, slope_a), (False, slope_b)):
                hmask = head_a if is_a else jnp.logical_not(head_a)
                qh = jnp.where(hmask, q, 0.0).astype(BF16)
                s = lax.dot_general(qh, k2, NT_DIMS, preferred_element_type=F32)
                s = s * scale - slope * distf
                s = jnp.where(valid, s, NEG_BIG)
                m = jnp.max(s, axis=-1, keepdims=True)
                p = jnp.exp(s - m)
                den = jnp.sum(p, axis=-1, keepdims=True)
                pv = jnp.dot(p.astype(BF16), v2, preferred_element_type=F32)
                outs.append(pv / den)
                lses.append(m + jnp.log(den))
            o_scr[g, rows(q_start), :] = jnp.where(head_a, outs[0], outs[1])
            l_scr[g, rows(q_start), :] = jnp.where(head_a, lses[0], lses[1])
            return carry

        lax.fori_loop(0, dil * n_blk, body, 0, unroll=8)

    def combine(c, carry):
        sl = pl.ds(pl.multiple_of(c * 256, 256), 256)
        l0, l1, l2 = l_scr[0, sl, :], l_scr[1, sl, :], l_scr[2, sl, :]
        m = jnp.maximum(jnp.maximum(l0, l1), l2)
        e0, e1, e2 = jnp.exp(l0 - m), jnp.exp(l1 - m), jnp.exp(l2 - m)
        tot = e0 + e1 + e2
        att = (e0 * o_scr[0, sl, :] + e1 * o_scr[1, sl, :] + e2 * o_scr[2, sl, :]) / tot
        o_ref[sl, :] = att.astype(o_ref.dtype)
        return carry

    lax.fori_loop(0, seq // 256, combine, 0)


def attn_prompt(qkv, slopes):
    bsz, seq, _ = qkv.shape
    in_specs = []
    for g in range(len(ATT_GROUPS)):
        for part in range(3):
            base = (g * QKV_GROUP + part * ATT_OUT) // LANES
            in_specs.append(pl.BlockSpec((None, seq, LANES),
                                         lambda b, h, sl, base=base: (b, 0, base + h)))
    grid_spec = pltpu.PrefetchScalarGridSpec(
        num_scalar_prefetch=1,
        grid=(bsz, ATT_OUT // LANES),
        in_specs=in_specs,
        out_specs=pl.BlockSpec((None, seq, LANES), lambda b, h, sl: (b, 0, h)),
        scratch_shapes=[pltpu.VMEM((3, seq, LANES), F32), pltpu.VMEM((3, seq, LANES), F32)],
    )
    return pl.pallas_call(
        functools.partial(_attn_prompt_kernel, seq=seq),
        grid_spec=grid_spec,
        out_shape=jax.ShapeDtypeStruct((bsz, seq, ATT_OUT), BF16),
        compiler_params=_cparams(("arbitrary", "arbitrary")),
        name="attn_prompt",
    )(slopes, *([qkv] * 9))


def _attn_sample_kernel(qkv_ref, col_ref, *refs):
    cache_refs = refs[:6]
    o_ref = refs[6]
    new_refs = refs[7:13]
    scale = HEAD_DIM ** -0.5
    n_grp = len(ATT_GROUPS)
    outs = [[None] * HEADS_PER_GROUP for _ in range(n_grp)]
    lses = [[None] * HEADS_PER_GROUP for _ in range(n_grp)]
    for g, (window, dil) in enumerate(ATT_GROUPS):
        lane = lax.broadcasted_iota(jnp.int32, (1, window), 1)
        on_grid = jnp.bitwise_and(lane, dil - 1) == 0
        dist = (window - lane).astype(F32)
        last = lax.broadcasted_iota(jnp.int32, (HEAD_DIM, window), 1) == window - 1
        for h in range(HEADS_PER_GROUP):
            slope = 2.0 ** (-8.0 * (g * HEADS_PER_GROUP + h + 1) / N_ATT_HEADS)
            q = qkv_ref[g, 0][h:h + 1, :]
            kn = qkv_ref[g, 1][h:h + 1, :]
            vn = qkv_ref[g, 2][h:h + 1, :]
            kt = cache_refs[2 * g][h]
            vt = cache_refs[2 * g + 1][h]
            s = _mm(q, kt, True) * scale - slope * dist
            s = jnp.where(on_grid, s, NEG_BIG)
            sn = jnp.sum(q * kn, axis=-1, keepdims=True) * scale
            m = jnp.maximum(jnp.max(s, axis=-1, keepdims=True), sn)
            p = jnp.exp(s - m)
            pn = jnp.exp(sn - m)
            den = jnp.sum(p, axis=-1, keepdims=True) + pn
            outs[g][h] = (_mm(p, vt, True, NT_DIMS) + pn * vn) / den
            lses[g][h] = m + jnp.log(den)
            new_refs[2 * g][h] = jnp.where(last, col_ref[g, 0, h], pltpu.roll(kt, window - 1, axis=1))
            new_refs[2 * g + 1][h] = jnp.where(last, col_ref[g, 1, h], pltpu.roll(vt, window - 1, axis=1))
    for h in range(HEADS_PER_GROUP):
        m = jnp.maximum(jnp.maximum(lses[0][h], lses[1][h]), lses[2][h])
        es = [jnp.exp(lses[g][h] - m) for g in range(n_grp)]
        num = es[0] * outs[0][h] + es[1] * outs[1][h] + es[2] * outs[2][h]
        o_ref[h:h + 1, :] = num / (es[0] + es[1] + es[2])


def attn_sample(qkv5, caches_t):
    bsz = qkv5.shape[0]
    cols = qkv5[:, :, 1:3].reshape(bsz, len(ATT_GROUPS), 2, HEADS_PER_GROUP, HEAD_DIM, 1)
    in_specs = [pl.BlockSpec((None, 3, 3, HEADS_PER_GROUP, HEAD_DIM), lambda b: (b, 0, 0, 0, 0)),
                pl.BlockSpec((None, 3, 2, HEADS_PER_GROUP, HEAD_DIM, 1), lambda b: (b, 0, 0, 0, 0, 0))]
    cache_specs = [pl.BlockSpec((None,) + c.shape[1:], lambda b: (b, 0, 0, 0)) for c in caches_t]
    return pl.pallas_call(
        _attn_sample_kernel,
        grid=(bsz,),
        in_specs=in_specs + cache_specs,
        out_specs=[pl.BlockSpec((None, HEADS_PER_GROUP, HEAD_DIM), lambda b: (b, 0, 0))] + cache_specs,
        out_shape=[jax.ShapeDtypeStruct((bsz, HEADS_PER_GROUP, HEAD_DIM), F32)]
                  + [jax.ShapeDtypeStruct(c.shape, F32) for c in caches_t],
        compiler_params=_cparams(("arbitrary",)),
        name="attn_sample",
    )(qkv5, cols, *caches_t)


SSM_KBLK = SSM_WIDTH // LANES
SSM_KSTATE = SSM_NSTATE // SSM_KBLK


def _ssm_tail(y, u, dsk, wglu, bglu, precise=False):
    y = y + dsk * u
    z = jax.nn.gelu(y)
    gl = _mm(z, wglu, precise) + bglu
    return z * jax.nn.sigmoid(gl)


def _ssm_prompt_kernel(u_ref, wb_ref, wc_ref, are_ref, aim_ref, dsk_ref, wglu_ref, bglu_ref,
                       o_ref, st_ref, uperm, bu, sbf, yperm, ynat, *, bsz, tt):
    rows = bsz * tt

    @pl.when(pl.program_id(0) == 0)
    def _():
        st_ref[...] = jnp.zeros_like(st_ref)

    for b in range(bsz):
        ub = u_ref[b]
        for k in range(SSM_KBLK):
            uperm[k, pl.ds(b, tt, stride=bsz), :] = ub[:, k * LANES:(k + 1) * LANES]

    for k in range(SSM_KBLK):
        bu[k] = jnp.dot(uperm[k].astype(BF16), wb_ref[k], preferred_element_type=F32)

    for k in range(SSM_KBLK):
        ar = jnp.broadcast_to(are_ref[k], (bsz, SSM_KSTATE))
        ai = jnp.broadcast_to(aim_ref[k], (bsz, SSM_KSTATE))
        st = st_ref[k]

        def step(t, carry, k=k, ar=ar, ai=ai):
            re, im = carry
            r0 = pl.multiple_of(t * bsz, bsz)
            b_t = bu[k, pl.ds(r0, bsz), :]
            nre = ar * re - ai * im + b_t[:, :SSM_KSTATE]
            nim = ar * im + ai * re + b_t[:, SSM_KSTATE:]
            sbf[k, pl.ds(r0, bsz), :] = jnp.concatenate([nre, nim], axis=1).astype(BF16)
            return nre, nim

        re, im = lax.fori_loop(0, tt, step, (st[:, :SSM_KSTATE], st[:, SSM_KSTATE:]), unroll=2)
        st_ref[k] = jnp.concatenate([re, im], axis=1)

    for k in range(SSM_KBLK):
        yperm[k] = jnp.dot(sbf[k], wc_ref[k], preferred_element_type=F32)

    for b in range(bsz):
        for k in range(SSM_KBLK):
            ynat[b * tt:(b + 1) * tt, k * LANES:(k + 1) * LANES] = yperm[k, pl.ds(b, tt, stride=bsz), :]

    u2 = u_ref[...].reshape(rows, SSM_WIDTH)
    out = _ssm_tail(ynat[...], u2, dsk_ref[...], wglu_ref[...], bglu_ref[...])
    o_ref[...] = out.reshape(bsz, tt, SSM_WIDTH).astype(o_ref.dtype)


def ssm_prompt(u, prm, tt=32):
    bsz, seq, _ = u.shape
    rows = bsz * tt
    const = lambda shape: pl.BlockSpec(shape, lambda i: (0,) * len(shape))
    return pl.pallas_call(
        functools.partial(_ssm_prompt_kernel, bsz=bsz, tt=tt),
        grid=(seq // tt,),
        in_specs=[
            pl.BlockSpec((bsz, tt, SSM_WIDTH), lambda i: (0, i, 0)),
            const((SSM_KBLK, LANES, 2 * SSM_KSTATE)),
            const((SSM_KBLK, 2 * SSM_KSTATE, LANES)),
            const((SSM_KBLK, 1, SSM_KSTATE)),
            const((SSM_KBLK, 1, SSM_KSTATE)),
            const((1, SSM_WIDTH)),
            const((SSM_WIDTH, SSM_WIDTH)),
            const((1, SSM_WIDTH)),
        ],
        out_specs=[
            pl.BlockSpec((bsz, tt, SSM_WIDTH), lambda i: (0, i, 0)),
            const((SSM_KBLK, bsz, 2 * SSM_KSTATE)),
        ],
        out_shape=[
            jax.ShapeDtypeStruct((bsz, seq, SSM_WIDTH), BF16),
            jax.ShapeDtypeStruct((SSM_KBLK, bsz, 2 * SSM_KSTATE), F32),
        ],
        scratch_shapes=[
            pltpu.VMEM((SSM_KBLK, rows, LANES), F32),
            pltpu.VMEM((SSM_KBLK, rows, 2 * SSM_KSTATE), F32),
            pltpu.VMEM((SSM_KBLK, rows, 2 * SSM_KSTATE), BF16),
            pltpu.VMEM((SSM_KBLK, rows, LANES), F32),
            pltpu.VMEM((rows, SSM_WIDTH), F32),
        ],
        compiler_params=_cparams(("arbitrary",)),
        name="ssm_prompt",
    )(u, prm["wb"], prm["wc"], prm["a_re"], prm["a_im"], prm["dsk"], prm["wglu"], prm["bglu"])


def _ssm_sample_kernel(u_ref, h0_ref, wb_ref, wc_ref, are_ref, aim_ref, dsk_ref, wglu_ref, bglu_ref,
                       o_ref, st_ref):
    u = u_ref[...]
    ys = []
    for k in range(SSM_KBLK):
        b_t = _mm(u[:, k * LANES:(k + 1) * LANES], wb_ref[k], True)
        h0 = h0_ref[k]
        re, im = h0[:, :SSM_KSTATE], h0[:, SSM_KSTATE:]
        ar, ai = are_ref[k], aim_ref[k]
        nre = ar * re - ai * im + b_t[:, :SSM_KSTATE]
        nim = ar * im + ai * re + b_t[:, SSM_KSTATE:]
        s = jnp.concatenate([nre, nim], axis=1)
        st_ref[k] = s
        ys.append(_mm(s, wc_ref[k], True))
    y = jnp.concatenate(ys, axis=1)
    o_ref[...] = _ssm_tail(y, u, dsk_ref[...], wglu_ref[...], bglu_ref[...], precise=True)


def ssm_sample(u, h0, prm):
    bsz = u.shape[0]
    return pl.pallas_call(
        _ssm_sample_kernel,
        out_shape=[jax.ShapeDtypeStruct((bsz, SSM_WIDTH), F32),
                   jax.ShapeDtypeStruct((SSM_KBLK, bsz, 2 * SSM_KSTATE), F32)],
        compiler_params=pltpu.CompilerParams(vmem_limit_bytes=VMEM_LIMIT),
        name="ssm_sample",
    )(u, h0, prm["wb_f32"], prm["wc_f32"], prm["a_re"], prm["a_im"], prm["dsk"], prm["wglu_f32"], prm["bglu"])


def ssm_params(lam_re, lam_im, log_dt, b_re, b_im, c_re, c_im, d_skip, w_glu, b_glu):
    dt = jnp.exp(log_dt)[:, None]
    mag = jnp.exp(lam_re * dt)
    ab_re = mag * jnp.cos(lam_im * dt)
    ab_im = mag * jnp.sin(lam_im * dt)
    den = lam_re * lam_re + lam_im * lam_im
    nr = ab_re - 1.0
    ni = ab_im
    z_re = ((nr * lam_re + ni * lam_im) / den)[..., None]
    z_im = ((ni * lam_re - nr * lam_im) / den)[..., None]
    bb_re = z_re * b_re - z_im * b_im
    bb_im = z_re * b_im + z_im * b_re
    gpb = SSM_GROUPS // SSM_KBLK
    eye = jnp.eye(gpb, dtype=F32)

    def in_mat(bb):
        bbk = bb.reshape(SSM_KBLK, gpb, SSM_STATE, SSM_GROUP_CH)
        m = jnp.einsum("kgpc,gh->kgchp", bbk, eye)
        return m.reshape(SSM_KBLK, LANES, SSM_KSTATE)

    def out_mat(c):
        ck = c.reshape(SSM_KBLK, gpb, SSM_GROUP_CH, SSM_STATE)
        m = jnp.einsum("kgcp,gh->kgphc", ck, eye)
        return m.reshape(SSM_KBLK, SSM_KSTATE, LANES)

    wb = jnp.concatenate([in_mat(bb_re), in_mat(bb_im)], axis=2)
    wc = jnp.concatenate([out_mat(c_re), -out_mat(c_im)], axis=1)
    return dict(
        wb=wb.astype(BF16), wc=wc.astype(BF16), wb_f32=wb, wc_f32=wc, wglu_f32=w_glu,
        a_re=ab_re.reshape(SSM_KBLK, 1, SSM_KSTATE), a_im=ab_im.reshape(SSM_KBLK, 1, SSM_KSTATE),
        dsk=d_skip.reshape(1, SSM_WIDTH), wglu=w_glu.astype(BF16), bglu=b_glu.reshape(1, SSM_WIDTH))


def _state_to_blocks(h):
    bsz = h.shape[0]
    return h.reshape(bsz, SSM_KBLK, SSM_KSTATE).transpose(1, 0, 2)


def _blocks_to_state(s):
    bsz = s.shape[1]
    return s.transpose(1, 0, 2).reshape(bsz, SSM_GROUPS, SSM_STATE)


def _mem_attn_kernel(q_ref, k_ref, v_ref, o_ref, *, precise):
    scale = MEM_HEAD_DIM ** -0.5
    for h in range(MEM_HEADS):
        sl = slice(h * MEM_HEAD_DIM, (h + 1) * MEM_HEAD_DIM)
        s = _mm(q_ref[:, sl], k_ref[:, sl], precise, NT_DIMS) * scale
        m = jnp.max(s, axis=-1, keepdims=True)
        p = jnp.exp(s - m)
        den = jnp.sum(p, axis=-1, keepdims=True)
        o_ref[:, sl] = _mm(p / den, v_ref[:, sl], precise).astype(o_ref.dtype)


def mem_attention(q, mk, mv, tl, out_dtype=F32, precise=False):
    bsz, seq, _ = q.shape
    tl = min(tl, seq)
    return pl.pallas_call(
        functools.partial(_mem_attn_kernel, precise=precise),
        grid=(bsz, seq // tl),
        in_specs=[
            pl.BlockSpec((None, tl, MEM_WIDTH), lambda b, i: (b, i, 0)),
            pl.BlockSpec((None, MEM_TOKENS, MEM_WIDTH), lambda b, i: (b, 0, 0)),
            pl.BlockSpec((None, MEM_TOKENS, MEM_WIDTH), lambda b, i: (b, 0, 0)),
        ],
        out_specs=pl.BlockSpec((None, tl, MEM_WIDTH), lambda b, i: (b, i, 0)),
        out_shape=jax.ShapeDtypeStruct((bsz, seq, MEM_WIDTH), out_dtype),
        compiler_params=_cparams(("arbitrary", "arbitrary")),
        name="mem_attention",
    )(q, mk, mv)


def _route(logits, seen_ref):
    lane = lax.broadcasted_iota(jnp.int32, logits.shape, 1)
    lanef = lane.astype(F32)
    none = float(ROUTER_COLS)
    neg = -jnp.inf
    gl = jnp.where(lane < MOE_GROUPS, logits, neg)
    gmax = jnp.max(gl, axis=-1, keepdims=True)
    grp = jnp.min(jnp.where(gl == gmax, lanef, none), axis=-1, keepdims=True)
    p_grp = 1.0 / jnp.sum(jnp.exp(gl - gmax), axis=-1, keepdims=True)
    lo = MOE_GROUPS + grp * EXPERTS_PER_GROUP
    el = jnp.where(jnp.logical_and(lanef >= lo, lanef < lo + EXPERTS_PER_GROUP), logits, neg)
    v1 = jnp.max(el, axis=-1, keepdims=True)
    i1 = jnp.min(jnp.where(el == v1, lanef, none), axis=-1, keepdims=True)
    el2 = jnp.where(lanef == i1, neg, el)
    v2 = jnp.max(el2, axis=-1, keepdims=True)
    i2 = jnp.min(jnp.where(el2 == v2, lanef, none), axis=-1, keepdims=True)
    t = jnp.exp(v2 - v1)
    g1 = p_grp / (1.0 + t)
    g2 = g1 * t
    e1 = i1 - MOE_GROUPS
    e2 = i2 - MOE_GROUPS
    oh1 = (lanef == e1).astype(F32)
    oh2 = (lanef == e2).astype(F32)
    both = oh1 + oh2
    tm = logits.shape[0]
    earlier = (lax.broadcasted_iota(jnp.int32, (tm, tm), 1) < lax.broadcasted_iota(jnp.int32, (tm, tm), 0))
    before = jnp.dot(earlier.astype(BF16), both.astype(BF16), preferred_element_type=F32) + seen_ref[...]
    r1 = jnp.sum(oh1 * before, axis=-1, keepdims=True)
    r2 = jnp.sum(oh2 * before, axis=-1, keepdims=True)
    seen_ref[...] += jnp.sum(both, axis=0, keepdims=True)
    out = jnp.where(lane == 0, g1, 0.0)
    out = jnp.where(lane == 1, g2, out)
    out = jnp.where(lane == 2, e1, out)
    out = jnp.where(lane == 3, e2, out)
    out = jnp.where(lane == 4, r1, out)
    return jnp.where(lane == 5, r2, out)


def _merge_kernel(att_ref, ssm_ref, mem_ref, gate_ref, x_ref, wa_ref, ws_ref, wm_ref, wo_ref,
                  n2_ref, wr_ref, br_ref, h_ref, hn_ref, lg_ref, seen_ref, *, tm, precise):
    @pl.when(pl.program_id(0) == 0)
    def _():
        seen_ref[...] = jnp.zeros_like(seen_ref)

    a = _mm(att_ref[...], wa_ref[...], precise)
    merged = gate_ref[:, 0:D_MODEL] * a
    s = _mm(ssm_ref[...], ws_ref[...], precise)
    merged = merged + gate_ref[:, D_MODEL:2 * D_MODEL] * s
    m = _mm(mem_ref[...], wm_ref[...], precise)
    merged = merged + gate_ref[:, 2 * D_MODEL:3 * D_MODEL] * m
    h = x_ref[...] + _mm(merged, wo_ref[...], precise)
    h_ref[...] = h
    hn = _rms(h, n2_ref[...])
    for j in range(D_MODEL // LANES):
        hn_ref[pl.ds(j, tm, stride=SUBLANES), :] = hn[:, j * LANES:(j + 1) * LANES]
    logits = _mm(hn, wr_ref[...], precise) + br_ref[...]
    lg_ref[...] = _route(logits, seen_ref)


def merge(att, ssm, mem, gates, x, wts, tm, precise=False):
    n = x.shape[0]
    tm = min(tm, n)
    row = lambda w: pl.BlockSpec((tm, w), lambda i: (i, 0))
    const = lambda a: pl.BlockSpec(a.shape, lambda i: (0, 0))
    ws = [wts["wa"], wts["ws"], wts["wm"], wts["wo"], wts["n2"], wts["wr"], wts["br"]]
    return pl.pallas_call(
        functools.partial(_merge_kernel, tm=tm, precise=precise),
        grid=(n // tm,),
        in_specs=[row(ATT_OUT), row(SSM_WIDTH), row(MEM_WIDTH), row(N_BRANCHES * D_MODEL), row(D_MODEL)]
                 + [const(w) for w in ws],
        out_specs=[row(D_MODEL), pl.BlockSpec((tm * SUBLANES, LANES), lambda i: (i, 0)), row(ROUTER_COLS)],
        out_shape=[jax.ShapeDtypeStruct((n, D_MODEL), F32),
                   jax.ShapeDtypeStruct((n * SUBLANES, LANES), F32),
                   jax.ShapeDtypeStruct((n, ROUTER_COLS), F32)],
        scratch_shapes=[pltpu.VMEM((1, ROUTER_COLS), F32)],
        compiler_params=_cparams(("arbitrary",)),
        name="merge",
    )(att, ssm, mem, gates, x, *ws)


TABLE_WORDS = SUBLANES * LANES


def _row_dma_loop(n, off, smem, slot, start):
    inner = min(n, LANES)
    assert n % inner == 0

    def body(r, carry):
        for j in range(inner):
            f = off + j
            start(smem[slot, f // LANES + r, f % LANES], r * inner + j, j % 2)
        return carry

    if n == inner:
        body(0, 0)
    else:
        lax.fori_loop(0, n // inner, body, 0)


def _tile_rows(i):
    start = i * SUBLANES
    return pl.ds(start if isinstance(start, int) else pl.multiple_of(start, SUBLANES), SUBLANES)


def _dispatch_kernel(pos_hbm, hn_ref, xs_hbm, pos_smem, zero, sem, psem, *, tm, n_pad):
    i = pl.program_id(0)
    slot = i % 2

    def pos_copy(b, s):
        return pltpu.make_async_copy(pos_hbm.at[b], pos_smem.at[s], psem.at[s])

    @pl.when(i == 0)
    def _():
        pos_copy(0, 0).start()

    @pl.when(i + 1 < pl.num_programs(0))
    def _():
        pos_copy(i + 1, 1 - slot).start()

    zero[...] = jnp.zeros_like(zero)
    pos_copy(i, slot).wait()

    def send_token(row, tok, prio):
        pltpu.make_async_copy(hn_ref.at[_tile_rows(tok)], xs_hbm.at[row], sem).start(priority=prio)

    def send_zero(row, _, prio):
        pltpu.make_async_copy(zero, xs_hbm.at[row], sem).start(priority=prio)

    for k in range(TOP_K):
        _row_dma_loop(tm, k * tm, pos_smem, slot, send_token)
    _row_dma_loop(n_pad, TOP_K * tm, pos_smem, slot, send_zero)
    whole, part = divmod(n_pad, tm)
    for _ in range(TOP_K + whole):
        pltpu.make_async_copy(hn_ref, hn_ref, sem).wait()
    if part:
        piece = hn_ref.at[pl.ds(0, part * SUBLANES)]
        pltpu.make_async_copy(piece, piece, sem).wait()


def moe_dispatch(pos, hn2d, n_rows, tm, n_pad):
    n_tiles = pos.shape[0]
    return pl.pallas_call(
        functools.partial(_dispatch_kernel, tm=tm, n_pad=n_pad),
        grid=(n_tiles,),
        in_specs=[pl.BlockSpec(memory_space=pl.ANY),
                  pl.BlockSpec((tm * SUBLANES, LANES), lambda i: (i, 0))],
        out_specs=pl.BlockSpec(memory_space=pl.ANY),
        out_shape=jax.ShapeDtypeStruct((n_rows, SUBLANES, LANES), F32),
        scratch_shapes=[pltpu.SMEM((2, SUBLANES, LANES), jnp.int32),
                        pltpu.VMEM((SUBLANES, LANES), F32),
                        pltpu.SemaphoreType.DMA, pltpu.SemaphoreType.DMA((2,))],
        compiler_params=_cparams(("arbitrary",)),
        name="moe_dispatch",
    )(pos, hn2d)


def _moe_kernel(be_ref, dst_hbm, xs_ref, w1_ref, w3_ref, w2_ref, out_hbm,
                dst_smem, ybuf, w1b, w3b, w2b, ssem, isem, *, bm):
    blk = pl.program_id(0)
    last = pl.num_programs(0) - 1
    slot = blk % 2
    tiles = D_MODEL // LANES

    def table_copy(entry, s):
        return pltpu.make_async_copy(dst_hbm.at[entry], dst_smem.at[s], isem.at[s])

    def wait_rows(s):
        pltpu.make_async_copy(ybuf.at[0], ybuf.at[0], ssem.at[s]).wait()

    def send_rows(buf, s):
        for i in range(bm):
            row = dst_smem[s, i // LANES, i % LANES]
            pltpu.make_async_copy(ybuf.at[buf, _tile_rows(i)], out_hbm.at[row], ssem.at[s]).start(priority=i % 2)

    @pl.when(blk == 0)
    def _():
        ybuf[1] = jnp.zeros(ybuf.shape[1:], F32)
        table_copy(0, 0).start()

    table_copy(blk + 1, 1 - slot).start()

    prev = jnp.maximum(blk - 1, 0)
    changed = jnp.logical_or(blk == 0, be_ref[blk] != be_ref[prev])

    @pl.when(changed)
    def _():
        w1b[...] = w1_ref[...].astype(BF16)
        w3b[...] = w3_ref[...].astype(BF16)
        w2b[...] = w2_ref[...].astype(BF16)

    table_copy(blk, slot).wait()
    x = jnp.concatenate([xs_ref[pl.ds(j, bm, stride=SUBLANES), :].astype(BF16) for j in range(tiles)], axis=1)
    h1 = jnp.dot(x, w1b[...], preferred_element_type=F32)
    send_rows(1 - slot, slot)
    h3 = jnp.dot(x, w3b[...], preferred_element_type=F32)
    act = (jax.nn.silu(h1) * h3).astype(BF16)
    y = jnp.dot(act, w2b[...], preferred_element_type=F32)

    @pl.when(blk > 0)
    def _():
        wait_rows(1 - slot)

    for j in range(tiles):
        ybuf[slot, pl.ds(j, bm, stride=SUBLANES), :] = y[:, j * LANES:(j + 1) * LANES]

    @pl.when(blk == last)
    def _():
        table_copy(blk + 1, 1 - slot).wait()
        send_rows(slot, 1 - slot)
        wait_rows(slot)
        wait_rows(1 - slot)


def moe_experts(block_expert, dst, xs2d, w1, w3, w2, n_out_rows, bm):
    n_blocks = dst.shape[0] - 1
    grid_spec = pltpu.PrefetchScalarGridSpec(
        num_scalar_prefetch=1,
        grid=(n_blocks,),
        in_specs=[
            pl.BlockSpec(memory_space=pl.ANY),
            pl.BlockSpec((bm * SUBLANES, LANES), lambda b, be: (b, 0)),
            pl.BlockSpec((None, D_MODEL, D_EXPERT), lambda b, be: (be[b], 0, 0)),
            pl.BlockSpec((None, D_MODEL, D_EXPERT), lambda b, be: (be[b], 0, 0)),
            pl.BlockSpec((None, D_EXPERT, D_MODEL), lambda b, be: (be[b], 0, 0)),
        ],
        out_specs=pl.BlockSpec(memory_space=pl.ANY),
        scratch_shapes=[
            pltpu.SMEM((2, SUBLANES, LANES), jnp.int32),
            pltpu.VMEM((2, bm * SUBLANES, LANES), F32),
            pltpu.VMEM((D_MODEL, D_EXPERT), BF16),
            pltpu.VMEM((D_MODEL, D_EXPERT), BF16),
            pltpu.VMEM((D_EXPERT, D_MODEL), BF16),
            pltpu.SemaphoreType.DMA((2,)),
            pltpu.SemaphoreType.DMA((2,)),
        ],
    )
    return pl.pallas_call(
        functools.partial(_moe_kernel, bm=bm),
        grid_spec=grid_spec,
        out_shape=jax.ShapeDtypeStruct((n_out_rows, SUBLANES, LANES), F32),
        compiler_params=_cparams(("arbitrary",)),
        name="moe_experts",
    )(block_expert, dst, xs2d, w1, w3, w2)


def _as_tables(cols, width):
    fill = jnp.zeros((cols.shape[0], TABLE_WORDS - width), jnp.int32)
    return jnp.concatenate([cols, fill], axis=1).reshape(cols.shape[0], SUBLANES, LANES)


def route_tables(route, bm, tm):
    n_tok = route.shape[0]
    n_pairs = n_tok * TOP_K
    e_slot = [route[:, 2 + k].astype(jnp.int32) for k in range(TOP_K)]
    rank = [route[:, 2 + TOP_K + k].astype(jnp.int32) for k in range(TOP_K)]
    e_flat = jnp.stack(e_slot, axis=1).reshape(n_pairs)
    experts = jnp.arange(N_EXPERTS, dtype=jnp.int32)
    counts = jnp.sum(experts[:, None] == e_flat[None, :], axis=1, dtype=jnp.int32)

    def lookup(table, idx):
        return jnp.sum(jnp.where(experts[:, None] == idx[None, :], table[:, None], 0), axis=0)

    padded = (counts + bm - 1) // bm * bm
    pad_ends = jnp.cumsum(padded)
    n_blocks = -(-n_pairs // bm) + N_EXPERTS
    n_rows = n_blocks * bm
    row0 = jnp.arange(n_blocks, dtype=jnp.int32) * bm
    block_expert = jnp.minimum(jnp.sum(pad_ends[None, :] <= row0[:, None], axis=1, dtype=jnp.int32),
                               N_EXPERTS - 1)
    n_fill = n_rows - n_pairs
    fill_id = jnp.arange(n_fill, dtype=jnp.int32)
    fill_e = jnp.minimum(fill_id // bm, N_EXPERTS - 1)
    fill_used = jnp.logical_and(fill_id < N_EXPERTS * bm, fill_id % bm < (padded - counts)[fill_e])
    keys = jnp.concatenate([e_flat * 2, jnp.where(fill_used, fill_e * 2 + 1, 2 * N_EXPERTS)])
    vals = jnp.concatenate([jnp.arange(n_pairs, dtype=jnp.int32), jnp.full((n_fill,), -1, jnp.int32)])
    _, pair_at = lax.sort((keys, vals), num_keys=1, is_stable=True)
    pad_starts = pad_ends - padded
    pair_row = [lookup(pad_starts, e_slot[k]) + rank[k] for k in range(TOP_K)]
    n_used_before = jnp.cumsum(fill_used.astype(jnp.int32)) - fill_used.astype(jnp.int32)
    pad_row = jnp.where(fill_used, (pad_starts + counts)[fill_e] + fill_id % bm,
                        pad_ends[-1] + fill_id - n_used_before)
    n_tiles = n_tok // tm
    n_pad = n_fill // n_tiles
    assert n_pad * n_tiles == n_fill and n_pad % SUBLANES == 0
    pos = jnp.concatenate([pair_row[k].reshape(n_tiles, tm) for k in range(TOP_K)]
                          + [pad_row.reshape(n_tiles, n_pad)], axis=1)
    pair_at = pair_at.reshape(n_blocks, bm)
    valid = pair_at >= 0
    bank = (jnp.arange(-1, n_blocks, dtype=jnp.int32) % 2)[:, None] * bm
    trash = n_pairs + bank + jnp.arange(bm, dtype=jnp.int32)[None, :]
    dst = jnp.where(valid, (pair_at % TOP_K) * n_tok + pair_at // TOP_K, trash[1:])
    dst = jnp.concatenate([trash[:1], dst], axis=0)
    return (block_expert, _as_tables(dst, bm), _as_tables(pos, TOP_K * tm + n_pad), n_pad, n_rows,
            n_pairs + 2 * bm)


def _final_kernel(h_ref, p0_ref, p1_ref, gw_ref, nf_ref, y_ref, *, tm):
    tiles = D_MODEL // LANES
    p0 = jnp.concatenate([p0_ref[pl.ds(j, tm, stride=SUBLANES), :] for j in range(tiles)], axis=1)
    p1 = jnp.concatenate([p1_ref[pl.ds(j, tm, stride=SUBLANES), :] for j in range(tiles)], axis=1)
    gw = gw_ref[...]
    h = h_ref[...] + (gw[:, 0:1] * p0 + gw[:, 1:2] * p1)
    y_ref[...] = _rms(h, nf_ref[...])


def final(h, pairs2d, gate_pad, norm_f, tm):
    n = h.shape[0]
    tm = min(tm, n)
    nb = n // tm
    row = lambda w: pl.BlockSpec((tm, w), lambda i: (i, 0))
    return pl.pallas_call(
        functools.partial(_final_kernel, tm=tm),
        grid=(nb,),
        in_specs=[
            row(D_MODEL),
            pl.BlockSpec((tm * SUBLANES, LANES), lambda i: (i, 0)),
            pl.BlockSpec((tm * SUBLANES, LANES), lambda i: (nb + i, 0)),
            row(LANES),
            pl.BlockSpec((1, D_MODEL), lambda i: (0, 0)),
        ],
        out_specs=row(D_MODEL),
        out_shape=jax.ShapeDtypeStruct((n, D_MODEL), F32),
        compiler_params=_cparams(("arbitrary",)),
        name="final",
    )(h, pairs2d, pairs2d, gate_pad, norm_f.reshape(1, D_MODEL))


def _alibi_slopes():
    h = jnp.arange(1, N_ATT_HEADS + 1, dtype=F32)
    return jnp.exp2(-8.0 * h / N_ATT_HEADS)


def _token_stage(att, ssm, mem, gates, x2, wts, w1, w3, w2, norm_f, tm, bm, precise=False):
    n = x2.shape[0]
    h, hn2d, route = merge(att, ssm, mem, gates, x2, wts, tm, precise)
    tm = min(tm, n)
    block_expert, dst, pos, n_pad, n_rows, out_rows = route_tables(route, bm, tm)
    xs = moe_dispatch(pos, hn2d, n_rows, tm, n_pad)
    pairs = moe_experts(block_expert, dst, xs.reshape(n_rows * SUBLANES, LANES), w1, w3, w2, out_rows, bm)
    return final(h, pairs.reshape(out_rows * SUBLANES, LANES), route, norm_f, tm)


def kernel(x_prompt, x_sample, cache_swa0_k, cache_swa0_v, cache_swa1_k, cache_swa1_v, cache_swa2_k, cache_swa2_v, cache_mem_k, cache_mem_v, state_ssm_re, state_ssm_im, mem_prompt, norm1, w_in, lam_re, lam_im, log_dt, ssm_b_re, ssm_b_im, ssm_c_re, ssm_c_im, ssm_d, w_glu, b_glu, w_att_out, w_ssm_out, w_mem_out, w_o, norm_mem, w_mk, w_mv, norm2, w_grp, b_grp, w_exp, b_exp, w1, w3, w2, norm_f):
    assert norm1.shape[0] == 1, "single-layer trunk"
    bsz, seq, _ = x_prompt.shape
    sbz = x_sample.shape[0]
    n_p = bsz * seq
    caches = (cache_swa0_k[0], cache_swa0_v[0], cache_swa1_k[0], cache_swa1_v[0], cache_swa2_k[0], cache_swa2_v[0])

    w_in_b = w_in[0].astype(BF16)
    w_mkv_b = jnp.concatenate([w_mk[0], w_mv[0]], axis=1).astype(BF16)
    wr = jnp.zeros((D_MODEL, ROUTER_COLS), F32)
    wr = wr.at[:, :MOE_GROUPS].set(w_grp[0]).at[:, MOE_GROUPS:MOE_GROUPS + N_EXPERTS].set(w_exp[0])
    br = jnp.zeros((1, ROUTER_COLS), F32)
    br = br.at[0, :MOE_GROUPS].set(b_grp[0]).at[0, MOE_GROUPS:MOE_GROUPS + N_EXPERTS].set(b_exp[0])
    wts_f32 = dict(wa=w_att_out[0], ws=w_ssm_out[0], wm=w_mem_out[0], wo=w_o[0],
                   n2=norm2[0].reshape(1, D_MODEL), wr=wr, br=br)
    wts = {k: (v.astype(BF16) if k in ("wa", "ws", "wm", "wo", "wr") else v) for k, v in wts_f32.items()}
    prm = ssm_params(lam_re[0], lam_im[0], log_dt[0], ssm_b_re[0], ssm_b_im[0], ssm_c_re[0], ssm_c_im[0],
                     ssm_d[0], w_glu[0], b_glu[0])
    slopes = _alibi_slopes()
    splits = (QKV_COLS, SSM_WIDTH, MEM_WIDTH, N_BRANCHES * D_MODEL)
    sig = (False, False, False, True)

    x2 = x_prompt.reshape(n_p, D_MODEL)
    qkv, u, mq, gates = norm_matmul(x2, norm1[0], w_in_b, splits, sig, tm=256, dtypes=(F32, F32, BF16, BF16))
    qkv3 = qkv.reshape(bsz, seq, QKV_COLS)
    att = attn_prompt(qkv3, slopes)
    ssm, st = ssm_prompt(u.reshape(bsz, seq, SSM_WIDTH), prm)
    mk, mv = norm_matmul(mem_prompt.reshape(bsz * MEM_TOKENS, D_MODEL), norm_mem[0], w_mkv_b,
                         (MEM_WIDTH, MEM_WIDTH), (False, False), tm=256)
    mk3 = mk.reshape(bsz, MEM_TOKENS, MEM_WIDTH)
    mv3 = mv.reshape(bsz, MEM_TOKENS, MEM_WIDTH)
    mem = mem_attention(mq.reshape(bsz, seq, MEM_WIDTH), mk3, mv3, tl=512, out_dtype=BF16)
    y_p = _token_stage(att.reshape(n_p, ATT_OUT), ssm.reshape(n_p, SSM_WIDTH), mem.reshape(n_p, MEM_WIDTH),
                       gates, x2, wts, w1[0], w3[0], w2[0], norm_f, tm=256, bm=256)

    xs2 = x_sample.reshape(sbz, D_MODEL)
    qkv_s, u_s, mq_s, gates_s = norm_matmul(xs2, norm1[0], w_in[0], splits, sig, tm=sbz, precise=True)
    qkv5 = qkv_s.reshape(sbz, len(ATT_GROUPS), 3, HEADS_PER_GROUP, HEAD_DIM)
    att_s, *new_caches = attn_sample(qkv5, [jnp.transpose(c, (0, 2, 3, 1)) for c in caches])
    att_s = att_s.reshape(sbz, ATT_OUT)
    h0 = jnp.concatenate([_state_to_blocks(state_ssm_re[0]), _state_to_blocks(state_ssm_im[0])], axis=2)
    ssm_s, st_s = ssm_sample(u_s, h0, prm)
    cmk = cache_mem_k[0].reshape(sbz, MEM_TOKENS, MEM_WIDTH)
    cmv = cache_mem_v[0].reshape(sbz, MEM_TOKENS, MEM_WIDTH)
    mem_s = mem_attention(mq_s.reshape(sbz, 1, MEM_WIDTH), cmk, cmv, tl=1, precise=True).reshape(sbz, MEM_WIDTH)
    y_s = _token_stage(att_s, ssm_s, mem_s, gates_s, xs2, wts_f32, w1[0], w3[0], w2[0], norm_f, tm=sbz, bm=16,
                       precise=True)

    outs = [y_p.reshape(bsz, seq, D_MODEL), y_s.reshape(sbz, 1, D_MODEL)]
    for g, (window, _) in enumerate(ATT_GROUPS):
        keep = min(window, seq)
        for part in (1, 2):
            c0 = g * QKV_GROUP + part * ATT_OUT
            outs.append(qkv3[:, seq - keep:, c0:c0 + ATT_OUT].reshape(1, bsz, keep, HEADS_PER_GROUP, HEAD_DIM))
    outs.append(mk3.reshape(1, bsz, MEM_TOKENS, MEM_HEADS, MEM_HEAD_DIM))
    outs.append(mv3.reshape(1, bsz, MEM_TOKENS, MEM_HEADS, MEM_HEAD_DIM))
    outs.append(_blocks_to_state(st[:, :, :SSM_KSTATE])[None])
    outs.append(_blocks_to_state(st[:, :, SSM_KSTATE:])[None])
    outs.extend(jnp.transpose(c, (0, 3, 1, 2))[None] for c in new_caches)
    outs.append(_blocks_to_state(st_s[:, :, :SSM_KSTATE])[None])
    outs.append(_blocks_to_state(st_s[:, :, SSM_KSTATE:])[None])
    return tuple(outs)
```

```python
import functools
import math

import jax
import jax.numpy as jnp
from jax import lax
from jax.experimental import pallas as pl
from jax.experimental.pallas import tpu as pltpu

F32 = jnp.float32
BF16 = jnp.bfloat16

D_MODEL = 1024
ATT_GROUPS = ((128, 1), (512, 4), (2048, 16))
HEADS_PER_GROUP = 4
HEAD_DIM = 64
N_ATT_HEADS = len(ATT_GROUPS) * HEADS_PER_GROUP
ATT_OUT = HEADS_PER_GROUP * HEAD_DIM
QKV_GROUP = 3 * ATT_OUT
QKV_COLS = len(ATT_GROUPS) * QKV_GROUP
SSM_WIDTH = D_MODEL // 2
SSM_GROUP_CH = 16
SSM_GROUPS = SSM_WIDTH // SSM_GROUP_CH
SSM_STATE = 64
SSM_NSTATE = SSM_GROUPS * SSM_STATE
MEM_TOKENS = 256
MEM_HEADS = 4
MEM_HEAD_DIM = D_MODEL // 8
MEM_WIDTH = MEM_HEADS * MEM_HEAD_DIM
N_BRANCHES = 3
MOE_GROUPS = 4
EXPERTS_PER_GROUP = 8
N_EXPERTS = MOE_GROUPS * EXPERTS_PER_GROUP
TOP_K = 2
D_EXPERT = D_MODEL // 2
RMS_EPS = 1e-6

LANES = 128
SUBLANES = 8
Q_TILE = 128
ROUTER_COLS = 128
NEG_BIG = -1e30
VMEM_LIMIT = 56 * 1024 * 1024


def _cparams(sem):
    return pltpu.CompilerParams(dimension_semantics=sem, vmem_limit_bytes=VMEM_LIMIT)


def _mm(a, b, precise, dims=None):
    if precise:
        a, b, kw = a.astype(F32), b.astype(F32), dict(precision=lax.Precision.HIGHEST)
    else:
        a, b, kw = a.astype(BF16), b.astype(BF16), {}
    if dims is None:
        return jnp.dot(a, b, preferred_element_type=F32, **kw)
    return lax.dot_general(a, b, dims, preferred_element_type=F32, **kw)


NT_DIMS = (((1,), (1,)), ((), ()))


def _rms(x, g):
    ms = jnp.mean(x * x, axis=-1, keepdims=True)
    return (x * lax.rsqrt(ms + RMS_EPS)) * g


def _norm_matmul_kernel(x_ref, g_ref, w_ref, *out_refs, splits, sigmoid, chunk, precise):
    xb = _rms(x_ref[...], g_ref[...])
    if not precise:
        xb = xb.astype(BF16)
    c0 = 0
    for o_ref, n, sg in zip(out_refs, splits, sigmoid):
        for j in range(0, n, chunk):
            w = min(chunk, n - j)
            y = _mm(xb, w_ref[:, c0 + j:c0 + j + w], precise)
            if sg:
                y = jax.nn.sigmoid(y)
            o_ref[:, j:j + w] = y.astype(o_ref.dtype)
        c0 += n


def norm_matmul(x, g, w, splits, sigmoid, tm, dtypes=None, precise=False):
    n, d = x.shape
    tm = min(tm, n)
    kern = functools.partial(_norm_matmul_kernel, splits=tuple(splits), sigmoid=tuple(sigmoid), chunk=512,
                             precise=precise)
    return pl.pallas_call(
        kern,
        grid=(n // tm,),
        in_specs=[
            pl.BlockSpec((tm, d), lambda i: (i, 0)),
            pl.BlockSpec((1, d), lambda i: (0, 0)),
            pl.BlockSpec((d, w.shape[1]), lambda i: (0, 0), pipeline_mode=pl.Buffered(1)),
        ],
        out_specs=[pl.BlockSpec((tm, s), lambda i: (i, 0)) for s in splits],
        out_shape=[jax.ShapeDtypeStruct((n, s), dt) for s, dt in zip(splits, dtypes or (F32,) * len(splits))],
        compiler_params=_cparams(("arbitrary",)),
        name="norm_matmul",
    )(x, g.reshape(1, d), w)


def _attn_prompt_kernel(slopes_ref, *refs, seq):
    qkv_refs = refs[:9]
    o_ref = refs[9]
    o_scr, l_scr = refs[10], refs[11]
    hp = pl.program_id(1)
    scale = HEAD_DIM ** -0.5
    row = lax.broadcasted_iota(jnp.int32, (Q_TILE, 2 * Q_TILE), 0)
    col = lax.broadcasted_iota(jnp.int32, (Q_TILE, 2 * Q_TILE), 1)
    dist = (row + Q_TILE - col)
    in_window = jnp.logical_and(dist >= 0, dist <= Q_TILE)
    distf = dist.astype(F32)
    lane = lax.broadcasted_iota(jnp.int32, (Q_TILE, LANES), 1)
    head_a = lane < HEAD_DIM

    for g, (_, dil) in enumerate(ATT_GROUPS):
        q_ref, k_ref, v_ref = qkv_refs[3 * g:3 * g + 3]
        sub_len = seq // dil
        n_blk = sub_len // Q_TILE
        slope_a = slopes_ref[g * HEADS_PER_GROUP + 2 * hp] * float(dil)
        slope_b = slopes_ref[g * HEADS_PER_GROUP + 2 * hp + 1] * float(dil)

        def rows(start):
            if dil == 1:
                return pl.ds(start, Q_TILE)
            return pl.ds(start, Q_TILE, stride=dil)

        def body(it, carry, q_ref=q_ref, k_ref=k_ref, v_ref=v_ref, n_blk=n_blk, dil=dil,
                 slope_a=slope_a, slope_b=slope_b, rows=rows, g=g):
            r = it // n_blk
            blk = it % n_blk
            prev = jnp.maximum(blk - 1, 0)
            q_start = r + dil * Q_TILE * blk
            p_start = r + dil * Q_TILE * prev
            q = q_ref[rows(q_start), :]
            k2 = jnp.concatenate([k_ref[rows(p_start), :], k_ref[rows(q_start), :]], axis=0).astype(BF16)
            v2 = jnp.concatenate([v_ref[rows(p_start), :], v_ref[rows(q_start), :]], axis=0).astype(BF16)
            valid = jnp.logical_and(in_window, jnp.logical_or(col >= Q_TILE, blk > 0))
            outs, lses = [], []
            for is_a, slope in ((True, slope_a), (False, slope_b)):
                hmask = head_a if is_a else jnp.logical_not(head_a)
                qh = jnp.where(hmask, q, 0.0).astype(BF16)
                s = lax.dot_general(qh, k2, NT_DIMS, preferred_element_type=F32)
                s = s * scale - slope * distf
                s = jnp.where(valid, s, NEG_BIG)
                m = jnp.max(s, axis=-1, keepdims=True)
                p = jnp.exp(s - m)
                den = jnp.sum(p, axis=-1, keepdims=True)
                pv = jnp.dot(p.astype(BF16), v2, preferred_element_type=F32)
                outs.append(pv / den)
                lses.append(m + jnp.log(den))
            o_scr[g, rows(q_start), :] = jnp.where(head_a, outs[0], outs[1])
            l_scr[g, rows(q_start), :] = jnp.where(head_a, lses[0], lses[1])
            return carry

        lax.fori_loop(0, dil * n_blk, body, 0, unroll=8)

    def combine(c, carry):
        sl = pl.ds(pl.multiple_of(c * 256, 256), 256)
        l0, l1, l2 = l_scr[0, sl, :], l_scr[1, sl, :], l_scr[2, sl, :]
        m = jnp.maximum(jnp.maximum(l0, l1), l2)
        e0, e1, e2 = jnp.exp(l0 - m), jnp.exp(l1 - m), jnp.exp(l2 - m)
        tot = e0 + e1 + e2
        att = (e0 * o_scr[0, sl, :] + e1 * o_scr[1, sl, :] + e2 * o_scr[2, sl, :]) / tot
        o_ref[sl, :] = att.astype(o_ref.dtype)
        return carry

    lax.fori_loop(0, seq // 256, combine, 0)


def attn_prompt(qkv, slopes):
    bsz, seq, _ = qkv.shape
    in_specs = []
    for g in range(len(ATT_GROUPS)):
        for part in range(3):
            base = (g * QKV_GROUP + part * ATT_OUT) // LANES
            in_specs.append(pl.BlockSpec((None, seq, LANES),
                                         lambda b, h, sl, base=base: (b, 0, base + h)))
    grid_spec = pltpu.PrefetchScalarGridSpec(
        num_scalar_prefetch=1,
        grid=(bsz, ATT_OUT // LANES),
        in_specs=in_specs,
        out_specs=pl.BlockSpec((None, seq, LANES), lambda b, h, sl: (b, 0, h)),
        scratch_shapes=[pltpu.VMEM((3, seq, LANES), F32), pltpu.VMEM((3, seq, LANES), F32)],
    )
    return pl.pallas_call(
        functools.partial(_attn_prompt_kernel, seq=seq),
        grid_spec=grid_spec,
        out_shape=jax.ShapeDtypeStruct((bsz, seq, ATT_OUT), BF16),
        compiler_params=_cparams(("arbitrary", "arbitrary")),
        name="attn_prompt",
    )(slopes, *([qkv] * 9))


def _attn_sample_kernel(qkv_ref, col_ref, *refs):
    cache_refs = refs[:6]
    o_ref = refs[6]
    new_refs = refs[7:13]
    scale = HEAD_DIM ** -0.5
    n_grp = len(ATT_GROUPS)
    outs = [[None] * HEADS_PER_GROUP for _ in range(n_grp)]
    lses = [[None] * HEADS_PER_GROUP for _ in range(n_grp)]
    for g, (window, dil) in enumerate(ATT_GROUPS):
        lane = lax.broadcasted_iota(jnp.int32, (1, window), 1)
        on_grid = jnp.bitwise_and(lane, dil - 1) == 0
        dist = (window - lane).astype(F32)
        last = lax.broadcasted_iota(jnp.int32, (HEAD_DIM, window), 1) == window - 1
        for h in range(HEADS_PER_GROUP):
            slope = 2.0 ** (-8.0 * (g * HEADS_PER_GROUP + h + 1) / N_ATT_HEADS)
            q = qkv_ref[g, 0][h:h + 1, :]
            kn = qkv_ref[g, 1][h:h + 1, :]
            vn = qkv_ref[g, 2][h:h + 1, :]
            kt = cache_refs[2 * g][h]
            vt = cache_refs[2 * g + 1][h]
            s = _mm(q, kt, True) * scale - slope * dist
            s = jnp.where(on_grid, s, NEG_BIG)
            sn = jnp.sum(q * kn, axis=-1, keepdims=True) * scale
            m = jnp.maximum(jnp.max(s, axis=-1, keepdims=True), sn)
            p = jnp.exp(s - m)
            pn = jnp.exp(sn - m)
            den = jnp.sum(p, axis=-1, keepdims=True) + pn
            outs[g][h] = (_mm(p, vt, True, NT_DIMS) + pn * vn) / den
            lses[g][h] = m + jnp.log(den)
            new_refs[2 * g][h] = jnp.where(last, col_ref[g, 0, h], pltpu.roll(kt, window - 1, axis=1))
            new_refs[2 * g + 1][h] = jnp.where(last, col_ref[g, 1, h], pltpu.roll(vt, window - 1, axis=1))
    for h in range(HEADS_PER_GROUP):
        m = jnp.maximum(jnp.maximum(lses[0][h], lses[1][h]), lses[2][h])
        es = [jnp.exp(lses[g][h] - m) for g in range(n_grp)]
        num = es[0] * outs[0][h] + es[1] * outs[1][h] + es[2] * outs[2][h]
        o_ref[h:h + 1, :] = num / (es[0] + es[1] + es[2])


def attn_sample(qkv5, caches_t):
    bsz = qkv5.shape[0]
    cols = qkv5[:, :, 1:3].reshape(bsz, len(ATT_GROUPS), 2, HEADS_PER_GROUP, HEAD_DIM, 1)
    in_specs = [pl.BlockSpec((None, 3, 3, HEADS_PER_GROUP, HEAD_DIM), lambda b: (b, 0, 0, 0, 0)),
                pl.BlockSpec((None, 3, 2, HEADS_PER_GROUP, HEAD_DIM, 1), lambda b: (b, 0, 0, 0, 0, 0))]
    cache_specs = [pl.BlockSpec((None,) + c.shape[1:], lambda b: (b, 0, 0, 0)) for c in caches_t]
    return pl.pallas_call(
        _attn_sample_kernel,
        grid=(bsz,),
        in_specs=in_specs + cache_specs,
        out_specs=[pl.BlockSpec((None, HEADS_PER_GROUP, HEAD_DIM), lambda b: (b, 0, 0))] + cache_specs,
        out_shape=[jax.ShapeDtypeStruct((bsz, HEADS_PER_GROUP, HEAD_DIM), F32)]
                  + [jax.ShapeDtypeStruct(c.shape, F32) for c in caches_t],
        compiler_params=_cparams(("arbitrary",)),
        name="attn_sample",
    )(qkv5, cols, *caches_t)


SSM_KBLK = SSM_WIDTH // LANES
SSM_KSTATE = SSM_NSTATE // SSM_KBLK


def _ssm_tail(y, u, dsk, wglu, bglu, precise=False):
    y = y + dsk * u
    z = jax.nn.gelu(y)
    gl = _mm(z, wglu, precise) + bglu
    return z * jax.nn.sigmoid(gl)


def _ssm_prompt_kernel(u_ref, wb_ref, wc_ref, are_ref, aim_ref, dsk_ref, wglu_ref, bglu_ref,
                       o_ref, st_ref, uperm, bu, sbf, yperm, ynat, *, bsz, tt):
    rows = bsz * tt

    @pl.when(pl.program_id(0) == 0)
    def _():
        st_ref[...] = jnp.zeros_like(st_ref)

    for b in range(bsz):
        ub = u_ref[b]
        for k in range(SSM_KBLK):
            uperm[k, pl.ds(b, tt, stride=bsz), :] = ub[:, k * LANES:(k + 1) * LANES]

    for k in range(SSM_KBLK):
        bu[k] = jnp.dot(uperm[k].astype(BF16), wb_ref[k], preferred_element_type=F32)

    for k in range(SSM_KBLK):
        ar = jnp.broadcast_to(are_ref[k], (bsz, SSM_KSTATE))
        ai = jnp.broadcast_to(aim_ref[k], (bsz, SSM_KSTATE))
        st = st_ref[k]

        def step(t, carry, k=k, ar=ar, ai=ai):
            re, im = carry
            r0 = pl.multiple_of(t * bsz, bsz)
            b_t = bu[k, pl.ds(r0, bsz), :]
            nre = ar * re - ai * im + b_t[:, :SSM_KSTATE]
            nim = ar * im + ai * re + b_t[:, SSM_KSTATE:]
            sbf[k, pl.ds(r0, bsz), :] = jnp.concatenate([nre, nim], axis=1).astype(BF16)
            return nre, nim

        re, im = lax.fori_loop(0, tt, step, (st[:, :SSM_KSTATE], st[:, SSM_KSTATE:]), unroll=2)
        st_ref[k] = jnp.concatenate([re, im], axis=1)

    for k in range(SSM_KBLK):
        yperm[k] = jnp.dot(sbf[k], wc_ref[k], preferred_element_type=F32)

    for b in range(bsz):
        for k in range(SSM_KBLK):
            ynat[b * tt:(b + 1) * tt, k * LANES:(k + 1) * LANES] = yperm[k, pl.ds(b, tt, stride=bsz), :]

    u2 = u_ref[...].reshape(rows, SSM_WIDTH)
    out = _ssm_tail(ynat[...], u2, dsk_ref[...], wglu_ref[...], bglu_ref[...])
    o_ref[...] = out.reshape(bsz, tt, SSM_WIDTH).astype(o_ref.dtype)


def ssm_prompt(u, prm, tt=32):
    bsz, seq, _ = u.shape
    rows = bsz * tt
    const = lambda shape: pl.BlockSpec(shape, lambda i: (0,) * len(shape))
    return pl.pallas_call(
        functools.partial(_ssm_prompt_kernel, bsz=bsz, tt=tt),
        grid=(seq // tt,),
        in_specs=[
            pl.BlockSpec((bsz, tt, SSM_WIDTH), lambda i: (0, i, 0)),
            const((SSM_KBLK, LANES, 2 * SSM_KSTATE)),
            const((SSM_KBLK, 2 * SSM_KSTATE, LANES)),
            const((SSM_KBLK, 1, SSM_KSTATE)),
            const((SSM_KBLK, 1, SSM_KSTATE)),
            const((1, SSM_WIDTH)),
            const((SSM_WIDTH, SSM_WIDTH)),
            const((1, SSM_WIDTH)),
        ],
        out_specs=[
            pl.BlockSpec((bsz, tt, SSM_WIDTH), lambda i: (0, i, 0)),
            const((SSM_KBLK, bsz, 2 * SSM_KSTATE)),
        ],
        out_shape=[
            jax.ShapeDtypeStruct((bsz, seq, SSM_WIDTH), BF16),
            jax.ShapeDtypeStruct((SSM_KBLK, bsz, 2 * SSM_KSTATE), F32),
        ],
        scratch_shapes=[
            pltpu.VMEM((SSM_KBLK, rows, LANES), F32),
            pltpu.VMEM((SSM_KBLK, rows, 2 * SSM_KSTATE), F32),
            pltpu.VMEM((SSM_KBLK, rows, 2 * SSM_KSTATE), BF16),
            pltpu.VMEM((SSM_KBLK, rows, LANES), F32),
            pltpu.VMEM((rows, SSM_WIDTH), F32),
        ],
        compiler_params=_cparams(("arbitrary",)),
        name="ssm_prompt",
    )(u, prm["wb"], prm["wc"], prm["a_re"], prm["a_im"], prm["dsk"], prm["wglu"], prm["bglu"])


def _ssm_sample_kernel(u_ref, h0_ref, wb_ref, wc_ref, are_ref, aim_ref, dsk_ref, wglu_ref, bglu_ref,
                       o_ref, st_ref):
    u = u_ref[...]
    ys = []
    for k in range(SSM_KBLK):
        b_t = _mm(u[:, k * LANES:(k + 1) * LANES], wb_ref[k], True)
        h0 = h0_ref[k]
        re, im = h0[:, :SSM_KSTATE], h0[:, SSM_KSTATE:]
        ar, ai = are_ref[k], aim_ref[k]
        nre = ar * re - ai * im + b_t[:, :SSM_KSTATE]
        nim = ar * im + ai * re + b_t[:, SSM_KSTATE:]
        s = jnp.concatenate([nre, nim], axis=1)
        st_ref[k] = s
        ys.append(_mm(s, wc_ref[k], True))
    y = jnp.concatenate(ys, axis=1)
    o_ref[...] = _ssm_tail(y, u, dsk_ref[...], wglu_ref[...], bglu_ref[...], precise=True)


def ssm_sample(u, h0, prm):
    bsz = u.shape[0]
    return pl.pallas_call(
        _ssm_sample_kernel,
        out_shape=[jax.ShapeDtypeStruct((bsz, SSM_WIDTH), F32),
                   jax.ShapeDtypeStruct((SSM_KBLK, bsz, 2 * SSM_KSTATE), F32)],
        compiler_params=pltpu.CompilerParams(vmem_limit_bytes=VMEM_LIMIT),
        name="ssm_sample",
    )(u, h0, prm["wb_f32"], prm["wc_f32"], prm["a_re"], prm["a_im"], prm["dsk"], prm["wglu_f32"], prm["bglu"])


def ssm_params(lam_re, lam_im, log_dt, b_re, b_im, c_re, c_im, d_skip, w_glu, b_glu):
    dt = jnp.exp(log_dt)[:, None]
    mag = jnp.exp(lam_re * dt)
    ab_re = mag * jnp.cos(lam_im * dt)
    ab_im = mag * jnp.sin(lam_im * dt)
    den = lam_re * lam_re + lam_im * lam_im
    nr = ab_re - 1.0
    ni = ab_im
    z_re = ((nr * lam_re + ni * lam_im) / den)[..., None]
    z_im = ((ni * lam_re - nr * lam_im) / den)[..., None]
    bb_re = z_re * b_re - z_im * b_im
    bb_im = z_re * b_im + z_im * b_re
    gpb = SSM_GROUPS // SSM_KBLK
    eye = jnp.eye(gpb, dtype=F32)

    def in_mat(bb):
        bbk = bb.reshape(SSM_KBLK, gpb, SSM_STATE, SSM_GROUP_CH)
        m = jnp.einsum("kgpc,gh->kgchp", bbk, eye)
        return m.reshape(SSM_KBLK, LANES, SSM_KSTATE)

    def out_mat(c):
        ck = c.reshape(SSM_KBLK, gpb, SSM_GROUP_CH, SSM_STATE)
        m = jnp.einsum("kgcp,gh->kgphc", ck, eye)
        return m.reshape(SSM_KBLK, SSM_KSTATE, LANES)

    wb = jnp.concatenate([in_mat(bb_re), in_mat(bb_im)], axis=2)
    wc = jnp.concatenate([out_mat(c_re), -out_mat(c_im)], axis=1)
    return dict(
        wb=wb.astype(BF16), wc=wc.astype(BF16), wb_f32=wb, wc_f32=wc, wglu_f32=w_glu,
        a_re=ab_re.reshape(SSM_KBLK, 1, SSM_KSTATE), a_im=ab_im.reshape(SSM_KBLK, 1, SSM_KSTATE),
        dsk=d_skip.reshape(1, SSM_WIDTH), wglu=w_glu.astype(BF16), bglu=b_glu.reshape(1, SSM_WIDTH))


def _state_to_blocks(h):
    bsz = h.shape[0]
    return h.reshape(bsz, SSM_KBLK, SSM_KSTATE).transpose(1, 0, 2)


def _blocks_to_state(s):
    bsz = s.shape[1]
    return s.transpose(1, 0, 2).reshape(bsz, SSM_GROUPS, SSM_STATE)


def _mem_attn_kernel(q_ref, k_ref, v_ref, o_ref, *, precise):
    scale = MEM_HEAD_DIM ** -0.5
    for h in range(MEM_HEADS):
        sl = slice(h * MEM_HEAD_DIM, (h + 1) * MEM_HEAD_DIM)
        s = _mm(q_ref[:, sl], k_ref[:, sl], precise, NT_DIMS) * scale
        m = jnp.max(s, axis=-1, keepdims=True)
        p = jnp.exp(s - m)
        den = jnp.sum(p, axis=-1, keepdims=True)
        o_ref[:, sl] = _mm(p / den, v_ref[:, sl], precise).astype(o_ref.dtype)


def mem_attention(q, mk, mv, tl, out_dtype=F32, precise=False):
    bsz, seq, _ = q.shape
    tl = min(tl, seq)
    return pl.pallas_call(
        functools.partial(_mem_attn_kernel, precise=precise),
        grid=(bsz, seq // tl),
        in_specs=[
            pl.BlockSpec((None, tl, MEM_WIDTH), lambda b, i: (b, i, 0)),
            pl.BlockSpec((None, MEM_TOKENS, MEM_WIDTH), lambda b, i: (b, 0, 0)),
            pl.BlockSpec((None, MEM_TOKENS, MEM_WIDTH), lambda b, i: (b, 0, 0)),
        ],
        out_specs=pl.BlockSpec((None, tl, MEM_WIDTH), lambda b, i: (b, i, 0)),
        out_shape=jax.ShapeDtypeStruct((bsz, seq, MEM_WIDTH), out_dtype),
        compiler_params=_cparams(("arbitrary", "arbitrary")),
        name="mem_attention",
    )(q, mk, mv)


def _route(logits, seen_ref):
    lane = lax.broadcasted_iota(jnp.int32, logits.shape, 1)
    lanef = lane.astype(F32)
    none = float(ROUTER_COLS)
    neg = -jnp.inf
    gl = jnp.where(lane < MOE_GROUPS, logits, neg)
    gmax = jnp.max(gl, axis=-1, keepdims=True)
    grp = jnp.min(jnp.where(gl == gmax, lanef, none), axis=-1, keepdims=True)
    p_grp = 1.0 / jnp.sum(jnp.exp(gl - gmax), axis=-1, keepdims=True)
    lo = MOE_GROUPS + grp * EXPERTS_PER_GROUP
    el = jnp.where(jnp.logical_and(lanef >= lo, lanef < lo + EXPERTS_PER_GROUP), logits, neg)
    v1 = jnp.max(el, axis=-1, keepdims=True)
    i1 = jnp.min(jnp.where(el == v1, lanef, none), axis=-1, keepdims=True)
    el2 = jnp.where(lanef == i1, neg, el)
    v2 = jnp.max(el2, axis=-1, keepdims=True)
    i2 = jnp.min(jnp.where(el2 == v2, lanef, none), axis=-1, keepdims=True)
    t = jnp.exp(v2 - v1)
    g1 = p_grp / (1.0 + t)
    g2 = g1 * t
    e1 = i1 - MOE_GROUPS
    e2 = i2 - MOE_GROUPS
    oh1 = (lanef == e1).astype(F32)
    oh2 = (lanef == e2).astype(F32)
    both = oh1 + oh2
    tm = logits.shape[0]
    earlier = (lax.broadcasted_iota(jnp.int32, (tm, tm), 1) < lax.broadcasted_iota(jnp.int32, (tm, tm), 0))
    before = jnp.dot(earlier.astype(BF16), both.astype(BF16), preferred_element_type=F32) + seen_ref[...]
    r1 = jnp.sum(oh1 * before, axis=-1, keepdims=True)
    r2 = jnp.sum(oh2 * before, axis=-1, keepdims=True)
    seen_ref[...] += jnp.sum(both, axis=0, keepdims=True)
    out = jnp.where(lane == 0, g1, 0.0)
    out = jnp.where(lane == 1, g2, out)
    out = jnp.where(lane == 2, e1, out)
    out = jnp.where(lane == 3, e2, out)
    out = jnp.where(lane == 4, r1, out)
    return jnp.where(lane == 5, r2, out)


def _merge_kernel(att_ref, ssm_ref, mem_ref, gate_ref, x_ref, wa_ref, ws_ref, wm_ref, wo_ref,
                  n2_ref, wr_ref, br_ref, h_ref, hn_ref, lg_ref, seen_ref, *, tm, precise):
    @pl.when(pl.program_id(0) == 0)
    def _():
        seen_ref[...] = jnp.zeros_like(seen_ref)

    a = _mm(att_ref[...], wa_ref[...], precise)
    merged = gate_ref[:, 0:D_MODEL] * a
    s = _mm(ssm_ref[...], ws_ref[...], precise)
    merged = merged + gate_ref[:, D_MODEL:2 * D_MODEL] * s
    m = _mm(mem_ref[...], wm_ref[...], precise)
    merged = merged + gate_ref[:, 2 * D_MODEL:3 * D_MODEL] * m
    h = x_ref[...] + _mm(merged, wo_ref[...], precise)
    h_ref[...] = h
    hn = _rms(h, n2_ref[...])
    for j in range(D_MODEL // LANES):
        hn_ref[pl.ds(j, tm, stride=SUBLANES), :] = hn[:, j * LANES:(j + 1) * LANES]
    logits = _mm(hn, wr_ref[...], precise) + br_ref[...]
    lg_ref[...] = _route(logits, seen_ref)


def merge(att, ssm, mem, gates, x, wts, tm, precise=False):
    n = x.shape[0]
    tm = min(tm, n)
    row = lambda w: pl.BlockSpec((tm, w), lambda i: (i, 0))
    const = lambda a: pl.BlockSpec(a.shape, lambda i: (0, 0))
    ws = [wts["wa"], wts["ws"], wts["wm"], wts["wo"], wts["n2"], wts["wr"], wts["br"]]
    return pl.pallas_call(
        functools.partial(_merge_kernel, tm=tm, precise=precise),
        grid=(n // tm,),
        in_specs=[row(ATT_OUT), row(SSM_WIDTH), row(MEM_WIDTH), row(N_BRANCHES * D_MODEL), row(D_MODEL)]
                 + [const(w) for w in ws],
        out_specs=[row(D_MODEL), pl.BlockSpec((tm * SUBLANES, LANES), lambda i: (i, 0)), row(ROUTER_COLS)],
        out_shape=[jax.ShapeDtypeStruct((n, D_MODEL), F32),
                   jax.ShapeDtypeStruct((n * SUBLANES, LANES), F32),
                   jax.ShapeDtypeStruct((n, ROUTER_COLS), F32)],
        scratch_shapes=[pltpu.VMEM((1, ROUTER_COLS), F32)],
        compiler_params=_cparams(("arbitrary",)),
        name="merge",
    )(att, ssm, mem, gates, x, *ws)


TABLE_WORDS = SUBLANES * LANES


def _row_dma_loop(n, off, smem, slot, start):
    inner = min(n, LANES)
    assert n % inner == 0

    def body(r, carry):
        for j in range(inner):
            f = off + j
            start(smem[slot, f // LANES + r, f % LANES], r * inner + j, j % 2)
        return carry

    if n == inner:
        body(0, 0)
    else:
        lax.fori_loop(0, n // inner, body, 0)


def _tile_rows(i):
    start = i * SUBLANES
    return pl.ds(start if isinstance(start, int) else pl.multiple_of(start, SUBLANES), SUBLANES)


def _dispatch_kernel(pos_hbm, hn_ref, xs_hbm, pos_smem, zero, sem, psem, *, tm, n_pad):
    i = pl.program_id(0)
    slot = i % 2

    def pos_copy(b, s):
        return pltpu.make_async_copy(pos_hbm.at[b], pos_smem.at[s], psem.at[s])

    @pl.when(i == 0)
    def _():
        pos_copy(0, 0).start()

    @pl.when(i + 1 < pl.num_programs(0))
    def _():
        pos_copy(i + 1, 1 - slot).start()

    zero[...] = jnp.zeros_like(zero)
    pos_copy(i, slot).wait()

    def send_token(row, tok, prio):
        pltpu.make_async_copy(hn_ref.at[_tile_rows(tok)], xs_hbm.at[row], sem).start(priority=prio)

    def send_zero(row, _, prio):
        pltpu.make_async_copy(zero, xs_hbm.at[row], sem).start(priority=prio)

    for k in range(TOP_K):
        _row_dma_loop(tm, k * tm, pos_smem, slot, send_token)
    _row_dma_loop(n_pad, TOP_K * tm, pos_smem, slot, send_zero)
    whole, part = divmod(n_pad, tm)
    for _ in range(TOP_K + whole):
        pltpu.make_async_copy(hn_ref, hn_ref, sem).wait()
    if part:
        piece = hn_ref.at[pl.ds(0, part * SUBLANES)]
        pltpu.make_async_copy(piece, piece, sem).wait()


def moe_dispatch(pos, hn2d, n_rows, tm, n_pad):
    n_tiles = pos.shape[0]
    return pl.pallas_call(
        functools.partial(_dispatch_kernel, tm=tm, n_pad=n_pad),
        grid=(n_tiles,),
        in_specs=[pl.BlockSpec(memory_space=pl.ANY),
                  pl.BlockSpec((tm * SUBLANES, LANES), lambda i: (i, 0))],
        out_specs=pl.BlockSpec(memory_space=pl.ANY),
        out_shape=jax.ShapeDtypeStruct((n_rows, SUBLANES, LANES), F32),
        scratch_shapes=[pltpu.SMEM((2, SUBLANES, LANES), jnp.int32),
                        pltpu.VMEM((SUBLANES, LANES), F32),
                        pltpu.SemaphoreType.DMA, pltpu.SemaphoreType.DMA((2,))],
        compiler_params=_cparams(("arbitrary",)),
        name="moe_dispatch",
    )(pos, hn2d)


def _moe_kernel(be_ref, dst_hbm, xs_ref, w1_ref, w3_ref, w2_ref, out_hbm,
                dst_smem, ybuf, w1b, w3b, w2b, ssem, isem, *, bm):
    blk = pl.program_id(0)
    last = pl.num_programs(0) - 1
    slot = blk % 2
    tiles = D_MODEL // LANES

    def table_copy(entry, s):
        return pltpu.make_async_copy(dst_hbm.at[entry], dst_smem.at[s], isem.at[s])

    def wait_rows(s):
        pltpu.make_async_copy(ybuf.at[0], ybuf.at[0], ssem.at[s]).wait()

    def send_rows(buf, s):
        for i in range(bm):
            row = dst_smem[s, i // LANES, i % LANES]
            pltpu.make_async_copy(ybuf.at[buf, _tile_rows(i)], out_hbm.at[row], ssem.at[s]).start(priority=i % 2)

    @pl.when(blk == 0)
    def _():
        ybuf[1] = jnp.zeros(ybuf.shape[1:], F32)
        table_copy(0, 0).start()

    table_copy(blk + 1, 1 - slot).start()

    prev = jnp.maximum(blk - 1, 0)
    changed = jnp.logical_or(blk == 0, be_ref[blk] != be_ref[prev])

    @pl.when(changed)
    def _():
        w1b[...] = w1_ref[...].astype(BF16)
        w3b[...] = w3_ref[...].astype(BF16)
        w2b[...] = w2_ref[...].astype(BF16)

    table_copy(blk, slot).wait()
    x = jnp.concatenate([xs_ref[pl.ds(j, bm, stride=SUBLANES), :].astype(BF16) for j in range(tiles)], axis=1)
    h1 = jnp.dot(x, w1b[...], preferred_element_type=F32)
    send_rows(1 - slot, slot)
    h3 = jnp.dot(x, w3b[...], preferred_element_type=F32)
    act = (jax.nn.silu(h1) * h3).astype(BF16)
    y = jnp.dot(act, w2b[...], preferred_element_type=F32)

    @pl.when(blk > 0)
    def _():
        wait_rows(1 - slot)

    for j in range(tiles):
        ybuf[slot, pl.ds(j, bm, stride=SUBLANES), :] = y[:, j * LANES:(j + 1) * LANES]

    @pl.when(blk == last)
    def _():
        table_copy(blk + 1, 1 - slot).wait()
        send_rows(slot, 1 - slot)
        wait_rows(slot)
        wait_rows(1 - slot)


def moe_experts(block_expert, dst, xs2d, w1, w3, w2, n_out_rows, bm):
    n_blocks = dst.shape[0] - 1
    grid_spec = pltpu.PrefetchScalarGridSpec(
        num_scalar_prefetch=1,
        grid=(n_blocks,),
        in_specs=[
            pl.BlockSpec(memory_space=pl.ANY),
            pl.BlockSpec((bm * SUBLANES, LANES), lambda b, be: (b, 0)),
            pl.BlockSpec((None, D_MODEL, D_EXPERT), lambda b, be: (be[b], 0, 0)),
            pl.BlockSpec((None, D_MODEL, D_EXPERT), lambda b, be: (be[b], 0, 0)),
            pl.BlockSpec((None, D_EXPERT, D_MODEL), lambda b, be: (be[b], 0, 0)),
        ],
        out_specs=pl.BlockSpec(memory_space=pl.ANY),
        scratch_shapes=[
            pltpu.SMEM((2, SUBLANES, LANES), jnp.int32),
            pltpu.VMEM((2, bm * SUBLANES, LANES), F32),
            pltpu.VMEM((D_MODEL, D_EXPERT), BF16),
            pltpu.VMEM((D_MODEL, D_EXPERT), BF16),
            pltpu.VMEM((D_EXPERT, D_MODEL), BF16),
            pltpu.SemaphoreType.DMA((2,)),
            pltpu.SemaphoreType.DMA((2,)),
        ],
    )
    return pl.pallas_call(
        functools.partial(_moe_kernel, bm=bm),
        grid_spec=grid_spec,
        out_shape=jax.ShapeDtypeStruct((n_out_rows, SUBLANES, LANES), F32),
        compiler_params=_cparams(("arbitrary",)),
        name="moe_experts",
    )(block_expert, dst, xs2d, w1, w3, w2)


def _as_tables(cols, width):
    fill = jnp.zeros((cols.shape[0], TABLE_WORDS - width), jnp.int32)
    return jnp.concatenate([cols, fill], axis=1).reshape(cols.shape[0], SUBLANES, LANES)


def route_tables(route, bm, tm):
    n_tok = route.shape[0]
    n_pairs = n_tok * TOP_K
    lanes = route[:, :SUBLANES].T.astype(jnp.int32)
    e_slot = [lanes[2 + k] for k in range(TOP_K)]
    rank = [lanes[2 + TOP_K + k] for k in range(TOP_K)]
    e_flat = jnp.stack(e_slot, axis=1).reshape(n_pairs)
    experts = jnp.arange(N_EXPERTS, dtype=jnp.int32)
    counts = jnp.sum(experts[:, None] == e_flat[None, :], axis=1, dtype=jnp.int32)

    def lookup(table, idx):
        return jnp.sum(jnp.where(experts[:, None] == idx[None, :], table[:, None], 0), axis=0)

    padded = (counts + bm - 1) // bm * bm
    pad_ends = jnp.cumsum(padded)
    n_blocks = -(-n_pairs // bm) + N_EXPERTS
    n_rows = n_blocks * bm
    row0 = jnp.arange(n_blocks, dtype=jnp.int32) * bm
    block_expert = jnp.minimum(jnp.sum(pad_ends[None, :] <= row0[:, None], axis=1, dtype=jnp.int32),
                               N_EXPERTS - 1)
    n_fill = n_rows - n_pairs
    fill_id = jnp.arange(n_fill, dtype=jnp.int32)
    fill_e = jnp.minimum(fill_id // bm, N_EXPERTS - 1)
    fill_used = jnp.logical_and(fill_id < N_EXPERTS * bm, fill_id % bm < (padded - counts)[fill_e])
    shift = n_pairs.bit_length()
    keys = jnp.concatenate([e_flat * 2, jnp.where(fill_used, fill_e * 2 + 1, 2 * N_EXPERTS)])
    ids = jnp.concatenate([jnp.arange(1, n_pairs + 1, dtype=jnp.int32), jnp.zeros((n_fill,), jnp.int32)])
    pair_at = (lax.sort(keys * (1 << shift) + ids) & ((1 << shift) - 1)) - 1
    pad_starts = pad_ends - padded
    pair_row = [lookup(pad_starts, e_slot[k]) + rank[k] for k in range(TOP_K)]
    n_used_before = jnp.cumsum(fill_used.astype(jnp.int32)) - fill_used.astype(jnp.int32)
    pad_row = jnp.where(fill_used, (pad_starts + counts)[fill_e] + fill_id % bm,
                        pad_ends[-1] + fill_id - n_used_before)
    n_tiles = n_tok // tm
    n_pad = n_fill // n_tiles
    assert n_pad * n_tiles == n_fill and n_pad % SUBLANES == 0
    pos = jnp.concatenate([pair_row[k].reshape(n_tiles, tm) for k in range(TOP_K)]
                          + [pad_row.reshape(n_tiles, n_pad)], axis=1)
    pair_at = pair_at.reshape(n_blocks, bm)
    valid = pair_at >= 0
    bank = (jnp.arange(-1, n_blocks, dtype=jnp.int32) % 2)[:, None] * bm
    trash = n_pairs + bank + jnp.arange(bm, dtype=jnp.int32)[None, :]
    dst = jnp.where(valid, (pair_at % TOP_K) * n_tok + pair_at // TOP_K, trash[1:])
    dst = jnp.concatenate([trash[:1], dst], axis=0)
    return (block_expert, _as_tables(dst, bm), _as_tables(pos, TOP_K * tm + n_pad), n_pad, n_rows,
            n_pairs + 2 * bm)


def _final_kernel(h_ref, p0_ref, p1_ref, gw_ref, nf_ref, y_ref, *, tm):
    tiles = D_MODEL // LANES
    p0 = jnp.concatenate([p0_ref[pl.ds(j, tm, stride=SUBLANES), :] for j in range(tiles)], axis=1)
    p1 = jnp.concatenate([p1_ref[pl.ds(j, tm, stride=SUBLANES), :] for j in range(tiles)], axis=1)
    gw = gw_ref[...]
    h = h_ref[...] + (gw[:, 0:1] * p0 + gw[:, 1:2] * p1)
    y_ref[...] = _rms(h, nf_ref[...])


def final(h, pairs2d, gate_pad, norm_f, tm):
    n = h.shape[0]
    tm = min(tm, n)
    nb = n // tm
    row = lambda w: pl.BlockSpec((tm, w), lambda i: (i, 0))
    return pl.pallas_call(
        functools.partial(_final_kernel, tm=tm),
        grid=(nb,),
        in_specs=[
            row(D_MODEL),
            pl.BlockSpec((tm * SUBLANES, LANES), lambda i: (i, 0)),
            pl.BlockSpec((tm * SUBLANES, LANES), lambda i: (nb + i, 0)),
            row(LANES),
            pl.BlockSpec((1, D_MODEL), lambda i: (0, 0)),
        ],
        out_specs=row(D_MODEL),
        out_shape=jax.ShapeDtypeStruct((n, D_MODEL), F32),
        compiler_params=_cparams(("arbitrary",)),
        name="final",
    )(h, pairs2d, pairs2d, gate_pad, norm_f.reshape(1, D_MODEL))


def _alibi_slopes():
    h = jnp.arange(1, N_ATT_HEADS + 1, dtype=F32)
    return jnp.exp2(-8.0 * h / N_ATT_HEADS)


def _token_stage(att, ssm, mem, gates, x2, wts, w1, w3, w2, norm_f, tm, bm, precise=False):
    n = x2.shape[0]
    h, hn2d, route = merge(att, ssm, mem, gates, x2, wts, tm, precise)
    tm = min(tm, n)
    block_expert, dst, pos, n_pad, n_rows, out_rows = route_tables(route, bm, tm)
    xs = moe_dispatch(pos, hn2d, n_rows, tm, n_pad)
    pairs = moe_experts(block_expert, dst, xs.reshape(n_rows * SUBLANES, LANES), w1, w3, w2, out_rows, bm)
    return final(h, pairs.reshape(out_rows * SUBLANES, LANES), route, norm_f, min(2 * tm, n))


def kernel(x_prompt, x_sample, cache_swa0_k, cache_swa0_v, cache_swa1_k, cache_swa1_v, cache_swa2_k, cache_swa2_v, cache_mem_k, cache_mem_v, state_ssm_re, state_ssm_im, mem_prompt, norm1, w_in, lam_re, lam_im, log_dt, ssm_b_re, ssm_b_im, ssm_c_re, ssm_c_im, ssm_d, w_glu, b_glu, w_att_out, w_ssm_out, w_mem_out, w_o, norm_mem, w_mk, w_mv, norm2, w_grp, b_grp, w_exp, b_exp, w1, w3, w2, norm_f):
    assert norm1.shape[0] == 1, "single-layer trunk"
    bsz, seq, _ = x_prompt.shape
    sbz = x_sample.shape[0]
    n_p = bsz * seq
    caches = (cache_swa0_k[0], cache_swa0_v[0], cache_swa1_k[0], cache_swa1_v[0], cache_swa2_k[0], cache_swa2_v[0])

    w_in_b = w_in[0].astype(BF16)
    w_mkv_b = jnp.concatenate([w_mk[0], w_mv[0]], axis=1).astype(BF16)
    wr = jnp.zeros((D_MODEL, ROUTER_COLS), F32)
    wr = wr.at[:, :MOE_GROUPS].set(w_grp[0]).at[:, MOE_GROUPS:MOE_GROUPS + N_EXPERTS].set(w_exp[0])
    br = jnp.zeros((1, ROUTER_COLS), F32)
    br = br.at[0, :MOE_GROUPS].set(b_grp[0]).at[0, MOE_GROUPS:MOE_GROUPS + N_EXPERTS].set(b_exp[0])
    wts_f32 = dict(wa=w_att_out[0], ws=w_ssm_out[0], wm=w_mem_out[0], wo=w_o[0],
                   n2=norm2[0].reshape(1, D_MODEL), wr=wr, br=br)
    wts = {k: (v.astype(BF16) if k in ("wa", "ws", "wm", "wo", "wr") else v) for k, v in wts_f32.items()}
    prm = ssm_params(lam_re[0], lam_im[0], log_dt[0], ssm_b_re[0], ssm_b_im[0], ssm_c_re[0], ssm_c_im[0],
                     ssm_d[0], w_glu[0], b_glu[0])
    slopes = _alibi_slopes()
    splits = (QKV_COLS, SSM_WIDTH, MEM_WIDTH, N_BRANCHES * D_MODEL)
    sig = (False, False, False, True)

    x2 = x_prompt.reshape(n_p, D_MODEL)
    qkv, u, mq, gates = norm_matmul(x2, norm1[0], w_in_b, splits, sig, tm=256, dtypes=(F32, F32, BF16, BF16))
    qkv3 = qkv.reshape(bsz, seq, QKV_COLS)
    att = attn_prompt(qkv3, slopes)
    ssm, st = ssm_prompt(u.reshape(bsz, seq, SSM_WIDTH), prm)
    mk, mv = norm_matmul(mem_prompt.reshape(bsz * MEM_TOKENS, D_MODEL), norm_mem[0], w_mkv_b,
                         (MEM_WIDTH, MEM_WIDTH), (False, False), tm=256)
    mk3 = mk.reshape(bsz, MEM_TOKENS, MEM_WIDTH)
    mv3 = mv.reshape(bsz, MEM_TOKENS, MEM_WIDTH)
    mem = mem_attention(mq.reshape(bsz, seq, MEM_WIDTH), mk3, mv3, tl=512, out_dtype=BF16)
    y_p = _token_stage(att.reshape(n_p, ATT_OUT), ssm.reshape(n_p, SSM_WIDTH), mem.reshape(n_p, MEM_WIDTH),
                       gates, x2, wts, w1[0], w3[0], w2[0], norm_f, tm=256, bm=256)

    xs2 = x_sample.reshape(sbz, D_MODEL)
    qkv_s, u_s, mq_s, gates_s = norm_matmul(xs2, norm1[0], w_in[0], splits, sig, tm=sbz, precise=True)
    qkv5 = qkv_s.reshape(sbz, len(ATT_GROUPS), 3, HEADS_PER_GROUP, HEAD_DIM)
    att_s, *new_caches = attn_sample(qkv5, [jnp.transpose(c, (0, 2, 3, 1)) for c in caches])
    att_s = att_s.reshape(sbz, ATT_OUT)
    h0 = jnp.concatenate([_state_to_blocks(state_ssm_re[0]), _state_to_blocks(state_ssm_im[0])], axis=2)
    ssm_s, st_s = ssm_sample(u_s, h0, prm)
    cmk = cache_mem_k[0].reshape(sbz, MEM_TOKENS, MEM_WIDTH)
    cmv = cache_mem_v[0].reshape(sbz, MEM_TOKENS, MEM_WIDTH)
    mem_s = mem_attention(mq_s.reshape(sbz, 1, MEM_WIDTH), cmk, cmv, tl=1, precise=True).reshape(sbz, MEM_WIDTH)
    y_s = _token_stage(att_s, ssm_s, mem_s, gates_s, xs2, wts_f32, w1[0], w3[0], w2[0], norm_f, tm=sbz, bm=16,
                       precise=True)

    outs = [y_p.reshape(bsz, seq, D_MODEL), y_s.reshape(sbz, 1, D_MODEL)]
    for g, (window, _) in enumerate(ATT_GROUPS):
        keep = min(window, seq)
        for part in (1, 2):
            c0 = g * QKV_GROUP + part * ATT_OUT
            outs.append(qkv3[:, seq - keep:, c0:c0 + ATT_OUT].reshape(1, bsz, keep, HEADS_PER_GROUP, HEAD_DIM))
    outs.append(mk3.reshape(1, bsz, MEM_TOKENS, MEM_HEADS, MEM_HEAD_DIM))
    outs.append(mv3.reshape(1, bsz, MEM_TOKENS, MEM_HEADS, MEM_HEAD_DIM))
    outs.append(_blocks_to_state(st[:, :, :SSM_KSTATE])[None])
    outs.append(_blocks_to_state(st[:, :, SSM_KSTATE:])[None])
    outs.extend(jnp.transpose(c, (0, 3, 1, 2))[None] for c in new_caches)
    outs.append(_blocks_to_state(st_s[:, :, :SSM_KSTATE])[None])
    outs.append(_blocks_to_state(st_s[:, :, SSM_KSTATE:])[None])
    return tuple(outs)
```

```python
import functools
import math

import jax
import jax.numpy as jnp
from jax import lax
from jax.experimental import pallas as pl
from jax.experimental.pallas import tpu as pltpu

F32 = jnp.float32
BF16 = jnp.bfloat16

D_MODEL = 1024
ATT_GROUPS = ((128, 1), (512, 4), (2048, 16))
HEADS_PER_GROUP = 4
HEAD_DIM = 64
N_ATT_HEADS = len(ATT_GROUPS) * HEADS_PER_GROUP
ATT_OUT = HEADS_PER_GROUP * HEAD_DIM
QKV_GROUP = 3 * ATT_OUT
QKV_COLS = len(ATT_GROUPS) * QKV_GROUP
SSM_WIDTH = D_MODEL // 2
SSM_GROUP_CH = 16
SSM_GROUPS = SSM_WIDTH // SSM_GROUP_CH
SSM_STATE = 64
SSM_NSTATE = SSM_GROUPS * SSM_STATE
MEM_TOKENS = 256
MEM_HEADS = 4
MEM_HEAD_DIM = D_MODEL // 8
MEM_WIDTH = MEM_HEADS * MEM_HEAD_DIM
N_BRANCHES = 3
MOE_GROUPS = 4
EXPERTS_PER_GROUP = 8
N_EXPERTS = MOE_GROUPS * EXPERTS_PER_GROUP
TOP_K = 2
D_EXPERT = D_MODEL // 2
RMS_EPS = 1e-6

LANES = 128
SUBLANES = 8
Q_TILE = 128
ROUTER_COLS = 128
NEG_BIG = -1e30
VMEM_LIMIT = 56 * 1024 * 1024


def _cparams(sem):
    return pltpu.CompilerParams(dimension_semantics=sem, vmem_limit_bytes=VMEM_LIMIT)


def _mm(a, b, precise, dims=None):
    if precise:
        a, b, kw = a.astype(F32), b.astype(F32), dict(precision=lax.Precision.HIGHEST)
    else:
        a, b, kw = a.astype(BF16), b.astype(BF16), {}
    if dims is None:
        return jnp.dot(a, b, preferred_element_type=F32, **kw)
    return lax.dot_general(a, b, dims, preferred_element_type=F32, **kw)


NT_DIMS = (((1,), (1,)), ((), ()))


def _rms(x, g):
    ms = jnp.mean(x * x, axis=-1, keepdims=True)
    return (x * lax.rsqrt(ms + RMS_EPS)) * g


def _norm_matmul_kernel(x_ref, g_ref, w_ref, *out_refs, splits, sigmoid, chunk, precise):
    xb = _rms(x_ref[...], g_ref[...])
    if not precise:
        xb = xb.astype(BF16)
    c0 = 0
    for o_ref, n, sg in zip(out_refs, splits, sigmoid):
        for j in range(0, n, chunk):
            w = min(chunk, n - j)
            y = _mm(xb, w_ref[:, c0 + j:c0 + j + w], precise)
            if sg:
                y = jax.nn.sigmoid(y)
            o_ref[:, j:j + w] = y.astype(o_ref.dtype)
        c0 += n


def norm_matmul(x, g, w, splits, sigmoid, tm, dtypes=None, precise=False):
    n, d = x.shape
    tm = min(tm, n)
    kern = functools.partial(_norm_matmul_kernel, splits=tuple(splits), sigmoid=tuple(sigmoid), chunk=512,
                             precise=precise)
    return pl.pallas_call(
        kern,
        grid=(n // tm,),
        in_specs=[
            pl.BlockSpec((tm, d), lambda i: (i, 0)),
            pl.BlockSpec((1, d), lambda i: (0, 0)),
            pl.BlockSpec((d, w.shape[1]), lambda i: (0, 0), pipeline_mode=pl.Buffered(1)),
        ],
        out_specs=[pl.BlockSpec((tm, s), lambda i: (i, 0)) for s in splits],
        out_shape=[jax.ShapeDtypeStruct((n, s), dt) for s, dt in zip(splits, dtypes or (F32,) * len(splits))],
        compiler_params=_cparams(("arbitrary",)),
        name="norm_matmul",
    )(x, g.reshape(1, d), w)


def _attn_prompt_kernel(slopes_ref, *refs, seq):
    qkv_refs = refs[:9]
    o_ref = refs[9]
    o_scr, l_scr = refs[10], refs[11]
    hp = pl.program_id(1)
    scale = HEAD_DIM ** -0.5
    row = lax.broadcasted_iota(jnp.int32, (Q_TILE, 2 * Q_TILE), 0)
    col = lax.broadcasted_iota(jnp.int32, (Q_TILE, 2 * Q_TILE), 1)
    dist = (row + Q_TILE - col)
    in_window = jnp.logical_and(dist >= 0, dist <= Q_TILE)
    distf = dist.astype(F32)
    lane = lax.broadcasted_iota(jnp.int32, (Q_TILE, LANES), 1)
    head_a = lane < HEAD_DIM

    for g, (_, dil) in enumerate(ATT_GROUPS):
        q_ref, k_ref, v_ref = qkv_refs[3 * g:3 * g + 3]
        sub_len = seq // dil
        n_blk = sub_len // Q_TILE
        slope_a = slopes_ref[g * HEADS_PER_GROUP + 2 * hp] * float(dil)
        slope_b = slopes_ref[g * HEADS_PER_GROUP + 2 * hp + 1] * float(dil)

        def rows(start):
            if dil == 1:
                return pl.ds(start, Q_TILE)
            return pl.ds(start, Q_TILE, stride=dil)

        def body(it, carry, q_ref=q_ref, k_ref=k_ref, v_ref=v_ref, n_blk=n_blk, dil=dil,
                 slope_a=slope_a, slope_b=slope_b, rows=rows, g=g):
            r = it // n_blk
            blk = it % n_blk
            prev = jnp.maximum(blk - 1, 0)
            q_start = r + dil * Q_TILE * blk
            p_start = r + dil * Q_TILE * prev
            q = q_ref[rows(q_start), :]
            k2 = jnp.concatenate([k_ref[rows(p_start), :], k_ref[rows(q_start), :]], axis=0).astype(BF16)
            v2 = jnp.concatenate([v_ref[rows(p_start), :], v_ref[rows(q_start), :]], axis=0).astype(BF16)
            valid = jnp.logical_and(in_window, jnp.logical_or(col >= Q_TILE, blk > 0))
            outs, lses = [], []
            for is_a, slope in ((True, slope_a), (False, slope_b)):
                hmask = head_a if is_a else jnp.logical_not(head_a)
                qh = jnp.where(hmask, q, 0.0).astype(BF16)
                s = lax.dot_general(qh, k2, NT_DIMS, preferred_element_type=F32)
                s = s * scale - slope * distf
                s = jnp.where(valid, s, NEG_BIG)
                m = jnp.max(s, axis=-1, keepdims=True)
                p = jnp.exp(s - m)
                den = jnp.sum(p, axis=-1, keepdims=True)
                pv = jnp.dot(p.astype(BF16), v2, preferred_element_type=F32)
                outs.append(pv / den)
                lses.append(m + jnp.log(den))
            o_scr[g, rows(q_start), :] = jnp.where(head_a, outs[0], outs[1])
            l_scr[g, rows(q_start), :] = jnp.where(head_a, lses[0], lses[1])
            return carry

        lax.fori_loop(0, dil * n_blk, body, 0, unroll=8)

    def combine(c, carry):
        sl = pl.ds(pl.multiple_of(c * 256, 256), 256)
        l0, l1, l2 = l_scr[0, sl, :], l_scr[1, sl, :], l_scr[2, sl, :]
        m = jnp.maximum(jnp.maximum(l0, l1), l2)
        e0, e1, e2 = jnp.exp(l0 - m), jnp.exp(l1 - m), jnp.exp(l2 - m)
        tot = e0 + e1 + e2
        att = (e0 * o_scr[0, sl, :] + e1 * o_scr[1, sl, :] + e2 * o_scr[2, sl, :]) / tot
        o_ref[sl, :] = att.astype(o_ref.dtype)
        return carry

    lax.fori_loop(0, seq // 256, combine, 0)


def attn_prompt(qkv, slopes):
    bsz, seq, _ = qkv.shape
    in_specs = []
    for g in range(len(ATT_GROUPS)):
        for part in range(3):
            base = (g * QKV_GROUP + part * ATT_OUT) // LANES
            in_specs.append(pl.BlockSpec((None, seq, LANES),
                                         lambda b, h, sl, base=base: (b, 0, base + h)))
    grid_spec = pltpu.PrefetchScalarGridSpec(
        num_scalar_prefetch=1,
        grid=(bsz, ATT_OUT // LANES),
        in_specs=in_specs,
        out_specs=pl.BlockSpec((None, seq, LANES), lambda b, h, sl: (b, 0, h)),
        scratch_shapes=[pltpu.VMEM((3, seq, LANES), F32), pltpu.VMEM((3, seq, LANES), F32)],
    )
    return pl.pallas_call(
        functools.partial(_attn_prompt_kernel, seq=seq),
        grid_spec=grid_spec,
        out_shape=jax.ShapeDtypeStruct((bsz, seq, ATT_OUT), BF16),
        compiler_params=_cparams(("arbitrary", "arbitrary")),
        name="attn_prompt",
    )(slopes, *([qkv] * 9))


def _attn_sample_kernel(qkv_ref, col_ref, *refs):
    cache_refs = refs[:6]
    o_ref = refs[6]
    new_refs = refs[7:13]
    scale = HEAD_DIM ** -0.5
    n_grp = len(ATT_GROUPS)
    outs = [[None] * HEADS_PER_GROUP for _ in range(n_grp)]
    lses = [[None] * HEADS_PER_GROUP for _ in range(n_grp)]
    for g, (window, dil) in enumerate(ATT_GROUPS):
        lane = lax.broadcasted_iota(jnp.int32, (1, window), 1)
        on_grid = jnp.bitwise_and(lane, dil - 1) == 0
        dist = (window - lane).astype(F32)
        last = lax.broadcasted_iota(jnp.int32, (HEAD_DIM, window), 1) == window - 1
        for h in range(HEADS_PER_GROUP):
            slope = 2.0 ** (-8.0 * (g * HEADS_PER_GROUP + h + 1) / N_ATT_HEADS)
            q = qkv_ref[g, 0][h:h + 1, :]
            kn = qkv_ref[g, 1][h:h + 1, :]
            vn = qkv_ref[g, 2][h:h + 1, :]
            kt = cache_refs[2 * g][h]
            vt = cache_refs[2 * g + 1][h]
            s = _mm(q, kt, True) * scale - slope * dist
            s = jnp.where(on_grid, s, NEG_BIG)
            sn = jnp.sum(q * kn, axis=-1, keepdims=True) * scale
            m = jnp.maximum(jnp.max(s, axis=-1, keepdims=True), sn)
            p = jnp.exp(s - m)
            pn = jnp.exp(sn - m)
            den = jnp.sum(p, axis=-1, keepdims=True) + pn
            outs[g][h] = (_mm(p, vt, True, NT_DIMS) + pn * vn) / den
            lses[g][h] = m + jnp.log(den)
            new_refs[2 * g][h] = jnp.where(last, col_ref[g, 0, h], pltpu.roll(kt, window - 1, axis=1))
            new_refs[2 * g + 1][h] = jnp.where(last, col_ref[g, 1, h], pltpu.roll(vt, window - 1, axis=1))
    for h in range(HEADS_PER_GROUP):
        m = jnp.maximum(jnp.maximum(lses[0][h], lses[1][h]), lses[2][h])
        es = [jnp.exp(lses[g][h] - m) for g in range(n_grp)]
        num = es[0] * outs[0][h] + es[1] * outs[1][h] + es[2] * outs[2][h]
        o_ref[h:h + 1, :] = num / (es[0] + es[1] + es[2])


def attn_sample(qkv5, caches_t):
    bsz = qkv5.shape[0]
    cols = qkv5[:, :, 1:3].reshape(bsz, len(ATT_GROUPS), 2, HEADS_PER_GROUP, HEAD_DIM, 1)
    in_specs = [pl.BlockSpec((None, 3, 3, HEADS_PER_GROUP, HEAD_DIM), lambda b: (b, 0, 0, 0, 0)),
                pl.BlockSpec((None, 3, 2, HEADS_PER_GROUP, HEAD_DIM, 1), lambda b: (b, 0, 0, 0, 0, 0))]
    cache_specs = [pl.BlockSpec((None,) + c.shape[1:], lambda b: (b, 0, 0, 0)) for c in caches_t]
    return pl.pallas_call(
        _attn_sample_kernel,
        grid=(bsz,),
        in_specs=in_specs + cache_specs,
        out_specs=[pl.BlockSpec((None, HEADS_PER_GROUP, HEAD_DIM), lambda b: (b, 0, 0))] + cache_specs,
        out_shape=[jax.ShapeDtypeStruct((bsz, HEADS_PER_GROUP, HEAD_DIM), F32)]
                  + [jax.ShapeDtypeStruct(c.shape, F32) for c in caches_t],
        compiler_params=_cparams(("arbitrary",)),
        name="attn_sample",
    )(qkv5, cols, *caches_t)


SSM_KBLK = SSM_WIDTH // LANES
SSM_KSTATE = SSM_NSTATE // SSM_KBLK


def _ssm_tail(y, u, dsk, wglu, bglu, precise=False):
    y = y + dsk * u
    z = jax.nn.gelu(y)
    gl = _mm(z, wglu, precise) + bglu
    return z * jax.nn.sigmoid(gl)


def _ssm_prompt_kernel(u_ref, wb_ref, wc_ref, are_ref, aim_ref, dsk_ref, wglu_ref, bglu_ref,
                       o_ref, st_ref, uperm, bu, sbf, yperm, ynat, *, bsz, tt):
    rows = bsz * tt

    @pl.when(pl.program_id(0) == 0)
    def _():
        st_ref[...] = jnp.zeros_like(st_ref)

    for b in range(bsz):
        ub = u_ref[b]
        for k in range(SSM_KBLK):
            uperm[k, pl.ds(b, tt, stride=bsz), :] = ub[:, k * LANES:(k + 1) * LANES]

    for k in range(SSM_KBLK):
        bu[k] = jnp.dot(uperm[k].astype(BF16), wb_ref[k], preferred_element_type=F32)

    for k in range(SSM_KBLK):
        ar = jnp.broadcast_to(are_ref[k], (bsz, SSM_KSTATE))
        ai = jnp.broadcast_to(aim_ref[k], (bsz, SSM_KSTATE))
        st = st_ref[k]

        def step(t, carry, k=k, ar=ar, ai=ai):
            re, im = carry
            r0 = pl.multiple_of(t * bsz, bsz)
            b_t = bu[k, pl.ds(r0, bsz), :]
            nre = ar * re - ai * im + b_t[:, :SSM_KSTATE]
            nim = ar * im + ai * re + b_t[:, SSM_KSTATE:]
            sbf[k, pl.ds(r0, bsz), :] = jnp.concatenate([nre, nim], axis=1).astype(BF16)
            return nre, nim

        re, im = lax.fori_loop(0, tt, step, (st[:, :SSM_KSTATE], st[:, SSM_KSTATE:]), unroll=2)
        st_ref[k] = jnp.concatenate([re, im], axis=1)

    for k in range(SSM_KBLK):
        yperm[k] = jnp.dot(sbf[k], wc_ref[k], preferred_element_type=F32)

    for b in range(bsz):
        for k in range(SSM_KBLK):
            ynat[b * tt:(b + 1) * tt, k * LANES:(k + 1) * LANES] = yperm[k, pl.ds(b, tt, stride=bsz), :]

    u2 = u_ref[...].reshape(rows, SSM_WIDTH)
    out = _ssm_tail(ynat[...], u2, dsk_ref[...], wglu_ref[...], bglu_ref[...])
    o_ref[...] = out.reshape(bsz, tt, SSM_WIDTH).astype(o_ref.dtype)


def ssm_prompt(u, prm, tt=32):
    bsz, seq, _ = u.shape
    rows = bsz * tt
    const = lambda shape: pl.BlockSpec(shape, lambda i: (0,) * len(shape))
    return pl.pallas_call(
        functools.partial(_ssm_prompt_kernel, bsz=bsz, tt=tt),
        grid=(seq // tt,),
        in_specs=[
            pl.BlockSpec((bsz, tt, SSM_WIDTH), lambda i: (0, i, 0)),
            const((SSM_KBLK, LANES, 2 * SSM_KSTATE)),
            const((SSM_KBLK, 2 * SSM_KSTATE, LANES)),
            const((SSM_KBLK, 1, SSM_KSTATE)),
            const((SSM_KBLK, 1, SSM_KSTATE)),
            const((1, SSM_WIDTH)),
            const((SSM_WIDTH, SSM_WIDTH)),
            const((1, SSM_WIDTH)),
        ],
        out_specs=[
            pl.BlockSpec((bsz, tt, SSM_WIDTH), lambda i: (0, i, 0)),
            const((SSM_KBLK, bsz, 2 * SSM_KSTATE)),
        ],
        out_shape=[
            jax.ShapeDtypeStruct((bsz, seq, SSM_WIDTH), BF16),
            jax.ShapeDtypeStruct((SSM_KBLK, bsz, 2 * SSM_KSTATE), F32),
        ],
        scratch_shapes=[
            pltpu.VMEM((SSM_KBLK, rows, LANES), F32),
            pltpu.VMEM((SSM_KBLK, rows, 2 * SSM_KSTATE), F32),
            pltpu.VMEM((SSM_KBLK, rows, 2 * SSM_KSTATE), BF16),
            pltpu.VMEM((SSM_KBLK, rows, LANES), F32),
            pltpu.VMEM((rows, SSM_WIDTH), F32),
        ],
        compiler_params=_cparams(("arbitrary",)),
        name="ssm_prompt",
    )(u, prm["wb"], prm["wc"], prm["a_re"], prm["a_im"], prm["dsk"], prm["wglu"], prm["bglu"])


def _ssm_sample_kernel(u_ref, h0_ref, wb_ref, wc_ref, are_ref, aim_ref, dsk_ref, wglu_ref, bglu_ref,
                       o_ref, st_ref):
    u = u_ref[...]
    ys = []
    for k in range(SSM_KBLK):
        b_t = _mm(u[:, k * LANES:(k + 1) * LANES], wb_ref[k], True)
        h0 = h0_ref[k]
        re, im = h0[:, :SSM_KSTATE], h0[:, SSM_KSTATE:]
        ar, ai = are_ref[k], aim_ref[k]
        nre = ar * re - ai * im + b_t[:, :SSM_KSTATE]
        nim = ar * im + ai * re + b_t[:, SSM_KSTATE:]
        s = jnp.concatenate([nre, nim], axis=1)
        st_ref[k] = s
        ys.append(_mm(s, wc_ref[k], True))
    y = jnp.concatenate(ys, axis=1)
    o_ref[...] = _ssm_tail(y, u, dsk_ref[...], wglu_ref[...], bglu_ref[...], precise=True)


def ssm_sample(u, h0, prm):
    bsz = u.shape[0]
    return pl.pallas_call(
        _ssm_sample_kernel,
        out_shape=[jax.ShapeDtypeStruct((bsz, SSM_WIDTH), F32),
                   jax.ShapeDtypeStruct((SSM_KBLK, bsz, 2 * SSM_KSTATE), F32)],
        compiler_params=pltpu.CompilerParams(vmem_limit_bytes=VMEM_LIMIT),
        name="ssm_sample",
    )(u, h0, prm["wb_f32"], prm["wc_f32"], prm["a_re"], prm["a_im"], prm["dsk"], prm["wglu_f32"], prm["bglu"])


def ssm_params(lam_re, lam_im, log_dt, b_re, b_im, c_re, c_im, d_skip, w_glu, b_glu):
    dt = jnp.exp(log_dt)[:, None]
    mag = jnp.exp(lam_re * dt)
    ab_re = mag * jnp.cos(lam_im * dt)
    ab_im = mag * jnp.sin(lam_im * dt)
    den = lam_re * lam_re + lam_im * lam_im
    nr = ab_re - 1.0
    ni = ab_im
    z_re = ((nr * lam_re + ni * lam_im) / den)[..., None]
    z_im = ((ni * lam_re - nr * lam_im) / den)[..., None]
    bb_re = z_re * b_re - z_im * b_im
    bb_im = z_re * b_im + z_im * b_re
    gpb = SSM_GROUPS // SSM_KBLK
    eye = jnp.eye(gpb, dtype=F32)

    def in_mat(bb):
        bbk = bb.reshape(SSM_KBLK, gpb, SSM_STATE, SSM_GROUP_CH)
        m = jnp.einsum("kgpc,gh->kgchp", bbk, eye)
        return m.reshape(SSM_KBLK, LANES, SSM_KSTATE)

    def out_mat(c):
        ck = c.reshape(SSM_KBLK, gpb, SSM_GROUP_CH, SSM_STATE)
        m = jnp.einsum("kgcp,gh->kgphc", ck, eye)
        return m.reshape(SSM_KBLK, SSM_KSTATE, LANES)

    wb = jnp.concatenate([in_mat(bb_re), in_mat(bb_im)], axis=2)
    wc = jnp.concatenate([out_mat(c_re), -out_mat(c_im)], axis=1)
    return dict(
        wb=wb.astype(BF16), wc=wc.astype(BF16), wb_f32=wb, wc_f32=wc, wglu_f32=w_glu,
        a_re=ab_re.reshape(SSM_KBLK, 1, SSM_KSTATE), a_im=ab_im.reshape(SSM_KBLK, 1, SSM_KSTATE),
        dsk=d_skip.reshape(1, SSM_WIDTH), wglu=w_glu.astype(BF16), bglu=b_glu.reshape(1, SSM_WIDTH))


def _state_to_blocks(h):
    bsz = h.shape[0]
    return h.reshape(bsz, SSM_KBLK, SSM_KSTATE).transpose(1, 0, 2)


def _blocks_to_state(s):
    bsz = s.shape[1]
    return s.transpose(1, 0, 2).reshape(bsz, SSM_GROUPS, SSM_STATE)


def _mem_attn_kernel(q_ref, k_ref, v_ref, o_ref, *, precise):
    scale = MEM_HEAD_DIM ** -0.5
    for h in range(MEM_HEADS):
        sl = slice(h * MEM_HEAD_DIM, (h + 1) * MEM_HEAD_DIM)
        s = _mm(q_ref[:, sl], k_ref[:, sl], precise, NT_DIMS) * scale
        m = jnp.max(s, axis=-1, keepdims=True)
        p = jnp.exp(s - m)
        den = jnp.sum(p, axis=-1, keepdims=True)
        o_ref[:, sl] = _mm(p / den, v_ref[:, sl], precise).astype(o_ref.dtype)


def mem_attention(q, mk, mv, tl, out_dtype=F32, precise=False):
    bsz, seq, _ = q.shape
    tl = min(tl, seq)
    return pl.pallas_call(
        functools.partial(_mem_attn_kernel, precise=precise),
        grid=(bsz, seq // tl),
        in_specs=[
            pl.BlockSpec((None, tl, MEM_WIDTH), lambda b, i: (b, i, 0)),
            pl.BlockSpec((None, MEM_TOKENS, MEM_WIDTH), lambda b, i: (b, 0, 0)),
            pl.BlockSpec((None, MEM_TOKENS, MEM_WIDTH), lambda b, i: (b, 0, 0)),
        ],
        out_specs=pl.BlockSpec((None, tl, MEM_WIDTH), lambda b, i: (b, i, 0)),
        out_shape=jax.ShapeDtypeStruct((bsz, seq, MEM_WIDTH), out_dtype),
        compiler_params=_cparams(("arbitrary", "arbitrary")),
        name="mem_attention",
    )(q, mk, mv)


def _route(logits, seen_ref):
    lane = lax.broadcasted_iota(jnp.int32, logits.shape, 1)
    lanef = lane.astype(F32)
    none = float(ROUTER_COLS)
    neg = -jnp.inf
    gl = jnp.where(lane < MOE_GROUPS, logits, neg)
    gmax = jnp.max(gl, axis=-1, keepdims=True)
    grp = jnp.min(jnp.where(gl == gmax, lanef, none), axis=-1, keepdims=True)
    p_grp = 1.0 / jnp.sum(jnp.exp(gl - gmax), axis=-1, keepdims=True)
    lo = MOE_GROUPS + grp * EXPERTS_PER_GROUP
    el = jnp.where(jnp.logical_and(lanef >= lo, lanef < lo + EXPERTS_PER_GROUP), logits, neg)
    v1 = jnp.max(el, axis=-1, keepdims=True)
    i1 = jnp.min(jnp.where(el == v1, lanef, none), axis=-1, keepdims=True)
    el2 = jnp.where(lanef == i1, neg, el)
    v2 = jnp.max(el2, axis=-1, keepdims=True)
    i2 = jnp.min(jnp.where(el2 == v2, lanef, none), axis=-1, keepdims=True)
    t = jnp.exp(v2 - v1)
    g1 = p_grp / (1.0 + t)
    g2 = g1 * t
    e1 = i1 - MOE_GROUPS
    e2 = i2 - MOE_GROUPS
    oh1 = (lanef == e1).astype(F32)
    oh2 = (lanef == e2).astype(F32)
    both = oh1 + oh2
    tm = logits.shape[0]
    earlier = (lax.broadcasted_iota(jnp.int32, (tm, tm), 1) < lax.broadcasted_iota(jnp.int32, (tm, tm), 0))
    before = jnp.dot(earlier.astype(BF16), both.astype(BF16), preferred_element_type=F32) + seen_ref[...]
    r1 = jnp.sum(oh1 * before, axis=-1, keepdims=True)
    r2 = jnp.sum(oh2 * before, axis=-1, keepdims=True)
    seen_ref[...] += jnp.sum(both, axis=0, keepdims=True)
    out = jnp.where(lane == 0, g1, 0.0)
    out = jnp.where(lane == 1, g2, out)
    out = jnp.where(lane == 2, e1, out)
    out = jnp.where(lane == 3, e2, out)
    out = jnp.where(lane == 4, r1, out)
    return jnp.where(lane == 5, r2, out)


def _merge_kernel(att_ref, ssm_ref, mem_ref, gate_ref, x_ref, wa_ref, ws_ref, wm_ref, wo_ref,
                  n2_ref, wr_ref, br_ref, h_ref, hn_ref, lg_ref, seen_ref, *, tm, precise):
    @pl.when(pl.program_id(0) == 0)
    def _():
        seen_ref[...] = jnp.zeros_like(seen_ref)

    a = _mm(att_ref[...], wa_ref[...], precise)
    merged = gate_ref[:, 0:D_MODEL] * a
    s = _mm(ssm_ref[...], ws_ref[...], precise)
    merged = merged + gate_ref[:, D_MODEL:2 * D_MODEL] * s
    m = _mm(mem_ref[...], wm_ref[...], precise)
    merged = merged + gate_ref[:, 2 * D_MODEL:3 * D_MODEL] * m
    h = x_ref[...] + _mm(merged, wo_ref[...], precise)
    h_ref[...] = h
    hn = _rms(h, n2_ref[...])
    for j in range(D_MODEL // LANES):
        hn_ref[pl.ds(j, tm, stride=SUBLANES), :] = hn[:, j * LANES:(j + 1) * LANES]
    logits = _mm(hn, wr_ref[...], precise) + br_ref[...]
    lg_ref[...] = _route(logits, seen_ref)


def merge(att, ssm, mem, gates, x, wts, tm, precise=False):
    n = x.shape[0]
    tm = min(tm, n)
    row = lambda w: pl.BlockSpec((tm, w), lambda i: (i, 0))
    const = lambda a: pl.BlockSpec(a.shape, lambda i: (0, 0))
    ws = [wts["wa"], wts["ws"], wts["wm"], wts["wo"], wts["n2"], wts["wr"], wts["br"]]
    return pl.pallas_call(
        functools.partial(_merge_kernel, tm=tm, precise=precise),
        grid=(n // tm,),
        in_specs=[row(ATT_OUT), row(SSM_WIDTH), row(MEM_WIDTH), row(N_BRANCHES * D_MODEL), row(D_MODEL)]
                 + [const(w) for w in ws],
        out_specs=[row(D_MODEL), pl.BlockSpec((tm * SUBLANES, LANES), lambda i: (i, 0)), row(ROUTER_COLS)],
        out_shape=[jax.ShapeDtypeStruct((n, D_MODEL), F32),
                   jax.ShapeDtypeStruct((n * SUBLANES, LANES), F32),
                   jax.ShapeDtypeStruct((n, ROUTER_COLS), F32)],
        scratch_shapes=[pltpu.VMEM((1, ROUTER_COLS), F32)],
        compiler_params=_cparams(("arbitrary",)),
        name="merge",
    )(att, ssm, mem, gates, x, *ws)


TABLE_WORDS = SUBLANES * LANES


def _row_dma_loop(n, off, smem, slot, start):
    inner = min(n, LANES)
    assert n % inner == 0

    def body(r, carry):
        for j in range(inner):
            f = off + j
            start(smem[slot, f // LANES + r, f % LANES], r * inner + j, j % 2)
        return carry

    if n == inner:
        body(0, 0)
    else:
        lax.fori_loop(0, n // inner, body, 0)


def _tile_rows(i):
    start = i * SUBLANES
    return pl.ds(start if isinstance(start, int) else pl.multiple_of(start, SUBLANES), SUBLANES)


def _dispatch_kernel(pos_hbm, hn_ref, xs_hbm, pos_smem, zero, sem, psem, *, tm, n_pad):
    i = pl.program_id(0)
    slot = i % 2

    def pos_copy(b, s):
        return pltpu.make_async_copy(pos_hbm.at[b], pos_smem.at[s], psem.at[s])

    @pl.when(i == 0)
    def _():
        pos_copy(0, 0).start()

    @pl.when(i + 1 < pl.num_programs(0))
    def _():
        pos_copy(i + 1, 1 - slot).start()

    zero[...] = jnp.zeros_like(zero)
    pos_copy(i, slot).wait()

    def send_token(row, tok, prio):
        pltpu.make_async_copy(hn_ref.at[_tile_rows(tok)], xs_hbm.at[row], sem).start(priority=prio)

    def send_zero(row, _, prio):
        pltpu.make_async_copy(zero, xs_hbm.at[row], sem).start(priority=prio)

    for k in range(TOP_K):
        _row_dma_loop(tm, k * tm, pos_smem, slot, send_token)
    _row_dma_loop(n_pad, TOP_K * tm, pos_smem, slot, send_zero)
    whole, part = divmod(n_pad, tm)
    for _ in range(TOP_K + whole):
        pltpu.make_async_copy(hn_ref, hn_ref, sem).wait()
    if part:
        piece = hn_ref.at[pl.ds(0, part * SUBLANES)]
        pltpu.make_async_copy(piece, piece, sem).wait()


def moe_dispatch(pos, hn2d, n_rows, tm, n_pad):
    n_tiles = pos.shape[0]
    return pl.pallas_call(
        functools.partial(_dispatch_kernel, tm=tm, n_pad=n_pad),
        grid=(n_tiles,),
        in_specs=[pl.BlockSpec(memory_space=pl.ANY),
                  pl.BlockSpec((tm * SUBLANES, LANES), lambda i: (i, 0))],
        out_specs=pl.BlockSpec(memory_space=pl.ANY),
        out_shape=jax.ShapeDtypeStruct((n_rows, SUBLANES, LANES), F32),
        scratch_shapes=[pltpu.SMEM((2, SUBLANES, LANES), jnp.int32),
                        pltpu.VMEM((SUBLANES, LANES), F32),
                        pltpu.SemaphoreType.DMA, pltpu.SemaphoreType.DMA((2,))],
        compiler_params=_cparams(("arbitrary",)),
        name="moe_dispatch",
    )(pos, hn2d)


def _moe_kernel(be_ref, dst_hbm, xs_ref, w1_ref, w3_ref, w2_ref, out_hbm,
                dst_smem, ybuf, w1b, w3b, w2b, ssem, isem, *, bm):
    blk = pl.program_id(0)
    last = pl.num_programs(0) - 1
    slot = blk % 2
    tiles = D_MODEL // LANES

    def table_copy(entry, s):
        return pltpu.make_async_copy(dst_hbm.at[entry], dst_smem.at[s], isem.at[s])

    def wait_rows(s):
        pltpu.make_async_copy(ybuf.at[0], ybuf.at[0], ssem.at[s]).wait()

    def send_rows(buf, s):
        for i in range(bm):
            row = dst_smem[s, i // LANES, i % LANES]
            pltpu.make_async_copy(ybuf.at[buf, _tile_rows(i)], out_hbm.at[row], ssem.at[s]).start(priority=i % 2)

    @pl.when(blk == 0)
    def _():
        ybuf[1] = jnp.zeros(ybuf.shape[1:], F32)
        table_copy(0, 0).start()

    table_copy(blk + 1, 1 - slot).start()

    prev = jnp.maximum(blk - 1, 0)
    changed = jnp.logical_or(blk == 0, be_ref[blk] != be_ref[prev])

    @pl.when(changed)
    def _():
        w1b[...] = w1_ref[...].astype(BF16)
        w3b[...] = w3_ref[...].astype(BF16)
        w2b[...] = w2_ref[...].astype(BF16)

    table_copy(blk, slot).wait()
    x = jnp.concatenate([xs_ref[pl.ds(j, bm, stride=SUBLANES), :].astype(BF16) for j in range(tiles)], axis=1)
    h1 = jnp.dot(x, w1b[...], preferred_element_type=F32)
    send_rows(1 - slot, slot)
    h3 = jnp.dot(x, w3b[...], preferred_element_type=F32)
    act = (jax.nn.silu(h1) * h3).astype(BF16)
    y = jnp.dot(act, w2b[...], preferred_element_type=F32)

    @pl.when(blk > 0)
    def _():
        wait_rows(1 - slot)

    for j in range(tiles):
        ybuf[slot, pl.ds(j, bm, stride=SUBLANES), :] = y[:, j * LANES:(j + 1) * LANES]

    @pl.when(blk == last)
    def _():
        table_copy(blk + 1, 1 - slot).wait()
        send_rows(slot, 1 - slot)
        wait_rows(slot)
        wait_rows(1 - slot)


def moe_experts(block_expert, dst, xs2d, w1, w3, w2, n_out_rows, bm):
    n_blocks = dst.shape[0] - 1
    grid_spec = pltpu.PrefetchScalarGridSpec(
        num_scalar_prefetch=1,
        grid=(n_blocks,),
        in_specs=[
            pl.BlockSpec(memory_space=pl.ANY),
            pl.BlockSpec((bm * SUBLANES, LANES), lambda b, be: (b, 0)),
            pl.BlockSpec((None, D_MODEL, D_EXPERT), lambda b, be: (be[b], 0, 0)),
            pl.BlockSpec((None, D_MODEL, D_EXPERT), lambda b, be: (be[b], 0, 0)),
            pl.BlockSpec((None, D_EXPERT, D_MODEL), lambda b, be: (be[b], 0, 0)),
        ],
        out_specs=pl.BlockSpec(memory_space=pl.ANY),
        scratch_shapes=[
            pltpu.SMEM((2, SUBLANES, LANES), jnp.int32),
            pltpu.VMEM((2, bm * SUBLANES, LANES), F32),
            pltpu.VMEM((D_MODEL, D_EXPERT), BF16),
            pltpu.VMEM((D_MODEL, D_EXPERT), BF16),
            pltpu.VMEM((D_EXPERT, D_MODEL), BF16),
            pltpu.SemaphoreType.DMA((2,)),
            pltpu.SemaphoreType.DMA((2,)),
        ],
    )
    return pl.pallas_call(
        functools.partial(_moe_kernel, bm=bm),
        grid_spec=grid_spec,
        out_shape=jax.ShapeDtypeStruct((n_out_rows, SUBLANES, LANES), F32),
        compiler_params=_cparams(("arbitrary",)),
        name="moe_experts",
    )(block_expert, dst, xs2d, w1, w3, w2)


def _as_tables(cols, width):
    fill = jnp.zeros((cols.shape[0], TABLE_WORDS - width), jnp.int32)
    return jnp.concatenate([cols, fill], axis=1).reshape(cols.shape[0], SUBLANES, LANES)


def route_tables(route, bm, tm):
    n_tok = route.shape[0]
    n_pairs = n_tok * TOP_K
    lanes = route[:, :SUBLANES].T.astype(jnp.int32)
    e_slot = [lanes[2 + k] for k in range(TOP_K)]
    rank = [lanes[2 + TOP_K + k] for k in range(TOP_K)]
    e_flat = jnp.stack(e_slot, axis=1).reshape(n_pairs)
    experts = jnp.arange(N_EXPERTS, dtype=jnp.int32)
    counts = jnp.sum(experts[:, None] == e_flat[None, :], axis=1, dtype=jnp.int32)

    def lookup(table, idx):
        return jnp.sum(jnp.where(experts[:, None] == idx[None, :], table[:, None], 0), axis=0)

    padded = (counts + bm - 1) // bm * bm
    pad_ends = jnp.cumsum(padded)
    n_blocks = -(-n_pairs // bm) + N_EXPERTS
    n_rows = n_blocks * bm
    row0 = jnp.arange(n_blocks, dtype=jnp.int32) * bm
    block_expert = jnp.minimum(jnp.sum(pad_ends[None, :] <= row0[:, None], axis=1, dtype=jnp.int32),
                               N_EXPERTS - 1)
    n_fill = n_rows - n_pairs
    fill_id = jnp.arange(n_fill, dtype=jnp.int32)
    fill_e = jnp.minimum(fill_id // bm, N_EXPERTS - 1)
    fill_used = jnp.logical_and(fill_id < N_EXPERTS * bm, fill_id % bm < (padded - counts)[fill_e])
    shift = n_pairs.bit_length()
    keys = jnp.concatenate([e_flat * 2, jnp.where(fill_used, fill_e * 2 + 1, 2 * N_EXPERTS)])
    ids = jnp.concatenate([jnp.arange(1, n_pairs + 1, dtype=jnp.int32), jnp.zeros((n_fill,), jnp.int32)])
    pair_at = (lax.sort(keys * (1 << shift) + ids) & ((1 << shift) - 1)) - 1
    pad_starts = pad_ends - padded
    pair_row = [lookup(pad_starts, e_slot[k]) + rank[k] for k in range(TOP_K)]
    n_used_before = jnp.cumsum(fill_used.astype(jnp.int32)) - fill_used.astype(jnp.int32)
    pad_row = jnp.where(fill_used, (pad_starts + counts)[fill_e] + fill_id % bm,
                        pad_ends[-1] + fill_id - n_used_before)
    n_tiles = n_tok // tm
    n_pad = n_fill // n_tiles
    assert n_pad * n_tiles == n_fill and n_pad % SUBLANES == 0
    pos = jnp.concatenate([pair_row[k].reshape(n_tiles, tm) for k in range(TOP_K)]
                          + [pad_row.reshape(n_tiles, n_pad)], axis=1)
    pair_at = pair_at.reshape(n_blocks, bm)
    valid = pair_at >= 0
    bank = (jnp.arange(-1, n_blocks, dtype=jnp.int32) % 2)[:, None] * bm
    trash = n_pairs + bank + jnp.arange(bm, dtype=jnp.int32)[None, :]
    dst = jnp.where(valid, (pair_at % TOP_K) * n_tok + pair_at // TOP_K, trash[1:])
    dst = jnp.concatenate([trash[:1], dst], axis=0)
    return (block_expert, _as_tables(dst, bm), _as_tables(pos, TOP_K * tm + n_pad), n_pad, n_rows,
            n_pairs + 2 * bm)


def _final_kernel(h_ref, p0_ref, p1_ref, gw_ref, nf_ref, y_ref, *, tm):
    tiles = D_MODEL // LANES
    p0 = jnp.concatenate([p0_ref[pl.ds(j, tm, stride=SUBLANES), :] for j in range(tiles)], axis=1)
    p1 = jnp.concatenate([p1_ref[pl.ds(j, tm, stride=SUBLANES), :] for j in range(tiles)], axis=1)
    gw = gw_ref[...]
    h = h_ref[...] + (gw[:, 0:1] * p0 + gw[:, 1:2] * p1)
    y_ref[...] = _rms(h, nf_ref[...])


def final(h, pairs2d, gate_pad, norm_f, tm):
    n = h.shape[0]
    tm = min(tm, n)
    nb = n // tm
    row = lambda w: pl.BlockSpec((tm, w), lambda i: (i, 0))
    return pl.pallas_call(
        functools.partial(_final_kernel, tm=tm),
        grid=(nb,),
        in_specs=[
            row(D_MODEL),
            pl.BlockSpec((tm * SUBLANES, LANES), lambda i: (i, 0)),
            pl.BlockSpec((tm * SUBLANES, LANES), lambda i: (nb + i, 0)),
            row(LANES),
            pl.BlockSpec((1, D_MODEL), lambda i: (0, 0)),
        ],
        out_specs=row(D_MODEL),
        out_shape=jax.ShapeDtypeStruct((n, D_MODEL), F32),
        compiler_params=_cparams(("arbitrary",)),
        name="final",
    )(h, pairs2d, pairs2d, gate_pad, norm_f.reshape(1, D_MODEL))


def _alibi_slopes():
    h = jnp.arange(1, N_ATT_HEADS + 1, dtype=F32)
    return jnp.exp2(-8.0 * h / N_ATT_HEADS)


def _token_stage(att, ssm, mem, gates, x2, wts, w1, w3, w2, norm_f, tm, bm, precise=False):
    n = x2.shape[0]
    h, hn2d, route = merge(att, ssm, mem, gates, x2, wts, min(2 * tm, n), precise)
    tm = min(tm, n)
    block_expert, dst, pos, n_pad, n_rows, out_rows = route_tables(route, bm, tm)
    xs = moe_dispatch(pos, hn2d, n_rows, tm, n_pad)
    pairs = moe_experts(block_expert, dst, xs.reshape(n_rows * SUBLANES, LANES), w1, w3, w2, out_rows, bm)
    return final(h, pairs.reshape(out_rows * SUBLANES, LANES), route, norm_f, min(4 * tm, n))


def kernel(x_prompt, x_sample, cache_swa0_k, cache_swa0_v, cache_swa1_k, cache_swa1_v, cache_swa2_k, cache_swa2_v, cache_mem_k, cache_mem_v, state_ssm_re, state_ssm_im, mem_prompt, norm1, w_in, lam_re, lam_im, log_dt, ssm_b_re, ssm_b_im, ssm_c_re, ssm_c_im, ssm_d, w_glu, b_glu, w_att_out, w_ssm_out, w_mem_out, w_o, norm_mem, w_mk, w_mv, norm2, w_grp, b_grp, w_exp, b_exp, w1, w3, w2, norm_f):
    assert norm1.shape[0] == 1, "single-layer trunk"
    bsz, seq, _ = x_prompt.shape
    sbz = x_sample.shape[0]
    n_p = bsz * seq
    caches = (cache_swa0_k[0], cache_swa0_v[0], cache_swa1_k[0], cache_swa1_v[0], cache_swa2_k[0], cache_swa2_v[0])

    w_in_b = w_in[0].astype(BF16)
    w_mkv_b = jnp.concatenate([w_mk[0], w_mv[0]], axis=1).astype(BF16)
    wr = jnp.zeros((D_MODEL, ROUTER_COLS), F32)
    wr = wr.at[:, :MOE_GROUPS].set(w_grp[0]).at[:, MOE_GROUPS:MOE_GROUPS + N_EXPERTS].set(w_exp[0])
    br = jnp.zeros((1, ROUTER_COLS), F32)
    br = br.at[0, :MOE_GROUPS].set(b_grp[0]).at[0, MOE_GROUPS:MOE_GROUPS + N_EXPERTS].set(b_exp[0])
    wts_f32 = dict(wa=w_att_out[0], ws=w_ssm_out[0], wm=w_mem_out[0], wo=w_o[0],
                   n2=norm2[0].reshape(1, D_MODEL), wr=wr, br=br)
    wts = {k: (v.astype(BF16) if k in ("wa", "ws", "wm", "wo", "wr") else v) for k, v in wts_f32.items()}
    prm = ssm_params(lam_re[0], lam_im[0], log_dt[0], ssm_b_re[0], ssm_b_im[0], ssm_c_re[0], ssm_c_im[0],
                     ssm_d[0], w_glu[0], b_glu[0])
    slopes = _alibi_slopes()
    splits = (QKV_COLS, SSM_WIDTH, MEM_WIDTH, N_BRANCHES * D_MODEL)
    sig = (False, False, False, True)

    x2 = x_prompt.reshape(n_p, D_MODEL)
    qkv, u, mq, gates = norm_matmul(x2, norm1[0], w_in_b, splits, sig, tm=256, dtypes=(F32, F32, BF16, BF16))
    qkv3 = qkv.reshape(bsz, seq, QKV_COLS)
    att = attn_prompt(qkv3, slopes)
    ssm, st = ssm_prompt(u.reshape(bsz, seq, SSM_WIDTH), prm)
    mk, mv = norm_matmul(mem_prompt.reshape(bsz * MEM_TOKENS, D_MODEL), norm_mem[0], w_mkv_b,
                         (MEM_WIDTH, MEM_WIDTH), (False, False), tm=256)
    mk3 = mk.reshape(bsz, MEM_TOKENS, MEM_WIDTH)
    mv3 = mv.reshape(bsz, MEM_TOKENS, MEM_WIDTH)
    mem = mem_attention(mq.reshape(bsz, seq, MEM_WIDTH), mk3, mv3, tl=2048, out_dtype=BF16)
    y_p = _token_stage(att.reshape(n_p, ATT_OUT), ssm.reshape(n_p, SSM_WIDTH), mem.reshape(n_p, MEM_WIDTH),
                       gates, x2, wts, w1[0], w3[0], w2[0], norm_f, tm=256, bm=256)

    xs2 = x_sample.reshape(sbz, D_MODEL)
    qkv_s, u_s, mq_s, gates_s = norm_matmul(xs2, norm1[0], w_in[0], splits, sig, tm=sbz, precise=True)
    qkv5 = qkv_s.reshape(sbz, len(ATT_GROUPS), 3, HEADS_PER_GROUP, HEAD_DIM)
    att_s, *new_caches = attn_sample(qkv5, [jnp.transpose(c, (0, 2, 3, 1)) for c in caches])
    att_s = att_s.reshape(sbz, ATT_OUT)
    h0 = jnp.concatenate([_state_to_blocks(state_ssm_re[0]), _state_to_blocks(state_ssm_im[0])], axis=2)
    ssm_s, st_s = ssm_sample(u_s, h0, prm)
    cmk = cache_mem_k[0].reshape(sbz, MEM_TOKENS, MEM_WIDTH)
    cmv = cache_mem_v[0].reshape(sbz, MEM_TOKENS, MEM_WIDTH)
    mem_s = mem_attention(mq_s.reshape(sbz, 1, MEM_WIDTH), cmk, cmv, tl=1, precise=True).reshape(sbz, MEM_WIDTH)
    y_s = _token_stage(att_s, ssm_s, mem_s, gates_s, xs2, wts_f32, w1[0], w3[0], w2[0], norm_f, tm=sbz, bm=16,
                       precise=True)

    outs = [y_p.reshape(bsz, seq, D_MODEL), y_s.reshape(sbz, 1, D_MODEL)]
    for g, (window, _) in enumerate(ATT_GROUPS):
        keep = min(window, seq)
        for part in (1, 2):
            c0 = g * QKV_GROUP + part * ATT_OUT
            outs.append(qkv3[:, seq - keep:, c0:c0 + ATT_OUT].reshape(1, bsz, keep, HEADS_PER_GROUP, HEAD_DIM))
    outs.append(mk3.reshape(1, bsz, MEM_TOKENS, MEM_HEADS, MEM_HEAD_DIM))
    outs.append(mv3.reshape(1, bsz, MEM_TOKENS, MEM_HEADS, MEM_HEAD_DIM))
    outs.append(_blocks_to_state(st[:, :, :SSM_KSTATE])[None])
    outs.append(_blocks_to_state(st[:, :, SSM_KSTATE:])[None])
    outs.extend(jnp.transpose(c, (0, 3, 1, 2))[None] for c in new_caches)
    outs.append(_blocks_to_state(st_s[:, :, :SSM_KSTATE])[None])
    outs.append(_blocks_to_state(st_s[:, :, SSM_KSTATE:])[None])
    return tuple(outs)
```

```python
import functools
import math

import jax
import jax.numpy as jnp
from jax import lax
from jax.experimental import pallas as pl
from jax.experimental.pallas import tpu as pltpu

F32 = jnp.float32
BF16 = jnp.bfloat16

D_MODEL = 1024
ATT_GROUPS = ((128, 1), (512, 4), (2048, 16))
HEADS_PER_GROUP = 4
HEAD_DIM = 64
N_ATT_HEADS = len(ATT_GROUPS) * HEADS_PER_GROUP
ATT_OUT = HEADS_PER_GROUP * HEAD_DIM
QKV_GROUP = 3 * ATT_OUT
QKV_COLS = len(ATT_GROUPS) * QKV_GROUP
SSM_WIDTH = D_MODEL // 2
SSM_GROUP_CH = 16
SSM_GROUPS = SSM_WIDTH // SSM_GROUP_CH
SSM_STATE = 64
SSM_NSTATE = SSM_GROUPS * SSM_STATE
MEM_TOKENS = 256
MEM_HEADS = 4
MEM_HEAD_DIM = D_MODEL // 8
MEM_WIDTH = MEM_HEADS * MEM_HEAD_DIM
N_BRANCHES = 3
MOE_GROUPS = 4
EXPERTS_PER_GROUP = 8
N_EXPERTS = MOE_GROUPS * EXPERTS_PER_GROUP
TOP_K = 2
D_EXPERT = D_MODEL // 2
RMS_EPS = 1e-6

LANES = 128
SUBLANES = 8
Q_TILE = 128
ROUTER_COLS = 128
NEG_BIG = -1e30
VMEM_LIMIT = 56 * 1024 * 1024


def _cparams(sem):
    return pltpu.CompilerParams(dimension_semantics=sem, vmem_limit_bytes=VMEM_LIMIT)


def _mm(a, b, precise, dims=None):
    if precise:
        a, b, kw = a.astype(F32), b.astype(F32), dict(precision=lax.Precision.HIGHEST)
    else:
        a, b, kw = a.astype(BF16), b.astype(BF16), {}
    if dims is None:
        return jnp.dot(a, b, preferred_element_type=F32, **kw)
    return lax.dot_general(a, b, dims, preferred_element_type=F32, **kw)


NT_DIMS = (((1,), (1,)), ((), ()))


def _rms(x, g):
    ms = jnp.mean(x * x, axis=-1, keepdims=True)
    return (x * lax.rsqrt(ms + RMS_EPS)) * g


def _norm_matmul_kernel(x_ref, g_ref, w_ref, *out_refs, splits, sigmoid, chunk, precise):
    xb = _rms(x_ref[...], g_ref[...])
    if not precise:
        xb = xb.astype(BF16)
    c0 = 0
    for o_ref, n, sg in zip(out_refs, splits, sigmoid):
        for j in range(0, n, chunk):
            w = min(chunk, n - j)
            y = _mm(xb, w_ref[:, c0 + j:c0 + j + w], precise)
            if sg:
                y = jax.nn.sigmoid(y)
            o_ref[:, j:j + w] = y.astype(o_ref.dtype)
        c0 += n


def norm_matmul(x, g, w, splits, sigmoid, tm, dtypes=None, precise=False):
    n, d = x.shape
    tm = min(tm, n)
    kern = functools.partial(_norm_matmul_kernel, splits=tuple(splits), sigmoid=tuple(sigmoid), chunk=512,
                             precise=precise)
    return pl.pallas_call(
        kern,
        grid=(n // tm,),
        in_specs=[
            pl.BlockSpec((tm, d), lambda i: (i, 0)),
            pl.BlockSpec((1, d), lambda i: (0, 0)),
            pl.BlockSpec((d, w.shape[1]), lambda i: (0, 0), pipeline_mode=pl.Buffered(1)),
        ],
        out_specs=[pl.BlockSpec((tm, s), lambda i: (i, 0)) for s in splits],
        out_shape=[jax.ShapeDtypeStruct((n, s), dt) for s, dt in zip(splits, dtypes or (F32,) * len(splits))],
        compiler_params=_cparams(("arbitrary",)),
        name="norm_matmul",
    )(x, g.reshape(1, d), w)


def _attn_prompt_kernel(slopes_ref, *refs, seq):
    qkv_refs = refs[:9]
    o_ref = refs[9]
    o_scr, l_scr, bias_scr = refs[10], refs[11], refs[12]
    hp = pl.program_id(1)
    scale = HEAD_DIM ** -0.5
    row = lax.broadcasted_iota(jnp.int32, (Q_TILE, 2 * Q_TILE), 0)
    col = lax.broadcasted_iota(jnp.int32, (Q_TILE, 2 * Q_TILE), 1)
    dist = (row + Q_TILE - col)
    in_window = jnp.logical_and(dist >= 0, dist <= Q_TILE)
    distf = dist.astype(F32)
    lane = lax.broadcasted_iota(jnp.int32, (Q_TILE, LANES), 1)
    head_a = lane < HEAD_DIM

    for g, (_, dil) in enumerate(ATT_GROUPS):
        q_ref, k_ref, v_ref = qkv_refs[3 * g:3 * g + 3]
        sub_len = seq // dil
        n_blk = sub_len // Q_TILE
        slope_a = slopes_ref[g * HEADS_PER_GROUP + 2 * hp] * float(dil)
        slope_b = slopes_ref[g * HEADS_PER_GROUP + 2 * hp + 1] * float(dil)

        def rows(start):
            if dil == 1:
                return pl.ds(start, Q_TILE)
            return pl.ds(start, Q_TILE, stride=dil)

        for h, slope in enumerate((slope_a, slope_b)):
            bias = jnp.where(in_window, -slope * distf, NEG_BIG)
            bias_scr[2 * h] = jnp.where(col >= Q_TILE, bias, NEG_BIG)
            bias_scr[2 * h + 1] = bias

        def body(it, carry, q_ref=q_ref, k_ref=k_ref, v_ref=v_ref, n_blk=n_blk, dil=dil,
                 rows=rows, g=g):
            r = it // n_blk
            blk = it % n_blk
            prev = jnp.maximum(blk - 1, 0)
            q_start = r + dil * Q_TILE * blk
            p_start = r + dil * Q_TILE * prev
            q = q_ref[rows(q_start), :]
            k2 = jnp.concatenate([k_ref[rows(p_start), :], k_ref[rows(q_start), :]], axis=0).astype(BF16)
            v2 = jnp.concatenate([v_ref[rows(p_start), :], v_ref[rows(q_start), :]], axis=0).astype(BF16)
            later = jnp.minimum(blk, 1)
            outs, lses = [], []
            for h in range(2):
                hmask = head_a if h == 0 else jnp.logical_not(head_a)
                qh = jnp.where(hmask, q * scale, 0.0).astype(BF16)
                s = lax.dot_general(qh, k2, NT_DIMS, preferred_element_type=F32) + bias_scr[2 * h + later]
                m = jnp.max(s, axis=-1, keepdims=True)
                p = jnp.exp(s - m)
                den = jnp.sum(p, axis=-1, keepdims=True)
                pv = jnp.dot(p.astype(BF16), v2, preferred_element_type=F32)
                outs.append(pv / den)
                lses.append(m + jnp.log(den))
            o_scr[g, rows(q_start), :] = jnp.where(head_a, outs[0], outs[1])
            l_scr[g, rows(q_start), :] = jnp.where(head_a, lses[0], lses[1])
            return carry

        lax.fori_loop(0, dil * n_blk, body, 0, unroll=8)

    def combine(c, carry):
        sl = pl.ds(pl.multiple_of(c * 256, 256), 256)
        l0, l1, l2 = l_scr[0, sl, :], l_scr[1, sl, :], l_scr[2, sl, :]
        m = jnp.maximum(jnp.maximum(l0, l1), l2)
        e0, e1, e2 = jnp.exp(l0 - m), jnp.exp(l1 - m), jnp.exp(l2 - m)
        tot = e0 + e1 + e2
        att = (e0 * o_scr[0, sl, :] + e1 * o_scr[1, sl, :] + e2 * o_scr[2, sl, :]) / tot
        o_ref[sl, :] = att.astype(o_ref.dtype)
        return carry

    lax.fori_loop(0, seq // 256, combine, 0)


def attn_prompt(qkv, slopes):
    bsz, seq, _ = qkv.shape
    in_specs = []
    for g in range(len(ATT_GROUPS)):
        for part in range(3):
            base = (g * QKV_GROUP + part * ATT_OUT) // LANES
            in_specs.append(pl.BlockSpec((None, seq, LANES),
                                         lambda b, h, sl, base=base: (b, 0, base + h)))
    grid_spec = pltpu.PrefetchScalarGridSpec(
        num_scalar_prefetch=1,
        grid=(bsz, ATT_OUT // LANES),
        in_specs=in_specs,
        out_specs=pl.BlockSpec((None, seq, LANES), lambda b, h, sl: (b, 0, h)),
        scratch_shapes=[pltpu.VMEM((3, seq, LANES), F32), pltpu.VMEM((3, seq, LANES), F32),
                        pltpu.VMEM((4, Q_TILE, 2 * Q_TILE), F32)],
    )
    return pl.pallas_call(
        functools.partial(_attn_prompt_kernel, seq=seq),
        grid_spec=grid_spec,
        out_shape=jax.ShapeDtypeStruct((bsz, seq, ATT_OUT), BF16),
        compiler_params=_cparams(("arbitrary", "arbitrary")),
        name="attn_prompt",
    )(slopes, *([qkv] * 9))


def _attn_sample_kernel(qkv_ref, col_ref, *refs):
    cache_refs = refs[:6]
    o_ref = refs[6]
    new_refs = refs[7:13]
    scale = HEAD_DIM ** -0.5
    n_grp = len(ATT_GROUPS)
    outs = [[None] * HEADS_PER_GROUP for _ in range(n_grp)]
    lses = [[None] * HEADS_PER_GROUP for _ in range(n_grp)]
    for g, (window, dil) in enumerate(ATT_GROUPS):
        lane = lax.broadcasted_iota(jnp.int32, (1, window), 1)
        on_grid = jnp.bitwise_and(lane, dil - 1) == 0
        dist = (window - lane).astype(F32)
        last = lax.broadcasted_iota(jnp.int32, (HEAD_DIM, window), 1) == window - 1
        for h in range(HEADS_PER_GROUP):
            slope = 2.0 ** (-8.0 * (g * HEADS_PER_GROUP + h + 1) / N_ATT_HEADS)
            q = qkv_ref[g, 0][h:h + 1, :]
            kn = qkv_ref[g, 1][h:h + 1, :]
            vn = qkv_ref[g, 2][h:h + 1, :]
            kt = cache_refs[2 * g][h]
            vt = cache_refs[2 * g + 1][h]
            s = _mm(q, kt, True) * scale - slope * dist
            s = jnp.where(on_grid, s, NEG_BIG)
            sn = jnp.sum(q * kn, axis=-1, keepdims=True) * scale
            m = jnp.maximum(jnp.max(s, axis=-1, keepdims=True), sn)
            p = jnp.exp(s - m)
            pn = jnp.exp(sn - m)
            den = jnp.sum(p, axis=-1, keepdims=True) + pn
            outs[g][h] = (_mm(p, vt, True, NT_DIMS) + pn * vn) / den
            lses[g][h] = m + jnp.log(den)
            new_refs[2 * g][h] = jnp.where(last, col_ref[g, 0, h], pltpu.roll(kt, window - 1, axis=1))
            new_refs[2 * g + 1][h] = jnp.where(last, col_ref[g, 1, h], pltpu.roll(vt, window - 1, axis=1))
    for h in range(HEADS_PER_GROUP):
        m = jnp.maximum(jnp.maximum(lses[0][h], lses[1][h]), lses[2][h])
        es = [jnp.exp(lses[g][h] - m) for g in range(n_grp)]
        num = es[0] * outs[0][h] + es[1] * outs[1][h] + es[2] * outs[2][h]
        o_ref[h:h + 1, :] = num / (es[0] + es[1] + es[2])


def attn_sample(qkv5, caches_t):
    bsz = qkv5.shape[0]
    cols = qkv5[:, :, 1:3].reshape(bsz, len(ATT_GROUPS), 2, HEADS_PER_GROUP, HEAD_DIM, 1)
    in_specs = [pl.BlockSpec((None, 3, 3, HEADS_PER_GROUP, HEAD_DIM), lambda b: (b, 0, 0, 0, 0)),
                pl.BlockSpec((None, 3, 2, HEADS_PER_GROUP, HEAD_DIM, 1), lambda b: (b, 0, 0, 0, 0, 0))]
    cache_specs = [pl.BlockSpec((None,) + c.shape[1:], lambda b: (b, 0, 0, 0)) for c in caches_t]
    return pl.pallas_call(
        _attn_sample_kernel,
        grid=(bsz,),
        in_specs=in_specs + cache_specs,
        out_specs=[pl.BlockSpec((None, HEADS_PER_GROUP, HEAD_DIM), lambda b: (b, 0, 0))] + cache_specs,
        out_shape=[jax.ShapeDtypeStruct((bsz, HEADS_PER_GROUP, HEAD_DIM), F32)]
                  + [jax.ShapeDtypeStruct(c.shape, F32) for c in caches_t],
        compiler_params=_cparams(("arbitrary",)),
        name="attn_sample",
    )(qkv5, cols, *caches_t)


SSM_KBLK = SSM_WIDTH // LANES
SSM_KSTATE = SSM_NSTATE // SSM_KBLK


def _ssm_tail(y, u, dsk, wglu, bglu, precise=False):
    y = y + dsk * u
    z = jax.nn.gelu(y)
    gl = _mm(z, wglu, precise) + bglu
    return z * jax.nn.sigmoid(gl)


def _ssm_prompt_kernel(u_ref, wb_ref, wc_ref, are_ref, aim_ref, dsk_ref, wglu_ref, bglu_ref,
                       o_ref, st_ref, uperm, bu, sbf, yperm, ynat, *, bsz, tt):
    rows = bsz * tt

    @pl.when(pl.program_id(0) == 0)
    def _():
        st_ref[...] = jnp.zeros_like(st_ref)

    for b in range(bsz):
        ub = u_ref[b]
        for k in range(SSM_KBLK):
            uperm[k, pl.ds(b, tt, stride=bsz), :] = ub[:, k * LANES:(k + 1) * LANES]

    for k in range(SSM_KBLK):
        bu[k] = jnp.dot(uperm[k].astype(BF16), wb_ref[k], preferred_element_type=F32)

    for k in range(SSM_KBLK):
        ar = jnp.broadcast_to(are_ref[k], (bsz, SSM_KSTATE))
        ai = jnp.broadcast_to(aim_ref[k], (bsz, SSM_KSTATE))
        st = st_ref[k]

        def step(t, carry, k=k, ar=ar, ai=ai):
            re, im = carry
            r0 = pl.multiple_of(t * bsz, bsz)
            b_t = bu[k, pl.ds(r0, bsz), :]
            nre = ar * re - ai * im + b_t[:, :SSM_KSTATE]
            nim = ar * im + ai * re + b_t[:, SSM_KSTATE:]
            sbf[k, pl.ds(r0, bsz), :] = jnp.concatenate([nre, nim], axis=1).astype(BF16)
            return nre, nim

        re, im = lax.fori_loop(0, tt, step, (st[:, :SSM_KSTATE], st[:, SSM_KSTATE:]), unroll=2)
        st_ref[k] = jnp.concatenate([re, im], axis=1)

    for k in range(SSM_KBLK):
        yperm[k] = jnp.dot(sbf[k], wc_ref[k], preferred_element_type=F32)

    for b in range(bsz):
        for k in range(SSM_KBLK):
            ynat[b * tt:(b + 1) * tt, k * LANES:(k + 1) * LANES] = yperm[k, pl.ds(b, tt, stride=bsz), :]

    u2 = u_ref[...].reshape(rows, SSM_WIDTH)
    out = _ssm_tail(ynat[...], u2, dsk_ref[...], wglu_ref[...], bglu_ref[...])
    o_ref[...] = out.reshape(bsz, tt, SSM_WIDTH).astype(o_ref.dtype)


def ssm_prompt(u, prm, tt=32):
    bsz, seq, _ = u.shape
    rows = bsz * tt
    const = lambda shape: pl.BlockSpec(shape, lambda i: (0,) * len(shape))
    return pl.pallas_call(
        functools.partial(_ssm_prompt_kernel, bsz=bsz, tt=tt),
        grid=(seq // tt,),
        in_specs=[
            pl.BlockSpec((bsz, tt, SSM_WIDTH), lambda i: (0, i, 0)),
            const((SSM_KBLK, LANES, 2 * SSM_KSTATE)),
            const((SSM_KBLK, 2 * SSM_KSTATE, LANES)),
            const((SSM_KBLK, 1, SSM_KSTATE)),
            const((SSM_KBLK, 1, SSM_KSTATE)),
            const((1, SSM_WIDTH)),
            const((SSM_WIDTH, SSM_WIDTH)),
            const((1, SSM_WIDTH)),
        ],
        out_specs=[
            pl.BlockSpec((bsz, tt, SSM_WIDTH), lambda i: (0, i, 0)),
            const((SSM_KBLK, bsz, 2 * SSM_KSTATE)),
        ],
        out_shape=[
            jax.ShapeDtypeStruct((bsz, seq, SSM_WIDTH), BF16),
            jax.ShapeDtypeStruct((SSM_KBLK, bsz, 2 * SSM_KSTATE), F32),
        ],
        scratch_shapes=[
            pltpu.VMEM((SSM_KBLK, rows, LANES), F32),
            pltpu.VMEM((SSM_KBLK, rows, 2 * SSM_KSTATE), F32),
            pltpu.VMEM((SSM_KBLK, rows, 2 * SSM_KSTATE), BF16),
            pltpu.VMEM((SSM_KBLK, rows, LANES), F32),
            pltpu.VMEM((rows, SSM_WIDTH), F32),
        ],
        compiler_params=_cparams(("arbitrary",)),
        name="ssm_prompt",
    )(u, prm["wb"], prm["wc"], prm["a_re"], prm["a_im"], prm["dsk"], prm["wglu"], prm["bglu"])


def _ssm_sample_kernel(u_ref, h0_ref, wb_ref, wc_ref, are_ref, aim_ref, dsk_ref, wglu_ref, bglu_ref,
                       o_ref, st_ref):
    u = u_ref[...]
    ys = []
    for k in range(SSM_KBLK):
        b_t = _mm(u[:, k * LANES:(k + 1) * LANES], wb_ref[k], True)
        h0 = h0_ref[k]
        re, im = h0[:, :SSM_KSTATE], h0[:, SSM_KSTATE:]
        ar, ai = are_ref[k], aim_ref[k]
        nre = ar * re - ai * im + b_t[:, :SSM_KSTATE]
        nim = ar * im + ai * re + b_t[:, SSM_KSTATE:]
        s = jnp.concatenate([nre, nim], axis=1)
        st_ref[k] = s
        ys.append(_mm(s, wc_ref[k], True))
    y = jnp.concatenate(ys, axis=1)
    o_ref[...] = _ssm_tail(y, u, dsk_ref[...], wglu_ref[...], bglu_ref[...], precise=True)


def ssm_sample(u, h0, prm):
    bsz = u.shape[0]
    return pl.pallas_call(
        _ssm_sample_kernel,
        out_shape=[jax.ShapeDtypeStruct((bsz, SSM_WIDTH), F32),
                   jax.ShapeDtypeStruct((SSM_KBLK, bsz, 2 * SSM_KSTATE), F32)],
        compiler_params=pltpu.CompilerParams(vmem_limit_bytes=VMEM_LIMIT),
        name="ssm_sample",
    )(u, h0, prm["wb_f32"], prm["wc_f32"], prm["a_re"], prm["a_im"], prm["dsk"], prm["wglu_f32"], prm["bglu"])


def ssm_params(lam_re, lam_im, log_dt, b_re, b_im, c_re, c_im, d_skip, w_glu, b_glu):
    dt = jnp.exp(log_dt)[:, None]
    mag = jnp.exp(lam_re * dt)
    ab_re = mag * jnp.cos(lam_im * dt)
    ab_im = mag * jnp.sin(lam_im * dt)
    den = lam_re * lam_re + lam_im * lam_im
    nr = ab_re - 1.0
    ni = ab_im
    z_re = ((nr * lam_re + ni * lam_im) / den)[..., None]
    z_im = ((ni * lam_re - nr * lam_im) / den)[..., None]
    bb_re = z_re * b_re - z_im * b_im
    bb_im = z_re * b_im + z_im * b_re
    gpb = SSM_GROUPS // SSM_KBLK
    eye = jnp.eye(gpb, dtype=F32)

    def in_mat(bb):
        bbk = bb.reshape(SSM_KBLK, gpb, SSM_STATE, SSM_GROUP_CH)
        m = jnp.einsum("kgpc,gh->kgchp", bbk, eye)
        return m.reshape(SSM_KBLK, LANES, SSM_KSTATE)

    def out_mat(c):
        ck = c.reshape(SSM_KBLK, gpb, SSM_GROUP_CH, SSM_STATE)
        m = jnp.einsum("kgcp,gh->kgphc", ck, eye)
        return m.reshape(SSM_KBLK, SSM_KSTATE, LANES)

    wb = jnp.concatenate([in_mat(bb_re), in_mat(bb_im)], axis=2)
    wc = jnp.concatenate([out_mat(c_re), -out_mat(c_im)], axis=1)
    return dict(
        wb=wb.astype(BF16), wc=wc.astype(BF16), wb_f32=wb, wc_f32=wc, wglu_f32=w_glu,
        a_re=ab_re.reshape(SSM_KBLK, 1, SSM_KSTATE), a_im=ab_im.reshape(SSM_KBLK, 1, SSM_KSTATE),
        dsk=d_skip.reshape(1, SSM_WIDTH), wglu=w_glu.astype(BF16), bglu=b_glu.reshape(1, SSM_WIDTH))


def _state_to_blocks(h):
    bsz = h.shape[0]
    return h.reshape(bsz, SSM_KBLK, SSM_KSTATE).transpose(1, 0, 2)


def _blocks_to_state(s):
    bsz = s.shape[1]
    return s.transpose(1, 0, 2).reshape(bsz, SSM_GROUPS, SSM_STATE)


def _mem_attn_kernel(q_ref, k_ref, v_ref, o_ref, *, precise):
    scale = MEM_HEAD_DIM ** -0.5
    for h in range(MEM_HEADS):
        sl = slice(h * MEM_HEAD_DIM, (h + 1) * MEM_HEAD_DIM)
        s = _mm(q_ref[:, sl], k_ref[:, sl], precise, NT_DIMS) * scale
        m = jnp.max(s, axis=-1, keepdims=True)
        p = jnp.exp(s - m)
        den = jnp.sum(p, axis=-1, keepdims=True)
        o_ref[:, sl] = _mm(p / den, v_ref[:, sl], precise).astype(o_ref.dtype)


def mem_attention(q, mk, mv, tl, out_dtype=F32, precise=False):
    bsz, seq, _ = q.shape
    tl = min(tl, seq)
    return pl.pallas_call(
        functools.partial(_mem_attn_kernel, precise=precise),
        grid=(bsz, seq // tl),
        in_specs=[
            pl.BlockSpec((None, tl, MEM_WIDTH), lambda b, i: (b, i, 0)),
            pl.BlockSpec((None, MEM_TOKENS, MEM_WIDTH), lambda b, i: (b, 0, 0)),
            pl.BlockSpec((None, MEM_TOKENS, MEM_WIDTH), lambda b, i: (b, 0, 0)),
        ],
        out_specs=pl.BlockSpec((None, tl, MEM_WIDTH), lambda b, i: (b, i, 0)),
        out_shape=jax.ShapeDtypeStruct((bsz, seq, MEM_WIDTH), out_dtype),
        compiler_params=_cparams(("arbitrary", "arbitrary")),
        name="mem_attention",
    )(q, mk, mv)


def _route(logits, seen_ref):
    lane = lax.broadcasted_iota(jnp.int32, logits.shape, 1)
    lanef = lane.astype(F32)
    none = float(ROUTER_COLS)
    neg = -jnp.inf
    gl = jnp.where(lane < MOE_GROUPS, logits, neg)
    gmax = jnp.max(gl, axis=-1, keepdims=True)
    grp = jnp.min(jnp.where(gl == gmax, lanef, none), axis=-1, keepdims=True)
    p_grp = 1.0 / jnp.sum(jnp.exp(gl - gmax), axis=-1, keepdims=True)
    lo = MOE_GROUPS + grp * EXPERTS_PER_GROUP
    el = jnp.where(jnp.logical_and(lanef >= lo, lanef < lo + EXPERTS_PER_GROUP), logits, neg)
    v1 = jnp.max(el, axis=-1, keepdims=True)
    i1 = jnp.min(jnp.where(el == v1, lanef, none), axis=-1, keepdims=True)
    el2 = jnp.where(lanef == i1, neg, el)
    v2 = jnp.max(el2, axis=-1, keepdims=True)
    i2 = jnp.min(jnp.where(el2 == v2, lanef, none), axis=-1, keepdims=True)
    t = jnp.exp(v2 - v1)
    g1 = p_grp / (1.0 + t)
    g2 = g1 * t
    e1 = i1 - MOE_GROUPS
    e2 = i2 - MOE_GROUPS
    oh1 = (lanef == e1).astype(F32)
    oh2 = (lanef == e2).astype(F32)
    both = oh1 + oh2
    tm = logits.shape[0]
    earlier = (lax.broadcasted_iota(jnp.int32, (tm, tm), 1) < lax.broadcasted_iota(jnp.int32, (tm, tm), 0))
    before = jnp.dot(earlier.astype(BF16), both.astype(BF16), preferred_element_type=F32) + seen_ref[...]
    r1 = jnp.sum(oh1 * before, axis=-1, keepdims=True)
    r2 = jnp.sum(oh2 * before, axis=-1, keepdims=True)
    seen_ref[...] += jnp.sum(both, axis=0, keepdims=True)
    out = jnp.where(lane == 0, g1, 0.0)
    out = jnp.where(lane == 1, g2, out)
    out = jnp.where(lane == 2, e1, out)
    out = jnp.where(lane == 3, e2, out)
    out = jnp.where(lane == 4, r1, out)
    return jnp.where(lane == 5, r2, out)


def _merge_kernel(att_ref, ssm_ref, mem_ref, gate_ref, x_ref, wa_ref, ws_ref, wm_ref, wo_ref,
                  n2_ref, wr_ref, br_ref, h_ref, hn_ref, lg_ref, seen_ref, *, tm, precise):
    @pl.when(pl.program_id(0) == 0)
    def _():
        seen_ref[...] = jnp.zeros_like(seen_ref)

    a = _mm(att_ref[...], wa_ref[...], precise)
    merged = gate_ref[:, 0:D_MODEL] * a
    s = _mm(ssm_ref[...], ws_ref[...], precise)
    merged = merged + gate_ref[:, D_MODEL:2 * D_MODEL] * s
    m = _mm(mem_ref[...], wm_ref[...], precise)
    merged = merged + gate_ref[:, 2 * D_MODEL:3 * D_MODEL] * m
    h = x_ref[...] + _mm(merged, wo_ref[...], precise)
    h_ref[...] = h
    hn = _rms(h, n2_ref[...])
    for j in range(D_MODEL // LANES):
        hn_ref[pl.ds(j, tm, stride=SUBLANES), :] = hn[:, j * LANES:(j + 1) * LANES]
    logits = _mm(hn, wr_ref[...], precise) + br_ref[...]
    lg_ref[...] = _route(logits, seen_ref)


def merge(att, ssm, mem, gates, x, wts, tm, precise=False):
    n = x.shape[0]
    tm = min(tm, n)
    row = lambda w: pl.BlockSpec((tm, w), lambda i: (i, 0))
    const = lambda a: pl.BlockSpec(a.shape, lambda i: (0, 0))
    ws = [wts["wa"], wts["ws"], wts["wm"], wts["wo"], wts["n2"], wts["wr"], wts["br"]]
    return pl.pallas_call(
        functools.partial(_merge_kernel, tm=tm, precise=precise),
        grid=(n // tm,),
        in_specs=[row(ATT_OUT), row(SSM_WIDTH), row(MEM_WIDTH), row(N_BRANCHES * D_MODEL), row(D_MODEL)]
                 + [const(w) for w in ws],
        out_specs=[row(D_MODEL), pl.BlockSpec((tm * SUBLANES, LANES), lambda i: (i, 0)), row(ROUTER_COLS)],
        out_shape=[jax.ShapeDtypeStruct((n, D_MODEL), F32),
                   jax.ShapeDtypeStruct((n * SUBLANES, LANES), F32),
                   jax.ShapeDtypeStruct((n, ROUTER_COLS), F32)],
        scratch_shapes=[pltpu.VMEM((1, ROUTER_COLS), F32)],
        compiler_params=_cparams(("arbitrary",)),
        name="merge",
    )(att, ssm, mem, gates, x, *ws)


TABLE_WORDS = SUBLANES * LANES


def _row_dma_loop(n, off, smem, slot, start):
    inner = min(n, LANES)
    assert n % inner == 0

    def body(r, carry):
        for j in range(inner):
            f = off + j
            start(smem[slot, f // LANES + r, f % LANES], r * inner + j, j % 2)
        return carry

    if n == inner:
        body(0, 0)
    else:
        lax.fori_loop(0, n // inner, body, 0)


def _tile_rows(i):
    start = i * SUBLANES
    return pl.ds(start if isinstance(start, int) else pl.multiple_of(start, SUBLANES), SUBLANES)


def _dispatch_kernel(pos_hbm, hn_ref, xs_hbm, pos_smem, zero, sem, psem, *, tm, n_pad):
    i = pl.program_id(0)
    slot = i % 2

    def pos_copy(b, s):
        return pltpu.make_async_copy(pos_hbm.at[b], pos_smem.at[s], psem.at[s])

    @pl.when(i == 0)
    def _():
        pos_copy(0, 0).start()

    @pl.when(i + 1 < pl.num_programs(0))
    def _():
        pos_copy(i + 1, 1 - slot).start()

    zero[...] = jnp.zeros_like(zero)
    pos_copy(i, slot).wait()

    def send_token(row, tok, prio):
        pltpu.make_async_copy(hn_ref.at[_tile_rows(tok)], xs_hbm.at[row], sem).start(priority=prio)

    def send_zero(row, _, prio):
        pltpu.make_async_copy(zero, xs_hbm.at[row], sem).start(priority=prio)

    for k in range(TOP_K):
        _row_dma_loop(tm, k * tm, pos_smem, slot, send_token)
    _row_dma_loop(n_pad, TOP_K * tm, pos_smem, slot, send_zero)
    whole, part = divmod(n_pad, tm)
    for _ in range(TOP_K + whole):
        pltpu.make_async_copy(hn_ref, hn_ref, sem).wait()
    if part:
        piece = hn_ref.at[pl.ds(0, part * SUBLANES)]
        pltpu.make_async_copy(piece, piece, sem).wait()


def moe_dispatch(pos, hn2d, n_rows, tm, n_pad):
    n_tiles = pos.shape[0]
    return pl.pallas_call(
        functools.partial(_dispatch_kernel, tm=tm, n_pad=n_pad),
        grid=(n_tiles,),
        in_specs=[pl.BlockSpec(memory_space=pl.ANY),
                  pl.BlockSpec((tm * SUBLANES, LANES), lambda i: (i, 0))],
        out_specs=pl.BlockSpec(memory_space=pl.ANY),
        out_shape=jax.ShapeDtypeStruct((n_rows, SUBLANES, LANES), F32),
        scratch_shapes=[pltpu.SMEM((2, SUBLANES, LANES), jnp.int32),
                        pltpu.VMEM((SUBLANES, LANES), F32),
                        pltpu.SemaphoreType.DMA, pltpu.SemaphoreType.DMA((2,))],
        compiler_params=_cparams(("arbitrary",)),
        name="moe_dispatch",
    )(pos, hn2d)


def _moe_kernel(be_ref, dst_hbm, xs_ref, w1_ref, w3_ref, w2_ref, out_hbm,
                dst_smem, ybuf, w1b, w3b, w2b, ssem, isem, *, bm):
    blk = pl.program_id(0)
    last = pl.num_programs(0) - 1
    slot = blk % 2
    tiles = D_MODEL // LANES

    def table_copy(entry, s):
        return pltpu.make_async_copy(dst_hbm.at[entry], dst_smem.at[s], isem.at[s])

    def wait_rows(s):
        pltpu.make_async_copy(ybuf.at[0], ybuf.at[0], ssem.at[s]).wait()

    def send_rows(buf, s):
        for i in range(bm):
            row = dst_smem[s, i // LANES, i % LANES]
            pltpu.make_async_copy(ybuf.at[buf, _tile_rows(i)], out_hbm.at[row], ssem.at[s]).start(priority=i % 2)

    @pl.when(blk == 0)
    def _():
        ybuf[1] = jnp.zeros(ybuf.shape[1:], F32)
        table_copy(0, 0).start()

    table_copy(blk + 1, 1 - slot).start()

    prev = jnp.maximum(blk - 1, 0)
    changed = jnp.logical_or(blk == 0, be_ref[blk] != be_ref[prev])

    @pl.when(changed)
    def _():
        w1b[...] = w1_ref[...].astype(BF16)
        w3b[...] = w3_ref[...].astype(BF16)
        w2b[...] = w2_ref[...].astype(BF16)

    table_copy(blk, slot).wait()
    x = jnp.concatenate([xs_ref[pl.ds(j, bm, stride=SUBLANES), :].astype(BF16) for j in range(tiles)], axis=1)
    h1 = jnp.dot(x, w1b[...], preferred_element_type=F32)
    send_rows(1 - slot, slot)
    h3 = jnp.dot(x, w3b[...], preferred_element_type=F32)
    act = (jax.nn.silu(h1) * h3).astype(BF16)
    y = jnp.dot(act, w2b[...], preferred_element_type=F32)

    @pl.when(blk > 0)
    def _():
        wait_rows(1 - slot)

    for j in range(tiles):
        ybuf[slot, pl.ds(j, bm, stride=SUBLANES), :] = y[:, j * LANES:(j + 1) * LANES]

    @pl.when(blk == last)
    def _():
        table_copy(blk + 1, 1 - slot).wait()
        send_rows(slot, 1 - slot)
        wait_rows(slot)
        wait_rows(1 - slot)


def moe_experts(block_expert, dst, xs2d, w1, w3, w2, n_out_rows, bm):
    n_blocks = dst.shape[0] - 1
    grid_spec = pltpu.PrefetchScalarGridSpec(
        num_scalar_prefetch=1,
        grid=(n_blocks,),
        in_specs=[
            pl.BlockSpec(memory_space=pl.ANY),
            pl.BlockSpec((bm * SUBLANES, LANES), lambda b, be: (b, 0)),
            pl.BlockSpec((None, D_MODEL, D_EXPERT), lambda b, be: (be[b], 0, 0)),
            pl.BlockSpec((None, D_MODEL, D_EXPERT), lambda b, be: (be[b], 0, 0)),
            pl.BlockSpec((None, D_EXPERT, D_MODEL), lambda b, be: (be[b], 0, 0)),
        ],
        out_specs=pl.BlockSpec(memory_space=pl.ANY),
        scratch_shapes=[
            pltpu.SMEM((2, SUBLANES, LANES), jnp.int32),
            pltpu.VMEM((2, bm * SUBLANES, LANES), F32),
            pltpu.VMEM((D_MODEL, D_EXPERT), BF16),
            pltpu.VMEM((D_MODEL, D_EXPERT), BF16),
            pltpu.VMEM((D_EXPERT, D_MODEL), BF16),
            pltpu.SemaphoreType.DMA((2,)),
            pltpu.SemaphoreType.DMA((2,)),
        ],
    )
    return pl.pallas_call(
        functools.partial(_moe_kernel, bm=bm),
        grid_spec=grid_spec,
        out_shape=jax.ShapeDtypeStruct((n_out_rows, SUBLANES, LANES), F32),
        compiler_params=_cparams(("arbitrary",)),
        name="moe_experts",
    )(block_expert, dst, xs2d, w1, w3, w2)


def _as_tables(cols, width):
    fill = jnp.zeros((cols.shape[0], TABLE_WORDS - width), jnp.int32)
    return jnp.concatenate([cols, fill], axis=1).reshape(cols.shape[0], SUBLANES, LANES)


def route_tables(route, bm, tm):
    n_tok = route.shape[0]
    n_pairs = n_tok * TOP_K
    lanes = route[:, :SUBLANES].T.astype(jnp.int32)
    e_slot = [lanes[2 + k] for k in range(TOP_K)]
    rank = [lanes[2 + TOP_K + k] for k in range(TOP_K)]
    e_flat = jnp.stack(e_slot, axis=1).reshape(n_pairs)
    experts = jnp.arange(N_EXPERTS, dtype=jnp.int32)
    counts = jnp.sum(experts[:, None] == e_flat[None, :], axis=1, dtype=jnp.int32)

    def lookup(table, idx):
        return jnp.sum(jnp.where(experts[:, None] == idx[None, :], table[:, None], 0), axis=0)

    padded = (counts + bm - 1) // bm * bm
    pad_ends = jnp.cumsum(padded)
    n_blocks = -(-n_pairs // bm) + N_EXPERTS
    n_rows = n_blocks * bm
    row0 = jnp.arange(n_blocks, dtype=jnp.int32) * bm
    block_expert = jnp.minimum(jnp.sum(pad_ends[None, :] <= row0[:, None], axis=1, dtype=jnp.int32),
                               N_EXPERTS - 1)
    n_fill = n_rows - n_pairs
    fill_id = jnp.arange(n_fill, dtype=jnp.int32)
    fill_e = jnp.minimum(fill_id // bm, N_EXPERTS - 1)
    fill_used = jnp.logical_and(fill_id < N_EXPERTS * bm, fill_id % bm < (padded - counts)[fill_e])
    shift = n_pairs.bit_length()
    keys = jnp.concatenate([e_flat * 2, jnp.where(fill_used, fill_e * 2 + 1, 2 * N_EXPERTS)])
    ids = jnp.concatenate([jnp.arange(1, n_pairs + 1, dtype=jnp.int32), jnp.zeros((n_fill,), jnp.int32)])
    pair_at = (lax.sort(keys * (1 << shift) + ids) & ((1 << shift) - 1)) - 1
    pad_starts = pad_ends - padded
    pair_row = [lookup(pad_starts, e_slot[k]) + rank[k] for k in range(TOP_K)]
    n_used_before = jnp.cumsum(fill_used.astype(jnp.int32)) - fill_used.astype(jnp.int32)
    pad_row = jnp.where(fill_used, (pad_starts + counts)[fill_e] + fill_id % bm,
                        pad_ends[-1] + fill_id - n_used_before)
    n_tiles = n_tok // tm
    n_pad = n_fill // n_tiles
    assert n_pad * n_tiles == n_fill and n_pad % SUBLANES == 0
    pos = jnp.concatenate([pair_row[k].reshape(n_tiles, tm) for k in range(TOP_K)]
                          + [pad_row.reshape(n_tiles, n_pad)], axis=1)
    pair_at = pair_at.reshape(n_blocks, bm)
    valid = pair_at >= 0
    bank = (jnp.arange(-1, n_blocks, dtype=jnp.int32) % 2)[:, None] * bm
    trash = n_pairs + bank + jnp.arange(bm, dtype=jnp.int32)[None, :]
    dst = jnp.where(valid, (pair_at % TOP_K) * n_tok + pair_at // TOP_K, trash[1:])
    dst = jnp.concatenate([trash[:1], dst], axis=0)
    return (block_expert, _as_tables(dst, bm), _as_tables(pos, TOP_K * tm + n_pad), n_pad, n_rows,
            n_pairs + 2 * bm)


def _final_kernel(h_ref, p0_ref, p1_ref, gw_ref, nf_ref, y_ref, *, tm):
    tiles = D_MODEL // LANES
    p0 = jnp.concatenate([p0_ref[pl.ds(j, tm, stride=SUBLANES), :] for j in range(tiles)], axis=1)
    p1 = jnp.concatenate([p1_ref[pl.ds(j, tm, stride=SUBLANES), :] for j in range(tiles)], axis=1)
    gw = gw_ref[...]
    h = h_ref[...] + (gw[:, 0:1] * p0 + gw[:, 1:2] * p1)
    y_ref[...] = _rms(h, nf_ref[...])


def final(h, pairs2d, gate_pad, norm_f, tm):
    n = h.shape[0]
    tm = min(tm, n)
    nb = n // tm
    row = lambda w: pl.BlockSpec((tm, w), lambda i: (i, 0))
    return pl.pallas_call(
        functools.partial(_final_kernel, tm=tm),
        grid=(nb,),
        in_specs=[
            row(D_MODEL),
            pl.BlockSpec((tm * SUBLANES, LANES), lambda i: (i, 0)),
            pl.BlockSpec((tm * SUBLANES, LANES), lambda i: (nb + i, 0)),
            row(LANES),
            pl.BlockSpec((1, D_MODEL), lambda i: (0, 0)),
        ],
        out_specs=row(D_MODEL),
        out_shape=jax.ShapeDtypeStruct((n, D_MODEL), F32),
        compiler_params=_cparams(("arbitrary",)),
        name="final",
    )(h, pairs2d, pairs2d, gate_pad, norm_f.reshape(1, D_MODEL))


def _alibi_slopes():
    h = jnp.arange(1, N_ATT_HEADS + 1, dtype=F32)
    return jnp.exp2(-8.0 * h / N_ATT_HEADS)


def _token_stage(att, ssm, mem, gates, x2, wts, w1, w3, w2, norm_f, tm, bm, precise=False):
    n = x2.shape[0]
    h, hn2d, route = merge(att, ssm, mem, gates, x2, wts, min(2 * tm, n), precise)
    tm = min(tm, n)
    block_expert, dst, pos, n_pad, n_rows, out_rows = route_tables(route, bm, tm)
    xs = moe_dispatch(pos, hn2d, n_rows, tm, n_pad)
    pairs = moe_experts(block_expert, dst, xs.reshape(n_rows * SUBLANES, LANES), w1, w3, w2, out_rows, bm)
    return final(h, pairs.reshape(out_rows * SUBLANES, LANES), route, norm_f, min(4 * tm, n))


def kernel(x_prompt, x_sample, cache_swa0_k, cache_swa0_v, cache_swa1_k, cache_swa1_v, cache_swa2_k, cache_swa2_v, cache_mem_k, cache_mem_v, state_ssm_re, state_ssm_im, mem_prompt, norm1, w_in, lam_re, lam_im, log_dt, ssm_b_re, ssm_b_im, ssm_c_re, ssm_c_im, ssm_d, w_glu, b_glu, w_att_out, w_ssm_out, w_mem_out, w_o, norm_mem, w_mk, w_mv, norm2, w_grp, b_grp, w_exp, b_exp, w1, w3, w2, norm_f):
    assert norm1.shape[0] == 1, "single-layer trunk"
    bsz, seq, _ = x_prompt.shape
    sbz = x_sample.shape[0]
    n_p = bsz * seq
    caches = (cache_swa0_k[0], cache_swa0_v[0], cache_swa1_k[0], cache_swa1_v[0], cache_swa2_k[0], cache_swa2_v[0])

    w_in_b = w_in[0].astype(BF16)
    w_mkv_b = jnp.concatenate([w_mk[0], w_mv[0]], axis=1).astype(BF16)
    wr = jnp.zeros((D_MODEL, ROUTER_COLS), F32)
    wr = wr.at[:, :MOE_GROUPS].set(w_grp[0]).at[:, MOE_GROUPS:MOE_GROUPS + N_EXPERTS].set(w_exp[0])
    br = jnp.zeros((1, ROUTER_COLS), F32)
    br = br.at[0, :MOE_GROUPS].set(b_grp[0]).at[0, MOE_GROUPS:MOE_GROUPS + N_EXPERTS].set(b_exp[0])
    wts_f32 = dict(wa=w_att_out[0], ws=w_ssm_out[0], wm=w_mem_out[0], wo=w_o[0],
                   n2=norm2[0].reshape(1, D_MODEL), wr=wr, br=br)
    wts = {k: (v.astype(BF16) if k in ("wa", "ws", "wm", "wo", "wr") else v) for k, v in wts_f32.items()}
    prm = ssm_params(lam_re[0], lam_im[0], log_dt[0], ssm_b_re[0], ssm_b_im[0], ssm_c_re[0], ssm_c_im[0],
                     ssm_d[0], w_glu[0], b_glu[0])
    slopes = _alibi_slopes()
    splits = (QKV_COLS, SSM_WIDTH, MEM_WIDTH, N_BRANCHES * D_MODEL)
    sig = (False, False, False, True)

    x2 = x_prompt.reshape(n_p, D_MODEL)
    qkv, u, mq, gates = norm_matmul(x2, norm1[0], w_in_b, splits, sig, tm=256, dtypes=(F32, F32, BF16, BF16))
    qkv3 = qkv.reshape(bsz, seq, QKV_COLS)
    att = attn_prompt(qkv3, slopes)
    ssm, st = ssm_prompt(u.reshape(bsz, seq, SSM_WIDTH), prm)
    mk, mv = norm_matmul(mem_prompt.reshape(bsz * MEM_TOKENS, D_MODEL), norm_mem[0], w_mkv_b,
                         (MEM_WIDTH, MEM_WIDTH), (False, False), tm=256)
    mk3 = mk.reshape(bsz, MEM_TOKENS, MEM_WIDTH)
    mv3 = mv.reshape(bsz, MEM_TOKENS, MEM_WIDTH)
    mem = mem_attention(mq.reshape(bsz, seq, MEM_WIDTH), mk3, mv3, tl=2048, out_dtype=BF16)
    y_p = _token_stage(att.reshape(n_p, ATT_OUT), ssm.reshape(n_p, SSM_WIDTH), mem.reshape(n_p, MEM_WIDTH),
                       gates, x2, wts, w1[0], w3[0], w2[0], norm_f, tm=256, bm=256)

    xs2 = x_sample.reshape(sbz, D_MODEL)
    qkv_s, u_s, mq_s, gates_s = norm_matmul(xs2, norm1[0], w_in[0], splits, sig, tm=sbz, precise=True)
    qkv5 = qkv_s.reshape(sbz, len(ATT_GROUPS), 3, HEADS_PER_GROUP, HEAD_DIM)
    att_s, *new_caches = attn_sample(qkv5, [jnp.transpose(c, (0, 2, 3, 1)) for c in caches])
    att_s = att_s.reshape(sbz, ATT_OUT)
    h0 = jnp.concatenate([_state_to_blocks(state_ssm_re[0]), _state_to_blocks(state_ssm_im[0])], axis=2)
    ssm_s, st_s = ssm_sample(u_s, h0, prm)
    cmk = cache_mem_k[0].reshape(sbz, MEM_TOKENS, MEM_WIDTH)
    cmv = cache_mem_v[0].reshape(sbz, MEM_TOKENS, MEM_WIDTH)
    mem_s = mem_attention(mq_s.reshape(sbz, 1, MEM_WIDTH), cmk, cmv, tl=1, precise=True).reshape(sbz, MEM_WIDTH)
    y_s = _token_stage(att_s, ssm_s, mem_s, gates_s, xs2, wts_f32, w1[0], w3[0], w2[0], norm_f, tm=sbz, bm=16,
                       precise=True)

    outs = [y_p.reshape(bsz, seq, D_MODEL), y_s.reshape(sbz, 1, D_MODEL)]
    for g, (window, _) in enumerate(ATT_GROUPS):
        keep = min(window, seq)
        for part in (1, 2):
            c0 = g * QKV_GROUP + part * ATT_OUT
            outs.append(qkv3[:, seq - keep:, c0:c0 + ATT_OUT].reshape(1, bsz, keep, HEADS_PER_GROUP, HEAD_DIM))
    outs.append(mk3.reshape(1, bsz, MEM_TOKENS, MEM_HEADS, MEM_HEAD_DIM))
    outs.append(mv3.reshape(1, bsz, MEM_TOKENS, MEM_HEADS, MEM_HEAD_DIM))
    outs.append(_blocks_to_state(st[:, :, :SSM_KSTATE])[None])
    outs.append(_blocks_to_state(st[:, :, SSM_KSTATE:])[None])
    outs.extend(jnp.transpose(c, (0, 3, 1, 2))[None] for c in new_caches)
    outs.append(_blocks_to_state(st_s[:, :, :SSM_KSTATE])[None])
    outs.append(_blocks_to_state(st_s[:, :, SSM_KSTATE:])[None])
    return tuple(outs)
```

```python
import functools
import math

import jax
import jax.numpy as jnp
from jax import lax
from jax.experimental import pallas as pl
from jax.experimental.pallas import tpu as pltpu

F32 = jnp.float32
BF16 = jnp.bfloat16

D_MODEL = 1024
ATT_GROUPS = ((128, 1), (512, 4), (2048, 16))
HEADS_PER_GROUP = 4
HEAD_DIM = 64
N_ATT_HEADS = len(ATT_GROUPS) * HEADS_PER_GROUP
ATT_OUT = HEADS_PER_GROUP * HEAD_DIM
QKV_GROUP = 3 * ATT_OUT
QKV_COLS = len(ATT_GROUPS) * QKV_GROUP
SSM_WIDTH = D_MODEL // 2
SSM_GROUP_CH = 16
SSM_GROUPS = SSM_WIDTH // SSM_GROUP_CH
SSM_STATE = 64
SSM_NSTATE = SSM_GROUPS * SSM_STATE
MEM_TOKENS = 256
MEM_HEADS = 4
MEM_HEAD_DIM = D_MODEL // 8
MEM_WIDTH = MEM_HEADS * MEM_HEAD_DIM
N_BRANCHES = 3
MOE_GROUPS = 4
EXPERTS_PER_GROUP = 8
N_EXPERTS = MOE_GROUPS * EXPERTS_PER_GROUP
TOP_K = 2
D_EXPERT = D_MODEL // 2
RMS_EPS = 1e-6

LANES = 128
SUBLANES = 8
Q_TILE = 128
ROUTER_COLS = 128
NEG_BIG = -1e30
VMEM_LIMIT = 56 * 1024 * 1024


def _cparams(sem):
    return pltpu.CompilerParams(dimension_semantics=sem, vmem_limit_bytes=VMEM_LIMIT)


def _mm(a, b, precise, dims=None):
    if precise:
        a, b, kw = a.astype(F32), b.astype(F32), dict(precision=lax.Precision.HIGHEST)
    else:
        a, b, kw = a.astype(BF16), b.astype(BF16), {}
    if dims is None:
        return jnp.dot(a, b, preferred_element_type=F32, **kw)
    return lax.dot_general(a, b, dims, preferred_element_type=F32, **kw)


NT_DIMS = (((1,), (1,)), ((), ()))


def _rms(x, g):
    ms = jnp.mean(x * x, axis=-1, keepdims=True)
    return (x * lax.rsqrt(ms + RMS_EPS)) * g


def _norm_matmul_kernel(x_ref, g_ref, w_ref, *out_refs, splits, sigmoid, chunk, precise):
    xb = _rms(x_ref[...], g_ref[...])
    if not precise:
        xb = xb.astype(BF16)
    c0 = 0
    for o_ref, n, sg in zip(out_refs, splits, sigmoid):
        for j in range(0, n, chunk):
            w = min(chunk, n - j)
            y = _mm(xb, w_ref[:, c0 + j:c0 + j + w], precise)
            if sg:
                y = jax.nn.sigmoid(y)
            o_ref[:, j:j + w] = y.astype(o_ref.dtype)
        c0 += n


def norm_matmul(x, g, w, splits, sigmoid, tm, dtypes=None, precise=False):
    n, d = x.shape
    tm = min(tm, n)
    kern = functools.partial(_norm_matmul_kernel, splits=tuple(splits), sigmoid=tuple(sigmoid), chunk=512,
                             precise=precise)
    return pl.pallas_call(
        kern,
        grid=(n // tm,),
        in_specs=[
            pl.BlockSpec((tm, d), lambda i: (i, 0)),
            pl.BlockSpec((1, d), lambda i: (0, 0)),
            pl.BlockSpec((d, w.shape[1]), lambda i: (0, 0), pipeline_mode=pl.Buffered(1)),
        ],
        out_specs=[pl.BlockSpec((tm, s), lambda i: (i, 0)) for s in splits],
        out_shape=[jax.ShapeDtypeStruct((n, s), dt) for s, dt in zip(splits, dtypes or (F32,) * len(splits))],
        compiler_params=_cparams(("arbitrary",)),
        name="norm_matmul",
    )(x, g.reshape(1, d), w)


def _attn_prompt_kernel(slopes_ref, *refs, seq):
    qkv_refs = refs[:9]
    o_ref = refs[9]
    o_scr, l_scr, bias_scr = refs[10], refs[11], refs[12]
    hp = pl.program_id(1)
    scale = HEAD_DIM ** -0.5
    row = lax.broadcasted_iota(jnp.int32, (Q_TILE, 2 * Q_TILE), 0)
    col = lax.broadcasted_iota(jnp.int32, (Q_TILE, 2 * Q_TILE), 1)
    dist = (row + Q_TILE - col)
    in_window = jnp.logical_and(dist >= 0, dist <= Q_TILE)
    distf = dist.astype(F32)
    lane = lax.broadcasted_iota(jnp.int32, (Q_TILE, LANES), 1)
    head_a = lane < HEAD_DIM

    for g, (_, dil) in enumerate(ATT_GROUPS):
        q_ref, k_ref, v_ref = qkv_refs[3 * g:3 * g + 3]
        sub_len = seq // dil
        n_blk = sub_len // Q_TILE
        slope_a = slopes_ref[g * HEADS_PER_GROUP + 2 * hp] * float(dil)
        slope_b = slopes_ref[g * HEADS_PER_GROUP + 2 * hp + 1] * float(dil)

        def rows(start):
            if dil == 1:
                return pl.ds(start, Q_TILE)
            return pl.ds(start, Q_TILE, stride=dil)

        for h, slope in enumerate((slope_a, slope_b)):
            bias = jnp.where(in_window, -slope * distf, NEG_BIG)
            bias_scr[2 * h] = jnp.where(col >= Q_TILE, bias, NEG_BIG)
            bias_scr[2 * h + 1] = bias

        def body(it, carry, q_ref=q_ref, k_ref=k_ref, v_ref=v_ref, n_blk=n_blk, dil=dil,
                 rows=rows, g=g):
            r = it // n_blk
            blk = it % n_blk
            prev = jnp.maximum(blk - 1, 0)
            q_start = r + dil * Q_TILE * blk
            p_start = r + dil * Q_TILE * prev
            q = q_ref[rows(q_start), :]
            k2 = jnp.concatenate([k_ref[rows(p_start), :], k_ref[rows(q_start), :]], axis=0).astype(BF16)
            v2 = jnp.concatenate([v_ref[rows(p_start), :], v_ref[rows(q_start), :]], axis=0).astype(BF16)
            later = jnp.minimum(blk, 1)
            outs, lses = [], []
            for h in range(2):
                hmask = head_a if h == 0 else jnp.logical_not(head_a)
                qh = jnp.where(hmask, q * scale, 0.0).astype(BF16)
                s = lax.dot_general(qh, k2, NT_DIMS, preferred_element_type=F32) + bias_scr[2 * h + later]
                m = jnp.max(s, axis=-1, keepdims=True)
                p = jnp.exp(s - m)
                den = jnp.sum(p, axis=-1, keepdims=True)
                pv = jnp.dot(p.astype(BF16), v2, preferred_element_type=F32)
                outs.append(pv / den)
                lses.append(m + jnp.log(den))
            o_scr[g, rows(q_start), :] = jnp.where(head_a, outs[0], outs[1])
            l_scr[g, rows(q_start), :] = jnp.where(head_a, lses[0], lses[1])
            return carry

        lax.fori_loop(0, dil * n_blk, body, 0, unroll=16)

    def combine(c, carry):
        sl = pl.ds(pl.multiple_of(c * 256, 256), 256)
        l0, l1, l2 = l_scr[0, sl, :], l_scr[1, sl, :], l_scr[2, sl, :]
        m = jnp.maximum(jnp.maximum(l0, l1), l2)
        e0, e1, e2 = jnp.exp(l0 - m), jnp.exp(l1 - m), jnp.exp(l2 - m)
        tot = e0 + e1 + e2
        att = (e0 * o_scr[0, sl, :] + e1 * o_scr[1, sl, :] + e2 * o_scr[2, sl, :]) / tot
        o_ref[sl, :] = att.astype(o_ref.dtype)
        return carry

    lax.fori_loop(0, seq // 256, combine, 0)


def attn_prompt(qkv, slopes):
    bsz, seq, _ = qkv.shape
    in_specs = []
    for g in range(len(ATT_GROUPS)):
        for part in range(3):
            base = (g * QKV_GROUP + part * ATT_OUT) // LANES
            in_specs.append(pl.BlockSpec((None, seq, LANES),
                                         lambda b, h, sl, base=base: (b, 0, base + h)))
    grid_spec = pltpu.PrefetchScalarGridSpec(
        num_scalar_prefetch=1,
        grid=(bsz, ATT_OUT // LANES),
        in_specs=in_specs,
        out_specs=pl.BlockSpec((None, seq, LANES), lambda b, h, sl: (b, 0, h)),
        scratch_shapes=[pltpu.VMEM((3, seq, LANES), F32), pltpu.VMEM((3, seq, LANES), F32),
                        pltpu.VMEM((4, Q_TILE, 2 * Q_TILE), F32)],
    )
    return pl.pallas_call(
        functools.partial(_attn_prompt_kernel, seq=seq),
        grid_spec=grid_spec,
        out_shape=jax.ShapeDtypeStruct((bsz, seq, ATT_OUT), BF16),
        compiler_params=_cparams(("arbitrary", "arbitrary")),
        name="attn_prompt",
    )(slopes, *([qkv] * 9))


def _attn_sample_kernel(qkv_ref, col_ref, *refs):
    cache_refs = refs[:6]
    o_ref = refs[6]
    new_refs = refs[7:13]
    scale = HEAD_DIM ** -0.5
    n_grp = len(ATT_GROUPS)
    outs = [[None] * HEADS_PER_GROUP for _ in range(n_grp)]
    lses = [[None] * HEADS_PER_GROUP for _ in range(n_grp)]
    for g, (window, dil) in enumerate(ATT_GROUPS):
        lane = lax.broadcasted_iota(jnp.int32, (1, window), 1)
        on_grid = jnp.bitwise_and(lane, dil - 1) == 0
        dist = (window - lane).astype(F32)
        last = lax.broadcasted_iota(jnp.int32, (HEAD_DIM, window), 1) == window - 1
        for h in range(HEADS_PER_GROUP):
            slope = 2.0 ** (-8.0 * (g * HEADS_PER_GROUP + h + 1) / N_ATT_HEADS)
            q = qkv_ref[g, 0][h:h + 1, :]
            kn = qkv_ref[g, 1][h:h + 1, :]
            vn = qkv_ref[g, 2][h:h + 1, :]
            kt = cache_refs[2 * g][h]
            vt = cache_refs[2 * g + 1][h]
            s = _mm(q, kt, True) * scale - slope * dist
            s = jnp.where(on_grid, s, NEG_BIG)
            sn = jnp.sum(q * kn, axis=-1, keepdims=True) * scale
            m = jnp.maximum(jnp.max(s, axis=-1, keepdims=True), sn)
            p = jnp.exp(s - m)
            pn = jnp.exp(sn - m)
            den = jnp.sum(p, axis=-1, keepdims=True) + pn
            outs[g][h] = (_mm(p, vt, True, NT_DIMS) + pn * vn) / den
            lses[g][h] = m + jnp.log(den)
            new_refs[2 * g][h] = jnp.where(last, col_ref[g, 0, h], pltpu.roll(kt, window - 1, axis=1))
            new_refs[2 * g + 1][h] = jnp.where(last, col_ref[g, 1, h], pltpu.roll(vt, window - 1, axis=1))
    for h in range(HEADS_PER_GROUP):
        m = jnp.maximum(jnp.maximum(lses[0][h], lses[1][h]), lses[2][h])
        es = [jnp.exp(lses[g][h] - m) for g in range(n_grp)]
        num = es[0] * outs[0][h] + es[1] * outs[1][h] + es[2] * outs[2][h]
        o_ref[h:h + 1, :] = num / (es[0] + es[1] + es[2])


def attn_sample(qkv5, caches_t):
    bsz = qkv5.shape[0]
    cols = qkv5[:, :, 1:3].reshape(bsz, len(ATT_GROUPS), 2, HEADS_PER_GROUP, HEAD_DIM, 1)
    in_specs = [pl.BlockSpec((None, 3, 3, HEADS_PER_GROUP, HEAD_DIM), lambda b: (b, 0, 0, 0, 0)),
                pl.BlockSpec((None, 3, 2, HEADS_PER_GROUP, HEAD_DIM, 1), lambda b: (b, 0, 0, 0, 0, 0))]
    cache_specs = [pl.BlockSpec((None,) + c.shape[1:], lambda b: (b, 0, 0, 0)) for c in caches_t]
    return pl.pallas_call(
        _attn_sample_kernel,
        grid=(bsz,),
        in_specs=in_specs + cache_specs,
        out_specs=[pl.BlockSpec((None, HEADS_PER_GROUP, HEAD_DIM), lambda b: (b, 0, 0))] + cache_specs,
        out_shape=[jax.ShapeDtypeStruct((bsz, HEADS_PER_GROUP, HEAD_DIM), F32)]
                  + [jax.ShapeDtypeStruct(c.shape, F32) for c in caches_t],
        compiler_params=_cparams(("arbitrary",)),
        name="attn_sample",
    )(qkv5, cols, *caches_t)


SSM_KBLK = SSM_WIDTH // LANES
SSM_KSTATE = SSM_NSTATE // SSM_KBLK


def _ssm_tail(y, u, dsk, wglu, bglu, precise=False):
    y = y + dsk * u
    z = jax.nn.gelu(y)
    gl = _mm(z, wglu, precise) + bglu
    return z * jax.nn.sigmoid(gl)


def _ssm_prompt_kernel(u_ref, wb_ref, wc_ref, are_ref, aim_ref, dsk_ref, wglu_ref, bglu_ref,
                       o_ref, st_ref, uperm, bu, sbf, yperm, ynat, *, bsz, tt):
    rows = bsz * tt

    @pl.when(pl.program_id(0) == 0)
    def _():
        st_ref[...] = jnp.zeros_like(st_ref)

    for b in range(bsz):
        ub = u_ref[b]
        for k in range(SSM_KBLK):
            uperm[k, pl.ds(b, tt, stride=bsz), :] = ub[:, k * LANES:(k + 1) * LANES]

    for k in range(SSM_KBLK):
        bu[k] = jnp.dot(uperm[k].astype(BF16), wb_ref[k], preferred_element_type=F32)

    for k in range(SSM_KBLK):
        ar = jnp.broadcast_to(are_ref[k], (bsz, SSM_KSTATE))
        ai = jnp.broadcast_to(aim_ref[k], (bsz, SSM_KSTATE))
        st = st_ref[k]

        def step(t, carry, k=k, ar=ar, ai=ai):
            re, im = carry
            r0 = pl.multiple_of(t * bsz, bsz)
            b_t = bu[k, pl.ds(r0, bsz), :]
            nre = ar * re - ai * im + b_t[:, :SSM_KSTATE]
            nim = ar * im + ai * re + b_t[:, SSM_KSTATE:]
            sbf[k, pl.ds(r0, bsz), :] = jnp.concatenate([nre, nim], axis=1).astype(BF16)
            return nre, nim

        re, im = lax.fori_loop(0, tt, step, (st[:, :SSM_KSTATE], st[:, SSM_KSTATE:]), unroll=2)
        st_ref[k] = jnp.concatenate([re, im], axis=1)

    for k in range(SSM_KBLK):
        yperm[k] = jnp.dot(sbf[k], wc_ref[k], preferred_element_type=F32)

    for b in range(bsz):
        for k in range(SSM_KBLK):
            ynat[b * tt:(b + 1) * tt, k * LANES:(k + 1) * LANES] = yperm[k, pl.ds(b, tt, stride=bsz), :]

    u2 = u_ref[...].reshape(rows, SSM_WIDTH)
    out = _ssm_tail(ynat[...], u2, dsk_ref[...], wglu_ref[...], bglu_ref[...])
    o_ref[...] = out.reshape(bsz, tt, SSM_WIDTH).astype(o_ref.dtype)


def ssm_prompt(u, prm, tt=32):
    bsz, seq, _ = u.shape
    rows = bsz * tt
    const = lambda shape: pl.BlockSpec(shape, lambda i: (0,) * len(shape))
    return pl.pallas_call(
        functools.partial(_ssm_prompt_kernel, bsz=bsz, tt=tt),
        grid=(seq // tt,),
        in_specs=[
            pl.BlockSpec((bsz, tt, SSM_WIDTH), lambda i: (0, i, 0)),
            const((SSM_KBLK, LANES, 2 * SSM_KSTATE)),
            const((SSM_KBLK, 2 * SSM_KSTATE, LANES)),
            const((SSM_KBLK, 1, SSM_KSTATE)),
            const((SSM_KBLK, 1, SSM_KSTATE)),
            const((1, SSM_WIDTH)),
            const((SSM_WIDTH, SSM_WIDTH)),
            const((1, SSM_WIDTH)),
        ],
        out_specs=[
            pl.BlockSpec((bsz, tt, SSM_WIDTH), lambda i: (0, i, 0)),
            const((SSM_KBLK, bsz, 2 * SSM_KSTATE)),
        ],
        out_shape=[
            jax.ShapeDtypeStruct((bsz, seq, SSM_WIDTH), BF16),
            jax.ShapeDtypeStruct((SSM_KBLK, bsz, 2 * SSM_KSTATE), F32),
        ],
        scratch_shapes=[
            pltpu.VMEM((SSM_KBLK, rows, LANES), F32),
            pltpu.VMEM((SSM_KBLK, rows, 2 * SSM_KSTATE), F32),
            pltpu.VMEM((SSM_KBLK, rows, 2 * SSM_KSTATE), BF16),
            pltpu.VMEM((SSM_KBLK, rows, LANES), F32),
            pltpu.VMEM((rows, SSM_WIDTH), F32),
        ],
        compiler_params=_cparams(("arbitrary",)),
        name="ssm_prompt",
    )(u, prm["wb"], prm["wc"], prm["a_re"], prm["a_im"], prm["dsk"], prm["wglu"], prm["bglu"])


def _ssm_sample_kernel(u_ref, h0_ref, wb_ref, wc_ref, are_ref, aim_ref, dsk_ref, wglu_ref, bglu_ref,
                       o_ref, st_ref):
    u = u_ref[...]
    ys = []
    for k in range(SSM_KBLK):
        b_t = _mm(u[:, k * LANES:(k + 1) * LANES], wb_ref[k], True)
        h0 = h0_ref[k]
        re, im = h0[:, :SSM_KSTATE], h0[:, SSM_KSTATE:]
        ar, ai = are_ref[k], aim_ref[k]
        nre = ar * re - ai * im + b_t[:, :SSM_KSTATE]
        nim = ar * im + ai * re + b_t[:, SSM_KSTATE:]
        s = jnp.concatenate([nre, nim], axis=1)
        st_ref[k] = s
        ys.append(_mm(s, wc_ref[k], True))
    y = jnp.concatenate(ys, axis=1)
    o_ref[...] = _ssm_tail(y, u, dsk_ref[...], wglu_ref[...], bglu_ref[...], precise=True)


def ssm_sample(u, h0, prm):
    bsz = u.shape[0]
    return pl.pallas_call(
        _ssm_sample_kernel,
        out_shape=[jax.ShapeDtypeStruct((bsz, SSM_WIDTH), F32),
                   jax.ShapeDtypeStruct((SSM_KBLK, bsz, 2 * SSM_KSTATE), F32)],
        compiler_params=pltpu.CompilerParams(vmem_limit_bytes=VMEM_LIMIT),
        name="ssm_sample",
    )(u, h0, prm["wb_f32"], prm["wc_f32"], prm["a_re"], prm["a_im"], prm["dsk"], prm["wglu_f32"], prm["bglu"])


def ssm_params(lam_re, lam_im, log_dt, b_re, b_im, c_re, c_im, d_skip, w_glu, b_glu):
    dt = jnp.exp(log_dt)[:, None]
    mag = jnp.exp(lam_re * dt)
    ab_re = mag * jnp.cos(lam_im * dt)
    ab_im = mag * jnp.sin(lam_im * dt)
    den = lam_re * lam_re + lam_im * lam_im
    nr = ab_re - 1.0
    ni = ab_im
    z_re = ((nr * lam_re + ni * lam_im) / den)[..., None]
    z_im = ((ni * lam_re - nr * lam_im) / den)[..., None]
    bb_re = z_re * b_re - z_im * b_im
    bb_im = z_re * b_im + z_im * b_re
    gpb = SSM_GROUPS // SSM_KBLK
    eye = jnp.eye(gpb, dtype=F32)

    def in_mat(bb):
        bbk = bb.reshape(SSM_KBLK, gpb, SSM_STATE, SSM_GROUP_CH)
        m = jnp.einsum("kgpc,gh->kgchp", bbk, eye)
        return m.reshape(SSM_KBLK, LANES, SSM_KSTATE)

    def out_mat(c):
        ck = c.reshape(SSM_KBLK, gpb, SSM_GROUP_CH, SSM_STATE)
        m = jnp.einsum("kgcp,gh->kgphc", ck, eye)
        return m.reshape(SSM_KBLK, SSM_KSTATE, LANES)

    wb = jnp.concatenate([in_mat(bb_re), in_mat(bb_im)], axis=2)
    wc = jnp.concatenate([out_mat(c_re), -out_mat(c_im)], axis=1)
    return dict(
        wb=wb.astype(BF16), wc=wc.astype(BF16), wb_f32=wb, wc_f32=wc, wglu_f32=w_glu,
        a_re=ab_re.reshape(SSM_KBLK, 1, SSM_KSTATE), a_im=ab_im.reshape(SSM_KBLK, 1, SSM_KSTATE),
        dsk=d_skip.reshape(1, SSM_WIDTH), wglu=w_glu.astype(BF16), bglu=b_glu.reshape(1, SSM_WIDTH))


def _state_to_blocks(h):
    bsz = h.shape[0]
    return h.reshape(bsz, SSM_KBLK, SSM_KSTATE).transpose(1, 0, 2)


def _blocks_to_state(s):
    bsz = s.shape[1]
    return s.transpose(1, 0, 2).reshape(bsz, SSM_GROUPS, SSM_STATE)


def _mem_attn_kernel(q_ref, k_ref, v_ref, o_ref, *, precise):
    scale = MEM_HEAD_DIM ** -0.5
    for h in range(MEM_HEADS):
        sl = slice(h * MEM_HEAD_DIM, (h + 1) * MEM_HEAD_DIM)
        s = _mm(q_ref[:, sl], k_ref[:, sl], precise, NT_DIMS) * scale
        m = jnp.max(s, axis=-1, keepdims=True)
        p = jnp.exp(s - m)
        den = jnp.sum(p, axis=-1, keepdims=True)
        o_ref[:, sl] = _mm(p / den, v_ref[:, sl], precise).astype(o_ref.dtype)


def mem_attention(q, mk, mv, tl, out_dtype=F32, precise=False):
    bsz, seq, _ = q.shape
    tl = min(tl, seq)
    return pl.pallas_call(
        functools.partial(_mem_attn_kernel, precise=precise),
        grid=(bsz, seq // tl),
        in_specs=[
            pl.BlockSpec((None, tl, MEM_WIDTH), lambda b, i: (b, i, 0)),
            pl.BlockSpec((None, MEM_TOKENS, MEM_WIDTH), lambda b, i: (b, 0, 0)),
            pl.BlockSpec((None, MEM_TOKENS, MEM_WIDTH), lambda b, i: (b, 0, 0)),
        ],
        out_specs=pl.BlockSpec((None, tl, MEM_WIDTH), lambda b, i: (b, i, 0)),
        out_shape=jax.ShapeDtypeStruct((bsz, seq, MEM_WIDTH), out_dtype),
        compiler_params=_cparams(("arbitrary", "arbitrary")),
        name="mem_attention",
    )(q, mk, mv)


def _route(logits, seen_ref):
    lane = lax.broadcasted_iota(jnp.int32, logits.shape, 1)
    lanef = lane.astype(F32)
    none = float(ROUTER_COLS)
    neg = -jnp.inf
    gl = jnp.where(lane < MOE_GROUPS, logits, neg)
    gmax = jnp.max(gl, axis=-1, keepdims=True)
    grp = jnp.min(jnp.where(gl == gmax, lanef, none), axis=-1, keepdims=True)
    p_grp = 1.0 / jnp.sum(jnp.exp(gl - gmax), axis=-1, keepdims=True)
    lo = MOE_GROUPS + grp * EXPERTS_PER_GROUP
    el = jnp.where(jnp.logical_and(lanef >= lo, lanef < lo + EXPERTS_PER_GROUP), logits, neg)
    v1 = jnp.max(el, axis=-1, keepdims=True)
    i1 = jnp.min(jnp.where(el == v1, lanef, none), axis=-1, keepdims=True)
    el2 = jnp.where(lanef == i1, neg, el)
    v2 = jnp.max(el2, axis=-1, keepdims=True)
    i2 = jnp.min(jnp.where(el2 == v2, lanef, none), axis=-1, keepdims=True)
    t = jnp.exp(v2 - v1)
    g1 = p_grp / (1.0 + t)
    g2 = g1 * t
    e1 = i1 - MOE_GROUPS
    e2 = i2 - MOE_GROUPS
    oh1 = (lanef == e1).astype(F32)
    oh2 = (lanef == e2).astype(F32)
    both = oh1 + oh2
    tm = logits.shape[0]
    earlier = (lax.broadcasted_iota(jnp.int32, (tm, tm), 1) < lax.broadcasted_iota(jnp.int32, (tm, tm), 0))
    before = jnp.dot(earlier.astype(BF16), both.astype(BF16), preferred_element_type=F32) + seen_ref[...]
    r1 = jnp.sum(oh1 * before, axis=-1, keepdims=True)
    r2 = jnp.sum(oh2 * before, axis=-1, keepdims=True)
    seen_ref[...] += jnp.sum(both, axis=0, keepdims=True)
    out = jnp.where(lane == 0, g1, 0.0)
    out = jnp.where(lane == 1, g2, out)
    out = jnp.where(lane == 2, e1, out)
    out = jnp.where(lane == 3, e2, out)
    out = jnp.where(lane == 4, r1, out)
    return jnp.where(lane == 5, r2, out)


def _merge_kernel(att_ref, ssm_ref, mem_ref, gate_ref, x_ref, wa_ref, ws_ref, wm_ref, wo_ref,
                  n2_ref, wr_ref, br_ref, h_ref, hn_ref, lg_ref, seen_ref, *, tm, precise):
    @pl.when(pl.program_id(0) == 0)
    def _():
        seen_ref[...] = jnp.zeros_like(seen_ref)

    a = _mm(att_ref[...], wa_ref[...], precise)
    merged = gate_ref[:, 0:D_MODEL] * a
    s = _mm(ssm_ref[...], ws_ref[...], precise)
    merged = merged + gate_ref[:, D_MODEL:2 * D_MODEL] * s
    m = _mm(mem_ref[...], wm_ref[...], precise)
    merged = merged + gate_ref[:, 2 * D_MODEL:3 * D_MODEL] * m
    h = x_ref[...] + _mm(merged, wo_ref[...], precise)
    h_ref[...] = h
    hn = _rms(h, n2_ref[...])
    for j in range(D_MODEL // LANES):
        hn_ref[pl.ds(j, tm, stride=SUBLANES), :] = hn[:, j * LANES:(j + 1) * LANES]
    logits = _mm(hn, wr_ref[...], precise) + br_ref[...]
    lg_ref[...] = _route(logits, seen_ref)


def merge(att, ssm, mem, gates, x, wts, tm, precise=False):
    n = x.shape[0]
    tm = min(tm, n)
    row = lambda w: pl.BlockSpec((tm, w), lambda i: (i, 0))
    const = lambda a: pl.BlockSpec(a.shape, lambda i: (0, 0))
    ws = [wts["wa"], wts["ws"], wts["wm"], wts["wo"], wts["n2"], wts["wr"], wts["br"]]
    return pl.pallas_call(
        functools.partial(_merge_kernel, tm=tm, precise=precise),
        grid=(n // tm,),
        in_specs=[row(ATT_OUT), row(SSM_WIDTH), row(MEM_WIDTH), row(N_BRANCHES * D_MODEL), row(D_MODEL)]
                 + [const(w) for w in ws],
        out_specs=[row(D_MODEL), pl.BlockSpec((tm * SUBLANES, LANES), lambda i: (i, 0)), row(ROUTER_COLS)],
        out_shape=[jax.ShapeDtypeStruct((n, D_MODEL), F32),
                   jax.ShapeDtypeStruct((n * SUBLANES, LANES), F32),
                   jax.ShapeDtypeStruct((n, ROUTER_COLS), F32)],
        scratch_shapes=[pltpu.VMEM((1, ROUTER_COLS), F32)],
        compiler_params=_cparams(("arbitrary",)),
        name="merge",
    )(att, ssm, mem, gates, x, *ws)


TABLE_WORDS = SUBLANES * LANES


def _row_dma_loop(n, off, smem, slot, start):
    inner = min(n, LANES)
    assert n % inner == 0

    def body(r, carry):
        for j in range(inner):
            f = off + j
            start(smem[slot, f // LANES + r, f % LANES], r * inner + j, j % 2)
        return carry

    if n == inner:
        body(0, 0)
    else:
        lax.fori_loop(0, n // inner, body, 0)


def _tile_rows(i):
    start = i * SUBLANES
    return pl.ds(start if isinstance(start, int) else pl.multiple_of(start, SUBLANES), SUBLANES)


def _dispatch_kernel(pos_hbm, hn_ref, xs_hbm, pos_smem, zero, sem, psem, *, tm, n_pad):
    i = pl.program_id(0)
    slot = i % 2

    def pos_copy(b, s):
        return pltpu.make_async_copy(pos_hbm.at[b], pos_smem.at[s], psem.at[s])

    @pl.when(i == 0)
    def _():
        pos_copy(0, 0).start()

    @pl.when(i + 1 < pl.num_programs(0))
    def _():
        pos_copy(i + 1, 1 - slot).start()

    zero[...] = jnp.zeros_like(zero)
    pos_copy(i, slot).wait()

    def send_token(row, tok, prio):
        pltpu.make_async_copy(hn_ref.at[_tile_rows(tok)], xs_hbm.at[row], sem).start(priority=prio)

    def send_zero(row, _, prio):
        pltpu.make_async_copy(zero, xs_hbm.at[row], sem).start(priority=prio)

    for k in range(TOP_K):
        _row_dma_loop(tm, k * tm, pos_smem, slot, send_token)
    _row_dma_loop(n_pad, TOP_K * tm, pos_smem, slot, send_zero)
    whole, part = divmod(n_pad, tm)
    for _ in range(TOP_K + whole):
        pltpu.make_async_copy(hn_ref, hn_ref, sem).wait()
    if part:
        piece = hn_ref.at[pl.ds(0, part * SUBLANES)]
        pltpu.make_async_copy(piece, piece, sem).wait()


def moe_dispatch(pos, hn2d, n_rows, tm, n_pad):
    n_tiles = pos.shape[0]
    return pl.pallas_call(
        functools.partial(_dispatch_kernel, tm=tm, n_pad=n_pad),
        grid=(n_tiles,),
        in_specs=[pl.BlockSpec(memory_space=pl.ANY),
                  pl.BlockSpec((tm * SUBLANES, LANES), lambda i: (i, 0))],
        out_specs=pl.BlockSpec(memory_space=pl.ANY),
        out_shape=jax.ShapeDtypeStruct((n_rows, SUBLANES, LANES), F32),
        scratch_shapes=[pltpu.SMEM((2, SUBLANES, LANES), jnp.int32),
                        pltpu.VMEM((SUBLANES, LANES), F32),
                        pltpu.SemaphoreType.DMA, pltpu.SemaphoreType.DMA((2,))],
        compiler_params=_cparams(("arbitrary",)),
        name="moe_dispatch",
    )(pos, hn2d)


def _moe_kernel(be_ref, dst_hbm, xs_ref, w1_ref, w3_ref, w2_ref, out_hbm,
                dst_smem, ybuf, w1b, w3b, w2b, ssem, isem, *, bm):
    blk = pl.program_id(0)
    last = pl.num_programs(0) - 1
    slot = blk % 2
    tiles = D_MODEL // LANES

    def table_copy(entry, s):
        return pltpu.make_async_copy(dst_hbm.at[entry], dst_smem.at[s], isem.at[s])

    def wait_rows(s):
        pltpu.make_async_copy(ybuf.at[0], ybuf.at[0], ssem.at[s]).wait()

    def send_rows(buf, s):
        for i in range(bm):
            row = dst_smem[s, i // LANES, i % LANES]
            pltpu.make_async_copy(ybuf.at[buf, _tile_rows(i)], out_hbm.at[row], ssem.at[s]).start(priority=i % 2)

    @pl.when(blk == 0)
    def _():
        ybuf[1] = jnp.zeros(ybuf.shape[1:], F32)
        table_copy(0, 0).start()

    table_copy(blk + 1, 1 - slot).start()

    prev = jnp.maximum(blk - 1, 0)
    changed = jnp.logical_or(blk == 0, be_ref[blk] != be_ref[prev])

    @pl.when(changed)
    def _():
        w1b[...] = w1_ref[...].astype(BF16)
        w3b[...] = w3_ref[...].astype(BF16)
        w2b[...] = w2_ref[...].astype(BF16)

    table_copy(blk, slot).wait()
    x = jnp.concatenate([xs_ref[pl.ds(j, bm, stride=SUBLANES), :].astype(BF16) for j in range(tiles)], axis=1)
    h1 = jnp.dot(x, w1b[...], preferred_element_type=F32)
    send_rows(1 - slot, slot)
    h3 = jnp.dot(x, w3b[...], preferred_element_type=F32)
    act = (jax.nn.silu(h1) * h3).astype(BF16)
    y = jnp.dot(act, w2b[...], preferred_element_type=F32)

    @pl.when(blk > 0)
    def _():
        wait_rows(1 - slot)

    for j in range(tiles):
        ybuf[slot, pl.ds(j, bm, stride=SUBLANES), :] = y[:, j * LANES:(j + 1) * LANES]

    @pl.when(blk == last)
    def _():
        table_copy(blk + 1, 1 - slot).wait()
        send_rows(slot, 1 - slot)
        wait_rows(slot)
        wait_rows(1 - slot)


def moe_experts(block_expert, dst, xs2d, w1, w3, w2, n_out_rows, bm):
    n_blocks = dst.shape[0] - 1
    grid_spec = pltpu.PrefetchScalarGridSpec(
        num_scalar_prefetch=1,
        grid=(n_blocks,),
        in_specs=[
            pl.BlockSpec(memory_space=pl.ANY),
            pl.BlockSpec((bm * SUBLANES, LANES), lambda b, be: (b, 0)),
            pl.BlockSpec((None, D_MODEL, D_EXPERT), lambda b, be: (be[b], 0, 0)),
            pl.BlockSpec((None, D_MODEL, D_EXPERT), lambda b, be: (be[b], 0, 0)),
            pl.BlockSpec((None, D_EXPERT, D_MODEL), lambda b, be: (be[b], 0, 0)),
        ],
        out_specs=pl.BlockSpec(memory_space=pl.ANY),
        scratch_shapes=[
            pltpu.SMEM((2, SUBLANES, LANES), jnp.int32),
            pltpu.VMEM((2, bm * SUBLANES, LANES), F32),
            pltpu.VMEM((D_MODEL, D_EXPERT), BF16),
            pltpu.VMEM((D_MODEL, D_EXPERT), BF16),
            pltpu.VMEM((D_EXPERT, D_MODEL), BF16),
            pltpu.SemaphoreType.DMA((2,)),
            pltpu.SemaphoreType.DMA((2,)),
        ],
    )
    return pl.pallas_call(
        functools.partial(_moe_kernel, bm=bm),
        grid_spec=grid_spec,
        out_shape=jax.ShapeDtypeStruct((n_out_rows, SUBLANES, LANES), F32),
        compiler_params=_cparams(("arbitrary",)),
        name="moe_experts",
    )(block_expert, dst, xs2d, w1, w3, w2)


def _as_tables(cols, width):
    fill = jnp.zeros((cols.shape[0], TABLE_WORDS - width), jnp.int32)
    return jnp.concatenate([cols, fill], axis=1).reshape(cols.shape[0], SUBLANES, LANES)


def route_tables(route, bm, tm):
    n_tok = route.shape[0]
    n_pairs = n_tok * TOP_K
    lanes = route[:, :SUBLANES].T.astype(jnp.int32)
    e_slot = [lanes[2 + k] for k in range(TOP_K)]
    rank = [lanes[2 + TOP_K + k] for k in range(TOP_K)]
    e_flat = jnp.stack(e_slot, axis=1).reshape(n_pairs)
    experts = jnp.arange(N_EXPERTS, dtype=jnp.int32)
    counts = jnp.sum(experts[:, None] == e_flat[None, :], axis=1, dtype=jnp.int32)

    def lookup(table, idx):
        return jnp.sum(jnp.where(experts[:, None] == idx[None, :], table[:, None], 0), axis=0)

    padded = (counts + bm - 1) // bm * bm
    pad_ends = jnp.cumsum(padded)
    n_blocks = -(-n_pairs // bm) + N_EXPERTS
    n_rows = n_blocks * bm
    row0 = jnp.arange(n_blocks, dtype=jnp.int32) * bm
    block_expert = jnp.minimum(jnp.sum(pad_ends[None, :] <= row0[:, None], axis=1, dtype=jnp.int32),
                               N_EXPERTS - 1)
    n_fill = n_rows - n_pairs
    fill_id = jnp.arange(n_fill, dtype=jnp.int32)
    fill_e = jnp.minimum(fill_id // bm, N_EXPERTS - 1)
    fill_used = jnp.logical_and(fill_id < N_EXPERTS * bm, fill_id % bm < (padded - counts)[fill_e])
    shift = n_pairs.bit_length()
    keys = jnp.concatenate([e_flat * 2, jnp.where(fill_used, fill_e * 2 + 1, 2 * N_EXPERTS)])
    ids = jnp.concatenate([jnp.arange(1, n_pairs + 1, dtype=jnp.int32), jnp.zeros((n_fill,), jnp.int32)])
    pair_at = (lax.sort(keys * (1 << shift) + ids) & ((1 << shift) - 1)) - 1
    pad_starts = pad_ends - padded
    pair_row = [lookup(pad_starts, e_slot[k]) + rank[k] for k in range(TOP_K)]
    n_used_before = jnp.cumsum(fill_used.astype(jnp.int32)) - fill_used.astype(jnp.int32)
    pad_row = jnp.where(fill_used, (pad_starts + counts)[fill_e] + fill_id % bm,
                        pad_ends[-1] + fill_id - n_used_before)
    n_tiles = n_tok // tm
    n_pad = n_fill // n_tiles
    assert n_pad * n_tiles == n_fill and n_pad % SUBLANES == 0
    pos = jnp.concatenate([pair_row[k].reshape(n_tiles, tm) for k in range(TOP_K)]
                          + [pad_row.reshape(n_tiles, n_pad)], axis=1)
    pair_at = pair_at.reshape(n_blocks, bm)
    valid = pair_at >= 0
    bank = (jnp.arange(-1, n_blocks, dtype=jnp.int32) % 2)[:, None] * bm
    trash = n_pairs + bank + jnp.arange(bm, dtype=jnp.int32)[None, :]
    dst = jnp.where(valid, (pair_at % TOP_K) * n_tok + pair_at // TOP_K, trash[1:])
    dst = jnp.concatenate([trash[:1], dst], axis=0)
    return (block_expert, _as_tables(dst, bm), _as_tables(pos, TOP_K * tm + n_pad), n_pad, n_rows,
            n_pairs + 2 * bm)


def _final_kernel(h_ref, p0_ref, p1_ref, gw_ref, nf_ref, y_ref, *, tm):
    tiles = D_MODEL // LANES
    p0 = jnp.concatenate([p0_ref[pl.ds(j, tm, stride=SUBLANES), :] for j in range(tiles)], axis=1)
    p1 = jnp.concatenate([p1_ref[pl.ds(j, tm, stride=SUBLANES), :] for j in range(tiles)], axis=1)
    gw = gw_ref[...]
    h = h_ref[...] + (gw[:, 0:1] * p0 + gw[:, 1:2] * p1)
    y_ref[...] = _rms(h, nf_ref[...])


def final(h, pairs2d, gate_pad, norm_f, tm):
    n = h.shape[0]
    tm = min(tm, n)
    nb = n // tm
    row = lambda w: pl.BlockSpec((tm, w), lambda i: (i, 0))
    return pl.pallas_call(
        functools.partial(_final_kernel, tm=tm),
        grid=(nb,),
        in_specs=[
            row(D_MODEL),
            pl.BlockSpec((tm * SUBLANES, LANES), lambda i: (i, 0)),
            pl.BlockSpec((tm * SUBLANES, LANES), lambda i: (nb + i, 0)),
            row(LANES),
            pl.BlockSpec((1, D_MODEL), lambda i: (0, 0)),
        ],
        out_specs=row(D_MODEL),
        out_shape=jax.ShapeDtypeStruct((n, D_MODEL), F32),
        compiler_params=_cparams(("arbitrary",)),
        name="final",
    )(h, pairs2d, pairs2d, gate_pad, norm_f.reshape(1, D_MODEL))


def _alibi_slopes():
    h = jnp.arange(1, N_ATT_HEADS + 1, dtype=F32)
    return jnp.exp2(-8.0 * h / N_ATT_HEADS)


def _token_stage(att, ssm, mem, gates, x2, wts, w1, w3, w2, norm_f, tm, bm, precise=False):
    n = x2.shape[0]
    h, hn2d, route = merge(att, ssm, mem, gates, x2, wts, min(2 * tm, n), precise)
    tm = min(tm, n)
    block_expert, dst, pos, n_pad, n_rows, out_rows = route_tables(route, bm, tm)
    xs = moe_dispatch(pos, hn2d, n_rows, tm, n_pad)
    pairs = moe_experts(block_expert, dst, xs.reshape(n_rows * SUBLANES, LANES), w1, w3, w2, out_rows, bm)
    return final(h, pairs.reshape(out_rows * SUBLANES, LANES), route, norm_f, min(4 * tm, n))


def kernel(x_prompt, x_sample, cache_swa0_k, cache_swa0_v, cache_swa1_k, cache_swa1_v, cache_swa2_k, cache_swa2_v, cache_mem_k, cache_mem_v, state_ssm_re, state_ssm_im, mem_prompt, norm1, w_in, lam_re, lam_im, log_dt, ssm_b_re, ssm_b_im, ssm_c_re, ssm_c_im, ssm_d, w_glu, b_glu, w_att_out, w_ssm_out, w_mem_out, w_o, norm_mem, w_mk, w_mv, norm2, w_grp, b_grp, w_exp, b_exp, w1, w3, w2, norm_f):
    assert norm1.shape[0] == 1, "single-layer trunk"
    bsz, seq, _ = x_prompt.shape
    sbz = x_sample.shape[0]
    n_p = bsz * seq
    caches = (cache_swa0_k[0], cache_swa0_v[0], cache_swa1_k[0], cache_swa1_v[0], cache_swa2_k[0], cache_swa2_v[0])

    w_in_b = w_in[0].astype(BF16)
    w_mkv_b = jnp.concatenate([w_mk[0], w_mv[0]], axis=1).astype(BF16)
    wr = jnp.zeros((D_MODEL, ROUTER_COLS), F32)
    wr = wr.at[:, :MOE_GROUPS].set(w_grp[0]).at[:, MOE_GROUPS:MOE_GROUPS + N_EXPERTS].set(w_exp[0])
    br = jnp.zeros((1, ROUTER_COLS), F32)
    br = br.at[0, :MOE_GROUPS].set(b_grp[0]).at[0, MOE_GROUPS:MOE_GROUPS + N_EXPERTS].set(b_exp[0])
    wts_f32 = dict(wa=w_att_out[0], ws=w_ssm_out[0], wm=w_mem_out[0], wo=w_o[0],
                   n2=norm2[0].reshape(1, D_MODEL), wr=wr, br=br)
    wts = {k: (v.astype(BF16) if k in ("wa", "ws", "wm", "wo", "wr") else v) for k, v in wts_f32.items()}
    prm = ssm_params(lam_re[0], lam_im[0], log_dt[0], ssm_b_re[0], ssm_b_im[0], ssm_c_re[0], ssm_c_im[0],
                     ssm_d[0], w_glu[0], b_glu[0])
    slopes = _alibi_slopes()
    splits = (QKV_COLS, SSM_WIDTH, MEM_WIDTH, N_BRANCHES * D_MODEL)
    sig = (False, False, False, True)

    x2 = x_prompt.reshape(n_p, D_MODEL)
    qkv, u, mq, gates = norm_matmul(x2, norm1[0], w_in_b, splits, sig, tm=256, dtypes=(F32, F32, BF16, BF16))
    qkv3 = qkv.reshape(bsz, seq, QKV_COLS)
    att = attn_prompt(qkv3, slopes)
    ssm, st = ssm_prompt(u.reshape(bsz, seq, SSM_WIDTH), prm)
    mk, mv = norm_matmul(mem_prompt.reshape(bsz * MEM_TOKENS, D_MODEL), norm_mem[0], w_mkv_b,
                         (MEM_WIDTH, MEM_WIDTH), (False, False), tm=256)
    mk3 = mk.reshape(bsz, MEM_TOKENS, MEM_WIDTH)
    mv3 = mv.reshape(bsz, MEM_TOKENS, MEM_WIDTH)
    mem = mem_attention(mq.reshape(bsz, seq, MEM_WIDTH), mk3, mv3, tl=2048, out_dtype=BF16)
    y_p = _token_stage(att.reshape(n_p, ATT_OUT), ssm.reshape(n_p, SSM_WIDTH), mem.reshape(n_p, MEM_WIDTH),
                       gates, x2, wts, w1[0], w3[0], w2[0], norm_f, tm=256, bm=256)

    xs2 = x_sample.reshape(sbz, D_MODEL)
    qkv_s, u_s, mq_s, gates_s = norm_matmul(xs2, norm1[0], w_in[0], splits, sig, tm=sbz, precise=True)
    qkv5 = qkv_s.reshape(sbz, len(ATT_GROUPS), 3, HEADS_PER_GROUP, HEAD_DIM)
    att_s, *new_caches = attn_sample(qkv5, [jnp.transpose(c, (0, 2, 3, 1)) for c in caches])
    att_s = att_s.reshape(sbz, ATT_OUT)
    h0 = jnp.concatenate([_state_to_blocks(state_ssm_re[0]), _state_to_blocks(state_ssm_im[0])], axis=2)
    ssm_s, st_s = ssm_sample(u_s, h0, prm)
    cmk = cache_mem_k[0].reshape(sbz, MEM_TOKENS, MEM_WIDTH)
    cmv = cache_mem_v[0].reshape(sbz, MEM_TOKENS, MEM_WIDTH)
    mem_s = mem_attention(mq_s.reshape(sbz, 1, MEM_WIDTH), cmk, cmv, tl=1, precise=True).reshape(sbz, MEM_WIDTH)
    y_s = _token_stage(att_s, ssm_s, mem_s, gates_s, xs2, wts_f32, w1[0], w3[0], w2[0], norm_f, tm=sbz, bm=16,
                       precise=True)

    outs = [y_p.reshape(bsz, seq, D_MODEL), y_s.reshape(sbz, 1, D_MODEL)]
    for g, (window, _) in enumerate(ATT_GROUPS):
        keep = min(window, seq)
        for part in (1, 2):
            c0 = g * QKV_GROUP + part * ATT_OUT
            outs.append(qkv3[:, seq - keep:, c0:c0 + ATT_OUT].reshape(1, bsz, keep, HEADS_PER_GROUP, HEAD_DIM))
    outs.append(mk3.reshape(1, bsz, MEM_TOKENS, MEM_HEADS, MEM_HEAD_DIM))
    outs.append(mv3.reshape(1, bsz, MEM_TOKENS, MEM_HEADS, MEM_HEAD_DIM))
    outs.append(_blocks_to_state(st[:, :, :SSM_KSTATE])[None])
    outs.append(_blocks_to_state(st[:, :, SSM_KSTATE:])[None])
    outs.extend(jnp.transpose(c, (0, 3, 1, 2))[None] for c in new_caches)
    outs.append(_blocks_to_state(st_s[:, :, :SSM_KSTATE])[None])
    outs.append(_blocks_to_state(st_s[:, :, SSM_KSTATE:])[None])
    return tuple(outs)
```
